```python
import math
import jax, jax.numpy as jnp
from jax import lax
import numpy as np

D_MODEL = 1024
BATCH = 32
SEQ = 2048
DEPTH = 2

GRID_W = 64
CTX_LEN = 256

ATT_HEADS = 8
ATT_KV_HEADS = 2
ATT_HEAD_DIM = 64
ATT_Q_BLOCK = 128
ROPE_THETA = 10000.0
GLA_HEADS = 4
GLA_DK = 64
GLA_DV = 128
GLA_RANK = 16
GLA_TAU = 16.0
GLA_CHUNK = 64
HY_WIDTH = 512
HY_SHORT = 3
HY_POS_FREQS = 16
HY_POS_DIM = 1 + 2 * HY_POS_FREQS
HY_FILTER_HIDDEN = 64
HY_DECAY_SLOW = math.log(1e-2) / 1.5
HY_DECAY_FAST = math.log(1e-2) / 0.3
N_EXPERTS = 32
TOP_K = 4
EXPERT_HIDDEN = D_MODEL
SWIGLU_LIMIT = 7.0
SWIGLU_ALPHA = 1.702
DEEPNORM_ALPHA = (2 * DEPTH) ** 0.25
DEEPNORM_BETA = (8 * DEPTH) ** -0.25
LN_EPS = 1e-5
RMS_EPS = 1e-6

ATT_Q_W = ATT_HEADS * ATT_HEAD_DIM
ATT_KV_W = ATT_KV_HEADS * ATT_HEAD_DIM
GLA_K_W = GLA_HEADS * GLA_DK
GLA_V_W = GLA_HEADS * GLA_DV
IN_SPLITS = (ATT_Q_W, ATT_KV_W, ATT_KV_W,
             GLA_K_W, GLA_K_W, GLA_V_W, GLA_RANK, GLA_RANK, GLA_V_W,
             3 * HY_WIDTH,
             3 * D_MODEL)
IN_SPLIT_IDX = tuple(int(i) for i in np.cumsum(IN_SPLITS)[:-1])
D_IN = sum(IN_SPLITS)

kernel_name = "hybrid_gqa_gla_hyena_moe_diffusion_block"


def layer_norm(x):
    xf = x.astype(jnp.float32)
    mu = jnp.mean(xf, axis=-1, keepdims=True)
    var = jnp.mean(jnp.square(xf - mu), axis=-1, keepdims=True)
    return ((xf - mu) * lax.rsqrt(var + LN_EPS)).astype(x.dtype)


def rms_norm(x, g):
    xf = x.astype(jnp.float32)
    y = xf * lax.rsqrt(jnp.mean(xf * xf, axis=-1, keepdims=True) + RMS_EPS)
    return y.astype(x.dtype) * g


def modulate(x, shift, scale):
    return x * (1.0 + scale) + shift


def axial_rope(rows):
    r = jnp.broadcast_to(jnp.arange(rows, dtype=jnp.float32)[:, None], (rows, GRID_W)).reshape(-1)
    col = jnp.broadcast_to(jnp.arange(GRID_W, dtype=jnp.float32)[None, :], (rows, GRID_W)).reshape(-1)
    axis_dim = ATT_HEAD_DIM // 2
    inv_freq = ROPE_THETA ** (-jnp.arange(0, axis_dim, 2, dtype=jnp.float32) / axis_dim)
    ang = jnp.concatenate([r[:, None] * inv_freq, col[:, None] * inv_freq], axis=-1)
    return jnp.cos(ang), jnp.sin(ang)


def apply_rope(t, cos, sin):
    t1, t2 = t[..., 0::2], t[..., 1::2]
    c, s = cos.astype(t.dtype), sin.astype(t.dtype)
    return jnp.stack([t1 * c - t2 * s, t1 * s + t2 * c], axis=-1).reshape(t.shape)


def split_heads(t, n_heads):
    b, l, _ = t.shape
    return t.reshape(b, l, n_heads, -1).transpose(0, 2, 1, 3)


def gqa_attend(q, k, v):
    s = jnp.einsum('bkgqd,bksd->bkgqs', q, k).astype(jnp.float32) * (ATT_HEAD_DIM ** -0.5)
    p = jax.nn.softmax(s, axis=-1).astype(v.dtype)
    return jnp.einsum('bkgqs,bksd->bkgqd', p, v)


def attention_mixer(q, k, v, qc, kc, vc, q_norm, k_norm, cos, sin, with_ctx):
    grp = ATT_HEADS // ATT_KV_HEADS

    def q_heads(t):
        b, l, _ = t.shape
        t = rms_norm(t.reshape(b, l, ATT_KV_HEADS, grp, ATT_HEAD_DIM), q_norm)
        return t.transpose(0, 2, 3, 1, 4)

    def merge_heads(o):
        b, _, _, l, _ = o.shape
        return o.transpose(0, 3, 1, 2, 4).reshape(b, l, ATT_Q_W)

    kc_h = rms_norm(split_heads(kc, ATT_KV_HEADS), k_norm)
    vc_h = split_heads(vc, ATT_KV_HEADS)
    q_h = apply_rope(q_heads(q), cos, sin)
    k_h = apply_rope(rms_norm(split_heads(k, ATT_KV_HEADS), k_norm), cos, sin)
    k_all = jnp.concatenate([k_h, kc_h], axis=2)
    v_all = jnp.concatenate([split_heads(v, ATT_KV_HEADS), vc_h], axis=2)
    b, _, _, l, _ = q_h.shape
    nb = l // ATT_Q_BLOCK
    q_blocks = jnp.moveaxis(q_h.reshape(b, ATT_KV_HEADS, grp, nb, ATT_Q_BLOCK, ATT_HEAD_DIM), 3, 0)
    o = lax.map(lambda qb: gqa_attend(qb, k_all, v_all), q_blocks)
    o = jnp.moveaxis(o, 0, 3).reshape(b, ATT_KV_HEADS, grp, l, ATT_HEAD_DIM)
    out = merge_heads(o)
    out_c = merge_heads(gqa_attend(q_heads(qc), kc_h, vc_h)) if with_ctx else None
    return out, out_c


def gla_chunked(q, k, v, log_a, s0):
    b, h, l, dk = q.shape
    dv = v.shape[-1]
    n = l // GLA_CHUNK
    rs = lambda t: t.reshape(b, h, n, GLA_CHUNK, t.shape[-1])
    q, k, v, g = rs(q), rs(k), rs(v), rs(log_a)
    cum = jnp.cumsum(g, axis=3)
    cum_last = cum[..., -1:, :]
    q_dec = q * jnp.exp(cum)
    k_inv = k * jnp.exp(-cum)
    k_to_end = k * jnp.exp(cum_last - cum)
    mask = jnp.tril(jnp.ones((GLA_CHUNK, GLA_CHUNK), dtype=bool))
    att = jnp.where(mask, jnp.einsum('bhnik,bhnjk->bhnij', q_dec, k_inv), 0.0)
    o_intra = jnp.einsum('bhnij,bhnjv->bhniv', att, v)
    kv_chunk = jnp.einsum('bhnjk,bhnjv->bhnkv', k_to_end, v)
    decay_chunk = jnp.exp(cum_last[..., 0, :])

    def step(state, inp):
        qd, kv, dec = inp
        o = jnp.einsum('bhik,bhkv->bhiv', qd, state)
        return dec[..., None] * state + kv, o

    s_fin, o_inter = lax.scan(step, s0, (jnp.moveaxis(q_dec, 2, 0), jnp.moveaxis(kv_chunk, 2, 0),
                                         jnp.moveaxis(decay_chunk, 2, 0)))
    o = o_intra + jnp.moveaxis(o_inter, 0, 2)
    return o.reshape(b, h, l, dv), s_fin


def bidir_gla(q, k, v, la_f, la_b, s0_f, s0_b):
    flip = lambda t: jnp.flip(t, axis=2)
    o_f, s_f = gla_chunked(q, k, v, la_f, s0_f)
    o_b, s_b = gla_chunked(flip(q), flip(k), flip(v), flip(la_b), s0_b)
    return o_f + flip(o_b), s_f, s_b


def gla_mixer(lat, ctx_in, wdec_f, bdec_f, wdec_b, bdec_b, norm_g, with_ctx):
    def heads(parts):
        q, k, v, lr_f, lr_b, og = parts
        qh = split_heads(q, GLA_HEADS).astype(jnp.float32) * (GLA_DK ** -0.5)
        kh = split_heads(k, GLA_HEADS).astype(jnp.float32)
        vh = split_heads(v, GLA_HEADS).astype(jnp.float32)
        la_f = split_heads(jax.nn.log_sigmoid((lr_f @ wdec_f + bdec_f).astype(jnp.float32)) / GLA_TAU, GLA_HEADS)
        la_b = split_heads(jax.nn.log_sigmoid((lr_b @ wdec_b + bdec_b).astype(jnp.float32)) / GLA_TAU, GLA_HEADS)
        return qh, kh, vh, la_f, la_b, og

    def finish(o, og):
        b, h, l, dv = o.shape
        o = rms_norm(o.transpose(0, 2, 1, 3), norm_g).reshape(b, l, h * dv).astype(og.dtype)
        return o * jax.nn.silu(og)

    qc, kc, vc, lafc, labc, ogc = heads(ctx_in)
    s0 = jnp.zeros((qc.shape[0], GLA_HEADS, GLA_DK, GLA_DV), jnp.float32)
    oc, s_f, s_b = bidir_gla(qc, kc, vc, lafc, labc, s0, s0)
    q, k, v, laf, lab, og = heads(lat)
    o, _, _ = bidir_gla(q, k, v, laf, lab, s_f, s_b)
    return finish(o, og), (finish(oc, ogc) if with_ctx else None)


def short_conv(u, w, bias):
    l = u.shape[1]
    pad = HY_SHORT // 2
    up = jnp.pad(u, ((0, 0), (pad, pad), (0, 0)))
    return sum(up[:, j:j + l] * w[j] for j in range(HY_SHORT)) + bias


def hyena_filters(l, fw1, fb1, fw2, fb2, fw3, fb3):
    t = jnp.arange(l, dtype=jnp.float32)[:, None] / l
    freqs = jnp.arange(1, HY_POS_FREQS + 1, dtype=jnp.float32)
    z = jnp.concatenate([t, jnp.cos(2.0 * math.pi * freqs * t), jnp.sin(2.0 * math.pi * freqs * t)], axis=-1)
    h = jnp.sin(z @ fw1 + fb1)
    h = jnp.sin(h @ fw2 + fb2)
    h = h @ fw3 + fb3
    deltas = jnp.abs(jnp.linspace(HY_DECAY_SLOW, HY_DECAY_FAST, HY_WIDTH))
    window = jnp.exp(-t * deltas)
    h_f, h_b = jnp.split(h, 2, axis=-1)
    return h_f * window, h_b * window


def long_conv(u, h_f, h_b, skip):
    l = u.shape[1]
    h_circ = jnp.concatenate([h_f, jnp.zeros_like(h_f[:1]), h_b[:0:-1]], axis=0)
    uf = jnp.fft.rfft(u.astype(jnp.float32), n=2 * l, axis=1)
    hf = jnp.fft.rfft(h_circ.astype(jnp.float32), n=2 * l, axis=0)
    y = jnp.fft.irfft(uf * hf[None], n=2 * l, axis=1)[:, :l]
    return (y + u.astype(jnp.float32) * skip).astype(u.dtype)


def hyena_mixer(proj, conv_w, conv_b, fw1, fb1, fw2, fb2, fw3, fb3, skip):
    z = short_conv(proj, conv_w, conv_b)
    x0, x1, v = jnp.split(z, 3, axis=-1)
    h_f, h_b = hyena_filters(proj.shape[1], fw1, fb1, fw2, fb2, fw3, fb3)
    return x0 * long_conv(x1 * v, h_f, h_b, skip)


def merge_branches(o_a, o_b, o_c, gates, w_a, w_b, w_c, w_out):
    g_a, g_b, g_c = jnp.split(jax.nn.sigmoid(gates), 3, axis=-1)
    merged = g_a * (o_a @ w_a) + g_b * (o_b @ w_b) + g_c * (o_c @ w_c)
    return merged @ w_out


def token_mixing(u, uc, w_in, att_p, gla_p, hy_p, branch_p, cos, sin, with_ctx):
    (a_q, a_k, a_v, b_q, b_k, b_v, b_lrf, b_lrb, b_og, c_in, gates) = jnp.split(u @ w_in, IN_SPLIT_IDX, axis=-1)
    (ac_q, ac_k, ac_v, bc_q, bc_k, bc_v, bc_lrf, bc_lrb, bc_og, cc_in, gates_c) = jnp.split(uc @ w_in, IN_SPLIT_IDX, axis=-1)
    o_a, o_a_c = attention_mixer(a_q, a_k, a_v, ac_q, ac_k, ac_v, *att_p, cos, sin, with_ctx)
    o_b, o_b_c = gla_mixer((b_q, b_k, b_v, b_lrf, b_lrb, b_og), (bc_q, bc_k, bc_v, bc_lrf, bc_lrb, bc_og),
                           *gla_p, with_ctx)
    out = merge_branches(o_a, o_b, hyena_mixer(c_in, *hy_p), gates, *branch_p)
    out_c = merge_branches(o_a_c, o_b_c, hyena_mixer(cc_in, *hy_p), gates_c, *branch_p) if with_ctx else None
    return out, out_c


def moe_ffn(u, router_w, router_b, w1, b1, w2, b2):
    logits = (u @ router_w + router_b).astype(jnp.float32)
    top_val, top_idx = lax.top_k(logits, TOP_K)
    top_w = jax.nn.softmax(top_val, axis=-1)
    combine = jnp.einsum('nk,nke->ne', top_w,
                         jax.nn.one_hot(top_idx, N_EXPERTS, dtype=jnp.float32)).astype(u.dtype)
    y = jnp.zeros((u.shape[0], w2.shape[-1]), u.dtype)
    for e in range(N_EXPERTS):
        h = u @ w1[e] + b1[e]
        g, up = h[:, 0::2], h[:, 1::2]
        g = jnp.minimum(g, SWIGLU_LIMIT)
        up = jnp.clip(up, -SWIGLU_LIMIT, SWIGLU_LIMIT)
        act = g * jax.nn.sigmoid(SWIGLU_ALPHA * g) * (up + 1.0)
        y = y + combine[:, e:e + 1] * (act @ w2[e] + b2[e])
    return y


def setup_inputs(seed: int = 0) -> dict:
    key = jax.random.key(seed)
    ks = iter(jax.random.split(key, 48))

    def nrm(shape, scale=1.0):
        return scale * jax.random.normal(next(ks), shape, jnp.float32)

    def gain(shape):
        return 1.0 + nrm(shape, 0.02)

    L_ = DEPTH
    D = D_MODEL
    return {
        "x": nrm((BATCH, SEQ, D)),
        "c": nrm((BATCH, D)),
        "ctx": nrm((BATCH, CTX_LEN, D)),
        "c_ctx": nrm((D,)),
        "w_mod": nrm((L_, D, 6 * D), D ** -0.5),
        "b_mod": nrm((L_, 6 * D), 0.01),
        "w_in": nrm((L_, D, D_IN), D ** -0.5),
        "att_q_norm": gain((L_, ATT_HEAD_DIM)),
        "att_k_norm": gain((L_, ATT_HEAD_DIM)),
        "gla_wdec_f": nrm((L_, GLA_RANK, GLA_K_W), GLA_RANK ** -0.5),
        "gla_bdec_f": nrm((L_, GLA_K_W), 0.1),
        "gla_wdec_b": nrm((L_, GLA_RANK, GLA_K_W), GLA_RANK ** -0.5),
        "gla_bdec_b": nrm((L_, GLA_K_W), 0.1),
        "gla_out_norm": gain((L_, GLA_DV)),
        "hy_conv_w": nrm((L_, HY_SHORT, 3 * HY_WIDTH), 0.5),
        "hy_conv_b": nrm((L_, 3 * HY_WIDTH), 0.01),
        "hy_fw1": nrm((L_, HY_POS_DIM, HY_FILTER_HIDDEN), HY_POS_DIM ** -0.5),
        "hy_fb1": nrm((L_, HY_FILTER_HIDDEN), 0.1),
        "hy_fw2": nrm((L_, HY_FILTER_HIDDEN, HY_FILTER_HIDDEN), HY_FILTER_HIDDEN ** -0.5),
        "hy_fb2": nrm((L_, HY_FILTER_HIDDEN), 0.1),
        "hy_fw3": nrm((L_, HY_FILTER_HIDDEN, 2 * HY_WIDTH), 0.1 * HY_FILTER_HIDDEN ** -0.5),
        "hy_fb3": nrm((L_, 2 * HY_WIDTH), 0.01),
        "hy_skip": nrm((L_, HY_WIDTH), 0.1),
        "w_branch_a": nrm((L_, ATT_Q_W, D), ATT_Q_W ** -0.5),
        "w_branch_b": nrm((L_, GLA_V_W, D), GLA_V_W ** -0.5),
        "w_branch_c": nrm((L_, HY_WIDTH, D), HY_WIDTH ** -0.5),
        "w_out": nrm((L_, D, D), DEEPNORM_BETA * D ** -0.5),
        "ln1_g": gain((L_, D)),
        "ln1_b": nrm((L_, D), 0.01),
        "router_w": nrm((L_, D, N_EXPERTS), D ** -0.5),
        "router_b": nrm((L_, N_EXPERTS), 0.01),
        "exp_w1": nrm((L_, N_EXPERTS, D, 2 * EXPERT_HIDDEN), D ** -0.5),
        "exp_b1": nrm((L_, N_EXPERTS, 2 * EXPERT_HIDDEN), 0.01),
        "exp_w2": nrm((L_, N_EXPERTS, EXPERT_HIDDEN, D), DEEPNORM_BETA * EXPERT_HIDDEN ** -0.5),
        "exp_b2": nrm((L_, N_EXPERTS, D), 0.01),
        "ln2_g": gain((L_, D)),
        "ln2_b": nrm((L_, D), 0.01),
    }


def reference(x, c, ctx, c_ctx, w_mod, b_mod, w_in, att_q_norm, att_k_norm,
              gla_wdec_f, gla_bdec_f, gla_wdec_b, gla_bdec_b, gla_out_norm,
              hy_conv_w, hy_conv_b, hy_fw1, hy_fb1, hy_fw2, hy_fb2, hy_fw3, hy_fb3, hy_skip,
              w_branch_a, w_branch_b, w_branch_c, w_out, ln1_g, ln1_b,
              router_w, router_b, exp_w1, exp_b1, exp_w2, exp_b2, ln2_g, ln2_b):
    b, l, d = x.shape
    rows = l // GRID_W
    cos, sin = axial_rope(rows)
    x_lat, x_ctx = x, ctx
    for i in range(DEPTH):
        with_ctx = i < DEPTH - 1
        mod = jax.nn.silu(c) @ w_mod[i] + b_mod[i]
        mod_c = jax.nn.silu(c_ctx) @ w_mod[i] + b_mod[i]
        sh1, sc1, g1, sh2, sc2, g2 = [m[:, None, :] for m in jnp.split(mod, 6, axis=-1)]
        sh1c, sc1c, g1c, sh2c, sc2c, g2c = jnp.split(mod_c, 6, axis=-1)

        u = modulate(layer_norm(x_lat), sh1, sc1)
        uc = modulate(layer_norm(x_ctx), sh1c, sc1c)
        mix, mix_c = token_mixing(
            u, uc, w_in[i],
            (att_q_norm[i], att_k_norm[i]),
            (gla_wdec_f[i], gla_bdec_f[i], gla_wdec_b[i], gla_bdec_b[i], gla_out_norm[i]),
            (hy_conv_w[i], hy_conv_b[i], hy_fw1[i], hy_fb1[i], hy_fw2[i], hy_fb2[i], hy_fw3[i], hy_fb3[i], hy_skip[i]),
            (w_branch_a[i], w_branch_b[i], w_branch_c[i], w_out[i]),
            cos, sin, with_ctx)
        x_lat = layer_norm(DEEPNORM_ALPHA * x_lat + g1 * mix) * ln1_g[i] + ln1_b[i]
        if with_ctx:
            x_ctx = layer_norm(DEEPNORM_ALPHA * x_ctx + g1c * mix_c) * ln1_g[i] + ln1_b[i]

        u2 = modulate(layer_norm(x_lat), sh2, sc2).reshape(b * l, d)
        if with_ctx:
            u2c = modulate(layer_norm(x_ctx), sh2c, sc2c).reshape(-1, d)
            y = moe_ffn(jnp.concatenate([u2, u2c], axis=0), router_w[i], router_b[i],
                        exp_w1[i], exp_b1[i], exp_w2[i], exp_b2[i])
            y_lat = y[:b * l].reshape(b, l, d)
            y_ctx = y[b * l:].reshape(x_ctx.shape)
            x_ctx = layer_norm(DEEPNORM_ALPHA * x_ctx + g2c * y_ctx) * ln2_g[i] + ln2_b[i]
        else:
            y_lat = moe_ffn(u2, router_w[i], router_b[i], exp_w1[i], exp_b1[i],
                            exp_w2[i], exp_b2[i]).reshape(b, l, d)
        x_lat = layer_norm(DEEPNORM_ALPHA * x_lat + g2 * y_lat) * ln2_g[i] + ln2_b[i]
    return x_lat
```

```python
import functools
import math

import numpy as np
import jax
import jax.numpy as jnp
from jax import lax
from jax.experimental import pallas as pl
from jax.experimental.pallas import tpu as pltpu

F32 = jnp.float32
BF16 = jnp.bfloat16

D_MODEL = 1024
DEPTH = 2
GRID_W = 64
ATT_HEADS = 8
ATT_KV_HEADS = 2
ATT_HEAD_DIM = 64
ROPE_THETA = 10000.0
GLA_HEADS = 4
GLA_DK = 64
GLA_DV = 128
GLA_RANK = 16
GLA_TAU = 16.0
GLA_CHUNK = 64
HY_WIDTH = 512
HY_POS_FREQS = 16
HY_DECAY_SLOW = math.log(1e-2) / 1.5
HY_DECAY_FAST = math.log(1e-2) / 0.3
N_EXPERTS = 32
TOP_K = 4
EXPERT_HIDDEN = D_MODEL
SWIGLU_LIMIT = 7.0
SWIGLU_ALPHA = 1.702
DEEPNORM_ALPHA = (2 * DEPTH) ** 0.25
LN_EPS = 1e-5
RMS_EPS = 1e-6

ATT_Q_W = ATT_HEADS * ATT_HEAD_DIM
ATT_KV_W = ATT_KV_HEADS * ATT_HEAD_DIM
GLA_K_W = GLA_HEADS * GLA_DK
GLA_V_W = GLA_HEADS * GLA_DV
GLA_GROUP_W = 1152
LANES = 128
NEG_BIG = -3.0e38

VMEM_LIMIT = 56 * 1024 * 1024


def _cparams(*sem):
    return pltpu.CompilerParams(dimension_semantics=sem, vmem_limit_bytes=VMEM_LIMIT)


def _dot(a, b):
    return jnp.dot(a, b, preferred_element_type=F32)


def _dot_nt(a, b):
    return lax.dot_general(a, b, (((1,), (1,)), ((), ())), preferred_element_type=F32)


def _dot_tn(a, b):
    return lax.dot_general(a, b, (((0,), (0,)), ((), ())), preferred_element_type=F32)


def _split2(x):
    hi = x.astype(BF16)
    lo = (x - hi.astype(F32)).astype(BF16)
    return hi, lo


def _split3(x):
    hi = x.astype(BF16)
    r = x - hi.astype(F32)
    mid = r.astype(BF16)
    lo = (r - mid.astype(F32)).astype(BF16)
    return hi, mid, lo


def _dot3(a, b):
    ah, al = _split2(a)
    bh, bl = _split2(b)
    return _dot(ah, bh) + (_dot(ah, bl) + _dot(al, bh))


def _dot_exact_lhs(a_bf16, b):
    b0, b1, b2 = _split3(b)
    return _dot(a_bf16, b0) + (_dot(a_bf16, b1) + _dot(a_bf16, b2))


def _ln(x):
    mu = jnp.mean(x, axis=-1, keepdims=True)
    xc = x - mu
    var = jnp.mean(xc * xc, axis=-1, keepdims=True)
    return xc * lax.rsqrt(var + LN_EPS)


def _sigmoid(x):
    return 1.0 / (1.0 + jnp.exp(-x))


def _mod_kernel(c_ref, w_ref, b_ref, o_ref):
    c = c_ref[...]
    o_ref[...] = _dot3(c * _sigmoid(c), w_ref[...]) + b_ref[...]


def _modulation(cc, w_mod, b_mod):
    rows, d = cc.shape
    n = w_mod.shape[1]
    tn = 1024
    return pl.pallas_call(
        _mod_kernel,
        grid=(n // tn,),
        in_specs=[pl.BlockSpec((rows, d), lambda j: (0, 0)),
                  pl.BlockSpec((d, tn), lambda j: (0, j)),
                  pl.BlockSpec((1, tn), lambda j: (0, j))],
        out_specs=pl.BlockSpec((rows, tn), lambda j: (0, j)),
        out_shape=jax.ShapeDtypeStruct((rows, n), F32),
        compiler_params=_cparams("arbitrary"),
        name="modulation",
    )(cc, w_mod, b_mod.reshape(1, n))


def _head_norm_rope(t, gain, rc, rs1, rs2, bd):
    sh, sl = _split2(t * t)
    ss = _dot(sh, bd) + _dot(sl, bd)
    tn = t * lax.rsqrt(ss * (1.0 / ATT_HEAD_DIM) + RMS_EPS) * gain
    return tn * rc + pltpu.roll(tn, 96, 1) * rs1 + pltpu.roll(tn, 32, 1) * rs2


def _inproj_kernel(x_ref, sh_ref, sc_ref, watt_ref, wgla_ref, wog_ref, why_ref, wgt_ref,
                   qg_ref, kg_ref, rc_ref, rs1_ref, rs2_ref, bd_ref,
                   qn_ref, kn_ref, v_ref, gla_ref, og_ref, hy_ref, gt_ref):
    u = _ln(x_ref[0]) * (1.0 + sc_ref[0]) + sh_ref[0]
    ub = u.astype(BF16)
    rc, rs1, rs2, bd = rc_ref[0], rs1_ref[0], rs2_ref[0], bd_ref[...]
    att = _dot(ub, watt_ref[...])
    for j in range(ATT_Q_W // LANES):
        t = att[:, LANES * j:LANES * (j + 1)]
        qn = _head_norm_rope(t, qg_ref[...], rc, rs1, rs2, bd) * (ATT_HEAD_DIM ** -0.5)
        qn_ref[0, :, LANES * j:LANES * (j + 1)] = qn.astype(BF16)
    kn = _head_norm_rope(att[:, ATT_Q_W:ATT_Q_W + LANES], kg_ref[...], rc, rs1, rs2, bd)
    kn_ref[0] = kn.astype(BF16)
    v_ref[0] = att[:, ATT_Q_W + LANES:].astype(BF16)
    gla_ref[0] = _dot(ub, wgla_ref[...])
    og_ref[0] = _dot(ub, wog_ref[...]).astype(BF16)
    hy_ref[0] = _dot(ub, why_ref[...]).astype(BF16)
    gt_ref[0] = _dot(ub, wgt_ref[...]).astype(BF16)


def _inproj(x_all, sh, sc, wts, qg, kg, rope, bd, n_lat):
    p, l, d = x_all.shape
    tm = min(256, l)
    watt, wgla, wog, why, wgt = wts
    rc, rs1, rs2 = rope
    const = lambda a: pl.BlockSpec(a.shape, lambda b, i: (0,) * a.ndim)
    row = lambda w: pl.BlockSpec((1, tm, w), lambda b, i: (b, i, 0))
    tab = pl.BlockSpec((1, tm, LANES), lambda b, i: (jnp.where(b >= n_lat, 1, 0), i, 0))
    vec = pl.BlockSpec((1, 1, d), lambda b, i: (b, 0, 0))
    widths = (ATT_Q_W, LANES, LANES, GLA_GROUP_W, GLA_V_W, 3 * HY_WIDTH, 3 * d)
    dtypes = (BF16, BF16, BF16, F32, BF16, BF16, BF16)
    return pl.pallas_call(
        _inproj_kernel,
        grid=(p, l // tm),
        in_specs=[row(d), vec, vec, const(watt), const(wgla), const(wog), const(why), const(wgt),
                  const(qg), const(kg), tab, tab, tab, const(bd)],
        out_specs=[row(w) for w in widths],
        out_shape=[jax.ShapeDtypeStruct((p, l, w), dt) for w, dt in zip(widths, dtypes)],
        compiler_params=_cparams("parallel", "arbitrary"),
        name="inproj",
    )(x_all, sh, sc, watt, wgla, wog, why, wgt, qg, kg, rc, rs1, rs2, bd)


def _attend_group(qg, ks, vs, lo):
    zero = jnp.zeros_like(qg)
    outs = []
    for qh in (jnp.where(lo, qg, zero), jnp.where(lo, zero, qg)):
        ss = [_dot_nt(qh, k) for k in ks]
        m = ss[0].max(axis=-1, keepdims=True)
        for s in ss[1:]:
            m = jnp.maximum(m, s.max(axis=-1, keepdims=True))
        den = 0.0
        acc = 0.0
        for s, v in zip(ss, vs):
            p = jnp.exp(s - m)
            den = den + p.sum(axis=-1, keepdims=True)
            acc = acc + _dot(p.astype(BF16), v)
        outs.append(acc / den)
    return jnp.where(lo, outs[0], outs[1])


def _attn_kernel(q_ref, k_ref, v_ref, kc_ref, vc_ref, o_ref, *, n_lat_tiles, with_ctx):
    i = pl.program_id(1)
    tq = q_ref.shape[1]
    lo = lax.broadcasted_iota(jnp.int32, (tq, LANES), 1) < ATT_HEAD_DIM

    def run(ks, vs):
        for j in range(ATT_Q_W // LANES):
            o = _attend_group(q_ref[0, :, LANES * j:LANES * (j + 1)], ks, vs, lo)
            o_ref[0, :, LANES * j:LANES * (j + 1)] = o.astype(BF16)

    if with_ctx:
        @pl.when(i < n_lat_tiles)
        def _():
            run([k_ref[0], kc_ref[0]], [v_ref[0], vc_ref[0]])

        @pl.when(i == n_lat_tiles)
        def _():
            run([kc_ref[0]], [vc_ref[0]])
    else:
        run([k_ref[0], kc_ref[0]], [v_ref[0], vc_ref[0]])


def _attention(qn, kn, v, n_lat, lc, with_ctx):
    p, l, _ = qn.shape
    tq = lc
    r = l // lc
    nt = l // tq
    ctx_idx = lambda b: (n_lat + b // r, b % r, 0)
    if with_ctx:
        qmap = lambda b, i: (jnp.where(i < nt, b, n_lat + b // r), jnp.where(i < nt, i, b % r), 0)
    else:
        qmap = lambda b, i: (b, i, 0)
    return pl.pallas_call(
        functools.partial(_attn_kernel, n_lat_tiles=nt, with_ctx=with_ctx),
        grid=(n_lat, nt + (1 if with_ctx else 0)),
        in_specs=[pl.BlockSpec((1, tq, ATT_Q_W), qmap),
                  pl.BlockSpec((1, l, LANES), lambda b, i: (b, 0, 0)),
                  pl.BlockSpec((1, l, LANES), lambda b, i: (b, 0, 0)),
                  pl.BlockSpec((1, lc, LANES), lambda b, i: ctx_idx(b)),
                  pl.BlockSpec((1, lc, LANES), lambda b, i: ctx_idx(b))],
        out_specs=pl.BlockSpec((1, tq, ATT_Q_W), qmap),
        out_shape=jax.ShapeDtypeStruct((p if with_ctx else n_lat, l, ATT_Q_W), BF16),
        compiler_params=_cparams("parallel", "arbitrary"),
        name="attention",
    )(qn, kn, v, kn, v)


def _gla_segment(src_ref, og_ref, out_ref, acc_ref, st_ref, wdec_ref, bdec_ref, gn_ref, ls):
    c = GLA_CHUNK
    nc = ls // c
    ri = lax.broadcasted_iota(jnp.int32, (c, c), 0)
    ci = lax.broadcasted_iota(jnp.int32, (c, c), 1)
    masks = (ri >= ci, ri <= ci)
    tris = tuple(jnp.where(m, 1.0, 0.0).astype(BF16) for m in masks)
    lo = lax.broadcasted_iota(jnp.int32, (c, LANES), 1) < GLA_DK
    acc_ref[0:ls, :] = jnp.zeros((ls, GLA_V_W), F32)

    def body(n, carry):
        for d, cn in ((0, n), (1, nc - 1 - n)):
            r0 = pl.multiple_of(cn * c, c)
            q = src_ref[0, pl.ds(r0, c), 0:GLA_K_W]
            k = src_ref[0, pl.ds(r0, c), GLA_K_W:2 * GLA_K_W]
            v = src_ref[0, pl.ds(r0, c), 2 * GLA_K_W:2 * GLA_K_W + GLA_V_W]
            lr = src_ref[0, pl.ds(r0, c), 2 * GLA_K_W + GLA_V_W:GLA_GROUP_W]
            z = _dot3(lr, wdec_ref[:, d * GLA_K_W:(d + 1) * GLA_K_W]) + bdec_ref[:, d * GLA_K_W:(d + 1) * GLA_K_W]
            la = jax.nn.log_sigmoid(z) * (1.0 / GLA_TAU)
            cum = _dot_exact_lhs(tris[d], la)
            tot = cum[c - 1:c, :] if d == 0 else cum[0:1, :]
            q_dec = q * (GLA_DK ** -0.5) * jnp.exp(cum)
            k_inv = (k * jnp.exp(-cum)).astype(BF16)
            k_end = (k * jnp.exp(tot - cum)).astype(BF16)
            dec = jnp.exp(tot)
            for h in range(GLA_HEADS):
                g, half = h // 2, h % 2
                sl = slice(LANES * g, LANES * (g + 1))
                qd = q_dec[:, sl]
                qd = (jnp.where(lo, qd, 0.0) if half == 0 else jnp.where(lo, 0.0, qd)).astype(BF16)
                vh = v[:, GLA_DV * h:GLA_DV * (h + 1)].astype(BF16)
                att = jnp.where(masks[d], _dot_nt(qd, k_inv[:, sl]), 0.0)
                st = st_ref[d * GLA_HEADS + h]
                o = _dot(att.astype(BF16), vh) + _dot_nt(qd, st.astype(BF16))
                st_ref[d * GLA_HEADS + h] = dec[:, sl] * st + _dot_tn(vh, k_end[:, sl])
                acc_ref[pl.ds(r0, c), GLA_DV * h:GLA_DV * (h + 1)] += o
        return carry

    lax.fori_loop(0, nc, body, 0)

    tr = min(256, ls)

    def fin(t, carry):
        r0 = pl.multiple_of(t * tr, tr)
        for h in range(GLA_HEADS):
            sl = slice(GLA_DV * h, GLA_DV * (h + 1))
            o = acc_ref[pl.ds(r0, tr), sl]
            y = o * lax.rsqrt(jnp.mean(o * o, axis=-1, keepdims=True) + RMS_EPS)
            y = y * gn_ref[...]
            og = og_ref[0, pl.ds(r0, tr), sl].astype(F32)
            out_ref[0, pl.ds(r0, tr), sl] = (y * (og * _sigmoid(og))).astype(BF16)
        return carry

    lax.fori_loop(0, ls // tr, fin, 0)


def _gla_kernel(gl_ref, glc_ref, og_ref, ogc_ref, wdec_ref, bdec_ref, gn_ref, o_ref, oc_ref, acc_ref, st_ref):
    st_ref[...] = jnp.zeros(st_ref.shape, F32)
    _gla_segment(glc_ref, ogc_ref, oc_ref, acc_ref, st_ref, wdec_ref, bdec_ref, gn_ref, glc_ref.shape[1])
    _gla_segment(gl_ref, og_ref, o_ref, acc_ref, st_ref, wdec_ref, bdec_ref, gn_ref, gl_ref.shape[1])


def _gla(gla, og, wdec, bdec, gn, n_lat, lc):
    p, l, _ = gla.shape
    r = l // lc
    ctx_idx = lambda b: (n_lat + b // r, b % r, 0)
    const = lambda a: pl.BlockSpec(a.shape, lambda b: (0,) * a.ndim)
    return pl.pallas_call(
        _gla_kernel,
        grid=(n_lat,),
        in_specs=[pl.BlockSpec((1, l, GLA_GROUP_W), lambda b: (b, 0, 0)),
                  pl.BlockSpec((1, lc, GLA_GROUP_W), ctx_idx),
                  pl.BlockSpec((1, l, GLA_V_W), lambda b: (b, 0, 0)),
                  pl.BlockSpec((1, lc, GLA_V_W), ctx_idx),
                  const(wdec), const(bdec), const(gn)],
        out_specs=[pl.BlockSpec((1, l, GLA_V_W), lambda b: (b, 0, 0)),
                   pl.BlockSpec((1, lc, GLA_V_W), lambda b: (b, 0, 0))],
        out_shape=[jax.ShapeDtypeStruct((n_lat, l, GLA_V_W), BF16),
                   jax.ShapeDtypeStruct((n_lat, lc, GLA_V_W), BF16)],
        scratch_shapes=[pltpu.VMEM((l, GLA_V_W), F32),
                        pltpu.VMEM((2 * GLA_HEADS, GLA_DV, LANES), F32)],
        compiler_params=_cparams("arbitrary"),
        name="gla",
    )(gla, gla, og, og, wdec, bdec, gn)


def _dft_tables(l):
    n = 2 * l
    k = np.arange(l, dtype=np.int64)[:, None]
    t = np.arange(l, dtype=np.int64)[None, :]
    ang = 2.0 * np.pi * ((k * t) % n).astype(np.float64) / n
    fre = np.cos(ang)
    fim = -np.sin(ang)
    fim[0, :] = np.where(np.arange(l) % 2 == 0, 1.0, -1.0)
    fwd = np.concatenate([fre, fim], axis=0)
    return jnp.asarray(fwd, dtype=BF16), jnp.asarray(fwd.T, dtype=BF16)


def _hy_filter_kernel(z_ref, win_ref, w1_ref, b1_ref, w2_ref, b2_ref, w3_ref, b3_ref, fre_ref, fim_ref,
                      hre_ref, him_ref, hf_ref):
    j = pl.program_id(0)
    l = z_ref.shape[0]

    @pl.when(j == 0)
    def _():
        h = jnp.sin(_dot3(z_ref[...], w1_ref[...]) + b1_ref[...])
        h = jnp.sin(_dot3(h, w2_ref[...]) + b2_ref[...])
        h = _dot3(h, w3_ref[...]) + b3_ref[...]
        win = win_ref[...]
        hf = h[:, :HY_WIDTH] * win
        hb = h[:, HY_WIDTH:] * win
        row = lax.broadcasted_iota(jnp.int32, (l, HY_WIDTH), 0)
        hf_ref[:, :HY_WIDTH] = hf.astype(BF16)
        hf_ref[:, HY_WIDTH:] = jnp.where(row == 0, 0.0, hb).astype(BF16)

    tk = fre_ref.shape[0]
    a = _dot(fre_ref[...], hf_ref[...])
    b = _dot(fim_ref[...], hf_ref[...])
    first = (lax.broadcasted_iota(jnp.int32, (tk, HY_WIDTH), 0) == 0) & (j == 0)
    scl = jnp.where(first, 1.0 / (2 * l), 2.0 / (2 * l))
    hre_ref[...] = (a[:, :HY_WIDTH] + a[:, HY_WIDTH:]) * scl
    him_ref[...] = (b[:, :HY_WIDTH] + jnp.where(first, 1.0, -1.0) * b[:, HY_WIDTH:]) * scl


def _hy_filter(l, fw1, fb1, fw2, fb2, fw3, fb3, fwd):
    t = np.arange(l, dtype=np.float32)[:, None] / np.float32(l)
    freqs = np.arange(1, HY_POS_FREQS + 1, dtype=np.float32)
    z = np.concatenate([t, np.cos(2.0 * math.pi * freqs * t), np.sin(2.0 * math.pi * freqs * t)], axis=-1)
    z = np.pad(z, ((0, 0), (0, LANES - z.shape[1])))
    hid = fw2.shape[0]
    fw1 = jnp.pad(fw1, ((0, LANES - fw1.shape[0]), (0, LANES - hid)))
    fb1 = jnp.pad(fb1, (0, LANES - hid))
    fw2 = jnp.pad(fw2, ((0, LANES - hid), (0, LANES - hid)))
    fb2 = jnp.pad(fb2, (0, LANES - hid))
    fw3 = jnp.pad(fw3, ((0, LANES - hid), (0, 0)))
    deltas = np.abs(np.linspace(HY_DECAY_SLOW, HY_DECAY_FAST, HY_WIDTH, dtype=np.float32))
    win = np.exp(-t * deltas).astype(np.float32)
    tk = min(512, l)
    nk = l // tk
    const = lambda a: pl.BlockSpec(a.shape, lambda j: (0,) * a.ndim)
    args = (jnp.asarray(z), jnp.asarray(win), fw1, fb1.reshape(1, -1), fw2, fb2.reshape(1, -1),
            fw3, fb3.reshape(1, -1))
    return pl.pallas_call(
        _hy_filter_kernel,
        grid=(nk,),
        in_specs=[const(a) for a in args] + [pl.BlockSpec((tk, l), lambda j: (j, 0)),
                                              pl.BlockSpec((tk, l), lambda j: (nk + j, 0))],
        out_specs=[pl.BlockSpec((tk, HY_WIDTH), lambda j: (j, 0))] * 2,
        out_shape=[jax.ShapeDtypeStruct((l, HY_WIDTH), F32)] * 2,
        scratch_shapes=[pltpu.VMEM((l, 2 * HY_WIDTH), BF16)],
        compiler_params=_cparams("arbitrary"),
        name="hyena_filter",
    )(*args, fwd, fwd)


def _hy_pre_kernel(p0_ref, p1_ref, p2_ref, w0_ref, w1_ref, w2_ref, b0_ref, b1_ref, b2_ref, skip_ref,
                   u_ref, x0_ref, t_ref):
    ls = p0_ref.shape[1]
    row = lax.broadcasted_iota(jnp.int32, (ls, LANES), 0)

    def conv(p_ref, w_ref, b_ref):
        p = p_ref[0].astype(F32)
        prev = jnp.where(row == 0, 0.0, pltpu.roll(p, 1, 0))
        nxt = jnp.where(row == ls - 1, 0.0, pltpu.roll(p, ls - 1, 0))
        return prev * w_ref[0:1, :] + p * w_ref[1:2, :] + nxt * w_ref[2:3, :] + b_ref[...]

    x0 = conv(p0_ref, w0_ref, b0_ref)
    u = conv(p1_ref, w1_ref, b1_ref) * conv(p2_ref, w2_ref, b2_ref)
    u_ref[0] = u.astype(BF16)
    x0_ref[0] = x0.astype(BF16)
    t_ref[0] = (x0 * u * skip_ref[...]).astype(BF16)


def _hy_pre(hy, conv_w, conv_b, skip, nseq, ls, n_lat):
    p, l, _ = hy.shape
    r = l // ls
    nj = HY_WIDTH // LANES
    base = 0 if ls == l else n_lat
    src = lambda off: pl.BlockSpec((1, ls, LANES), lambda b, j: (base + b // r, b % r, off * nj + j))
    wsp = lambda off: pl.BlockSpec((3, LANES), lambda b, j: (0, off * nj + j))
    bsp = lambda off: pl.BlockSpec((1, LANES), lambda b, j: (0, off * nj + j))
    out = pl.BlockSpec((1, ls, LANES), lambda b, j: (b, 0, j))
    cb = conv_b.reshape(1, -1)
    return pl.pallas_call(
        _hy_pre_kernel,
        grid=(nseq, nj),
        in_specs=[src(0), src(1), src(2), wsp(0), wsp(1), wsp(2), bsp(0), bsp(1), bsp(2), bsp(0)],
        out_specs=[out] * 3,
        out_shape=[jax.ShapeDtypeStruct((nseq, ls, HY_WIDTH), BF16)] * 3,
        compiler_params=_cparams("parallel", "arbitrary"),
        name="hyena_pre",
    )(hy, hy, hy, conv_w, conv_w, conv_w, cb, cb, cb, skip.reshape(1, -1))


def _hy_conv_kernel(u_ref, fre_ref, fim_ref, ire_ref, iim_ref, hre_ref, him_ref, y_ref):
    j = pl.program_id(1)
    u = u_ref[0]
    ure = _dot(fre_ref[...], u)
    uim = _dot(fim_ref[...], u)
    hre, him = hre_ref[...], him_ref[...]
    first = (lax.broadcasted_iota(jnp.int32, ure.shape, 0) == 0) & (j == 0)
    yre = jnp.where(first, ure * hre, ure * hre - uim * him)
    yim = jnp.where(first, uim * him, ure * him + uim * hre)
    y = _dot(ire_ref[...], yre.astype(BF16)) + _dot(iim_ref[...], yim.astype(BF16))

    @pl.when(j == 0)
    def _():
        y_ref[0] = y

    @pl.when(j > 0)
    def _():
        y_ref[0] += y


def _hy_conv(u, fwd, inv, hre, him):
    nseq, ls, _ = u.shape
    tk = min(512, ls)
    nk = ls // tk
    return pl.pallas_call(
        _hy_conv_kernel,
        grid=(nseq, nk),
        in_specs=[pl.BlockSpec((1, ls, HY_WIDTH), lambda b, j: (b, 0, 0)),
                  pl.BlockSpec((tk, ls), lambda b, j: (j, 0)),
                  pl.BlockSpec((tk, ls), lambda b, j: (nk + j, 0)),
                  pl.BlockSpec((ls, tk), lambda b, j: (0, j)),
                  pl.BlockSpec((ls, tk), lambda b, j: (0, nk + j)),
                  pl.BlockSpec((tk, HY_WIDTH), lambda b, j: (j, 0)),
                  pl.BlockSpec((tk, HY_WIDTH), lambda b, j: (j, 0))],
        out_specs=pl.BlockSpec((1, ls, HY_WIDTH), lambda b, j: (b, 0, 0)),
        out_shape=jax.ShapeDtypeStruct((nseq, ls, HY_WIDTH), F32),
        compiler_params=_cparams("parallel", "arbitrary"),
        name="hyena_conv",
    )(u, fwd, fwd, inv, inv, hre, him)


def _merge_kernel(x_ref, oa_ref, ob_ref, y_ref, x0_ref, t_ref, gt_ref, g1_ref,
                  wa_ref, wb_ref, wc_ref, wo_ref, lg_ref, lb_ref, o_ref):
    d = x_ref.shape[2]
    oc = (x0_ref[0].astype(F32) * y_ref[0] + t_ref[0].astype(F32)).astype(BF16)
    gt = gt_ref[0]
    m = (_sigmoid(gt[:, 0:d].astype(F32)) * _dot(oa_ref[0], wa_ref[...])
         + _sigmoid(gt[:, d:2 * d].astype(F32)) * _dot(ob_ref[0], wb_ref[...])
         + _sigmoid(gt[:, 2 * d:3 * d].astype(F32)) * _dot(oc, wc_ref[...]))
    mix = _dot(m.astype(BF16), wo_ref[...])
    o_ref[0] = _ln(DEEPNORM_ALPHA * x_ref[0] + g1_ref[0] * mix) * lg_ref[...] + lb_ref[...]


def _merge(x_all, o_a, o_b, y, x0, t, gates, g1, wa, wb, wc, wo, lg, lb, npb):
    _, l, d = x_all.shape
    tm = min(512, l)
    row = lambda w: pl.BlockSpec((1, tm, w), lambda b, i: (b, i, 0))
    const = lambda a: pl.BlockSpec(a.shape, lambda b, i: (0,) * a.ndim)
    return pl.pallas_call(
        _merge_kernel,
        grid=(npb, l // tm),
        in_specs=[row(d), row(ATT_Q_W), row(GLA_V_W), row(HY_WIDTH), row(HY_WIDTH), row(HY_WIDTH),
                  row(3 * d), pl.BlockSpec((1, 1, d), lambda b, i: (b, 0, 0)),
                  const(wa), const(wb), const(wc), const(wo), const(lg), const(lb)],
        out_specs=row(d),
        out_shape=jax.ShapeDtypeStruct((npb, l, d), F32),
        compiler_params=_cparams("parallel", "arbitrary"),
        name="merge",
    )(x_all, o_a, o_b, y, x0, t, gates, g1, wa, wb, wc, wo, lg, lb)


def _route_kernel(x_ref, sh_ref, sc_ref, rw_ref, rb_ref, u_ref, comb_ref):
    u = _ln(x_ref[0]) * (1.0 + sc_ref[0]) + sh_ref[0]
    u_ref[0] = u.astype(BF16)
    logits = _dot3(u, rw_ref[...]) + rb_ref[...]
    lane = lax.broadcasted_iota(jnp.int32, logits.shape, 1).astype(F32)
    work = logits
    hots, vals = [], []
    for _ in range(TOP_K):
        m = work.max(axis=-1, keepdims=True)
        idx = jnp.where(work == m, lane, float(LANES)).min(axis=-1, keepdims=True)
        hot = lane == idx
        hots.append(hot)
        vals.append(m)
        work = jnp.where(hot, NEG_BIG, work)
    es = [jnp.exp(v - vals[0]) for v in vals]
    den = es[0] + es[1] + es[2] + es[3]
    comb = jnp.zeros_like(logits)
    for hot, e in zip(hots, es):
        comb = comb + jnp.where(hot, e / den, 0.0)
    comb_ref[0] = comb


def _route(x_all, sh, sc, rw, rb, npb):
    _, l, d = x_all.shape
    tm = min(512, l)
    row = lambda w: pl.BlockSpec((1, tm, w), lambda b, i: (b, i, 0))
    vec = pl.BlockSpec((1, 1, d), lambda b, i: (b, 0, 0))
    const = lambda a: pl.BlockSpec(a.shape, lambda b, i: (0,) * a.ndim)
    return pl.pallas_call(
        _route_kernel,
        grid=(npb, l // tm),
        in_specs=[row(d), vec, vec, const(rw), const(rb)],
        out_specs=[row(d), row(LANES)],
        out_shape=[jax.ShapeDtypeStruct((npb, l, d), BF16), jax.ShapeDtypeStruct((npb, l, LANES), F32)],
        compiler_params=_cparams("parallel", "arbitrary"),
        name="route",
    )(x_all, sh, sc, rw, rb)


def _moe_dense_kernel(u_ref, comb_ref, x_ref, g2_ref, w1_ref, b1_ref, w2_ref, b2_ref, lg_ref, lb_ref,
                      o_ref, acc_ref):
    e = pl.program_id(2)

    @pl.when(e == 0)
    def _():
        acc_ref[...] = jnp.zeros(acc_ref.shape, F32)

    h = _dot(u_ref[0], w1_ref[0]) + b1_ref[0]
    g = jnp.minimum(h[:, :EXPERT_HIDDEN], SWIGLU_LIMIT)
    up = jnp.clip(h[:, EXPERT_HIDDEN:], -SWIGLU_LIMIT, SWIGLU_LIMIT)
    act = g * _sigmoid(SWIGLU_ALPHA * g) * (up + 1.0)
    y = _dot(act.astype(BF16), w2_ref[0]) + b2_ref[0]
    comb = comb_ref[0]
    lane = lax.broadcasted_iota(jnp.int32, comb.shape, 1)
    ce = jnp.where(lane == e, comb, 0.0).sum(axis=-1, keepdims=True)
    acc_ref[...] += ce * y

    @pl.when(e == N_EXPERTS - 1)
    def _():
        o_ref[0] = _ln(DEEPNORM_ALPHA * x_ref[0] + g2_ref[0] * acc_ref[...]) * lg_ref[...] + lb_ref[...]


def _moe_dense(u2, comb, x1, g2, w1p, b1p, w2, b2, lg, lb):
    npb, l, d = x1.shape
    tm = min(1024, l)
    row = lambda w: pl.BlockSpec((1, tm, w), lambda b, i, e: (b, i, 0))
    const = lambda a: pl.BlockSpec(a.shape, lambda b, i, e: (0,) * a.ndim)
    return pl.pallas_call(
        _moe_dense_kernel,
        grid=(npb, l // tm, N_EXPERTS),
        in_specs=[row(d), row(LANES), row(d), pl.BlockSpec((1, 1, d), lambda b, i, e: (b, 0, 0)),
                  pl.BlockSpec((1, d, 2 * EXPERT_HIDDEN), lambda b, i, e: (e, 0, 0)),
                  pl.BlockSpec((1, 1, 2 * EXPERT_HIDDEN), lambda b, i, e: (e, 0, 0)),
                  pl.BlockSpec((1, EXPERT_HIDDEN, d), lambda b, i, e: (e, 0, 0)),
                  pl.BlockSpec((1, 1, d), lambda b, i, e: (e, 0, 0)),
                  const(lg), const(lb)],
        out_specs=row(d),
        out_shape=jax.ShapeDtypeStruct((npb, l, d), F32),
        scratch_shapes=[pltpu.VMEM((tm, d), F32)],
        compiler_params=_cparams("parallel", "arbitrary", "arbitrary"),
        name="moe_dense",
    )(u2, comb, x1, g2, w1p, b1p, w2, b2, lg, lb)


def _deinterleave(n):
    return np.concatenate([np.arange(0, n, 2), np.arange(1, n, 2)])


def _rope_tables(l):
    rows = l // GRID_W
    r = np.repeat(np.arange(rows, dtype=np.float32), GRID_W)
    col = np.tile(np.arange(GRID_W, dtype=np.float32), rows)
    axis_dim = ATT_HEAD_DIM // 2
    inv_freq = (ROPE_THETA ** (-np.arange(0, axis_dim, 2, dtype=np.float32) / axis_dim)).astype(np.float32)
    ang = np.concatenate([r[:, None] * inv_freq, col[:, None] * inv_freq], axis=-1)
    c, s = np.cos(ang).astype(np.float32), np.sin(ang).astype(np.float32)
    z = np.zeros_like(s)
    rc = np.tile(np.concatenate([c, c], -1), (1, 2))
    rs1 = np.tile(np.concatenate([-s, z], -1), (1, 2))
    rs2 = np.tile(np.concatenate([z, s], -1), (1, 2))
    ident = (np.ones_like(rc), np.zeros_like(rc), np.zeros_like(rc))
    return tuple(jnp.asarray(np.stack([a, b])) for a, b in zip((rc, rs1, rs2), ident))


def kernel(x, c, ctx, c_ctx, w_mod, b_mod, w_in, att_q_norm, att_k_norm, gla_wdec_f, gla_bdec_f, gla_wdec_b, gla_bdec_b, gla_out_norm, hy_conv_w, hy_conv_b, hy_fw1, hy_fb1, hy_fw2, hy_fb2, hy_fw3, hy_fb3, hy_skip, w_branch_a, w_branch_b, w_branch_c, w_out, ln1_g, ln1_b, router_w, router_b, exp_w1, exp_b1, exp_w2, exp_b2, ln2_g, ln2_b):
    b, l, d = x.shape
    lc = ctx.shape[1]
    assert d == D_MODEL and (b * lc) % l == 0 and l % lc == 0 and lc % GLA_CHUNK == 0
    pc = (b * lc) // l
    p = b + pc

    perm64 = _deinterleave(ATT_HEAD_DIM)
    q_cols = np.concatenate([np.concatenate([64 * j + perm64, 64 * (j + 4) + perm64]) for j in range(4)])
    k_cols = ATT_Q_W + np.concatenate([perm64, 64 + perm64])
    v_cols = ATT_Q_W + ATT_KV_W + np.arange(ATT_KV_W)
    att_cols = np.concatenate([q_cols, k_cols, v_cols])
    o0 = ATT_Q_W + 2 * ATT_KV_W
    gla_cols = o0 + np.concatenate([np.arange(0, 2 * GLA_K_W + GLA_V_W + 2 * GLA_RANK)])
    og0 = o0 + 2 * GLA_K_W + GLA_V_W + 2 * GLA_RANK
    hy0 = og0 + GLA_V_W
    gt0 = hy0 + 3 * HY_WIDTH
    gain_perm = np.concatenate([perm64, perm64])
    bd = jnp.asarray(np.kron(np.eye(2), np.ones((64, 64))), dtype=BF16)
    rope = _rope_tables(l)
    fwd_l, inv_l = _dft_tables(l)
    fwd_c, inv_c = _dft_tables(lc)
    cmap = np.concatenate([np.arange(b), np.full(pc, b)])

    x_all = jnp.concatenate([x, ctx.reshape(pc, l, d)], axis=0)
    cc = jnp.concatenate([c, c_ctx[None], jnp.zeros((7, d), F32)], axis=0)

    for i in range(DEPTH):
        with_ctx = i < DEPTH - 1
        npb = p if with_ctx else b
        mod = _modulation(cc, w_mod[i], b_mod[i])[cmap]
        sh1, sc1, g1, sh2, sc2, g2 = [m[:, None, :] for m in jnp.split(mod, 6, axis=-1)]

        wi = w_in[i]
        watt = wi[:, att_cols].astype(BF16)
        wgla = jnp.pad(wi[:, gla_cols], ((0, 0), (0, GLA_GROUP_W - gla_cols.size))).astype(BF16)
        wog = wi[:, og0:hy0].astype(BF16)
        why = wi[:, hy0:gt0].astype(BF16)
        wgt = wi[:, gt0:].astype(BF16)
        qg = att_q_norm[i][gain_perm].reshape(1, LANES)
        kg = att_k_norm[i][gain_perm].reshape(1, LANES)
        qn, kn, v, gla, og, hy, gates = _inproj(x_all, sh1, sc1, (watt, wgla, wog, why, wgt), qg, kg, rope, bd, b)

        o_a = _attention(qn, kn, v, b, lc, with_ctx)

        wdec = jnp.zeros((LANES, 2 * GLA_K_W), F32)
        wdec = wdec.at[0:GLA_RANK, 0:GLA_K_W].set(gla_wdec_f[i])
        wdec = wdec.at[GLA_RANK:2 * GLA_RANK, GLA_K_W:].set(gla_wdec_b[i])
        bdec = jnp.concatenate([gla_bdec_f[i], gla_bdec_b[i]]).reshape(1, -1)
        o_b, o_b_ctx = _gla(gla, og, wdec, bdec, gla_out_norm[i].reshape(1, -1), b, lc)

        hre, him = _hy_filter(l, hy_fw1[i], hy_fb1[i], hy_fw2[i], hy_fb2[i], hy_fw3[i], hy_fb3[i], fwd_l)
        u_h, x0_h, t_h = _hy_pre(hy, hy_conv_w[i], hy_conv_b[i], hy_skip[i], b, l, b)
        y_h = _hy_conv(u_h, fwd_l, inv_l, hre, him)
        if with_ctx:
            hre_c, him_c = _hy_filter(lc, hy_fw1[i], hy_fb1[i], hy_fw2[i], hy_fb2[i], hy_fw3[i], hy_fb3[i], fwd_c)
            u_c, x0_c, t_c = _hy_pre(hy, hy_conv_w[i], hy_conv_b[i], hy_skip[i], b, lc, b)
            y_c = _hy_conv(u_c, fwd_c, inv_c, hre_c, him_c)
            cat = lambda a_lat, a_ctx: jnp.concatenate([a_lat, a_ctx.reshape(pc, l, a_ctx.shape[-1])], axis=0)
            o_b, y_h, x0_h, t_h = cat(o_b, o_b_ctx), cat(y_h, y_c), cat(x0_h, x0_c), cat(t_h, t_c)

        wa_rows = np.concatenate([np.concatenate([64 * j + np.arange(64), 64 * (j + 4) + np.arange(64)])
                                  for j in range(4)])
        x1 = _merge(x_all, o_a, o_b, y_h, x0_h, t_h, gates, g1,
                    w_branch_a[i][wa_rows].astype(BF16), w_branch_b[i].astype(BF16),
                    w_branch_c[i].astype(BF16), w_out[i].astype(BF16),
                    ln1_g[i].reshape(1, d), ln1_b[i].reshape(1, d), npb)

        rw = jnp.pad(router_w[i], ((0, 0), (0, LANES - N_EXPERTS)))
        rb = jnp.concatenate([router_b[i], jnp.full((LANES - N_EXPERTS,), NEG_BIG, F32)]).reshape(1, LANES)
        u2, comb = _route(x1, sh2, sc2, rw, rb, npb)

        w1p = jnp.concatenate([exp_w1[i][..., 0::2], exp_w1[i][..., 1::2]], axis=-1).astype(BF16)
        b1p = jnp.concatenate([exp_b1[i][..., 0::2], exp_b1[i][..., 1::2]], axis=-1)[:, None, :]
        x_all = _moe_dense(u2, comb, x1, g2, w1p, b1p, exp_w2[i].astype(BF16), exp_b2[i][:, None, :],
                           ln2_g[i].reshape(1, d), ln2_b[i].reshape(1, d))
    return x_all
```

```python
import functools
import math

import numpy as np
import jax
import jax.numpy as jnp
from jax import lax
from jax.experimental import pallas as pl
from jax.experimental.pallas import tpu as pltpu

F32 = jnp.float32
BF16 = jnp.bfloat16

D_MODEL = 1024
DEPTH = 2
GRID_W = 64
ATT_HEADS = 8
ATT_KV_HEADS = 2
ATT_HEAD_DIM = 64
ROPE_THETA = 10000.0
GLA_HEADS = 4
GLA_DK = 64
GLA_DV = 128
GLA_RANK = 16
GLA_TAU = 16.0
GLA_CHUNK = 64
HY_WIDTH = 512
HY_POS_FREQS = 16
HY_DECAY_SLOW = math.log(1e-2) / 1.5
HY_DECAY_FAST = math.log(1e-2) / 0.3
N_EXPERTS = 32
TOP_K = 4
EXPERT_HIDDEN = D_MODEL
SWIGLU_LIMIT = 7.0
SWIGLU_ALPHA = 1.702
DEEPNORM_ALPHA = (2 * DEPTH) ** 0.25
LN_EPS = 1e-5
RMS_EPS = 1e-6

ATT_Q_W = ATT_HEADS * ATT_HEAD_DIM
ATT_KV_W = ATT_KV_HEADS * ATT_HEAD_DIM
GLA_K_W = GLA_HEADS * GLA_DK
GLA_V_W = GLA_HEADS * GLA_DV
GLA_GROUP_W = 1152
LANES = 128
NEG_BIG = -3.0e38

VMEM_LIMIT = 56 * 1024 * 1024


def _cparams(*sem):
    return pltpu.CompilerParams(dimension_semantics=sem, vmem_limit_bytes=VMEM_LIMIT)


def _dot(a, b):
    return jnp.dot(a, b, preferred_element_type=F32)


def _dot_nt(a, b):
    return lax.dot_general(a, b, (((1,), (1,)), ((), ())), preferred_element_type=F32)


def _dot_tn(a, b):
    return lax.dot_general(a, b, (((0,), (0,)), ((), ())), preferred_element_type=F32)


def _split2(x):
    hi = x.astype(BF16)
    lo = (x - hi.astype(F32)).astype(BF16)
    return hi, lo


def _split3(x):
    hi = x.astype(BF16)
    r = x - hi.astype(F32)
    mid = r.astype(BF16)
    lo = (r - mid.astype(F32)).astype(BF16)
    return hi, mid, lo


def _dot3(a, b):
    ah, al = _split2(a)
    bh, bl = _split2(b)
    return _dot(ah, bh) + (_dot(ah, bl) + _dot(al, bh))


def _dot_exact_lhs(a_bf16, b):
    b0, b1, b2 = _split3(b)
    return _dot(a_bf16, b0) + (_dot(a_bf16, b1) + _dot(a_bf16, b2))


def _ln(x):
    mu = jnp.mean(x, axis=-1, keepdims=True)
    xc = x - mu
    var = jnp.mean(xc * xc, axis=-1, keepdims=True)
    return xc * lax.rsqrt(var + LN_EPS)


def _sigmoid(x):
    return 1.0 / (1.0 + jnp.exp(-x))


def _mod_kernel(c_ref, w_ref, b_ref, o_ref):
    c = c_ref[...]
    o_ref[...] = _dot3(c * _sigmoid(c), w_ref[...]) + b_ref[...]


def _modulation(cc, w_mod, b_mod):
    rows, d = cc.shape
    n = w_mod.shape[1]
    tn = 1024
    return pl.pallas_call(
        _mod_kernel,
        grid=(n // tn,),
        in_specs=[pl.BlockSpec((rows, d), lambda j: (0, 0)),
                  pl.BlockSpec((d, tn), lambda j: (0, j)),
                  pl.BlockSpec((1, tn), lambda j: (0, j))],
        out_specs=pl.BlockSpec((rows, tn), lambda j: (0, j)),
        out_shape=jax.ShapeDtypeStruct((rows, n), F32),
        compiler_params=_cparams("arbitrary"),
        name="modulation",
    )(cc, w_mod, b_mod.reshape(1, n))


def _head_norm_rope(t, gain, rc, rs1, rs2, bd):
    sh, sl = _split2(t * t)
    ss = _dot(sh, bd) + _dot(sl, bd)
    tn = t * lax.rsqrt(ss * (1.0 / ATT_HEAD_DIM) + RMS_EPS) * gain
    return tn * rc + pltpu.roll(tn, 96, 1) * rs1 + pltpu.roll(tn, 32, 1) * rs2


def _inproj_kernel(x_ref, sh_ref, sc_ref, watt_ref, wgla_ref, wog_ref, why_ref, wgt_ref,
                   qg_ref, kg_ref, rc_ref, rs1_ref, rs2_ref, bd_ref,
                   qn_ref, kn_ref, v_ref, gla_ref, og_ref, hy_ref, gt_ref):
    u = _ln(x_ref[0]) * (1.0 + sc_ref[0]) + sh_ref[0]
    ub = u.astype(BF16)
    rc, rs1, rs2, bd = rc_ref[0], rs1_ref[0], rs2_ref[0], bd_ref[...]
    att = _dot(ub, watt_ref[...])
    for j in range(ATT_Q_W // LANES):
        t = att[:, LANES * j:LANES * (j + 1)]
        qn = _head_norm_rope(t, qg_ref[...], rc, rs1, rs2, bd) * (ATT_HEAD_DIM ** -0.5)
        qn_ref[0, :, LANES * j:LANES * (j + 1)] = qn.astype(BF16)
    kn = _head_norm_rope(att[:, ATT_Q_W:ATT_Q_W + LANES], kg_ref[...], rc, rs1, rs2, bd)
    kn_ref[0] = kn.astype(BF16)
    v_ref[0] = att[:, ATT_Q_W + LANES:].astype(BF16)
    gla_ref[0] = _dot(ub, wgla_ref[...])
    og_ref[0] = _dot(ub, wog_ref[...]).astype(BF16)
    hy_ref[0] = _dot(ub, why_ref[...]).astype(BF16)
    gt_ref[0] = _dot(ub, wgt_ref[...]).astype(BF16)


def _inproj(x_all, sh, sc, wts, qg, kg, rope, bd, n_lat):
    p, l, d = x_all.shape
    tm = min(256, l)
    watt, wgla, wog, why, wgt = wts
    rc, rs1, rs2 = rope
    const = lambda a: pl.BlockSpec(a.shape, lambda b, i: (0,) * a.ndim)
    row = lambda w: pl.BlockSpec((1, tm, w), lambda b, i: (b, i, 0))
    tab = pl.BlockSpec((1, tm, LANES), lambda b, i: (jnp.where(b >= n_lat, 1, 0), i, 0))
    vec = pl.BlockSpec((1, 1, d), lambda b, i: (b, 0, 0))
    widths = (ATT_Q_W, LANES, LANES, GLA_GROUP_W, GLA_V_W, 3 * HY_WIDTH, 3 * d)
    dtypes = (BF16, BF16, BF16, F32, BF16, BF16, BF16)
    return pl.pallas_call(
        _inproj_kernel,
        grid=(p, l // tm),
        in_specs=[row(d), vec, vec, const(watt), const(wgla), const(wog), const(why), const(wgt),
                  const(qg), const(kg), tab, tab, tab, const(bd)],
        out_specs=[row(w) for w in widths],
        out_shape=[jax.ShapeDtypeStruct((p, l, w), dt) for w, dt in zip(widths, dtypes)],
        compiler_params=_cparams("parallel", "arbitrary"),
        name="inproj",
    )(x_all, sh, sc, watt, wgla, wog, why, wgt, qg, kg, rc, rs1, rs2, bd)


def _attend_group(qg, ks, vs, lo):
    zero = jnp.zeros_like(qg)
    outs = []
    for qh in (jnp.where(lo, qg, zero), jnp.where(lo, zero, qg)):
        ss = [_dot_nt(qh, k) for k in ks]
        m = ss[0].max(axis=-1, keepdims=True)
        for s in ss[1:]:
            m = jnp.maximum(m, s.max(axis=-1, keepdims=True))
        den = 0.0
        acc = 0.0
        for s, v in zip(ss, vs):
            p = jnp.exp(s - m)
            den = den + p.sum(axis=-1, keepdims=True)
            acc = acc + _dot(p.astype(BF16), v)
        outs.append(acc / den)
    return jnp.where(lo, outs[0], outs[1])


def _attn_kernel(q_ref, k_ref, v_ref, kc_ref, vc_ref, o_ref, *, n_lat_tiles, with_ctx):
    i = pl.program_id(1)
    tq = q_ref.shape[1]
    lo = lax.broadcasted_iota(jnp.int32, (tq, LANES), 1) < ATT_HEAD_DIM

    def run(ks, vs):
        for j in range(ATT_Q_W // LANES):
            o = _attend_group(q_ref[0, :, LANES * j:LANES * (j + 1)], ks, vs, lo)
            o_ref[0, :, LANES * j:LANES * (j + 1)] = o.astype(BF16)

    if with_ctx:
        @pl.when(i < n_lat_tiles)
        def _():
            run([k_ref[0], kc_ref[0]], [v_ref[0], vc_ref[0]])

        @pl.when(i == n_lat_tiles)
        def _():
            run([kc_ref[0]], [vc_ref[0]])
    else:
        run([k_ref[0], kc_ref[0]], [v_ref[0], vc_ref[0]])


def _attention(qn, kn, v, n_lat, lc, with_ctx):
    p, l, _ = qn.shape
    tq = lc
    r = l // lc
    nt = l // tq
    ctx_idx = lambda b: (n_lat + b // r, b % r, 0)
    if with_ctx:
        qmap = lambda b, i: (jnp.where(i < nt, b, n_lat + b // r), jnp.where(i < nt, i, b % r), 0)
    else:
        qmap = lambda b, i: (b, i, 0)
    return pl.pallas_call(
        functools.partial(_attn_kernel, n_lat_tiles=nt, with_ctx=with_ctx),
        grid=(n_lat, nt + (1 if with_ctx else 0)),
        in_specs=[pl.BlockSpec((1, tq, ATT_Q_W), qmap),
                  pl.BlockSpec((1, l, LANES), lambda b, i: (b, 0, 0)),
                  pl.BlockSpec((1, l, LANES), lambda b, i: (b, 0, 0)),
                  pl.BlockSpec((1, lc, LANES), lambda b, i: ctx_idx(b)),
                  pl.BlockSpec((1, lc, LANES), lambda b, i: ctx_idx(b))],
        out_specs=pl.BlockSpec((1, tq, ATT_Q_W), qmap),
        out_shape=jax.ShapeDtypeStruct((p if with_ctx else n_lat, l, ATT_Q_W), BF16),
        compiler_params=_cparams("parallel", "arbitrary"),
        name="attention",
    )(qn, kn, v, kn, v)


def _gla_segment(src_ref, og_ref, out_ref, acc_ref, st_ref, wdec_ref, bdec_ref, gn_ref, ls):
    c = GLA_CHUNK
    nc = ls // c
    ri = lax.broadcasted_iota(jnp.int32, (c, c), 0)
    ci = lax.broadcasted_iota(jnp.int32, (c, c), 1)
    masks = (ri >= ci, ri <= ci)
    tris = tuple(jnp.where(m, 1.0, 0.0).astype(BF16) for m in masks)
    lo = lax.broadcasted_iota(jnp.int32, (c, LANES), 1) < GLA_DK
    acc_ref[0:ls, :] = jnp.zeros((ls, GLA_V_W), F32)

    def body(n, carry):
        for d, cn in ((0, n), (1, nc - 1 - n)):
            r0 = pl.multiple_of(cn * c, c)
            q = src_ref[0, pl.ds(r0, c), 0:GLA_K_W]
            k = src_ref[0, pl.ds(r0, c), GLA_K_W:2 * GLA_K_W]
            v = src_ref[0, pl.ds(r0, c), 2 * GLA_K_W:2 * GLA_K_W + GLA_V_W]
            lr = src_ref[0, pl.ds(r0, c), 2 * GLA_K_W + GLA_V_W:GLA_GROUP_W]
            z = _dot3(lr, wdec_ref[:, d * GLA_K_W:(d + 1) * GLA_K_W]) + bdec_ref[:, d * GLA_K_W:(d + 1) * GLA_K_W]
            la = jax.nn.log_sigmoid(z) * (1.0 / GLA_TAU)
            cum = _dot_exact_lhs(tris[d], la)
            tot = cum[c - 1:c, :] if d == 0 else cum[0:1, :]
            q_dec = q * (GLA_DK ** -0.5) * jnp.exp(cum)
            k_inv = (k * jnp.exp(-cum)).astype(BF16)
            k_end = (k * jnp.exp(tot - cum)).astype(BF16)
            dec = jnp.exp(tot)
            for h in range(GLA_HEADS):
                g, half = h // 2, h % 2
                sl = slice(LANES * g, LANES * (g + 1))
                qd = q_dec[:, sl]
                qd = (jnp.where(lo, qd, 0.0) if half == 0 else jnp.where(lo, 0.0, qd)).astype(BF16)
                vh = v[:, GLA_DV * h:GLA_DV * (h + 1)].astype(BF16)
                att = jnp.where(masks[d], _dot_nt(qd, k_inv[:, sl]), 0.0)
                st = st_ref[d * GLA_HEADS + h]
                o = _dot(att.astype(BF16), vh) + _dot_nt(qd, st.astype(BF16))
                st_ref[d * GLA_HEADS + h] = dec[:, sl] * st + _dot_tn(vh, k_end[:, sl])
                acc_ref[pl.ds(r0, c), GLA_DV * h:GLA_DV * (h + 1)] += o
        return carry

    lax.fori_loop(0, nc, body, 0)

    tr = min(256, ls)

    def fin(t, carry):
        r0 = pl.multiple_of(t * tr, tr)
        for h in range(GLA_HEADS):
            sl = slice(GLA_DV * h, GLA_DV * (h + 1))
            o = acc_ref[pl.ds(r0, tr), sl]
            y = o * lax.rsqrt(jnp.mean(o * o, axis=-1, keepdims=True) + RMS_EPS)
            y = y * gn_ref[...]
            og = og_ref[0, pl.ds(r0, tr), sl].astype(F32)
            out_ref[0, pl.ds(r0, tr), sl] = (y * (og * _sigmoid(og))).astype(BF16)
        return carry

    lax.fori_loop(0, ls // tr, fin, 0)


def _gla_kernel(gl_ref, glc_ref, og_ref, ogc_ref, wdec_ref, bdec_ref, gn_ref, o_ref, oc_ref, acc_ref, st_ref):
    st_ref[...] = jnp.zeros(st_ref.shape, F32)
    _gla_segment(glc_ref, ogc_ref, oc_ref, acc_ref, st_ref, wdec_ref, bdec_ref, gn_ref, glc_ref.shape[1])
    _gla_segment(gl_ref, og_ref, o_ref, acc_ref, st_ref, wdec_ref, bdec_ref, gn_ref, gl_ref.shape[1])


def _gla(gla, og, wdec, bdec, gn, n_lat, lc):
    p, l, _ = gla.shape
    r = l // lc
    ctx_idx = lambda b: (n_lat + b // r, b % r, 0)
    const = lambda a: pl.BlockSpec(a.shape, lambda b: (0,) * a.ndim)
    return pl.pallas_call(
        _gla_kernel,
        grid=(n_lat,),
        in_specs=[pl.BlockSpec((1, l, GLA_GROUP_W), lambda b: (b, 0, 0)),
                  pl.BlockSpec((1, lc, GLA_GROUP_W), ctx_idx),
                  pl.BlockSpec((1, l, GLA_V_W), lambda b: (b, 0, 0)),
                  pl.BlockSpec((1, lc, GLA_V_W), ctx_idx),
                  const(wdec), const(bdec), const(gn)],
        out_specs=[pl.BlockSpec((1, l, GLA_V_W), lambda b: (b, 0, 0)),
                   pl.BlockSpec((1, lc, GLA_V_W), lambda b: (b, 0, 0))],
        out_shape=[jax.ShapeDtypeStruct((n_lat, l, GLA_V_W), BF16),
                   jax.ShapeDtypeStruct((n_lat, lc, GLA_V_W), BF16)],
        scratch_shapes=[pltpu.VMEM((l, GLA_V_W), F32),
                        pltpu.VMEM((2 * GLA_HEADS, GLA_DV, LANES), F32)],
        compiler_params=_cparams("arbitrary"),
        name="gla",
    )(gla, gla, og, og, wdec, bdec, gn)


def _dft_tables(l):
    n = 2 * l
    k = np.arange(l, dtype=np.int64)[:, None]
    t = np.arange(l, dtype=np.int64)[None, :]
    ang = 2.0 * np.pi * ((k * t) % n).astype(np.float64) / n
    fre = np.cos(ang)
    fim = -np.sin(ang)
    fim[0, :] = np.where(np.arange(l) % 2 == 0, 1.0, -1.0)
    fwd = np.concatenate([fre, fim], axis=0)
    return jnp.asarray(fwd, dtype=BF16), jnp.asarray(fwd.T, dtype=BF16)


def _hy_filter_kernel(z_ref, win_ref, w1_ref, b1_ref, w2_ref, b2_ref, w3_ref, b3_ref, fre_ref, fim_ref,
                      hre_ref, him_ref, hf_ref):
    j = pl.program_id(0)
    l = z_ref.shape[0]

    @pl.when(j == 0)
    def _():
        h = jnp.sin(_dot3(z_ref[...], w1_ref[...]) + b1_ref[...])
        h = jnp.sin(_dot3(h, w2_ref[...]) + b2_ref[...])
        h = _dot3(h, w3_ref[...]) + b3_ref[...]
        win = win_ref[...]
        hf = h[:, :HY_WIDTH] * win
        hb = h[:, HY_WIDTH:] * win
        row = lax.broadcasted_iota(jnp.int32, (l, HY_WIDTH), 0)
        hf_ref[:, :HY_WIDTH] = hf.astype(BF16)
        hf_ref[:, HY_WIDTH:] = jnp.where(row == 0, 0.0, hb).astype(BF16)

    tk = fre_ref.shape[0]
    a = _dot(fre_ref[...], hf_ref[...])
    b = _dot(fim_ref[...], hf_ref[...])
    first = (lax.broadcasted_iota(jnp.int32, (tk, HY_WIDTH), 0) == 0) & (j == 0)
    scl = jnp.where(first, 1.0 / (2 * l), 2.0 / (2 * l))
    hre_ref[...] = (a[:, :HY_WIDTH] + a[:, HY_WIDTH:]) * scl
    him_ref[...] = (b[:, :HY_WIDTH] + jnp.where(first, 1.0, -1.0) * b[:, HY_WIDTH:]) * scl


def _hy_filter(l, fw1, fb1, fw2, fb2, fw3, fb3, fwd):
    t = np.arange(l, dtype=np.float32)[:, None] / np.float32(l)
    freqs = np.arange(1, HY_POS_FREQS + 1, dtype=np.float32)
    z = np.concatenate([t, np.cos(2.0 * math.pi * freqs * t), np.sin(2.0 * math.pi * freqs * t)], axis=-1)
    z = np.pad(z, ((0, 0), (0, LANES - z.shape[1])))
    hid = fw2.shape[0]
    fw1 = jnp.pad(fw1, ((0, LANES - fw1.shape[0]), (0, LANES - hid)))
    fb1 = jnp.pad(fb1, (0, LANES - hid))
    fw2 = jnp.pad(fw2, ((0, LANES - hid), (0, LANES - hid)))
    fb2 = jnp.pad(fb2, (0, LANES - hid))
    fw3 = jnp.pad(fw3, ((0, LANES - hid), (0, 0)))
    deltas = np.abs(np.linspace(HY_DECAY_SLOW, HY_DECAY_FAST, HY_WIDTH, dtype=np.float32))
    win = np.exp(-t * deltas).astype(np.float32)
    tk = min(512, l)
    nk = l // tk
    const = lambda a: pl.BlockSpec(a.shape, lambda j: (0,) * a.ndim)
    args = (jnp.asarray(z), jnp.asarray(win), fw1, fb1.reshape(1, -1), fw2, fb2.reshape(1, -1),
            fw3, fb3.reshape(1, -1))
    return pl.pallas_call(
        _hy_filter_kernel,
        grid=(nk,),
        in_specs=[const(a) for a in args] + [pl.BlockSpec((tk, l), lambda j: (j, 0)),
                                              pl.BlockSpec((tk, l), lambda j: (nk + j, 0))],
        out_specs=[pl.BlockSpec((tk, HY_WIDTH), lambda j: (j, 0))] * 2,
        out_shape=[jax.ShapeDtypeStruct((l, HY_WIDTH), F32)] * 2,
        scratch_shapes=[pltpu.VMEM((l, 2 * HY_WIDTH), BF16)],
        compiler_params=_cparams("arbitrary"),
        name="hyena_filter",
    )(*args, fwd, fwd)


def _hy_pre_kernel(p0_ref, p1_ref, p2_ref, w0_ref, w1_ref, w2_ref, b0_ref, b1_ref, b2_ref, skip_ref,
                   u_ref, x0_ref, t_ref):
    ls = p0_ref.shape[1]
    row = lax.broadcasted_iota(jnp.int32, (ls, LANES), 0)

    def conv(p_ref, w_ref, b_ref):
        p = p_ref[0].astype(F32)
        prev = jnp.where(row == 0, 0.0, pltpu.roll(p, 1, 0))
        nxt = jnp.where(row == ls - 1, 0.0, pltpu.roll(p, ls - 1, 0))
        return prev * w_ref[0:1, :] + p * w_ref[1:2, :] + nxt * w_ref[2:3, :] + b_ref[...]

    x0 = conv(p0_ref, w0_ref, b0_ref)
    u = conv(p1_ref, w1_ref, b1_ref) * conv(p2_ref, w2_ref, b2_ref)
    u_ref[0] = u.astype(BF16)
    x0_ref[0] = x0.astype(BF16)
    t_ref[0] = (x0 * u * skip_ref[...]).astype(BF16)


def _hy_pre(hy, conv_w, conv_b, skip, nseq, ls, n_lat):
    p, l, _ = hy.shape
    r = l // ls
    nj = HY_WIDTH // LANES
    base = 0 if ls == l else n_lat
    src = lambda off: pl.BlockSpec((1, ls, LANES), lambda b, j: (base + b // r, b % r, off * nj + j))
    wsp = lambda off: pl.BlockSpec((3, LANES), lambda b, j: (0, off * nj + j))
    bsp = lambda off: pl.BlockSpec((1, LANES), lambda b, j: (0, off * nj + j))
    out = pl.BlockSpec((1, ls, LANES), lambda b, j: (b, 0, j))
    cb = conv_b.reshape(1, -1)
    return pl.pallas_call(
        _hy_pre_kernel,
        grid=(nseq, nj),
        in_specs=[src(0), src(1), src(2), wsp(0), wsp(1), wsp(2), bsp(0), bsp(1), bsp(2), bsp(0)],
        out_specs=[out] * 3,
        out_shape=[jax.ShapeDtypeStruct((nseq, ls, HY_WIDTH), BF16)] * 3,
        compiler_params=_cparams("parallel", "arbitrary"),
        name="hyena_pre",
    )(hy, hy, hy, conv_w, conv_w, conv_w, cb, cb, cb, skip.reshape(1, -1))


def _hy_conv_kernel(u_ref, fre_ref, fim_ref, ire_ref, iim_ref, hre_ref, him_ref, y_ref):
    j = pl.program_id(1)
    u = u_ref[0]
    ure = _dot(fre_ref[...], u)
    uim = _dot(fim_ref[...], u)
    hre, him = hre_ref[...], him_ref[...]
    first = (lax.broadcasted_iota(jnp.int32, ure.shape, 0) == 0) & (j == 0)
    yre = jnp.where(first, ure * hre, ure * hre - uim * him)
    yim = jnp.where(first, uim * him, ure * him + uim * hre)
    y = _dot(ire_ref[...], yre.astype(BF16)) + _dot(iim_ref[...], yim.astype(BF16))

    @pl.when(j == 0)
    def _():
        y_ref[0] = y

    @pl.when(j > 0)
    def _():
        y_ref[0] += y


def _hy_conv(u, fwd, inv, hre, him):
    nseq, ls, _ = u.shape
    tk = min(512, ls)
    nk = ls // tk
    return pl.pallas_call(
        _hy_conv_kernel,
        grid=(nseq, nk),
        in_specs=[pl.BlockSpec((1, ls, HY_WIDTH), lambda b, j: (b, 0, 0)),
                  pl.BlockSpec((tk, ls), lambda b, j: (j, 0)),
                  pl.BlockSpec((tk, ls), lambda b, j: (nk + j, 0)),
                  pl.BlockSpec((ls, tk), lambda b, j: (0, j)),
                  pl.BlockSpec((ls, tk), lambda b, j: (0, nk + j)),
                  pl.BlockSpec((tk, HY_WIDTH), lambda b, j: (j, 0)),
                  pl.BlockSpec((tk, HY_WIDTH), lambda b, j: (j, 0))],
        out_specs=pl.BlockSpec((1, ls, HY_WIDTH), lambda b, j: (b, 0, 0)),
        out_shape=jax.ShapeDtypeStruct((nseq, ls, HY_WIDTH), F32),
        compiler_params=_cparams("parallel", "arbitrary"),
        name="hyena_conv",
    )(u, fwd, fwd, inv, inv, hre, him)


def _merge_kernel(x_ref, oa_ref, ob_ref, y_ref, x0_ref, t_ref, gt_ref, g1_ref,
                  wa_ref, wb_ref, wc_ref, wo_ref, lg_ref, lb_ref, o_ref):
    d = x_ref.shape[2]
    oc = (x0_ref[0].astype(F32) * y_ref[0] + t_ref[0].astype(F32)).astype(BF16)
    gt = gt_ref[0]
    m = (_sigmoid(gt[:, 0:d].astype(F32)) * _dot(oa_ref[0], wa_ref[...])
         + _sigmoid(gt[:, d:2 * d].astype(F32)) * _dot(ob_ref[0], wb_ref[...])
         + _sigmoid(gt[:, 2 * d:3 * d].astype(F32)) * _dot(oc, wc_ref[...]))
    mix = _dot(m.astype(BF16), wo_ref[...])
    o_ref[0] = _ln(DEEPNORM_ALPHA * x_ref[0] + g1_ref[0] * mix) * lg_ref[...] + lb_ref[...]


def _merge(x_all, o_a, o_b, y, x0, t, gates, g1, wa, wb, wc, wo, lg, lb, npb):
    _, l, d = x_all.shape
    tm = min(512, l)
    row = lambda w: pl.BlockSpec((1, tm, w), lambda b, i: (b, i, 0))
    const = lambda a: pl.BlockSpec(a.shape, lambda b, i: (0,) * a.ndim)
    return pl.pallas_call(
        _merge_kernel,
        grid=(npb, l // tm),
        in_specs=[row(d), row(ATT_Q_W), row(GLA_V_W), row(HY_WIDTH), row(HY_WIDTH), row(HY_WIDTH),
                  row(3 * d), pl.BlockSpec((1, 1, d), lambda b, i: (b, 0, 0)),
                  const(wa), const(wb), const(wc), const(wo), const(lg), const(lb)],
        out_specs=row(d),
        out_shape=jax.ShapeDtypeStruct((npb, l, d), F32),
        compiler_params=_cparams("parallel", "arbitrary"),
        name="merge",
    )(x_all, o_a, o_b, y, x0, t, gates, g1, wa, wb, wc, wo, lg, lb)


MOE_T = 256
MOE_G = 16
MOE_TM = 512
MOE_SLOTS = TOP_K * MOE_T + 512
AUG_W = D_MODEL + LANES


def _route_kernel(x_ref, sh_ref, sc_ref, rw_ref, rb_ref, xa_ref, sel_ref, cnt_ref):
    d = x_ref.shape[2]
    u = _ln(x_ref[0]) * (1.0 + sc_ref[0]) + sh_ref[0]
    xa_ref[0, :, 0:d] = u.astype(BF16)
    logits = _dot3(u, rw_ref[...]) + rb_ref[...]
    lane = lax.broadcasted_iota(jnp.int32, logits.shape, 1).astype(F32)
    work = logits
    hots, vals = [], []
    for _ in range(TOP_K):
        m = work.max(axis=-1, keepdims=True)
        idx = jnp.where(work == m, lane, float(LANES)).min(axis=-1, keepdims=True)
        hot = lane == idx
        hots.append(hot)
        vals.append(m)
        work = jnp.where(hot, NEG_BIG, work)
    es = [jnp.exp(v - vals[0]) for v in vals]
    den = es[0] + es[1] + es[2] + es[3]
    comb = jnp.zeros_like(logits)
    sel = jnp.zeros_like(logits)
    for hot, e in zip(hots, es):
        comb = comb + jnp.where(hot, e / den, 0.0)
        sel = sel + jnp.where(hot, 1.0, 0.0)
    c0, c1, c2 = _split3(comb)
    aug = c0.astype(F32) + pltpu.roll(c1.astype(F32), 32, 1) + pltpu.roll(c2.astype(F32), 64, 1)
    xa_ref[0, :, d:d + LANES] = aug.astype(BF16)
    sel_ref[0] = sel.astype(BF16)
    cnt_ref[0] = sel.sum(axis=0, keepdims=True)


def _route(x_all, sh, sc, rw, rb, npb):
    _, l, d = x_all.shape
    t = MOE_T
    nt_b = l // t
    row = lambda w: pl.BlockSpec((1, t, w), lambda b, i: (b, i, 0))
    vec = pl.BlockSpec((1, 1, d), lambda b, i: (b, 0, 0))
    const = lambda a: pl.BlockSpec(a.shape, lambda b, i: (0,) * a.ndim)
    return pl.pallas_call(
        _route_kernel,
        grid=(npb, nt_b),
        in_specs=[row(d), vec, vec, const(rw), const(rb)],
        out_specs=[row(AUG_W), row(LANES), pl.BlockSpec((1, 1, LANES), lambda b, i: (b * nt_b + i, 0, 0))],
        out_shape=[jax.ShapeDtypeStruct((npb, l, AUG_W), BF16), jax.ShapeDtypeStruct((npb, l, LANES), BF16),
                   jax.ShapeDtypeStruct((npb * nt_b, 1, LANES), F32)],
        compiler_params=_cparams("parallel", "arbitrary"),
        name="route",
    )(x_all, sh, sc, rw, rb)


def _group_copies(j, cntp_ref, loc_ref, goff_ref, make_copy):
    def per_expert(e, total):
        n = cntp_ref[j * N_EXPERTS + e] // MOE_G
        loc = loc_ref[j * N_EXPERTS + e]
        off = goff_ref[j * N_EXPERTS + e]

        def per_chunk(c, carry):
            make_copy(pl.multiple_of(loc + c * MOE_G, MOE_G), pl.multiple_of(off + c * MOE_G, MOE_G)).start()
            return carry

        lax.fori_loop(0, n, per_chunk, 0)
        return total + n

    return lax.fori_loop(0, N_EXPERTS, per_expert, 0)


def _dispatch_kernel(cntp_ref, loc_ref, goff_ref, tail_ref, tailn_ref,
                     xa_ref, sel_ref, locv_ref, pt_ref, xs_ref, stage_ref, zero_ref, sem):
    j = pl.program_id(0)
    t = MOE_T
    sel = sel_ref[0]
    ri = lax.broadcasted_iota(jnp.int32, (t, t), 0)
    ci = lax.broadcasted_iota(jnp.int32, (t, t), 1)
    rank = _dot(jnp.where(ci < ri, 1.0, 0.0).astype(BF16), sel)
    dest = rank + locv_ref[0]
    lane = lax.broadcasted_iota(jnp.int32, (t, LANES), 1).astype(F32)
    slot = lax.broadcasted_iota(jnp.int32, (t, MOE_SLOTS), 1)
    avail = sel.astype(F32)
    pt = jnp.zeros((t, MOE_SLOTS), F32)
    for _ in range(TOP_K):
        ek = jnp.where(avail > 0.0, lane, float(LANES)).min(axis=-1, keepdims=True)
        hot = lane == ek
        dk = jnp.where(hot, dest, 0.0).sum(axis=-1, keepdims=True).astype(jnp.int32)
        pt = pt + jnp.where(slot == dk, 1.0, 0.0)
        avail = jnp.where(hot, 0.0, avail)
    ptb = pt.astype(BF16)
    pt_ref[0] = ptb
    stage_ref[...] = _dot_tn(ptb, xa_ref[0]).astype(BF16)

    def out_copy(src, dst):
        return pltpu.make_async_copy(stage_ref.at[pl.ds(src, MOE_G)], xs_ref.at[pl.ds(dst, MOE_G)], sem)

    total = _group_copies(j, cntp_ref, loc_ref, goff_ref, out_copy)

    def wait_one(c, carry):
        out_copy(0, 0).wait()
        return carry

    lax.fori_loop(0, total, wait_one, 0)

    @pl.when(j == pl.num_programs(0) - 1)
    def _():
        zero_ref[...] = jnp.zeros(zero_ref.shape, BF16)

        def zcopy(dst):
            return pltpu.make_async_copy(zero_ref.at[pl.ds(0, MOE_G)], xs_ref.at[pl.ds(dst, MOE_G)], sem)

        def zcopy_big(dst):
            return pltpu.make_async_copy(zero_ref, xs_ref.at[pl.ds(dst, MOE_TM)], sem)

        def per_expert(e, tot):
            n = tailn_ref[e]
            base = tail_ref[e]

            def per_chunk(c, carry):
                zcopy(pl.multiple_of(base + c * MOE_G, MOE_G)).start()
                return carry

            lax.fori_loop(0, n, per_chunk, 0)
            return tot + n

        ztotal = lax.fori_loop(0, N_EXPERTS, per_expert, 0)

        def zwait(c, carry):
            zcopy(0).wait()
            return carry

        lax.fori_loop(0, ztotal, zwait, 0)

        nbig = tailn_ref[N_EXPERTS]
        big0 = tail_ref[N_EXPERTS]

        def big_start(c, carry):
            zcopy_big(pl.multiple_of(big0 + c * MOE_TM, MOE_TM)).start()
            return carry

        def big_wait(c, carry):
            zcopy_big(0).wait()
            return carry

        lax.fori_loop(0, nbig, big_start, 0)
        lax.fori_loop(0, nbig, big_wait, 0)


def _dispatch(xaug, sel, locv, cntp, loc, goff, tail, tailn, s_rows):
    npb, l, _ = xaug.shape
    t = MOE_T
    nt_b = l // t
    nt = npb * nt_b
    tile = lambda w: pl.BlockSpec((1, t, w), lambda j, *_: (j // nt_b, j % nt_b, 0))
    return pl.pallas_call(
        _dispatch_kernel,
        grid_spec=pltpu.PrefetchScalarGridSpec(
            num_scalar_prefetch=5,
            grid=(nt,),
            in_specs=[tile(AUG_W), tile(LANES), pl.BlockSpec((1, 1, LANES), lambda j, *_: (j, 0, 0))],
            out_specs=[pl.BlockSpec((1, t, MOE_SLOTS), lambda j, *_: (j, 0, 0)),
                       pl.BlockSpec(memory_space=pl.ANY)],
            scratch_shapes=[pltpu.VMEM((MOE_SLOTS, AUG_W), BF16), pltpu.VMEM((MOE_TM, AUG_W), BF16),
                            pltpu.SemaphoreType.DMA(())]),
        out_shape=[jax.ShapeDtypeStruct((nt, t, MOE_SLOTS), BF16), jax.ShapeDtypeStruct((s_rows, AUG_W), BF16)],
        compiler_params=_cparams("arbitrary"),
        name="moe_dispatch",
    )(cntp, loc, goff, tail, tailn, xaug, sel, locv)


def _ffn_kernel(te_ref, na_ref, xs_ref, w1_ref, b1_ref, w2_ref, b2_ref, perm_ref, ys_ref, w1p_ref, w2b_ref):
    i = pl.program_id(0)
    e = te_ref[i]
    active = i < na_ref[0]
    fresh = jnp.logical_or(i == 0, e != te_ref[jnp.maximum(i - 1, 0)])
    hh = EXPERT_HIDDEN // 2

    @pl.when(jnp.logical_and(active, fresh))
    def _():
        for c in range(2):
            blk = w1_ref[0, :, 2 * hh * c:2 * hh * (c + 1)].astype(BF16)
            r = _dot(blk, perm_ref[...])
            w1p_ref[:, hh * c:hh * (c + 1)] = r[:, :hh].astype(BF16)
            w1p_ref[:, EXPERT_HIDDEN + hh * c:EXPERT_HIDDEN + hh * (c + 1)] = r[:, hh:].astype(BF16)
        w2b_ref[...] = w2_ref[0].astype(BF16)

    @pl.when(active)
    def _():
        x = xs_ref[:, 0:D_MODEL]
        extra = xs_ref[:, D_MODEL:AUG_W].astype(F32)
        lane = lax.broadcasted_iota(jnp.int32, extra.shape, 1)
        mine = jnp.logical_and(lane % N_EXPERTS == e, lane < 3 * N_EXPERTS)
        wslot = jnp.where(mine, extra, 0.0).sum(axis=-1, keepdims=True)
        h = _dot(x, w1p_ref[...]) + b1_ref[0]
        g = jnp.minimum(h[:, :EXPERT_HIDDEN], SWIGLU_LIMIT)
        up = jnp.clip(h[:, EXPERT_HIDDEN:], -SWIGLU_LIMIT, SWIGLU_LIMIT)
        act = g * _sigmoid(SWIGLU_ALPHA * g) * (up + 1.0)
        y = _dot(act.astype(BF16), w2b_ref[...]) + b2_ref[0]
        ys_ref[...] = (wslot * y).astype(BF16)

    @pl.when(jnp.logical_not(active))
    def _():
        ys_ref[...] = jnp.zeros(ys_ref.shape, BF16)


def _ffn(xs, tile_e, n_active, w1, b1p, w2, b2, perm, n_tiles):
    tm = MOE_TM
    d, h2 = w1.shape[1], w1.shape[2]
    return pl.pallas_call(
        _ffn_kernel,
        grid_spec=pltpu.PrefetchScalarGridSpec(
            num_scalar_prefetch=2,
            grid=(n_tiles,),
            in_specs=[pl.BlockSpec((tm, AUG_W), lambda i, te, na: (jnp.where(i < na[0], i, 0), 0)),
                      pl.BlockSpec((1, d, h2), lambda i, te, na: (te[i], 0, 0)),
                      pl.BlockSpec((1, 1, h2), lambda i, te, na: (te[i], 0, 0)),
                      pl.BlockSpec((1, h2 // 2, d), lambda i, te, na: (te[i], 0, 0)),
                      pl.BlockSpec((1, 1, d), lambda i, te, na: (te[i], 0, 0)),
                      pl.BlockSpec(perm.shape, lambda i, te, na: (0, 0))],
            out_specs=pl.BlockSpec((tm, d), lambda i, te, na: (i, 0)),
            scratch_shapes=[pltpu.VMEM((d, h2), BF16), pltpu.VMEM((h2 // 2, d), BF16)]),
        out_shape=jax.ShapeDtypeStruct((n_tiles * tm, d), BF16),
        compiler_params=_cparams("arbitrary"),
        name="moe_ffn",
    )(tile_e, n_active, xs, w1, b1p, w2, b2, perm)


def _combine_kernel(cntp_ref, loc_ref, goff_ref, pt_ref, x_ref, g2_ref, lg_ref, lb_ref, ys_ref,
                    o_ref, stage_ref, sem):
    j = pl.program_id(0)

    @pl.when(j == 0)
    def _():
        stage_ref[...] = jnp.zeros(stage_ref.shape, BF16)

    def in_copy(dst, src):
        return pltpu.make_async_copy(ys_ref.at[pl.ds(src, MOE_G)], stage_ref.at[pl.ds(dst, MOE_G)], sem)

    total = _group_copies(j, cntp_ref, loc_ref, goff_ref, in_copy)

    def wait_one(c, carry):
        in_copy(0, 0).wait()
        return carry

    lax.fori_loop(0, total, wait_one, 0)
    y = _dot(pt_ref[0], stage_ref[...])
    o_ref[0] = _ln(DEEPNORM_ALPHA * x_ref[0] + g2_ref[0] * y) * lg_ref[...] + lb_ref[...]


def _combine(pt, ys, x1, g2, lg, lb, cntp, loc, goff):
    npb, l, d = x1.shape
    t = MOE_T
    nt_b = l // t
    tile = lambda w: pl.BlockSpec((1, t, w), lambda j, *_: (j // nt_b, j % nt_b, 0))
    const = lambda a: pl.BlockSpec(a.shape, lambda j, *_: (0,) * a.ndim)
    return pl.pallas_call(
        _combine_kernel,
        grid_spec=pltpu.PrefetchScalarGridSpec(
            num_scalar_prefetch=3,
            grid=(npb * nt_b,),
            in_specs=[pl.BlockSpec((1, t, MOE_SLOTS), lambda j, *_: (j, 0, 0)), tile(d),
                      pl.BlockSpec((1, 1, d), lambda j, *_: (j // nt_b, 0, 0)), const(lg), const(lb),
                      pl.BlockSpec(memory_space=pl.ANY)],
            out_specs=tile(d),
            scratch_shapes=[pltpu.VMEM((MOE_SLOTS, d), BF16), pltpu.SemaphoreType.DMA(())]),
        out_shape=jax.ShapeDtypeStruct((npb, l, d), F32),
        compiler_params=_cparams("arbitrary"),
        name="moe_combine",
    )(cntp, loc, goff, pt, x1, g2, lg, lb, ys)


def _moe(x1, sh2, sc2, g2, rw, rb, w1, b1p, w2, b2, perm, lg, lb, npb):
    _, l, d = x1.shape
    n_tok = npb * l
    nt = n_tok // MOE_T
    xaug, sel, cnt = _route(x1, sh2, sc2, rw, rb, npb)

    i32 = jnp.int32
    cnt = cnt[:, 0, :N_EXPERTS].astype(i32)
    cntp = (cnt + (MOE_G - 1)) // MOE_G * MOE_G
    loc = jnp.cumsum(cntp, axis=1) - cntp
    tot = cntp.sum(axis=0)
    seg = (tot + (MOE_TM - 1)) // MOE_TM * MOE_TM
    seg_start = jnp.cumsum(seg) - seg
    goff = seg_start[None, :] + jnp.cumsum(cntp, axis=0) - cntp
    s_max = TOP_K * n_tok + nt * N_EXPERTS * (MOE_G - 1) + N_EXPERTS * (MOE_TM - MOE_G)
    n_tiles = -(-s_max // MOE_TM)
    cum_tiles = jnp.cumsum(seg // MOE_TM)
    tile_e = jnp.minimum(jnp.searchsorted(cum_tiles, jnp.arange(n_tiles, dtype=i32), side="right"),
                         N_EXPERTS - 1).astype(i32)
    n_active = cum_tiles[-1:].astype(i32)
    locv = jnp.pad(loc.astype(F32), ((0, 0), (0, LANES - N_EXPERTS)))[:, None, :]
    flat = lambda a: a.reshape(-1).astype(i32)

    tail = jnp.concatenate([seg_start + tot, n_active * MOE_TM]).astype(i32)
    tailn = jnp.concatenate([(seg - tot) // MOE_G, n_tiles - n_active]).astype(i32)
    pt, xs = _dispatch(xaug, sel, locv, flat(cntp), flat(loc), flat(goff), tail, tailn, n_tiles * MOE_TM)
    ys = _ffn(xs, tile_e, n_active, w1, b1p, w2, b2, perm, n_tiles)
    return _combine(pt, ys, x1, g2, lg, lb, flat(cntp), flat(loc), flat(goff))


def _deinterleave(n):
    return np.concatenate([np.arange(0, n, 2), np.arange(1, n, 2)])


def _rope_tables(l):
    rows = l // GRID_W
    r = np.repeat(np.arange(rows, dtype=np.float32), GRID_W)
    col = np.tile(np.arange(GRID_W, dtype=np.float32), rows)
    axis_dim = ATT_HEAD_DIM // 2
    inv_freq = (ROPE_THETA ** (-np.arange(0, axis_dim, 2, dtype=np.float32) / axis_dim)).astype(np.float32)
    ang = np.concatenate([r[:, None] * inv_freq, col[:, None] * inv_freq], axis=-1)
    c, s = np.cos(ang).astype(np.float32), np.sin(ang).astype(np.float32)
    z = np.zeros_like(s)
    rc = np.tile(np.concatenate([c, c], -1), (1, 2))
    rs1 = np.tile(np.concatenate([-s, z], -1), (1, 2))
    rs2 = np.tile(np.concatenate([z, s], -1), (1, 2))
    ident = (np.ones_like(rc), np.zeros_like(rc), np.zeros_like(rc))
    return tuple(jnp.asarray(np.stack([a, b])) for a, b in zip((rc, rs1, rs2), ident))


def kernel(x, c, ctx, c_ctx, w_mod, b_mod, w_in, att_q_norm, att_k_norm, gla_wdec_f, gla_bdec_f, gla_wdec_b, gla_bdec_b, gla_out_norm, hy_conv_w, hy_conv_b, hy_fw1, hy_fb1, hy_fw2, hy_fb2, hy_fw3, hy_fb3, hy_skip, w_branch_a, w_branch_b, w_branch_c, w_out, ln1_g, ln1_b, router_w, router_b, exp_w1, exp_b1, exp_w2, exp_b2, ln2_g, ln2_b):
    b, l, d = x.shape
    lc = ctx.shape[1]
    assert d == D_MODEL and (b * lc) % l == 0 and l % lc == 0 and lc % GLA_CHUNK == 0
    pc = (b * lc) // l
    p = b + pc

    perm64 = _deinterleave(ATT_HEAD_DIM)
    q_cols = np.concatenate([np.concatenate([64 * j + perm64, 64 * (j + 4) + perm64]) for j in range(4)])
    k_cols = ATT_Q_W + np.concatenate([perm64, 64 + perm64])
    v_cols = ATT_Q_W + ATT_KV_W + np.arange(ATT_KV_W)
    att_cols = np.concatenate([q_cols, k_cols, v_cols])
    o0 = ATT_Q_W + 2 * ATT_KV_W
    gla_cols = o0 + np.concatenate([np.arange(0, 2 * GLA_K_W + GLA_V_W + 2 * GLA_RANK)])
    og0 = o0 + 2 * GLA_K_W + GLA_V_W + 2 * GLA_RANK
    hy0 = og0 + GLA_V_W
    gt0 = hy0 + 3 * HY_WIDTH
    gain_perm = np.concatenate([perm64, perm64])
    bd = jnp.asarray(np.kron(np.eye(2), np.ones((64, 64))), dtype=BF16)
    rope = _rope_tables(l)
    fwd_l, inv_l = _dft_tables(l)
    fwd_c, inv_c = _dft_tables(lc)
    cmap = np.concatenate([np.arange(b), np.full(pc, b)])
    hh = EXPERT_HIDDEN // 2
    perm_np = np.zeros((2 * hh, 2 * hh), np.float32)
    perm_np[2 * np.arange(hh), np.arange(hh)] = 1.0
    perm_np[2 * np.arange(hh) + 1, hh + np.arange(hh)] = 1.0
    perm = jnp.asarray(perm_np, dtype=BF16)

    x_all = jnp.concatenate([x, ctx.reshape(pc, l, d)], axis=0)
    cc = jnp.concatenate([c, c_ctx[None], jnp.zeros((7, d), F32)], axis=0)

    for i in range(DEPTH):
        with_ctx = i < DEPTH - 1
        npb = p if with_ctx else b
        mod = _modulation(cc, w_mod[i], b_mod[i])[cmap]
        sh1, sc1, g1, sh2, sc2, g2 = [m[:, None, :] for m in jnp.split(mod, 6, axis=-1)]

        wi = w_in[i]
        watt = wi[:, att_cols].astype(BF16)
        wgla = jnp.pad(wi[:, gla_cols], ((0, 0), (0, GLA_GROUP_W - gla_cols.size))).astype(BF16)
        wog = wi[:, og0:hy0].astype(BF16)
        why = wi[:, hy0:gt0].astype(BF16)
        wgt = wi[:, gt0:].astype(BF16)
        qg = att_q_norm[i][gain_perm].reshape(1, LANES)
        kg = att_k_norm[i][gain_perm].reshape(1, LANES)
        qn, kn, v, gla, og, hy, gates = _inproj(x_all, sh1, sc1, (watt, wgla, wog, why, wgt), qg, kg, rope, bd, b)

        o_a = _attention(qn, kn, v, b, lc, with_ctx)

        wdec = jnp.zeros((LANES, 2 * GLA_K_W), F32)
        wdec = wdec.at[0:GLA_RANK, 0:GLA_K_W].set(gla_wdec_f[i])
        wdec = wdec.at[GLA_RANK:2 * GLA_RANK, GLA_K_W:].set(gla_wdec_b[i])
        bdec = jnp.concatenate([gla_bdec_f[i], gla_bdec_b[i]]).reshape(1, -1)
        o_b, o_b_ctx = _gla(gla, og, wdec, bdec, gla_out_norm[i].reshape(1, -1), b, lc)

        hre, him = _hy_filter(l, hy_fw1[i], hy_fb1[i], hy_fw2[i], hy_fb2[i], hy_fw3[i], hy_fb3[i], fwd_l)
        u_h, x0_h, t_h = _hy_pre(hy, hy_conv_w[i], hy_conv_b[i], hy_skip[i], b, l, b)
        y_h = _hy_conv(u_h, fwd_l, inv_l, hre, him)
        if with_ctx:
            hre_c, him_c = _hy_filter(lc, hy_fw1[i], hy_fb1[i], hy_fw2[i], hy_fb2[i], hy_fw3[i], hy_fb3[i], fwd_c)
            u_c, x0_c, t_c = _hy_pre(hy, hy_conv_w[i], hy_conv_b[i], hy_skip[i], b, lc, b)
            y_c = _hy_conv(u_c, fwd_c, inv_c, hre_c, him_c)
            cat = lambda a_lat, a_ctx: jnp.concatenate([a_lat, a_ctx.reshape(pc, l, a_ctx.shape[-1])], axis=0)
            o_b, y_h, x0_h, t_h = cat(o_b, o_b_ctx), cat(y_h, y_c), cat(x0_h, x0_c), cat(t_h, t_c)

        wa_rows = np.concatenate([np.concatenate([64 * j + np.arange(64), 64 * (j + 4) + np.arange(64)])
                                  for j in range(4)])
        x1 = _merge(x_all, o_a, o_b, y_h, x0_h, t_h, gates, g1,
                    w_branch_a[i][wa_rows].astype(BF16), w_branch_b[i].astype(BF16),
                    w_branch_c[i].astype(BF16), w_out[i].astype(BF16),
                    ln1_g[i].reshape(1, d), ln1_b[i].reshape(1, d), npb)

        rw = jnp.pad(router_w[i], ((0, 0), (0, LANES - N_EXPERTS)))
        rb = jnp.concatenate([router_b[i], jnp.full((LANES - N_EXPERTS,), NEG_BIG, F32)]).reshape(1, LANES)
        b1p = jnp.concatenate([exp_b1[i][..., 0::2], exp_b1[i][..., 1::2]], axis=-1)[:, None, :]
        x_all = _moe(x1, sh2, sc2, g2, rw, rb, exp_w1[i], b1p, exp_w2[i], exp_b2[i][:, None, :], perm,
                     ln2_g[i].reshape(1, d), ln2_b[i].reshape(1, d), npb)
    return x_all
```

```python
import functools
import math

import numpy as np
import jax
import jax.numpy as jnp
from jax import lax
from jax.experimental import pallas as pl
from jax.experimental.pallas import tpu as pltpu

F32 = jnp.float32
BF16 = jnp.bfloat16

D_MODEL = 1024
DEPTH = 2
GRID_W = 64
ATT_HEADS = 8
ATT_KV_HEADS = 2
ATT_HEAD_DIM = 64
ROPE_THETA = 10000.0
GLA_HEADS = 4
GLA_DK = 64
GLA_DV = 128
GLA_RANK = 16
GLA_TAU = 16.0
GLA_CHUNK = 64
HY_WIDTH = 512
HY_POS_FREQS = 16
HY_DECAY_SLOW = math.log(1e-2) / 1.5
HY_DECAY_FAST = math.log(1e-2) / 0.3
N_EXPERTS = 32
TOP_K = 4
EXPERT_HIDDEN = D_MODEL
SWIGLU_LIMIT = 7.0
SWIGLU_ALPHA = 1.702
DEEPNORM_ALPHA = (2 * DEPTH) ** 0.25
LN_EPS = 1e-5
RMS_EPS = 1e-6

ATT_Q_W = ATT_HEADS * ATT_HEAD_DIM
ATT_KV_W = ATT_KV_HEADS * ATT_HEAD_DIM
GLA_K_W = GLA_HEADS * GLA_DK
GLA_V_W = GLA_HEADS * GLA_DV
GLA_GROUP_W = 1152
LANES = 128
NEG_BIG = -3.0e38

VMEM_LIMIT = 56 * 1024 * 1024


def _cparams(*sem):
    return pltpu.CompilerParams(dimension_semantics=sem, vmem_limit_bytes=VMEM_LIMIT)


def _dot(a, b):
    return jnp.dot(a, b, preferred_element_type=F32)


def _dot_nt(a, b):
    return lax.dot_general(a, b, (((1,), (1,)), ((), ())), preferred_element_type=F32)


def _dot_tn(a, b):
    return lax.dot_general(a, b, (((0,), (0,)), ((), ())), preferred_element_type=F32)


def _split2(x):
    hi = x.astype(BF16)
    lo = (x - hi.astype(F32)).astype(BF16)
    return hi, lo


def _split3(x):
    hi = x.astype(BF16)
    r = x - hi.astype(F32)
    mid = r.astype(BF16)
    lo = (r - mid.astype(F32)).astype(BF16)
    return hi, mid, lo


def _dot3(a, b):
    ah, al = _split2(a)
    bh, bl = _split2(b)
    return _dot(ah, bh) + (_dot(ah, bl) + _dot(al, bh))


def _dot_exact_lhs(a_bf16, b):
    b0, b1, b2 = _split3(b)
    return _dot(a_bf16, b0) + (_dot(a_bf16, b1) + _dot(a_bf16, b2))


def _ln(x):
    mu = jnp.mean(x, axis=-1, keepdims=True)
    xc = x - mu
    var = jnp.mean(xc * xc, axis=-1, keepdims=True)
    return xc * lax.rsqrt(var + LN_EPS)


def _sigmoid(x):
    return 1.0 / (1.0 + jnp.exp(-x))


def _mod_kernel(c_ref, w_ref, b_ref, o_ref):
    c = c_ref[...]
    o_ref[...] = _dot3(c * _sigmoid(c), w_ref[...]) + b_ref[...]


def _modulation(cc, w_mod, b_mod):
    rows, d = cc.shape
    n = w_mod.shape[1]
    tn = 1024
    return pl.pallas_call(
        _mod_kernel,
        grid=(n // tn,),
        in_specs=[pl.BlockSpec((rows, d), lambda j: (0, 0)),
                  pl.BlockSpec((d, tn), lambda j: (0, j)),
                  pl.BlockSpec((1, tn), lambda j: (0, j))],
        out_specs=pl.BlockSpec((rows, tn), lambda j: (0, j)),
        out_shape=jax.ShapeDtypeStruct((rows, n), F32),
        compiler_params=_cparams("arbitrary"),
        name="modulation",
    )(cc, w_mod, b_mod.reshape(1, n))


def _head_norm_rope(t, gain, rc, rs1, rs2, bd):
    sh, sl = _split2(t * t)
    ss = _dot(sh, bd) + _dot(sl, bd)
    tn = t * lax.rsqrt(ss * (1.0 / ATT_HEAD_DIM) + RMS_EPS) * gain
    return tn * rc + pltpu.roll(tn, 96, 1) * rs1 + pltpu.roll(tn, 32, 1) * rs2


def _inproj_kernel(x_ref, sh_ref, sc_ref, watt_ref, wgla_ref, wog_ref, why_ref, wgt_ref,
                   qg_ref, kg_ref, rc_ref, rs1_ref, rs2_ref, bd_ref,
                   qn_ref, kn_ref, v_ref, gla_ref, og_ref, hy_ref, gt_ref):
    u = _ln(x_ref[0]) * (1.0 + sc_ref[0]) + sh_ref[0]
    ub = u.astype(BF16)
    rc, rs1, rs2, bd = rc_ref[0], rs1_ref[0], rs2_ref[0], bd_ref[...]
    att = _dot(ub, watt_ref[...])
    for j in range(ATT_Q_W // LANES):
        t = att[:, LANES * j:LANES * (j + 1)]
        qn = _head_norm_rope(t, qg_ref[...], rc, rs1, rs2, bd) * (ATT_HEAD_DIM ** -0.5)
        qn_ref[0, :, LANES * j:LANES * (j + 1)] = qn.astype(BF16)
    kn = _head_norm_rope(att[:, ATT_Q_W:ATT_Q_W + LANES], kg_ref[...], rc, rs1, rs2, bd)
    kn_ref[0] = kn.astype(BF16)
    v_ref[0] = att[:, ATT_Q_W + LANES:].astype(BF16)
    gla_ref[0] = _dot(ub, wgla_ref[...])
    og_ref[0] = _dot(ub, wog_ref[...]).astype(BF16)
    hy_ref[0] = _dot(ub, why_ref[...]).astype(BF16)
    gt_ref[0] = _dot(ub, wgt_ref[...]).astype(BF16)


def _inproj(x_all, sh, sc, wts, qg, kg, rope, bd, n_lat):
    p, l, d = x_all.shape
    tm = min(256, l)
    watt, wgla, wog, why, wgt = wts
    rc, rs1, rs2 = rope
    const = lambda a: pl.BlockSpec(a.shape, lambda b, i: (0,) * a.ndim)
    row = lambda w: pl.BlockSpec((1, tm, w), lambda b, i: (b, i, 0))
    tab = pl.BlockSpec((1, tm, LANES), lambda b, i: (jnp.where(b >= n_lat, 1, 0), i, 0))
    vec = pl.BlockSpec((1, 1, d), lambda b, i: (b, 0, 0))
    widths = (ATT_Q_W, LANES, LANES, GLA_GROUP_W, GLA_V_W, 3 * HY_WIDTH, 3 * d)
    dtypes = (BF16, BF16, BF16, F32, BF16, BF16, BF16)
    return pl.pallas_call(
        _inproj_kernel,
        grid=(p, l // tm),
        in_specs=[row(d), vec, vec, const(watt), const(wgla), const(wog), const(why), const(wgt),
                  const(qg), const(kg), tab, tab, tab, const(bd)],
        out_specs=[row(w) for w in widths],
        out_shape=[jax.ShapeDtypeStruct((p, l, w), dt) for w, dt in zip(widths, dtypes)],
        compiler_params=_cparams("parallel", "arbitrary"),
        name="inproj",
    )(x_all, sh, sc, watt, wgla, wog, why, wgt, qg, kg, rc, rs1, rs2, bd)


def _attend_group(qg, ks, vs, lo):
    zero = jnp.zeros_like(qg)
    outs = []
    for qh in (jnp.where(lo, qg, zero), jnp.where(lo, zero, qg)):
        ss = [_dot_nt(qh, k) for k in ks]
        m = ss[0].max(axis=-1, keepdims=True)
        for s in ss[1:]:
            m = jnp.maximum(m, s.max(axis=-1, keepdims=True))
        den = 0.0
        acc = 0.0
        for s, v in zip(ss, vs):
            p = jnp.exp(s - m)
            den = den + p.sum(axis=-1, keepdims=True)
            acc = acc + _dot(p.astype(BF16), v)
        outs.append(acc / den)
    return jnp.where(lo, outs[0], outs[1])


def _attn_kernel(q_ref, k_ref, v_ref, kc_ref, vc_ref, o_ref, *, n_lat_tiles, with_ctx):
    i = pl.program_id(1)
    tq = q_ref.shape[1]
    lo = lax.broadcasted_iota(jnp.int32, (tq, LANES), 1) < ATT_HEAD_DIM

    def run(ks, vs):
        for j in range(ATT_Q_W // LANES):
            o = _attend_group(q_ref[0, :, LANES * j:LANES * (j + 1)], ks, vs, lo)
            o_ref[0, :, LANES * j:LANES * (j + 1)] = o.astype(BF16)

    if with_ctx:
        @pl.when(i < n_lat_tiles)
        def _():
            run([k_ref[0], kc_ref[0]], [v_ref[0], vc_ref[0]])

        @pl.when(i == n_lat_tiles)
        def _():
            run([kc_ref[0]], [vc_ref[0]])
    else:
        run([k_ref[0], kc_ref[0]], [v_ref[0], vc_ref[0]])


def _attention(qn, kn, v, n_lat, lc, with_ctx):
    p, l, _ = qn.shape
    tq = lc
    r = l // lc
    nt = l // tq
    ctx_idx = lambda b: (n_lat + b // r, b % r, 0)
    if with_ctx:
        qmap = lambda b, i: (jnp.where(i < nt, b, n_lat + b // r), jnp.where(i < nt, i, b % r), 0)
    else:
        qmap = lambda b, i: (b, i, 0)
    return pl.pallas_call(
        functools.partial(_attn_kernel, n_lat_tiles=nt, with_ctx=with_ctx),
        grid=(n_lat, nt + (1 if with_ctx else 0)),
        in_specs=[pl.BlockSpec((1, tq, ATT_Q_W), qmap),
                  pl.BlockSpec((1, l, LANES), lambda b, i: (b, 0, 0)),
                  pl.BlockSpec((1, l, LANES), lambda b, i: (b, 0, 0)),
                  pl.BlockSpec((1, lc, LANES), lambda b, i: ctx_idx(b)),
                  pl.BlockSpec((1, lc, LANES), lambda b, i: ctx_idx(b))],
        out_specs=pl.BlockSpec((1, tq, ATT_Q_W), qmap),
        out_shape=jax.ShapeDtypeStruct((p if with_ctx else n_lat, l, ATT_Q_W), BF16),
        compiler_params=_cparams("parallel", "arbitrary"),
        name="attention",
    )(qn, kn, v, kn, v)


def _gla_segment(src_ref, og_ref, out_ref, acc_ref, st_ref, qd_ref, ke_ref, vb_ref, dec_ref,
                 wdec_ref, bdec_ref, gn_ref, ls):
    c = GLA_CHUNK
    nc = ls // c
    rb = min(256, ls)

    bi = lax.broadcasted_iota(jnp.int32, (rb, rb), 0)
    bj = lax.broadcasted_iota(jnp.int32, (rb, rb), 1)
    same = (bi // c) == (bj // c)
    att_masks = (same & (bi >= bj), same & (bi <= bj))
    tri_blk = tuple(jnp.where(m, 1.0, 0.0).astype(BF16) for m in att_masks)
    ones_blk = jnp.where(same, 1.0, 0.0).astype(BF16)
    lo_b = lax.broadcasted_iota(jnp.int32, (rb, LANES), 1) < GLA_DK

    def prep(t, carry):
        r0 = pl.multiple_of(t * rb, rb)
        q = src_ref[0, pl.ds(r0, rb), 0:GLA_K_W] * (GLA_DK ** -0.5)
        k = src_ref[0, pl.ds(r0, rb), GLA_K_W:2 * GLA_K_W]
        vb = src_ref[0, pl.ds(r0, rb), 2 * GLA_K_W:2 * GLA_K_W + GLA_V_W].astype(BF16)
        vb_ref[pl.ds(r0, rb), :] = vb
        lr = src_ref[0, pl.ds(r0, rb), 2 * GLA_K_W + GLA_V_W:GLA_GROUP_W]
        la = jax.nn.log_sigmoid(_dot3(lr, wdec_ref[...]) + bdec_ref[...]) * (1.0 / GLA_TAU)
        for d in range(2):
            l0, l1, l2 = _split3(la[:, d * GLA_K_W:(d + 1) * GLA_K_W])
            cum = _dot(tri_blk[d], l0) + (_dot(tri_blk[d], l1) + _dot(tri_blk[d], l2))
            tot = _dot(ones_blk, l0) + (_dot(ones_blk, l1) + _dot(ones_blk, l2))
            qd = (q * jnp.exp(cum)).astype(BF16)
            ki = (k * jnp.exp(-cum)).astype(BF16)
            qd_ref[d, pl.ds(r0, rb), :] = qd
            ke_ref[d, pl.ds(r0, rb), :] = (k * jnp.exp(tot - cum)).astype(BF16)
            dec_ref[d, pl.ds(r0, rb), :] = jnp.exp(tot)
            for h in range(GLA_HEADS):
                g, half = h // 2, h % 2
                sl = slice(LANES * g, LANES * (g + 1))
                zero = jnp.zeros_like(qd[:, sl])
                qh = jnp.where(lo_b, qd[:, sl], zero) if half == 0 else jnp.where(lo_b, zero, qd[:, sl])
                att = jnp.where(att_masks[d], _dot_nt(qh, ki[:, sl]), 0.0)
                o = _dot(att.astype(BF16), vb[:, GLA_DV * h:GLA_DV * (h + 1)])
                if d == 0:
                    acc_ref[pl.ds(r0, rb), GLA_DV * h:GLA_DV * (h + 1)] = o
                else:
                    acc_ref[pl.ds(r0, rb), GLA_DV * h:GLA_DV * (h + 1)] += o
        return carry

    lax.fori_loop(0, ls // rb, prep, 0)

    lo = lax.broadcasted_iota(jnp.int32, (c, LANES), 1) < GLA_DK

    def body(n, carry):
        for d, cn in ((0, n), (1, nc - 1 - n)):
            r0 = pl.multiple_of(cn * c, c)
            dec = dec_ref[d, pl.ds(r0, 1), :]
            for h in range(GLA_HEADS):
                g, half = h // 2, h % 2
                sl = slice(LANES * g, LANES * (g + 1))
                qd = qd_ref[d, pl.ds(r0, c), sl]
                zero = jnp.zeros_like(qd)
                qd = jnp.where(lo, qd, zero) if half == 0 else jnp.where(lo, zero, qd)
                vh = vb_ref[pl.ds(r0, c), GLA_DV * h:GLA_DV * (h + 1)]
                st = st_ref[d * GLA_HEADS + h]
                acc_ref[pl.ds(r0, c), GLA_DV * h:GLA_DV * (h + 1)] += _dot_nt(qd, st.astype(BF16))
                st_ref[d * GLA_HEADS + h] = dec[:, sl] * st + _dot_tn(vh, ke_ref[d, pl.ds(r0, c), sl])
        return carry

    lax.fori_loop(0, nc, body, 0, unroll=2 if nc % 2 == 0 else 1)

    tr = min(256, ls)

    def fin(t, carry):
        r0 = pl.multiple_of(t * tr, tr)
        for h in range(GLA_HEADS):
            sl = slice(GLA_DV * h, GLA_DV * (h + 1))
            o = acc_ref[pl.ds(r0, tr), sl]
            y = o * lax.rsqrt(jnp.mean(o * o, axis=-1, keepdims=True) + RMS_EPS)
            y = y * gn_ref[...]
            og = og_ref[0, pl.ds(r0, tr), sl].astype(F32)
            out_ref[0, pl.ds(r0, tr), sl] = (y * (og * _sigmoid(og))).astype(BF16)
        return carry

    lax.fori_loop(0, ls // tr, fin, 0)


def _gla_kernel(gl_ref, glc_ref, og_ref, ogc_ref, wdec_ref, bdec_ref, gn_ref, o_ref, oc_ref,
                acc_ref, st_ref, qd_ref, ke_ref, vb_ref, dec_ref):
    st_ref[...] = jnp.zeros(st_ref.shape, F32)
    scratch = (acc_ref, st_ref, qd_ref, ke_ref, vb_ref, dec_ref)
    _gla_segment(glc_ref, ogc_ref, oc_ref, *scratch, wdec_ref, bdec_ref, gn_ref, glc_ref.shape[1])
    _gla_segment(gl_ref, og_ref, o_ref, *scratch, wdec_ref, bdec_ref, gn_ref, gl_ref.shape[1])


def _gla(gla, og, wdec, bdec, gn, n_lat, lc):
    p, l, _ = gla.shape
    r = l // lc
    ctx_idx = lambda b: (n_lat + b // r, b % r, 0)
    const = lambda a: pl.BlockSpec(a.shape, lambda b: (0,) * a.ndim)
    return pl.pallas_call(
        _gla_kernel,
        grid=(n_lat,),
        in_specs=[pl.BlockSpec((1, l, GLA_GROUP_W), lambda b: (b, 0, 0)),
                  pl.BlockSpec((1, lc, GLA_GROUP_W), ctx_idx),
                  pl.BlockSpec((1, l, GLA_V_W), lambda b: (b, 0, 0)),
                  pl.BlockSpec((1, lc, GLA_V_W), ctx_idx),
                  const(wdec), const(bdec), const(gn)],
        out_specs=[pl.BlockSpec((1, l, GLA_V_W), lambda b: (b, 0, 0)),
                   pl.BlockSpec((1, lc, GLA_V_W), lambda b: (b, 0, 0))],
        out_shape=[jax.ShapeDtypeStruct((n_lat, l, GLA_V_W), BF16),
                   jax.ShapeDtypeStruct((n_lat, lc, GLA_V_W), BF16)],
        scratch_shapes=[pltpu.VMEM((l, GLA_V_W), F32),
                        pltpu.VMEM((2 * GLA_HEADS, GLA_DV, LANES), F32),
                        pltpu.VMEM((2, l, GLA_K_W), BF16),
                        pltpu.VMEM((2, l, GLA_K_W), BF16), pltpu.VMEM((l, GLA_V_W), BF16),
                        pltpu.VMEM((2, l, GLA_K_W), F32)],
        compiler_params=_cparams("arbitrary"),
        name="gla",
    )(gla, gla, og, og, wdec, bdec, gn)


def _dft_tables(l):
    n = 2 * l
    k = np.arange(l, dtype=np.int64)[:, None]
    t = np.arange(l, dtype=np.int64)[None, :]
    ang = 2.0 * np.pi * ((k * t) % n).astype(np.float64) / n
    fre = np.cos(ang)
    fim = -np.sin(ang)
    fim[0, :] = np.where(np.arange(l) % 2 == 0, 1.0, -1.0)
    fwd = np.concatenate([fre, fim], axis=0)
    return jnp.asarray(fwd, dtype=BF16), jnp.asarray(fwd.T, dtype=BF16)


def _hy_filter_kernel(z_ref, win_ref, w1_ref, b1_ref, w2_ref, b2_ref, w3_ref, b3_ref, fre_ref, fim_ref,
                      hre_ref, him_ref, hf_ref):
    j = pl.program_id(0)
    l = z_ref.shape[0]

    @pl.when(j == 0)
    def _():
        h = jnp.sin(_dot3(z_ref[...], w1_ref[...]) + b1_ref[...])
        h = jnp.sin(_dot3(h, w2_ref[...]) + b2_ref[...])
        h = _dot3(h, w3_ref[...]) + b3_ref[...]
        win = win_ref[...]
        hf = h[:, :HY_WIDTH] * win
        hb = h[:, HY_WIDTH:] * win
        row = lax.broadcasted_iota(jnp.int32, (l, HY_WIDTH), 0)
        hf_ref[:, :HY_WIDTH] = hf.astype(BF16)
        hf_ref[:, HY_WIDTH:] = jnp.where(row == 0, 0.0, hb).astype(BF16)

    tk = fre_ref.shape[0]
    a = _dot(fre_ref[...], hf_ref[...])
    b = _dot(fim_ref[...], hf_ref[...])
    first = (lax.broadcasted_iota(jnp.int32, (tk, HY_WIDTH), 0) == 0) & (j == 0)
    scl = jnp.where(first, 1.0 / (2 * l), 2.0 / (2 * l))
    hre_ref[...] = (a[:, :HY_WIDTH] + a[:, HY_WIDTH:]) * scl
    him_ref[...] = (b[:, :HY_WIDTH] + jnp.where(first, 1.0, -1.0) * b[:, HY_WIDTH:]) * scl


def _hy_filter(l, fw1, fb1, fw2, fb2, fw3, fb3, fwd):
    t = np.arange(l, dtype=np.float32)[:, None] / np.float32(l)
    freqs = np.arange(1, HY_POS_FREQS + 1, dtype=np.float32)
    z = np.concatenate([t, np.cos(2.0 * math.pi * freqs * t), np.sin(2.0 * math.pi * freqs * t)], axis=-1)
    z = np.pad(z, ((0, 0), (0, LANES - z.shape[1])))
    hid = fw2.shape[0]
    fw1 = jnp.pad(fw1, ((0, LANES - fw1.shape[0]), (0, LANES - hid)))
    fb1 = jnp.pad(fb1, (0, LANES - hid))
    fw2 = jnp.pad(fw2, ((0, LANES - hid), (0, LANES - hid)))
    fb2 = jnp.pad(fb2, (0, LANES - hid))
    fw3 = jnp.pad(fw3, ((0, LANES - hid), (0, 0)))
    deltas = np.abs(np.linspace(HY_DECAY_SLOW, HY_DECAY_FAST, HY_WIDTH, dtype=np.float32))
    win = np.exp(-t * deltas).astype(np.float32)
    tk = min(512, l)
    nk = l // tk
    const = lambda a: pl.BlockSpec(a.shape, lambda j: (0,) * a.ndim)
    args = (jnp.asarray(z), jnp.asarray(win), fw1, fb1.reshape(1, -1), fw2, fb2.reshape(1, -1),
            fw3, fb3.reshape(1, -1))
    return pl.pallas_call(
        _hy_filter_kernel,
        grid=(nk,),
        in_specs=[const(a) for a in args] + [pl.BlockSpec((tk, l), lambda j: (j, 0)),
                                              pl.BlockSpec((tk, l), lambda j: (nk + j, 0))],
        out_specs=[pl.BlockSpec((tk, HY_WIDTH), lambda j: (j, 0))] * 2,
        out_shape=[jax.ShapeDtypeStruct((l, HY_WIDTH), F32)] * 2,
        scratch_shapes=[pltpu.VMEM((l, 2 * HY_WIDTH), BF16)],
        compiler_params=_cparams("arbitrary"),
        name="hyena_filter",
    )(*args, fwd, fwd)


def _hy_pre_kernel(p0_ref, p1_ref, p2_ref, w0_ref, w1_ref, w2_ref, b0_ref, b1_ref, b2_ref, skip_ref,
                   u_ref, x0_ref, t_ref):
    ls = p0_ref.shape[1]
    row = lax.broadcasted_iota(jnp.int32, (ls, LANES), 0)

    def conv(p_ref, w_ref, b_ref):
        p = p_ref[0].astype(F32)
        prev = jnp.where(row == 0, 0.0, pltpu.roll(p, 1, 0))
        nxt = jnp.where(row == ls - 1, 0.0, pltpu.roll(p, ls - 1, 0))
        return prev * w_ref[0:1, :] + p * w_ref[1:2, :] + nxt * w_ref[2:3, :] + b_ref[...]

    x0 = conv(p0_ref, w0_ref, b0_ref)
    u = conv(p1_ref, w1_ref, b1_ref) * conv(p2_ref, w2_ref, b2_ref)
    u_ref[0] = u.astype(BF16)
    x0_ref[0] = x0.astype(BF16)
    t_ref[0] = (x0 * u * skip_ref[...]).astype(BF16)


def _hy_pre(hy, conv_w, conv_b, skip, nseq, ls, n_lat):
    p, l, _ = hy.shape
    r = l // ls
    nj = HY_WIDTH // LANES
    base = 0 if ls == l else n_lat
    src = lambda off: pl.BlockSpec((1, ls, LANES), lambda b, j: (base + b // r, b % r, off * nj + j))
    wsp = lambda off: pl.BlockSpec((3, LANES), lambda b, j: (0, off * nj + j))
    bsp = lambda off: pl.BlockSpec((1, LANES), lambda b, j: (0, off * nj + j))
    out = pl.BlockSpec((1, ls, LANES), lambda b, j: (b, 0, j))
    cb = conv_b.reshape(1, -1)
    return pl.pallas_call(
        _hy_pre_kernel,
        grid=(nseq, nj),
        in_specs=[src(0), src(1), src(2), wsp(0), wsp(1), wsp(2), bsp(0), bsp(1), bsp(2), bsp(0)],
        out_specs=[out] * 3,
        out_shape=[jax.ShapeDtypeStruct((nseq, ls, HY_WIDTH), BF16)] * 3,
        compiler_params=_cparams("parallel", "arbitrary"),
        name="hyena_pre",
    )(hy, hy, hy, conv_w, conv_w, conv_w, cb, cb, cb, skip.reshape(1, -1))


def _hy_conv_kernel(u_ref, fre_ref, fim_ref, ire_ref, iim_ref, hre_ref, him_ref, y_ref):
    j = pl.program_id(1)
    u = u_ref[0]
    ure = _dot(fre_ref[...], u)
    uim = _dot(fim_ref[...], u)
    hre, him = hre_ref[...], him_ref[...]
    first = (lax.broadcasted_iota(jnp.int32, ure.shape, 0) == 0) & (j == 0)
    yre = jnp.where(first, ure * hre, ure * hre - uim * him)
    yim = jnp.where(first, uim * him, ure * him + uim * hre)
    y = _dot(ire_ref[...], yre.astype(BF16)) + _dot(iim_ref[...], yim.astype(BF16))

    @pl.when(j == 0)
    def _():
        y_ref[0] = y

    @pl.when(j > 0)
    def _():
        y_ref[0] += y


def _hy_conv(u, fwd, inv, hre, him):
    nseq, ls, _ = u.shape
    tk = min(512, ls)
    nk = ls // tk
    return pl.pallas_call(
        _hy_conv_kernel,
        grid=(nseq, nk),
        in_specs=[pl.BlockSpec((1, ls, HY_WIDTH), lambda b, j: (b, 0, 0)),
                  pl.BlockSpec((tk, ls), lambda b, j: (j, 0)),
                  pl.BlockSpec((tk, ls), lambda b, j: (nk + j, 0)),
                  pl.BlockSpec((ls, tk), lambda b, j: (0, j)),
                  pl.BlockSpec((ls, tk), lambda b, j: (0, nk + j)),
                  pl.BlockSpec((tk, HY_WIDTH), lambda b, j: (j, 0)),
                  pl.BlockSpec((tk, HY_WIDTH), lambda b, j: (j, 0))],
        out_specs=pl.BlockSpec((1, ls, HY_WIDTH), lambda b, j: (b, 0, 0)),
        out_shape=jax.ShapeDtypeStruct((nseq, ls, HY_WIDTH), F32),
        compiler_params=_cparams("parallel", "arbitrary"),
        name="hyena_conv",
    )(u, fwd, fwd, inv, inv, hre, him)


def _merge_kernel(x_ref, oa_ref, ob_ref, y_ref, x0_ref, t_ref, gt_ref, g1_ref,
                  wa_ref, wb_ref, wc_ref, wo_ref, lg_ref, lb_ref, o_ref):
    d = x_ref.shape[2]
    oc = (x0_ref[0].astype(F32) * y_ref[0] + t_ref[0].astype(F32)).astype(BF16)
    gt = gt_ref[0]
    m = (_sigmoid(gt[:, 0:d].astype(F32)) * _dot(oa_ref[0], wa_ref[...])
         + _sigmoid(gt[:, d:2 * d].astype(F32)) * _dot(ob_ref[0], wb_ref[...])
         + _sigmoid(gt[:, 2 * d:3 * d].astype(F32)) * _dot(oc, wc_ref[...]))
    mix = _dot(m.astype(BF16), wo_ref[...])
    o_ref[0] = _ln(DEEPNORM_ALPHA * x_ref[0] + g1_ref[0] * mix) * lg_ref[...] + lb_ref[...]


def _merge(x_all, o_a, o_b, y, x0, t, gates, g1, wa, wb, wc, wo, lg, lb, npb):
    _, l, d = x_all.shape
    tm = min(512, l)
    row = lambda w: pl.BlockSpec((1, tm, w), lambda b, i: (b, i, 0))
    const = lambda a: pl.BlockSpec(a.shape, lambda b, i: (0,) * a.ndim)
    return pl.pallas_call(
        _merge_kernel,
        grid=(npb, l // tm),
        in_specs=[row(d), row(ATT_Q_W), row(GLA_V_W), row(HY_WIDTH), row(HY_WIDTH), row(HY_WIDTH),
                  row(3 * d), pl.BlockSpec((1, 1, d), lambda b, i: (b, 0, 0)),
                  const(wa), const(wb), const(wc), const(wo), const(lg), const(lb)],
        out_specs=row(d),
        out_shape=jax.ShapeDtypeStruct((npb, l, d), F32),
        compiler_params=_cparams("parallel", "arbitrary"),
        name="merge",
    )(x_all, o_a, o_b, y, x0, t, gates, g1, wa, wb, wc, wo, lg, lb)


MOE_T = 256
MOE_G = 16
MOE_TM = 512
MOE_SLOTS = TOP_K * MOE_T + 512
AUG_W = D_MODEL + LANES


def _route_kernel(x_ref, sh_ref, sc_ref, rw_ref, rb_ref, xa_ref, sel_ref, cnt_ref):
    d = x_ref.shape[2]
    u = _ln(x_ref[0]) * (1.0 + sc_ref[0]) + sh_ref[0]
    xa_ref[0, :, 0:d] = u.astype(BF16)
    logits = _dot3(u, rw_ref[...]) + rb_ref[...]
    lane = lax.broadcasted_iota(jnp.int32, logits.shape, 1).astype(F32)
    work = logits
    hots, vals = [], []
    for _ in range(TOP_K):
        m = work.max(axis=-1, keepdims=True)
        idx = jnp.where(work == m, lane, float(LANES)).min(axis=-1, keepdims=True)
        hot = lane == idx
        hots.append(hot)
        vals.append(m)
        work = jnp.where(hot, NEG_BIG, work)
    es = [jnp.exp(v - vals[0]) for v in vals]
    den = es[0] + es[1] + es[2] + es[3]
    comb = jnp.zeros_like(logits)
    sel = jnp.zeros_like(logits)
    for hot, e in zip(hots, es):
        comb = comb + jnp.where(hot, e / den, 0.0)
        sel = sel + jnp.where(hot, 1.0, 0.0)
    c0, c1, c2 = _split3(comb)
    aug = c0.astype(F32) + pltpu.roll(c1.astype(F32), 32, 1) + pltpu.roll(c2.astype(F32), 64, 1)
    xa_ref[0, :, d:d + LANES] = aug.astype(BF16)
    sel_ref[0] = sel.astype(BF16)
    cnt_ref[0] = sel.sum(axis=0, keepdims=True)


def _route(x_all, sh, sc, rw, rb, npb):
    _, l, d = x_all.shape
    t = MOE_T
    nt_b = l // t
    row = lambda w: pl.BlockSpec((1, t, w), lambda b, i: (b, i, 0))
    vec = pl.BlockSpec((1, 1, d), lambda b, i: (b, 0, 0))
    const = lambda a: pl.BlockSpec(a.shape, lambda b, i: (0,) * a.ndim)
    return pl.pallas_call(
        _route_kernel,
        grid=(npb, nt_b),
        in_specs=[row(d), vec, vec, const(rw), const(rb)],
        out_specs=[row(AUG_W), row(LANES), pl.BlockSpec((1, 1, LANES), lambda b, i: (b * nt_b + i, 0, 0))],
        out_shape=[jax.ShapeDtypeStruct((npb, l, AUG_W), BF16), jax.ShapeDtypeStruct((npb, l, LANES), BF16),
                   jax.ShapeDtypeStruct((npb * nt_b, 1, LANES), F32)],
        compiler_params=_cparams("parallel", "arbitrary"),
        name="route",
    )(x_all, sh, sc, rw, rb)


def _group_copies(j, cntp_ref, loc_ref, goff_ref, make_copy):
    def per_expert(e, total):
        n = cntp_ref[j * N_EXPERTS + e] // MOE_G
        loc = loc_ref[j * N_EXPERTS + e]
        off = goff_ref[j * N_EXPERTS + e]

        def per_chunk(c, carry):
            make_copy(pl.multiple_of(loc + c * MOE_G, MOE_G), pl.multiple_of(off + c * MOE_G, MOE_G)).start()
            return carry

        lax.fori_loop(0, n, per_chunk, 0)
        return total + n

    return lax.fori_loop(0, N_EXPERTS, per_expert, 0)


def _wait_copies(n, make_copy):
    def wait_one(c, carry):
        make_copy(0, 0).wait()
        return carry

    lax.fori_loop(0, n, wait_one, 0)


def _dispatch_kernel(cntp_ref, loc_ref, goff_ref, nch_ref, tail_ref, tailn_ref,
                     xa_ref, sel_ref, locv_ref, pt_ref, xs_ref, stage_ref, zero_ref, sems):
    j = pl.program_id(0)
    nt = pl.num_programs(0)
    buf = j % 2
    sem = sems.at[buf]
    stage = stage_ref.at[buf]
    t = MOE_T

    def out_copy(src, dst):
        return pltpu.make_async_copy(stage.at[pl.ds(src, MOE_G)], xs_ref.at[pl.ds(dst, MOE_G)], sem)

    @pl.when(j >= 2)
    def _():
        _wait_copies(nch_ref[jnp.maximum(j - 2, 0)], out_copy)

    sel = sel_ref[0]
    ri = lax.broadcasted_iota(jnp.int32, (t, t), 0)
    ci = lax.broadcasted_iota(jnp.int32, (t, t), 1)
    rank = _dot(jnp.where(ci < ri, 1.0, 0.0).astype(BF16), sel)
    dest = rank + locv_ref[0]
    lane = lax.broadcasted_iota(jnp.int32, (t, LANES), 1).astype(F32)
    slot = lax.broadcasted_iota(jnp.int32, (t, MOE_SLOTS), 1)
    avail = sel.astype(F32)
    pt = jnp.zeros((t, MOE_SLOTS), F32)
    for _ in range(TOP_K):
        ek = jnp.where(avail > 0.0, lane, float(LANES)).min(axis=-1, keepdims=True)
        hot = lane == ek
        dk = jnp.where(hot, dest, 0.0).sum(axis=-1, keepdims=True).astype(jnp.int32)
        pt = pt + jnp.where(slot == dk, 1.0, 0.0)
        avail = jnp.where(hot, 0.0, avail)
    ptb = pt.astype(BF16)
    pt_ref[0] = ptb
    stage[...] = _dot_tn(ptb, xa_ref[0]).astype(BF16)
    _group_copies(j, cntp_ref, loc_ref, goff_ref, out_copy)

    @pl.when(j == nt - 1)
    def _():
        _wait_copies(nch_ref[j], out_copy)

        @pl.when(j >= 1)
        def _():
            other = sems.at[1 - buf]
            _wait_copies(nch_ref[jnp.maximum(j - 1, 0)], lambda s, d: pltpu.make_async_copy(
                stage_ref.at[1 - buf].at[pl.ds(s, MOE_G)], xs_ref.at[pl.ds(d, MOE_G)], other))

        zero_ref[...] = jnp.zeros(zero_ref.shape, BF16)

        def zcopy(dst):
            return pltpu.make_async_copy(zero_ref.at[pl.ds(0, MOE_G)], xs_ref.at[pl.ds(dst, MOE_G)], sem)

        def zcopy_big(dst):
            return pltpu.make_async_copy(zero_ref, xs_ref.at[pl.ds(dst, MOE_TM)], sem)

        def per_expert(e, tot):
            n = tailn_ref[e]
            base = tail_ref[e]

            def per_chunk(c, carry):
                zcopy(pl.multiple_of(base + c * MOE_G, MOE_G)).start()
                return carry

            lax.fori_loop(0, n, per_chunk, 0)
            return tot + n

        ztotal = lax.fori_loop(0, N_EXPERTS, per_expert, 0)

        def zwait(c, carry):
            zcopy(0).wait()
            return carry

        lax.fori_loop(0, ztotal, zwait, 0)

        nbig = tailn_ref[N_EXPERTS]
        big0 = tail_ref[N_EXPERTS]

        def big_start(c, carry):
            zcopy_big(pl.multiple_of(big0 + c * MOE_TM, MOE_TM)).start()
            return carry

        def big_wait(c, carry):
            zcopy_big(0).wait()
            return carry

        lax.fori_loop(0, nbig, big_start, 0)
        lax.fori_loop(0, nbig, big_wait, 0)


def _dispatch(xaug, sel, locv, cntp, loc, goff, nch, tail, tailn, s_rows):
    npb, l, _ = xaug.shape
    t = MOE_T
    nt_b = l // t
    nt = npb * nt_b
    tile = lambda w: pl.BlockSpec((1, t, w), lambda j, *_: (j // nt_b, j % nt_b, 0))
    return pl.pallas_call(
        _dispatch_kernel,
        grid_spec=pltpu.PrefetchScalarGridSpec(
            num_scalar_prefetch=6,
            grid=(nt,),
            in_specs=[tile(AUG_W), tile(LANES), pl.BlockSpec((1, 1, LANES), lambda j, *_: (j, 0, 0))],
            out_specs=[pl.BlockSpec((1, t, MOE_SLOTS), lambda j, *_: (j, 0, 0)),
                       pl.BlockSpec(memory_space=pl.ANY)],
            scratch_shapes=[pltpu.VMEM((2, MOE_SLOTS, AUG_W), BF16), pltpu.VMEM((MOE_TM, AUG_W), BF16),
                            pltpu.SemaphoreType.DMA((2,))]),
        out_shape=[jax.ShapeDtypeStruct((nt, t, MOE_SLOTS), BF16), jax.ShapeDtypeStruct((s_rows, AUG_W), BF16)],
        compiler_params=_cparams("arbitrary"),
        name="moe_dispatch",
    )(cntp, loc, goff, nch, tail, tailn, xaug, sel, locv)


def _ffn_kernel(te_ref, na_ref, xs_ref, w1_ref, b1_ref, w2_ref, b2_ref, perm_ref, ys_ref, w1p_ref, w2b_ref):
    i = pl.program_id(0)
    e = te_ref[i]
    active = i < na_ref[0]
    fresh = jnp.logical_or(i == 0, e != te_ref[jnp.maximum(i - 1, 0)])
    hh = EXPERT_HIDDEN // 2

    @pl.when(jnp.logical_and(active, fresh))
    def _():
        for c in range(2):
            blk = w1_ref[0, :, 2 * hh * c:2 * hh * (c + 1)].astype(BF16)
            r = _dot(blk, perm_ref[...])
            w1p_ref[:, hh * c:hh * (c + 1)] = r[:, :hh].astype(BF16)
            w1p_ref[:, EXPERT_HIDDEN + hh * c:EXPERT_HIDDEN + hh * (c + 1)] = r[:, hh:].astype(BF16)
        w2b_ref[...] = w2_ref[0].astype(BF16)

    @pl.when(active)
    def _():
        x = xs_ref[:, 0:D_MODEL]
        extra = xs_ref[:, D_MODEL:AUG_W].astype(F32)
        lane = lax.broadcasted_iota(jnp.int32, extra.shape, 1)
        mine = jnp.logical_and(lane % N_EXPERTS == e, lane < 3 * N_EXPERTS)
        wslot = jnp.where(mine, extra, 0.0).sum(axis=-1, keepdims=True)
        h = _dot(x, w1p_ref[...]) + b1_ref[0]
        g = jnp.minimum(h[:, :EXPERT_HIDDEN], SWIGLU_LIMIT)
        up = jnp.clip(h[:, EXPERT_HIDDEN:], -SWIGLU_LIMIT, SWIGLU_LIMIT)
        act = g * _sigmoid(SWIGLU_ALPHA * g) * (up + 1.0)
        y = _dot(act.astype(BF16), w2b_ref[...]) + b2_ref[0]
        ys_ref[...] = (wslot * y).astype(BF16)

    @pl.when(jnp.logical_not(active))
    def _():
        ys_ref[...] = jnp.zeros(ys_ref.shape, BF16)


def _ffn(xs, tile_e, n_active, w1, b1p, w2, b2, perm, n_tiles):
    tm = MOE_TM
    d, h2 = w1.shape[1], w1.shape[2]
    return pl.pallas_call(
        _ffn_kernel,
        grid_spec=pltpu.PrefetchScalarGridSpec(
            num_scalar_prefetch=2,
            grid=(n_tiles,),
            in_specs=[pl.BlockSpec((tm, AUG_W), lambda i, te, na: (jnp.where(i < na[0], i, 0), 0)),
                      pl.BlockSpec((1, d, h2), lambda i, te, na: (te[i], 0, 0)),
                      pl.BlockSpec((1, 1, h2), lambda i, te, na: (te[i], 0, 0)),
                      pl.BlockSpec((1, h2 // 2, d), lambda i, te, na: (te[i], 0, 0)),
                      pl.BlockSpec((1, 1, d), lambda i, te, na: (te[i], 0, 0)),
                      pl.BlockSpec(perm.shape, lambda i, te, na: (0, 0))],
            out_specs=pl.BlockSpec((tm, d), lambda i, te, na: (i, 0)),
            scratch_shapes=[pltpu.VMEM((d, h2), BF16), pltpu.VMEM((h2 // 2, d), BF16)]),
        out_shape=jax.ShapeDtypeStruct((n_tiles * tm, d), BF16),
        compiler_params=_cparams("arbitrary"),
        name="moe_ffn",
    )(tile_e, n_active, xs, w1, b1p, w2, b2, perm)


def _combine_kernel(cntp_ref, loc_ref, goff_ref, nch_ref, pt_ref, x_ref, g2_ref, lg_ref, lb_ref, ys_ref,
                    o_ref, stage_ref, sems):
    j = pl.program_id(0)
    nt = pl.num_programs(0)
    buf = j % 2

    def in_copy(b):
        return lambda dst, src: pltpu.make_async_copy(
            ys_ref.at[pl.ds(src, MOE_G)], stage_ref.at[b].at[pl.ds(dst, MOE_G)], sems.at[b])

    @pl.when(j == 0)
    def _():
        stage_ref[...] = jnp.zeros(stage_ref.shape, BF16)
        _group_copies(j, cntp_ref, loc_ref, goff_ref, in_copy(buf))

    @pl.when(j + 1 < nt)
    def _():
        _group_copies(j + 1, cntp_ref, loc_ref, goff_ref, in_copy(1 - buf))

    _wait_copies(nch_ref[j], in_copy(buf))
    y = _dot(pt_ref[0], stage_ref[buf])
    o_ref[0] = _ln(DEEPNORM_ALPHA * x_ref[0] + g2_ref[0] * y) * lg_ref[...] + lb_ref[...]


def _combine(pt, ys, x1, g2, lg, lb, cntp, loc, goff, nch):
    npb, l, d = x1.shape
    t = MOE_T
    nt_b = l // t
    tile = lambda w: pl.BlockSpec((1, t, w), lambda j, *_: (j // nt_b, j % nt_b, 0))
    const = lambda a: pl.BlockSpec(a.shape, lambda j, *_: (0,) * a.ndim)
    return pl.pallas_call(
        _combine_kernel,
        grid_spec=pltpu.PrefetchScalarGridSpec(
            num_scalar_prefetch=4,
            grid=(npb * nt_b,),
            in_specs=[pl.BlockSpec((1, t, MOE_SLOTS), lambda j, *_: (j, 0, 0)), tile(d),
                      pl.BlockSpec((1, 1, d), lambda j, *_: (j // nt_b, 0, 0)), const(lg), const(lb),
                      pl.BlockSpec(memory_space=pl.ANY)],
            out_specs=tile(d),
            scratch_shapes=[pltpu.VMEM((2, MOE_SLOTS, d), BF16), pltpu.SemaphoreType.DMA((2,))]),
        out_shape=jax.ShapeDtypeStruct((npb, l, d), F32),
        compiler_params=_cparams("arbitrary"),
        name="moe_combine",
    )(cntp, loc, goff, nch, pt, x1, g2, lg, lb, ys)


def _moe(x1, sh2, sc2, g2, rw, rb, w1, b1p, w2, b2, perm, lg, lb, npb):
    _, l, d = x1.shape
    n_tok = npb * l
    nt = n_tok // MOE_T
    xaug, sel, cnt = _route(x1, sh2, sc2, rw, rb, npb)

    i32 = jnp.int32
    cnt = cnt[:, 0, :N_EXPERTS].astype(i32)
    cntp = (cnt + (MOE_G - 1)) // MOE_G * MOE_G
    loc = jnp.cumsum(cntp, axis=1) - cntp
    tot = cntp.sum(axis=0)
    seg = (tot + (MOE_TM - 1)) // MOE_TM * MOE_TM
    seg_start = jnp.cumsum(seg) - seg
    goff = seg_start[None, :] + jnp.cumsum(cntp, axis=0) - cntp
    s_max = TOP_K * n_tok + nt * N_EXPERTS * (MOE_G - 1) + N_EXPERTS * (MOE_TM - MOE_G)
    n_tiles = -(-s_max // MOE_TM)
    cum_tiles = jnp.cumsum(seg // MOE_TM)
    tile_e = jnp.minimum((jnp.arange(n_tiles, dtype=i32)[:, None] >= cum_tiles[None, :]).astype(i32).sum(axis=1),
                         N_EXPERTS - 1)
    n_active = cum_tiles[-1:].astype(i32)
    nch = (cntp.sum(axis=1) // MOE_G).astype(i32)
    locv = jnp.pad(loc.astype(F32), ((0, 0), (0, LANES - N_EXPERTS)))[:, None, :]
    flat = lambda a: a.reshape(-1).astype(i32)

    tail = jnp.concatenate([seg_start + tot, n_active * MOE_TM]).astype(i32)
    tailn = jnp.concatenate([(seg - tot) // MOE_G, n_tiles - n_active]).astype(i32)
    pt, xs = _dispatch(xaug, sel, locv, flat(cntp), flat(loc), flat(goff), nch, tail, tailn, n_tiles * MOE_TM)
    ys = _ffn(xs, tile_e, n_active, w1, b1p, w2, b2, perm, n_tiles)
    return _combine(pt, ys, x1, g2, lg, lb, flat(cntp), flat(loc), flat(goff), nch)


def _deinterleave(n):
    return np.concatenate([np.arange(0, n, 2), np.arange(1, n, 2)])


def _rope_tables(l):
    rows = l // GRID_W
    r = np.repeat(np.arange(rows, dtype=np.float32), GRID_W)
    col = np.tile(np.arange(GRID_W, dtype=np.float32), rows)
    axis_dim = ATT_HEAD_DIM // 2
    inv_freq = (ROPE_THETA ** (-np.arange(0, axis_dim, 2, dtype=np.float32) / axis_dim)).astype(np.float32)
    ang = np.concatenate([r[:, None] * inv_freq, col[:, None] * inv_freq], axis=-1)
    c, s = np.cos(ang).astype(np.float32), np.sin(ang).astype(np.float32)
    z = np.zeros_like(s)
    rc = np.tile(np.concatenate([c, c], -1), (1, 2))
    rs1 = np.tile(np.concatenate([-s, z], -1), (1, 2))
    rs2 = np.tile(np.concatenate([z, s], -1), (1, 2))
    ident = (np.ones_like(rc), np.zeros_like(rc), np.zeros_like(rc))
    return tuple(jnp.asarray(np.stack([a, b])) for a, b in zip((rc, rs1, rs2), ident))


def kernel(x, c, ctx, c_ctx, w_mod, b_mod, w_in, att_q_norm, att_k_norm, gla_wdec_f, gla_bdec_f, gla_wdec_b, gla_bdec_b, gla_out_norm, hy_conv_w, hy_conv_b, hy_fw1, hy_fb1, hy_fw2, hy_fb2, hy_fw3, hy_fb3, hy_skip, w_branch_a, w_branch_b, w_branch_c, w_out, ln1_g, ln1_b, router_w, router_b, exp_w1, exp_b1, exp_w2, exp_b2, ln2_g, ln2_b):
    b, l, d = x.shape
    lc = ctx.shape[1]
    assert d == D_MODEL and (b * lc) % l == 0 and l % lc == 0 and lc % GLA_CHUNK == 0
    pc = (b * lc) // l
    p = b + pc

    perm64 = _deinterleave(ATT_HEAD_DIM)
    q_cols = np.concatenate([np.concatenate([64 * j + perm64, 64 * (j + 4) + perm64]) for j in range(4)])
    k_cols = ATT_Q_W + np.concatenate([perm64, 64 + perm64])
    v_cols = ATT_Q_W + ATT_KV_W + np.arange(ATT_KV_W)
    att_cols = np.concatenate([q_cols, k_cols, v_cols])
    o0 = ATT_Q_W + 2 * ATT_KV_W
    gla_cols = o0 + np.concatenate([np.arange(0, 2 * GLA_K_W + GLA_V_W + 2 * GLA_RANK)])
    og0 = o0 + 2 * GLA_K_W + GLA_V_W + 2 * GLA_RANK
    hy0 = og0 + GLA_V_W
    gt0 = hy0 + 3 * HY_WIDTH
    gain_perm = np.concatenate([perm64, perm64])
    bd = jnp.asarray(np.kron(np.eye(2), np.ones((64, 64))), dtype=BF16)
    rope = _rope_tables(l)
    fwd_l, inv_l = _dft_tables(l)
    fwd_c, inv_c = _dft_tables(lc)
    cmap = np.concatenate([np.arange(b), np.full(pc, b)])
    hh = EXPERT_HIDDEN // 2
    perm_np = np.zeros((2 * hh, 2 * hh), np.float32)
    perm_np[2 * np.arange(hh), np.arange(hh)] = 1.0
    perm_np[2 * np.arange(hh) + 1, hh + np.arange(hh)] = 1.0
    perm = jnp.asarray(perm_np, dtype=BF16)

    x_all = jnp.concatenate([x, ctx.reshape(pc, l, d)], axis=0)
    cc = jnp.concatenate([c, c_ctx[None], jnp.zeros((7, d), F32)], axis=0)

    for i in range(DEPTH):
        with_ctx = i < DEPTH - 1
        npb = p if with_ctx else b
        mod = _modulation(cc, w_mod[i], b_mod[i])[cmap]
        sh1, sc1, g1, sh2, sc2, g2 = [m[:, None, :] for m in jnp.split(mod, 6, axis=-1)]

        wi = w_in[i]
        watt = wi[:, att_cols].astype(BF16)
        wgla = jnp.pad(wi[:, gla_cols], ((0, 0), (0, GLA_GROUP_W - gla_cols.size))).astype(BF16)
        wog = wi[:, og0:hy0].astype(BF16)
        why = wi[:, hy0:gt0].astype(BF16)
        wgt = wi[:, gt0:].astype(BF16)
        qg = att_q_norm[i][gain_perm].reshape(1, LANES)
        kg = att_k_norm[i][gain_perm].reshape(1, LANES)
        qn, kn, v, gla, og, hy, gates = _inproj(x_all, sh1, sc1, (watt, wgla, wog, why, wgt), qg, kg, rope, bd, b)

        o_a = _attention(qn, kn, v, b, lc, with_ctx)

        wdec = jnp.zeros((LANES, 2 * GLA_K_W), F32)
        wdec = wdec.at[0:GLA_RANK, 0:GLA_K_W].set(gla_wdec_f[i])
        wdec = wdec.at[GLA_RANK:2 * GLA_RANK, GLA_K_W:].set(gla_wdec_b[i])
        bdec = jnp.concatenate([gla_bdec_f[i], gla_bdec_b[i]]).reshape(1, -1)
        o_b, o_b_ctx = _gla(gla, og, wdec, bdec, gla_out_norm[i].reshape(1, -1), b, lc)

        hre, him = _hy_filter(l, hy_fw1[i], hy_fb1[i], hy_fw2[i], hy_fb2[i], hy_fw3[i], hy_fb3[i], fwd_l)
        u_h, x0_h, t_h = _hy_pre(hy, hy_conv_w[i], hy_conv_b[i], hy_skip[i], b, l, b)
        y_h = _hy_conv(u_h, fwd_l, inv_l, hre, him)
        if with_ctx:
            hre_c, him_c = _hy_filter(lc, hy_fw1[i], hy_fb1[i], hy_fw2[i], hy_fb2[i], hy_fw3[i], hy_fb3[i], fwd_c)
            u_c, x0_c, t_c = _hy_pre(hy, hy_conv_w[i], hy_conv_b[i], hy_skip[i], b, lc, b)
            y_c = _hy_conv(u_c, fwd_c, inv_c, hre_c, him_c)
            cat = lambda a_lat, a_ctx: jnp.concatenate([a_lat, a_ctx.reshape(pc, l, a_ctx.shape[-1])], axis=0)
            o_b, y_h, x0_h, t_h = cat(o_b, o_b_ctx), cat(y_h, y_c), cat(x0_h, x0_c), cat(t_h, t_c)

        wa_rows = np.concatenate([np.concatenate([64 * j + np.arange(64), 64 * (j + 4) + np.arange(64)])
                                  for j in range(4)])
        x1 = _merge(x_all, o_a, o_b, y_h, x0_h, t_h, gates, g1,
                    w_branch_a[i][wa_rows].astype(BF16), w_branch_b[i].astype(BF16),
                    w_branch_c[i].astype(BF16), w_out[i].astype(BF16),
                    ln1_g[i].reshape(1, d), ln1_b[i].reshape(1, d), npb)

        rw = jnp.pad(router_w[i], ((0, 0), (0, LANES - N_EXPERTS)))
        rb = jnp.concatenate([router_b[i], jnp.full((LANES - N_EXPERTS,), NEG_BIG, F32)]).reshape(1, LANES)
        b1p = jnp.concatenate([exp_b1[i][..., 0::2], exp_b1[i][..., 1::2]], axis=-1)[:, None, :]
        x_all = _moe(x1, sh2, sc2, g2, rw, rb, exp_w1[i], b1p, exp_w2[i], exp_b2[i][:, None, :], perm,
                     ln2_g[i].reshape(1, d), ln2_b[i].reshape(1, d), npb)
    return x_all
```

```python
import functools
import math

import numpy as np
import jax
import jax.numpy as jnp
from jax import lax
from jax.experimental import pallas as pl
from jax.experimental.pallas import tpu as pltpu

F32 = jnp.float32
BF16 = jnp.bfloat16

D_MODEL = 1024
DEPTH = 2
GRID_W = 64
ATT_HEADS = 8
ATT_KV_HEADS = 2
ATT_HEAD_DIM = 64
ROPE_THETA = 10000.0
GLA_HEADS = 4
GLA_DK = 64
GLA_DV = 128
GLA_RANK = 16
GLA_TAU = 16.0
GLA_CHUNK = 64
HY_WIDTH = 512
HY_POS_FREQS = 16
HY_DECAY_SLOW = math.log(1e-2) / 1.5
HY_DECAY_FAST = math.log(1e-2) / 0.3
N_EXPERTS = 32
TOP_K = 4
EXPERT_HIDDEN = D_MODEL
SWIGLU_LIMIT = 7.0
SWIGLU_ALPHA = 1.702
DEEPNORM_ALPHA = (2 * DEPTH) ** 0.25
LN_EPS = 1e-5
RMS_EPS = 1e-6

ATT_Q_W = ATT_HEADS * ATT_HEAD_DIM
ATT_KV_W = ATT_KV_HEADS * ATT_HEAD_DIM
GLA_K_W = GLA_HEADS * GLA_DK
GLA_V_W = GLA_HEADS * GLA_DV
GLA_GROUP_W = 1152
LANES = 128
NEG_BIG = -3.0e38

VMEM_LIMIT = 56 * 1024 * 1024


def _cparams(*sem):
    return pltpu.CompilerParams(dimension_semantics=sem, vmem_limit_bytes=VMEM_LIMIT)


def _dot(a, b):
    return jnp.dot(a, b, preferred_element_type=F32)


def _dot_nt(a, b):
    return lax.dot_general(a, b, (((1,), (1,)), ((), ())), preferred_element_type=F32)


def _dot_tn(a, b):
    return lax.dot_general(a, b, (((0,), (0,)), ((), ())), preferred_element_type=F32)


def _split2(x):
    hi = x.astype(BF16)
    lo = (x - hi.astype(F32)).astype(BF16)
    return hi, lo


def _split3(x):
    hi = x.astype(BF16)
    r = x - hi.astype(F32)
    mid = r.astype(BF16)
    lo = (r - mid.astype(F32)).astype(BF16)
    return hi, mid, lo


def _dot3(a, b):
    ah, al = _split2(a)
    bh, bl = _split2(b)
    return _dot(ah, bh) + (_dot(ah, bl) + _dot(al, bh))


def _dot_exact_lhs(a_bf16, b):
    b0, b1, b2 = _split3(b)
    return _dot(a_bf16, b0) + (_dot(a_bf16, b1) + _dot(a_bf16, b2))


def _ln(x):
    mu = jnp.mean(x, axis=-1, keepdims=True)
    xc = x - mu
    var = jnp.mean(xc * xc, axis=-1, keepdims=True)
    return xc * lax.rsqrt(var + LN_EPS)


def _sigmoid(x):
    return 1.0 / (1.0 + jnp.exp(-x))


def _mod_kernel(c_ref, w_ref, b_ref, o_ref):
    c = c_ref[...]
    o_ref[...] = _dot3(c * _sigmoid(c), w_ref[...]) + b_ref[...]


def _modulation(cc, w_mod, b_mod):
    rows, d = cc.shape
    n = w_mod.shape[1]
    tn = 1024
    return pl.pallas_call(
        _mod_kernel,
        grid=(n // tn,),
        in_specs=[pl.BlockSpec((rows, d), lambda j: (0, 0)),
                  pl.BlockSpec((d, tn), lambda j: (0, j)),
                  pl.BlockSpec((1, tn), lambda j: (0, j))],
        out_specs=pl.BlockSpec((rows, tn), lambda j: (0, j)),
        out_shape=jax.ShapeDtypeStruct((rows, n), F32),
        compiler_params=_cparams("arbitrary"),
        name="modulation",
    )(cc, w_mod, b_mod.reshape(1, n))


def _head_norm_rope(t, gain, rc, rs1, rs2, bd):
    sh, sl = _split2(t * t)
    ss = _dot(sh, bd) + _dot(sl, bd)
    tn = t * lax.rsqrt(ss * (1.0 / ATT_HEAD_DIM) + RMS_EPS) * gain
    return tn * rc + pltpu.roll(tn, 96, 1) * rs1 + pltpu.roll(tn, 32, 1) * rs2


def _pair_specs(tm, w, n_split, n_i):
    first = pl.BlockSpec((1, tm, w), lambda b, i: (jnp.minimum(b, n_split - 1), jnp.where(b < n_split, i, n_i - 1), 0))
    rest = pl.BlockSpec((1, tm, w), lambda b, i: (jnp.maximum(b - n_split, 0), jnp.where(b >= n_split, i, 0), 0))
    return [first, rest]


def _pick(use_rest, first_ref, rest_ref):
    return jnp.where(use_rest, rest_ref[0], first_ref[0])


def _inproj_kernel(x_ref, xr_ref, sh_ref, sc_ref, watt_ref, wgla_ref, wog_ref, why_ref, wgt_ref,
                   qg_ref, kg_ref, rc_ref, rs1_ref, rs2_ref, bd_ref,
                   qn_ref, kn_ref, v_ref, gla_ref, og_ref, hy_ref, gt_ref, *, n_split):
    x = _pick(pl.program_id(0) >= n_split, x_ref, xr_ref)
    u = _ln(x) * (1.0 + sc_ref[0]) + sh_ref[0]
    ub = u.astype(BF16)
    rc, rs1, rs2, bd = rc_ref[0], rs1_ref[0], rs2_ref[0], bd_ref[...]
    att = _dot(ub, watt_ref[...])
    for j in range(ATT_Q_W // LANES):
        t = att[:, LANES * j:LANES * (j + 1)]
        qn = _head_norm_rope(t, qg_ref[...], rc, rs1, rs2, bd) * (ATT_HEAD_DIM ** -0.5)
        qn_ref[0, :, LANES * j:LANES * (j + 1)] = qn.astype(BF16)
    kn = _head_norm_rope(att[:, ATT_Q_W:ATT_Q_W + LANES], kg_ref[...], rc, rs1, rs2, bd)
    kn_ref[0] = kn.astype(BF16)
    v_ref[0] = att[:, ATT_Q_W + LANES:].astype(BF16)
    gla_ref[0] = _dot(ub, wgla_ref[...])
    og_ref[0] = _dot(ub, wog_ref[...]).astype(BF16)
    hy_ref[0] = _dot(ub, why_ref[...]).astype(BF16)
    gt_ref[0] = _dot(ub, wgt_ref[...]).astype(BF16)


def _inproj(x_first, x_rest, sh, sc, wts, qg, kg, rope, bd, n_lat):
    n_split, l, d = x_first.shape
    p = sh.shape[0]
    tm = min(256, l)
    watt, wgla, wog, why, wgt = wts
    rc, rs1, rs2 = rope
    const = lambda a: pl.BlockSpec(a.shape, lambda b, i: (0,) * a.ndim)
    row = lambda w: pl.BlockSpec((1, tm, w), lambda b, i: (b, i, 0))
    tab = pl.BlockSpec((1, tm, LANES), lambda b, i: (jnp.where(b >= n_lat, 1, 0), i, 0))
    vec = pl.BlockSpec((1, 1, d), lambda b, i: (b, 0, 0))
    widths = (ATT_Q_W, LANES, LANES, GLA_GROUP_W, GLA_V_W, 3 * HY_WIDTH, 3 * d)
    dtypes = (BF16, BF16, BF16, F32, BF16, BF16, BF16)
    return pl.pallas_call(
        functools.partial(_inproj_kernel, n_split=n_split),
        grid=(p, l // tm),
        in_specs=_pair_specs(tm, d, n_split, l // tm) + [
            vec, vec, const(watt), const(wgla), const(wog), const(why), const(wgt),
            const(qg), const(kg), tab, tab, tab, const(bd)],
        out_specs=[row(w) for w in widths],
        out_shape=[jax.ShapeDtypeStruct((p, l, w), dt) for w, dt in zip(widths, dtypes)],
        compiler_params=_cparams("parallel", "arbitrary"),
        name="inproj",
    )(x_first, x_rest, sh, sc, watt, wgla, wog, why, wgt, qg, kg, rc, rs1, rs2, bd)


def _attend_group(qg, ks, vs, lo):
    zero = jnp.zeros_like(qg)
    outs = []
    for qh in (jnp.where(lo, qg, zero), jnp.where(lo, zero, qg)):
        ss = [_dot_nt(qh, k) for k in ks]
        m = ss[0].max(axis=-1, keepdims=True)
        for s in ss[1:]:
            m = jnp.maximum(m, s.max(axis=-1, keepdims=True))
        den = 0.0
        acc = 0.0
        for s, v in zip(ss, vs):
            p = jnp.exp(s - m)
            den = den + p.sum(axis=-1, keepdims=True)
            acc = acc + _dot(p.astype(BF16), v)
        outs.append(acc / den)
    return jnp.where(lo, outs[0], outs[1])


def _attn_kernel(q_ref, k_ref, v_ref, kc_ref, vc_ref, o_ref, *, n_lat_tiles, with_ctx):
    i = pl.program_id(1)
    tq = q_ref.shape[1]
    lo = lax.broadcasted_iota(jnp.int32, (tq, LANES), 1) < ATT_HEAD_DIM

    def run(ks, vs):
        for j in range(ATT_Q_W // LANES):
            o = _attend_group(q_ref[0, :, LANES * j:LANES * (j + 1)], ks, vs, lo)
            o_ref[0, :, LANES * j:LANES * (j + 1)] = o.astype(BF16)

    if with_ctx:
        @pl.when(i < n_lat_tiles)
        def _():
            run([k_ref[0], kc_ref[0]], [v_ref[0], vc_ref[0]])

        @pl.when(i == n_lat_tiles)
        def _():
            run([kc_ref[0]], [vc_ref[0]])
    else:
        run([k_ref[0], kc_ref[0]], [v_ref[0], vc_ref[0]])


def _attention(qn, kn, v, n_lat, lc, with_ctx):
    p, l, _ = qn.shape
    tq = lc
    r = l // lc
    nt = l // tq
    ctx_idx = lambda b: (n_lat + b // r, b % r, 0)
    if with_ctx:
        qmap = lambda b, i: (jnp.where(i < nt, b, n_lat + b // r), jnp.where(i < nt, i, b % r), 0)
    else:
        qmap = lambda b, i: (b, i, 0)
    return pl.pallas_call(
        functools.partial(_attn_kernel, n_lat_tiles=nt, with_ctx=with_ctx),
        grid=(n_lat, nt + (1 if with_ctx else 0)),
        in_specs=[pl.BlockSpec((1, tq, ATT_Q_W), qmap),
                  pl.BlockSpec((1, l, LANES), lambda b, i: (b, 0, 0)),
                  pl.BlockSpec((1, l, LANES), lambda b, i: (b, 0, 0)),
                  pl.BlockSpec((1, lc, LANES), lambda b, i: ctx_idx(b)),
                  pl.BlockSpec((1, lc, LANES), lambda b, i: ctx_idx(b))],
        out_specs=pl.BlockSpec((1, tq, ATT_Q_W), qmap),
        out_shape=jax.ShapeDtypeStruct((p if with_ctx else n_lat, l, ATT_Q_W), BF16),
        compiler_params=_cparams("parallel", "arbitrary"),
        name="attention",
    )(qn, kn, v, kn, v)


def _gla_segment(src_ref, og_ref, out_ref, acc_ref, st_ref, qd_ref, ke_ref, vb_ref, dec_ref,
                 wdec_ref, bdec_ref, gn_ref, ls):
    c = GLA_CHUNK
    nc = ls // c
    rb = min(256, ls)

    bi = lax.broadcasted_iota(jnp.int32, (rb, rb), 0)
    bj = lax.broadcasted_iota(jnp.int32, (rb, rb), 1)
    same = (bi // c) == (bj // c)
    att_masks = (same & (bi >= bj), same & (bi <= bj))
    tri_blk = tuple(jnp.where(m, 1.0, 0.0).astype(BF16) for m in att_masks)
    ones_blk = jnp.where(same, 1.0, 0.0).astype(BF16)
    lo_b = lax.broadcasted_iota(jnp.int32, (rb, LANES), 1) < GLA_DK

    def prep(t, carry):
        r0 = pl.multiple_of(t * rb, rb)
        q = src_ref[0, pl.ds(r0, rb), 0:GLA_K_W] * (GLA_DK ** -0.5)
        k = src_ref[0, pl.ds(r0, rb), GLA_K_W:2 * GLA_K_W]
        vb = src_ref[0, pl.ds(r0, rb), 2 * GLA_K_W:2 * GLA_K_W + GLA_V_W].astype(BF16)
        vb_ref[pl.ds(r0, rb), :] = vb
        lr = src_ref[0, pl.ds(r0, rb), 2 * GLA_K_W + GLA_V_W:GLA_GROUP_W]
        la = jax.nn.log_sigmoid(_dot3(lr, wdec_ref[...]) + bdec_ref[...]) * (1.0 / GLA_TAU)
        for d in range(2):
            l0, l1, l2 = _split3(la[:, d * GLA_K_W:(d + 1) * GLA_K_W])
            cum = _dot(tri_blk[d], l0) + (_dot(tri_blk[d], l1) + _dot(tri_blk[d], l2))
            tot = _dot(ones_blk, l0) + (_dot(ones_blk, l1) + _dot(ones_blk, l2))
            qd = (q * jnp.exp(cum)).astype(BF16)
            ki = (k * jnp.exp(-cum)).astype(BF16)
            qd_ref[d, pl.ds(r0, rb), :] = qd
            ke_ref[d, pl.ds(r0, rb), :] = (k * jnp.exp(tot - cum)).astype(BF16)
            dec_ref[d, pl.ds(r0, rb), :] = jnp.exp(tot)
            for h in range(GLA_HEADS):
                g, half = h // 2, h % 2
                sl = slice(LANES * g, LANES * (g + 1))
                zero = jnp.zeros_like(qd[:, sl])
                qh = jnp.where(lo_b, qd[:, sl], zero) if half == 0 else jnp.where(lo_b, zero, qd[:, sl])
                att = jnp.where(att_masks[d], _dot_nt(qh, ki[:, sl]), 0.0)
                o = _dot(att.astype(BF16), vb[:, GLA_DV * h:GLA_DV * (h + 1)])
                if d == 0:
                    acc_ref[pl.ds(r0, rb), GLA_DV * h:GLA_DV * (h + 1)] = o
                else:
                    acc_ref[pl.ds(r0, rb), GLA_DV * h:GLA_DV * (h + 1)] += o
        return carry

    lax.fori_loop(0, ls // rb, prep, 0)

    lo = lax.broadcasted_iota(jnp.int32, (c, LANES), 1) < GLA_DK

    def body(n, carry):
        for d, cn in ((0, n), (1, nc - 1 - n)):
            r0 = pl.multiple_of(cn * c, c)
            dec = dec_ref[d, pl.ds(r0, 1), :]
            for h in range(GLA_HEADS):
                g, half = h // 2, h % 2
                sl = slice(LANES * g, LANES * (g + 1))
                qd = qd_ref[d, pl.ds(r0, c), sl]
                zero = jnp.zeros_like(qd)
                qd = jnp.where(lo, qd, zero) if half == 0 else jnp.where(lo, zero, qd)
                vh = vb_ref[pl.ds(r0, c), GLA_DV * h:GLA_DV * (h + 1)]
                st = st_ref[d * GLA_HEADS + h]
                acc_ref[pl.ds(r0, c), GLA_DV * h:GLA_DV * (h + 1)] += _dot_nt(qd, st.astype(BF16))
                st_ref[d * GLA_HEADS + h] = dec[:, sl] * st + _dot_tn(vh, ke_ref[d, pl.ds(r0, c), sl])
        return carry

    lax.fori_loop(0, nc, body, 0, unroll=2 if nc % 2 == 0 else 1)

    tr = min(256, ls)

    def fin(t, carry):
        r0 = pl.multiple_of(t * tr, tr)
        for h in range(GLA_HEADS):
            sl = slice(GLA_DV * h, GLA_DV * (h + 1))
            o = acc_ref[pl.ds(r0, tr), sl]
            y = o * lax.rsqrt(jnp.mean(o * o, axis=-1, keepdims=True) + RMS_EPS)
            y = y * gn_ref[...]
            og = og_ref[0, pl.ds(r0, tr), sl].astype(F32)
            out_ref[0, pl.ds(r0, tr), sl] = (y * (og * _sigmoid(og))).astype(BF16)
        return carry

    lax.fori_loop(0, ls // tr, fin, 0)


def _gla_kernel(gl_ref, glc_ref, og_ref, ogc_ref, wdec_ref, bdec_ref, gn_ref, o_ref, oc_ref,
                acc_ref, st_ref, qd_ref, ke_ref, vb_ref, dec_ref):
    st_ref[...] = jnp.zeros(st_ref.shape, F32)
    scratch = (acc_ref, st_ref, qd_ref, ke_ref, vb_ref, dec_ref)
    _gla_segment(glc_ref, ogc_ref, oc_ref, *scratch, wdec_ref, bdec_ref, gn_ref, glc_ref.shape[1])
    _gla_segment(gl_ref, og_ref, o_ref, *scratch, wdec_ref, bdec_ref, gn_ref, gl_ref.shape[1])


def _gla(gla, og, wdec, bdec, gn, n_lat, lc):
    p, l, _ = gla.shape
    r = l // lc
    ctx_idx = lambda b: (n_lat + b // r, b % r, 0)
    const = lambda a: pl.BlockSpec(a.shape, lambda b: (0,) * a.ndim)
    return pl.pallas_call(
        _gla_kernel,
        grid=(n_lat,),
        in_specs=[pl.BlockSpec((1, l, GLA_GROUP_W), lambda b: (b, 0, 0)),
                  pl.BlockSpec((1, lc, GLA_GROUP_W), ctx_idx),
                  pl.BlockSpec((1, l, GLA_V_W), lambda b: (b, 0, 0)),
                  pl.BlockSpec((1, lc, GLA_V_W), ctx_idx),
                  const(wdec), const(bdec), const(gn)],
        out_specs=[pl.BlockSpec((1, l, GLA_V_W), lambda b: (b, 0, 0)),
                   pl.BlockSpec((1, lc, GLA_V_W), lambda b: (b, 0, 0))],
        out_shape=[jax.ShapeDtypeStruct((n_lat, l, GLA_V_W), BF16),
                   jax.ShapeDtypeStruct((n_lat, lc, GLA_V_W), BF16)],
        scratch_shapes=[pltpu.VMEM((l, GLA_V_W), F32),
                        pltpu.VMEM((2 * GLA_HEADS, GLA_DV, LANES), F32),
                        pltpu.VMEM((2, l, GLA_K_W), BF16),
                        pltpu.VMEM((2, l, GLA_K_W), BF16), pltpu.VMEM((l, GLA_V_W), BF16),
                        pltpu.VMEM((2, l, GLA_K_W), F32)],
        compiler_params=_cparams("arbitrary"),
        name="gla",
    )(gla, gla, og, og, wdec, bdec, gn)


def _dft_tables(l):
    n = 2 * l
    k = np.arange(l, dtype=np.int64)[:, None]
    t = np.arange(l, dtype=np.int64)[None, :]
    ang = 2.0 * np.pi * ((k * t) % n).astype(np.float64) / n
    fre = np.cos(ang)
    fim = -np.sin(ang)
    fim[0, :] = np.where(np.arange(l) % 2 == 0, 1.0, -1.0)
    fwd = np.concatenate([fre, fim], axis=0)
    return jnp.asarray(fwd, dtype=BF16), jnp.asarray(fwd.T, dtype=BF16)


def _hy_filter_kernel(z_ref, win_ref, w1_ref, b1_ref, w2_ref, b2_ref, w3_ref, b3_ref, fre_ref, fim_ref,
                      hre_ref, him_ref, hf_ref):
    j = pl.program_id(0)
    l = z_ref.shape[0]

    @pl.when(j == 0)
    def _():
        h = jnp.sin(_dot3(z_ref[...], w1_ref[...]) + b1_ref[...])
        h = jnp.sin(_dot3(h, w2_ref[...]) + b2_ref[...])
        h = _dot3(h, w3_ref[...]) + b3_ref[...]
        win = win_ref[...]
        hf = h[:, :HY_WIDTH] * win
        hb = h[:, HY_WIDTH:] * win
        row = lax.broadcasted_iota(jnp.int32, (l, HY_WIDTH), 0)
        hf_ref[:, :HY_WIDTH] = hf.astype(BF16)
        hf_ref[:, HY_WIDTH:] = jnp.where(row == 0, 0.0, hb).astype(BF16)

    tk = fre_ref.shape[0]
    a = _dot(fre_ref[...], hf_ref[...])
    b = _dot(fim_ref[...], hf_ref[...])
    first = (lax.broadcasted_iota(jnp.int32, (tk, HY_WIDTH), 0) == 0) & (j == 0)
    scl = jnp.where(first, 1.0 / (2 * l), 2.0 / (2 * l))
    hre_ref[...] = (a[:, :HY_WIDTH] + a[:, HY_WIDTH:]) * scl
    him_ref[...] = (b[:, :HY_WIDTH] + jnp.where(first, 1.0, -1.0) * b[:, HY_WIDTH:]) * scl


def _hy_filter(l, fw1, fb1, fw2, fb2, fw3, fb3, fwd):
    t = np.arange(l, dtype=np.float32)[:, None] / np.float32(l)
    freqs = np.arange(1, HY_POS_FREQS + 1, dtype=np.float32)
    z = np.concatenate([t, np.cos(2.0 * math.pi * freqs * t), np.sin(2.0 * math.pi * freqs * t)], axis=-1)
    z = np.pad(z, ((0, 0), (0, LANES - z.shape[1])))
    hid = fw2.shape[0]
    fw1 = jnp.pad(fw1, ((0, LANES - fw1.shape[0]), (0, LANES - hid)))
    fb1 = jnp.pad(fb1, (0, LANES - hid))
    fw2 = jnp.pad(fw2, ((0, LANES - hid), (0, LANES - hid)))
    fb2 = jnp.pad(fb2, (0, LANES - hid))
    fw3 = jnp.pad(fw3, ((0, LANES - hid), (0, 0)))
    deltas = np.abs(np.linspace(HY_DECAY_SLOW, HY_DECAY_FAST, HY_WIDTH, dtype=np.float32))
    win = np.exp(-t * deltas).astype(np.float32)
    tk = min(512, l)
    nk = l // tk
    const = lambda a: pl.BlockSpec(a.shape, lambda j: (0,) * a.ndim)
    args = (jnp.asarray(z), jnp.asarray(win), fw1, fb1.reshape(1, -1), fw2, fb2.reshape(1, -1),
            fw3, fb3.reshape(1, -1))
    return pl.pallas_call(
        _hy_filter_kernel,
        grid=(nk,),
        in_specs=[const(a) for a in args] + [pl.BlockSpec((tk, l), lambda j: (j, 0)),
                                              pl.BlockSpec((tk, l), lambda j: (nk + j, 0))],
        out_specs=[pl.BlockSpec((tk, HY_WIDTH), lambda j: (j, 0))] * 2,
        out_shape=[jax.ShapeDtypeStruct((l, HY_WIDTH), F32)] * 2,
        scratch_shapes=[pltpu.VMEM((l, 2 * HY_WIDTH), BF16)],
        compiler_params=_cparams("arbitrary"),
        name="hyena_filter",
    )(*args, fwd, fwd)


def _hy_pre_kernel(p0_ref, p1_ref, p2_ref, w0_ref, w1_ref, w2_ref, b0_ref, b1_ref, b2_ref, skip_ref,
                   u_ref, x0_ref, t_ref):
    ls = p0_ref.shape[1]
    row = lax.broadcasted_iota(jnp.int32, (ls, LANES), 0)

    def conv(p_ref, w_ref, b_ref):
        p = p_ref[0].astype(F32)
        prev = jnp.where(row == 0, 0.0, pltpu.roll(p, 1, 0))
        nxt = jnp.where(row == ls - 1, 0.0, pltpu.roll(p, ls - 1, 0))
        return prev * w_ref[0:1, :] + p * w_ref[1:2, :] + nxt * w_ref[2:3, :] + b_ref[...]

    x0 = conv(p0_ref, w0_ref, b0_ref)
    u = conv(p1_ref, w1_ref, b1_ref) * conv(p2_ref, w2_ref, b2_ref)
    u_ref[0] = u.astype(BF16)
    x0_ref[0] = x0.astype(BF16)
    t_ref[0] = (x0 * u * skip_ref[...]).astype(BF16)


def _hy_pre(hy, conv_w, conv_b, skip, nseq, ls, n_lat):
    p, l, _ = hy.shape
    r = l // ls
    nj = HY_WIDTH // LANES
    base = 0 if ls == l else n_lat
    src = lambda off: pl.BlockSpec((1, ls, LANES), lambda b, j: (base + b // r, b % r, off * nj + j))
    wsp = lambda off: pl.BlockSpec((3, LANES), lambda b, j: (0, off * nj + j))
    bsp = lambda off: pl.BlockSpec((1, LANES), lambda b, j: (0, off * nj + j))
    out = pl.BlockSpec((1, ls, LANES), lambda b, j: (b, 0, j))
    cb = conv_b.reshape(1, -1)
    return pl.pallas_call(
        _hy_pre_kernel,
        grid=(nseq, nj),
        in_specs=[src(0), src(1), src(2), wsp(0), wsp(1), wsp(2), bsp(0), bsp(1), bsp(2), bsp(0)],
        out_specs=[out] * 3,
        out_shape=[jax.ShapeDtypeStruct((nseq, ls, HY_WIDTH), BF16)] * 3,
        compiler_params=_cparams("parallel", "arbitrary"),
        name="hyena_pre",
    )(hy, hy, hy, conv_w, conv_w, conv_w, cb, cb, cb, skip.reshape(1, -1))


def _hy_conv_kernel(u_ref, fre_ref, fim_ref, ire_ref, iim_ref, hre_ref, him_ref, y_ref):
    j = pl.program_id(1)
    u = u_ref[0]
    ure = _dot(fre_ref[...], u)
    uim = _dot(fim_ref[...], u)
    hre, him = hre_ref[...], him_ref[...]
    first = (lax.broadcasted_iota(jnp.int32, ure.shape, 0) == 0) & (j == 0)
    yre = jnp.where(first, ure * hre, ure * hre - uim * him)
    yim = jnp.where(first, uim * him, ure * him + uim * hre)
    y = _dot(ire_ref[...], yre.astype(BF16)) + _dot(iim_ref[...], yim.astype(BF16))

    @pl.when(j == 0)
    def _():
        y_ref[0] = y

    @pl.when(j > 0)
    def _():
        y_ref[0] += y


def _hy_conv(u, fwd, inv, hre, him):
    nseq, ls, _ = u.shape
    tk = min(512, ls)
    nk = ls // tk
    return pl.pallas_call(
        _hy_conv_kernel,
        grid=(nseq, nk),
        in_specs=[pl.BlockSpec((1, ls, HY_WIDTH), lambda b, j: (b, 0, 0)),
                  pl.BlockSpec((tk, ls), lambda b, j: (j, 0)),
                  pl.BlockSpec((tk, ls), lambda b, j: (nk + j, 0)),
                  pl.BlockSpec((ls, tk), lambda b, j: (0, j)),
                  pl.BlockSpec((ls, tk), lambda b, j: (0, nk + j)),
                  pl.BlockSpec((tk, HY_WIDTH), lambda b, j: (j, 0)),
                  pl.BlockSpec((tk, HY_WIDTH), lambda b, j: (j, 0))],
        out_specs=pl.BlockSpec((1, ls, HY_WIDTH), lambda b, j: (b, 0, 0)),
        out_shape=jax.ShapeDtypeStruct((nseq, ls, HY_WIDTH), F32),
        compiler_params=_cparams("parallel", "arbitrary"),
        name="hyena_conv",
    )(u, fwd, fwd, inv, inv, hre, him)


def _merge_kernel(x_ref, xr_ref, oa_ref, ob_ref, obr_ref, y_ref, yr_ref, x0_ref, x0r_ref, t_ref, tr_ref,
                  gt_ref, g1_ref, wa_ref, wb_ref, wc_ref, wo_ref, lg_ref, lb_ref, o_ref, *, n_split):
    d = x_ref.shape[2]
    rest = pl.program_id(0) >= n_split
    x = _pick(rest, x_ref, xr_ref)
    ob = _pick(rest, ob_ref, obr_ref)
    oc = (_pick(rest, x0_ref, x0r_ref).astype(F32) * _pick(rest, y_ref, yr_ref)
          + _pick(rest, t_ref, tr_ref).astype(F32)).astype(BF16)
    gt = gt_ref[0]
    m = (_sigmoid(gt[:, 0:d].astype(F32)) * _dot(oa_ref[0], wa_ref[...])
         + _sigmoid(gt[:, d:2 * d].astype(F32)) * _dot(ob, wb_ref[...])
         + _sigmoid(gt[:, 2 * d:3 * d].astype(F32)) * _dot(oc, wc_ref[...]))
    mix = _dot(m.astype(BF16), wo_ref[...])
    o_ref[0] = _ln(DEEPNORM_ALPHA * x + g1_ref[0] * mix) * lg_ref[...] + lb_ref[...]


def _merge(x, o_a, o_b, y, x0, t, gates, g1, wa, wb, wc, wo, lg, lb, npb):
    n_split, l, d = x[0].shape
    tm = min(512, l)
    n_i = l // tm
    row = lambda w: pl.BlockSpec((1, tm, w), lambda b, i: (b, i, 0))
    const = lambda a: pl.BlockSpec(a.shape, lambda b, i: (0,) * a.ndim)
    pair = lambda w: _pair_specs(tm, w, n_split, n_i)
    return pl.pallas_call(
        functools.partial(_merge_kernel, n_split=n_split),
        grid=(npb, n_i),
        in_specs=pair(d) + [row(ATT_Q_W)] + pair(GLA_V_W) + pair(HY_WIDTH) + pair(HY_WIDTH) + pair(HY_WIDTH) + [
            row(3 * d), pl.BlockSpec((1, 1, d), lambda b, i: (b, 0, 0)),
            const(wa), const(wb), const(wc), const(wo), const(lg), const(lb)],
        out_specs=row(d),
        out_shape=jax.ShapeDtypeStruct((npb, l, d), F32),
        compiler_params=_cparams("parallel", "arbitrary"),
        name="merge",
    )(*x, o_a, *o_b, *y, *x0, *t, gates, g1, wa, wb, wc, wo, lg, lb)


MOE_T = 256
MOE_G = 16
MOE_TM = 1024
MOE_SLOTS = TOP_K * MOE_T + 512
AUG_W = D_MODEL + LANES


def _route_kernel(x_ref, sh_ref, sc_ref, rw_ref, rb_ref, xa_ref, sel_ref, cnt_ref):
    d = x_ref.shape[2]
    u = _ln(x_ref[0]) * (1.0 + sc_ref[0]) + sh_ref[0]
    xa_ref[0, :, 0:d] = u.astype(BF16)
    logits = _dot3(u, rw_ref[...]) + rb_ref[...]
    lane = lax.broadcasted_iota(jnp.int32, logits.shape, 1).astype(F32)
    work = logits
    hots, vals = [], []
    for _ in range(TOP_K):
        m = work.max(axis=-1, keepdims=True)
        idx = jnp.where(work == m, lane, float(LANES)).min(axis=-1, keepdims=True)
        hot = lane == idx
        hots.append(hot)
        vals.append(m)
        work = jnp.where(hot, NEG_BIG, work)
    es = [jnp.exp(v - vals[0]) for v in vals]
    den = es[0] + es[1] + es[2] + es[3]
    comb = jnp.zeros_like(logits)
    sel = jnp.zeros_like(logits)
    for hot, e in zip(hots, es):
        comb = comb + jnp.where(hot, e / den, 0.0)
        sel = sel + jnp.where(hot, 1.0, 0.0)
    c0, c1, c2 = _split3(comb)
    aug = c0.astype(F32) + pltpu.roll(c1.astype(F32), 32, 1) + pltpu.roll(c2.astype(F32), 64, 1)
    xa_ref[0, :, d:d + LANES] = aug.astype(BF16)
    sel_ref[0] = sel.astype(BF16)
    cnt_ref[0] = sel.sum(axis=0, keepdims=True)


def _route(x_all, sh, sc, rw, rb, npb):
    _, l, d = x_all.shape
    t = MOE_T
    nt_b = l // t
    row = lambda w: pl.BlockSpec((1, t, w), lambda b, i: (b, i, 0))
    vec = pl.BlockSpec((1, 1, d), lambda b, i: (b, 0, 0))
    const = lambda a: pl.BlockSpec(a.shape, lambda b, i: (0,) * a.ndim)
    return pl.pallas_call(
        _route_kernel,
        grid=(npb, nt_b),
        in_specs=[row(d), vec, vec, const(rw), const(rb)],
        out_specs=[row(AUG_W), row(LANES), pl.BlockSpec((1, 1, LANES), lambda b, i: (b * nt_b + i, 0, 0))],
        out_shape=[jax.ShapeDtypeStruct((npb, l, AUG_W), BF16), jax.ShapeDtypeStruct((npb, l, LANES), BF16),
                   jax.ShapeDtypeStruct((npb * nt_b, 1, LANES), F32)],
        compiler_params=_cparams("parallel", "arbitrary"),
        name="route",
    )(x_all, sh, sc, rw, rb)


def _group_copies(j, cntp_ref, loc_ref, goff_ref, make_copy):
    def per_expert(e, total):
        n = cntp_ref[j * N_EXPERTS + e] // MOE_G
        loc = loc_ref[j * N_EXPERTS + e]
        off = goff_ref[j * N_EXPERTS + e]

        def per_chunk(c, carry):
            make_copy(pl.multiple_of(loc + c * MOE_G, MOE_G), pl.multiple_of(off + c * MOE_G, MOE_G)).start()
            return carry

        lax.fori_loop(0, n, per_chunk, 0)
        return total + n

    return lax.fori_loop(0, N_EXPERTS, per_expert, 0)


def _wait_copies(n, make_copy):
    def wait_one(c, carry):
        make_copy(0, 0).wait()
        return carry

    lax.fori_loop(0, n, wait_one, 0)


def _dispatch_kernel(cntp_ref, loc_ref, goff_ref, nch_ref, tail_ref, tailn_ref,
                     xa_ref, sel_ref, locv_ref, pt_ref, xs_ref, stage_ref, zero_ref, sems):
    j = pl.program_id(0)
    nt = pl.num_programs(0)
    buf = j % 2
    sem = sems.at[buf]
    stage = stage_ref.at[buf]
    t = MOE_T

    def out_copy(src, dst):
        return pltpu.make_async_copy(stage.at[pl.ds(src, MOE_G)], xs_ref.at[pl.ds(dst, MOE_G)], sem)

    @pl.when(j >= 2)
    def _():
        _wait_copies(nch_ref[jnp.maximum(j - 2, 0)], out_copy)

    sel = sel_ref[0]
    ri = lax.broadcasted_iota(jnp.int32, (t, t), 0)
    ci = lax.broadcasted_iota(jnp.int32, (t, t), 1)
    rank = _dot(jnp.where(ci < ri, 1.0, 0.0).astype(BF16), sel)
    dest = rank + locv_ref[0]
    lane = lax.broadcasted_iota(jnp.int32, (t, LANES), 1).astype(F32)
    slot = lax.broadcasted_iota(jnp.int32, (t, MOE_SLOTS), 1)
    avail = sel.astype(F32)
    pt = jnp.zeros((t, MOE_SLOTS), F32)
    for _ in range(TOP_K):
        ek = jnp.where(avail > 0.0, lane, float(LANES)).min(axis=-1, keepdims=True)
        hot = lane == ek
        dk = jnp.where(hot, dest, 0.0).sum(axis=-1, keepdims=True).astype(jnp.int32)
        pt = pt + jnp.where(slot == dk, 1.0, 0.0)
        avail = jnp.where(hot, 0.0, avail)
    ptb = pt.astype(BF16)
    pt_ref[0] = ptb
    stage[...] = _dot_tn(ptb, xa_ref[0]).astype(BF16)
    _group_copies(j, cntp_ref, loc_ref, goff_ref, out_copy)

    @pl.when(j == nt - 1)
    def _():
        _wait_copies(nch_ref[j], out_copy)

        @pl.when(j >= 1)
        def _():
            other = sems.at[1 - buf]
            _wait_copies(nch_ref[jnp.maximum(j - 1, 0)], lambda s, d: pltpu.make_async_copy(
                stage_ref.at[1 - buf].at[pl.ds(s, MOE_G)], xs_ref.at[pl.ds(d, MOE_G)], other))

        zero_ref[...] = jnp.zeros(zero_ref.shape, BF16)

        def zcopy(dst):
            return pltpu.make_async_copy(zero_ref.at[pl.ds(0, MOE_G)], xs_ref.at[pl.ds(dst, MOE_G)], sem)

        def zcopy_big(dst):
            return pltpu.make_async_copy(zero_ref, xs_ref.at[pl.ds(dst, MOE_TM)], sem)

        def per_expert(e, tot):
            n = tailn_ref[e]
            base = tail_ref[e]

            def per_chunk(c, carry):
                zcopy(pl.multiple_of(base + c * MOE_G, MOE_G)).start()
                return carry

            lax.fori_loop(0, n, per_chunk, 0)
            return tot + n

        ztotal = lax.fori_loop(0, N_EXPERTS, per_expert, 0)

        def zwait(c, carry):
            zcopy(0).wait()
            return carry

        lax.fori_loop(0, ztotal, zwait, 0)

        nbig = tailn_ref[N_EXPERTS]
        big0 = tail_ref[N_EXPERTS]

        def big_start(c, carry):
            zcopy_big(pl.multiple_of(big0 + c * MOE_TM, MOE_TM)).start()
            return carry

        def big_wait(c, carry):
            zcopy_big(0).wait()
            return carry

        lax.fori_loop(0, nbig, big_start, 0)
        lax.fori_loop(0, nbig, big_wait, 0)


def _dispatch(xaug, sel, locv, cntp, loc, goff, nch, tail, tailn, s_rows):
    npb, l, _ = xaug.shape
    t = MOE_T
    nt_b = l // t
    nt = npb * nt_b
    tile = lambda w: pl.BlockSpec((1, t, w), lambda j, *_: (j // nt_b, j % nt_b, 0))
    return pl.pallas_call(
        _dispatch_kernel,
        grid_spec=pltpu.PrefetchScalarGridSpec(
            num_scalar_prefetch=6,
            grid=(nt,),
            in_specs=[tile(AUG_W), tile(LANES), pl.BlockSpec((1, 1, LANES), lambda j, *_: (j, 0, 0))],
            out_specs=[pl.BlockSpec((1, t, MOE_SLOTS), lambda j, *_: (j, 0, 0)),
                       pl.BlockSpec(memory_space=pl.ANY)],
            scratch_shapes=[pltpu.VMEM((2, MOE_SLOTS, AUG_W), BF16), pltpu.VMEM((MOE_TM, AUG_W), BF16),
                            pltpu.SemaphoreType.DMA((2,))]),
        out_shape=[jax.ShapeDtypeStruct((nt, t, MOE_SLOTS), BF16), jax.ShapeDtypeStruct((s_rows, AUG_W), BF16)],
        compiler_params=_cparams("arbitrary"),
        name="moe_dispatch",
    )(cntp, loc, goff, nch, tail, tailn, xaug, sel, locv)


def _ffn_kernel(te_ref, na_ref, xs_ref, w1_ref, b1_ref, w2_ref, b2_ref, perm_ref, ys_ref, w1p_ref, w2b_ref):
    i = pl.program_id(0)
    e = te_ref[i]
    active = i < na_ref[0]
    fresh = jnp.logical_or(i == 0, e != te_ref[jnp.maximum(i - 1, 0)])
    hh = EXPERT_HIDDEN // 2

    @pl.when(jnp.logical_and(active, fresh))
    def _():
        for c in range(2):
            blk = w1_ref[0, :, 2 * hh * c:2 * hh * (c + 1)].astype(BF16)
            r = _dot(blk, perm_ref[...])
            w1p_ref[:, hh * c:hh * (c + 1)] = r[:, :hh].astype(BF16)
            w1p_ref[:, EXPERT_HIDDEN + hh * c:EXPERT_HIDDEN + hh * (c + 1)] = r[:, hh:].astype(BF16)
        w2b_ref[...] = w2_ref[0].astype(BF16)

    @pl.when(active)
    def _():
        x = xs_ref[:, 0:D_MODEL]
        extra = xs_ref[:, D_MODEL:AUG_W].astype(F32)
        lane = lax.broadcasted_iota(jnp.int32, extra.shape, 1)
        mine = jnp.logical_and(lane % N_EXPERTS == e, lane < 3 * N_EXPERTS)
        wslot = jnp.where(mine, extra, 0.0).sum(axis=-1, keepdims=True)
        h = _dot(x, w1p_ref[...]) + b1_ref[0]
        g = jnp.minimum(h[:, :EXPERT_HIDDEN], SWIGLU_LIMIT)
        up = jnp.clip(h[:, EXPERT_HIDDEN:], -SWIGLU_LIMIT, SWIGLU_LIMIT)
        act = g * _sigmoid(SWIGLU_ALPHA * g) * (up + 1.0)
        y = _dot(act.astype(BF16), w2b_ref[...]) + b2_ref[0]
        ys_ref[...] = (wslot * y).astype(BF16)

    @pl.when(jnp.logical_not(active))
    def _():
        ys_ref[...] = jnp.zeros(ys_ref.shape, BF16)


def _ffn(xs, tile_e, n_active, w1, b1p, w2, b2, perm, n_tiles):
    tm = MOE_TM
    d, h2 = w1.shape[1], w1.shape[2]
    once = pl.Buffered(1)
    return pl.pallas_call(
        _ffn_kernel,
        grid_spec=pltpu.PrefetchScalarGridSpec(
            num_scalar_prefetch=2,
            grid=(n_tiles,),
            in_specs=[pl.BlockSpec((tm, AUG_W), lambda i, te, na: (jnp.where(i < na[0], i, 0), 0)),
                      pl.BlockSpec((1, d, h2), lambda i, te, na: (te[i], 0, 0), pipeline_mode=once),
                      pl.BlockSpec((1, 1, h2), lambda i, te, na: (te[i], 0, 0)),
                      pl.BlockSpec((1, h2 // 2, d), lambda i, te, na: (te[i], 0, 0), pipeline_mode=once),
                      pl.BlockSpec((1, 1, d), lambda i, te, na: (te[i], 0, 0)),
                      pl.BlockSpec(perm.shape, lambda i, te, na: (0, 0), pipeline_mode=once)],
            out_specs=pl.BlockSpec((tm, d), lambda i, te, na: (i, 0)),
            scratch_shapes=[pltpu.VMEM((d, h2), BF16), pltpu.VMEM((h2 // 2, d), BF16)]),
        out_shape=jax.ShapeDtypeStruct((n_tiles * tm, d), BF16),
        compiler_params=_cparams("arbitrary"),
        name="moe_ffn",
    )(tile_e, n_active, xs, w1, b1p, w2, b2, perm)


def _combine_kernel(cntp_ref, loc_ref, goff_ref, nch_ref, pt_ref, x_ref, g2_ref, lg_ref, lb_ref, ys_ref,
                    o_ref, stage_ref, sems):
    j = pl.program_id(0)
    nt = pl.num_programs(0)
    buf = j % 2

    def in_copy(b):
        return lambda dst, src: pltpu.make_async_copy(
            ys_ref.at[pl.ds(src, MOE_G)], stage_ref.at[b].at[pl.ds(dst, MOE_G)], sems.at[b])

    @pl.when(j == 0)
    def _():
        stage_ref[...] = jnp.zeros(stage_ref.shape, BF16)
        _group_copies(j, cntp_ref, loc_ref, goff_ref, in_copy(buf))

    @pl.when(j + 1 < nt)
    def _():
        _group_copies(j + 1, cntp_ref, loc_ref, goff_ref, in_copy(1 - buf))

    _wait_copies(nch_ref[j], in_copy(buf))
    y = _dot(pt_ref[0], stage_ref[buf])
    o_ref[0] = _ln(DEEPNORM_ALPHA * x_ref[0] + g2_ref[0] * y) * lg_ref[...] + lb_ref[...]


def _combine(pt, ys, x1, g2, lg, lb, cntp, loc, goff, nch):
    npb, l, d = x1.shape
    t = MOE_T
    nt_b = l // t
    tile = lambda w: pl.BlockSpec((1, t, w), lambda j, *_: (j // nt_b, j % nt_b, 0))
    const = lambda a: pl.BlockSpec(a.shape, lambda j, *_: (0,) * a.ndim)
    return pl.pallas_call(
        _combine_kernel,
        grid_spec=pltpu.PrefetchScalarGridSpec(
            num_scalar_prefetch=4,
            grid=(npb * nt_b,),
            in_specs=[pl.BlockSpec((1, t, MOE_SLOTS), lambda j, *_: (j, 0, 0)), tile(d),
                      pl.BlockSpec((1, 1, d), lambda j, *_: (j // nt_b, 0, 0)), const(lg), const(lb),
                      pl.BlockSpec(memory_space=pl.ANY)],
            out_specs=tile(d),
            scratch_shapes=[pltpu.VMEM((2, MOE_SLOTS, d), BF16), pltpu.SemaphoreType.DMA((2,))]),
        out_shape=jax.ShapeDtypeStruct((npb, l, d), F32),
        compiler_params=_cparams("arbitrary"),
        name="moe_combine",
    )(cntp, loc, goff, nch, pt, x1, g2, lg, lb, ys)


def _moe(x1, sh2, sc2, g2, rw, rb, w1, b1p, w2, b2, perm, lg, lb, npb):
    _, l, d = x1.shape
    n_tok = npb * l
    nt = n_tok // MOE_T
    xaug, sel, cnt = _route(x1, sh2, sc2, rw, rb, npb)

    i32 = jnp.int32
    cnt = cnt[:, 0, :N_EXPERTS].astype(i32)
    cntp = (cnt + (MOE_G - 1)) // MOE_G * MOE_G
    loc = jnp.cumsum(cntp, axis=1) - cntp
    tot = cntp.sum(axis=0)
    seg = (tot + (MOE_TM - 1)) // MOE_TM * MOE_TM
    seg_start = jnp.cumsum(seg) - seg
    goff = seg_start[None, :] + jnp.cumsum(cntp, axis=0) - cntp
    s_max = TOP_K * n_tok + nt * N_EXPERTS * (MOE_G - 1) + N_EXPERTS * (MOE_TM - MOE_G)
    n_tiles = -(-s_max // MOE_TM)
    cum_tiles = jnp.cumsum(seg // MOE_TM)
    tile_e = jnp.minimum((jnp.arange(n_tiles, dtype=i32)[:, None] >= cum_tiles[None, :]).astype(i32).sum(axis=1),
                         N_EXPERTS - 1)
    n_active = cum_tiles[-1:].astype(i32)
    nch = (cntp.sum(axis=1) // MOE_G).astype(i32)
    locv = jnp.pad(loc.astype(F32), ((0, 0), (0, LANES - N_EXPERTS)))[:, None, :]
    flat = lambda a: a.reshape(-1).astype(i32)

    tail = jnp.concatenate([seg_start + tot, n_active * MOE_TM]).astype(i32)
    tailn = jnp.concatenate([(seg - tot) // MOE_G, n_tiles - n_active]).astype(i32)
    pt, xs = _dispatch(xaug, sel, locv, flat(cntp), flat(loc), flat(goff), nch, tail, tailn, n_tiles * MOE_TM)
    ys = _ffn(xs, tile_e, n_active, w1, b1p, w2, b2, perm, n_tiles)
    return _combine(pt, ys, x1, g2, lg, lb, flat(cntp), flat(loc), flat(goff), nch)


def _deinterleave(n):
    return np.concatenate([np.arange(0, n, 2), np.arange(1, n, 2)])


def _rope_tables(l):
    rows = l // GRID_W
    r = np.repeat(np.arange(rows, dtype=np.float32), GRID_W)
    col = np.tile(np.arange(GRID_W, dtype=np.float32), rows)
    axis_dim = ATT_HEAD_DIM // 2
    inv_freq = (ROPE_THETA ** (-np.arange(0, axis_dim, 2, dtype=np.float32) / axis_dim)).astype(np.float32)
    ang = np.concatenate([r[:, None] * inv_freq, col[:, None] * inv_freq], axis=-1)
    c, s = np.cos(ang).astype(np.float32), np.sin(ang).astype(np.float32)
    z = np.zeros_like(s)
    rc = np.tile(np.concatenate([c, c], -1), (1, 2))
    rs1 = np.tile(np.concatenate([-s, z], -1), (1, 2))
    rs2 = np.tile(np.concatenate([z, s], -1), (1, 2))
    ident = (np.ones_like(rc), np.zeros_like(rc), np.zeros_like(rc))
    return tuple(jnp.asarray(np.stack([a, b])) for a, b in zip((rc, rs1, rs2), ident))


def kernel(x, c, ctx, c_ctx, w_mod, b_mod, w_in, att_q_norm, att_k_norm, gla_wdec_f, gla_bdec_f, gla_wdec_b, gla_bdec_b, gla_out_norm, hy_conv_w, hy_conv_b, hy_fw1, hy_fb1, hy_fw2, hy_fb2, hy_fw3, hy_fb3, hy_skip, w_branch_a, w_branch_b, w_branch_c, w_out, ln1_g, ln1_b, router_w, router_b, exp_w1, exp_b1, exp_w2, exp_b2, ln2_g, ln2_b):
    b, l, d = x.shape
    lc = ctx.shape[1]
    assert d == D_MODEL and (b * lc) % l == 0 and l % lc == 0 and lc % GLA_CHUNK == 0
    pc = (b * lc) // l
    p = b + pc

    perm64 = _deinterleave(ATT_HEAD_DIM)
    q_cols = np.concatenate([np.concatenate([64 * j + perm64, 64 * (j + 4) + perm64]) for j in range(4)])
    k_cols = ATT_Q_W + np.concatenate([perm64, 64 + perm64])
    v_cols = ATT_Q_W + ATT_KV_W + np.arange(ATT_KV_W)
    att_cols = np.concatenate([q_cols, k_cols, v_cols])
    o0 = ATT_Q_W + 2 * ATT_KV_W
    gla_cols = o0 + np.concatenate([np.arange(0, 2 * GLA_K_W + GLA_V_W + 2 * GLA_RANK)])
    og0 = o0 + 2 * GLA_K_W + GLA_V_W + 2 * GLA_RANK
    hy0 = og0 + GLA_V_W
    gt0 = hy0 + 3 * HY_WIDTH
    gain_perm = np.concatenate([perm64, perm64])
    bd = jnp.asarray(np.kron(np.eye(2), np.ones((64, 64))), dtype=BF16)
    rope = _rope_tables(l)
    fwd_l, inv_l = _dft_tables(l)
    fwd_c, inv_c = _dft_tables(lc)
    cmap = np.concatenate([np.arange(b), np.full(pc, b)])
    hh = EXPERT_HIDDEN // 2
    perm_np = np.zeros((2 * hh, 2 * hh), np.float32)
    perm_np[2 * np.arange(hh), np.arange(hh)] = 1.0
    perm_np[2 * np.arange(hh) + 1, hh + np.arange(hh)] = 1.0
    perm = jnp.asarray(perm_np, dtype=BF16)

    x_pair = (x, ctx.reshape(pc, l, d))
    cc = jnp.concatenate([c, c_ctx[None], jnp.zeros((7, d), F32)], axis=0)

    for i in range(DEPTH):
        with_ctx = i < DEPTH - 1
        npb = p if with_ctx else b
        mod = _modulation(cc, w_mod[i], b_mod[i])[cmap]
        sh1, sc1, g1, sh2, sc2, g2 = [m[:, None, :] for m in jnp.split(mod, 6, axis=-1)]

        wi = w_in[i]
        watt = wi[:, att_cols].astype(BF16)
        wgla = jnp.pad(wi[:, gla_cols], ((0, 0), (0, GLA_GROUP_W - gla_cols.size))).astype(BF16)
        wog = wi[:, og0:hy0].astype(BF16)
        why = wi[:, hy0:gt0].astype(BF16)
        wgt = wi[:, gt0:].astype(BF16)
        qg = att_q_norm[i][gain_perm].reshape(1, LANES)
        kg = att_k_norm[i][gain_perm].reshape(1, LANES)
        qn, kn, v, gla, og, hy, gates = _inproj(*x_pair, sh1, sc1, (watt, wgla, wog, why, wgt), qg, kg, rope, bd, b)

        o_a = _attention(qn, kn, v, b, lc, with_ctx)

        wdec = jnp.zeros((LANES, 2 * GLA_K_W), F32)
        wdec = wdec.at[0:GLA_RANK, 0:GLA_K_W].set(gla_wdec_f[i])
        wdec = wdec.at[GLA_RANK:2 * GLA_RANK, GLA_K_W:].set(gla_wdec_b[i])
        bdec = jnp.concatenate([gla_bdec_f[i], gla_bdec_b[i]]).reshape(1, -1)
        o_b, o_b_ctx = _gla(gla, og, wdec, bdec, gla_out_norm[i].reshape(1, -1), b, lc)

        hre, him = _hy_filter(l, hy_fw1[i], hy_fb1[i], hy_fw2[i], hy_fb2[i], hy_fw3[i], hy_fb3[i], fwd_l)
        u_h, x0_h, t_h = _hy_pre(hy, hy_conv_w[i], hy_conv_b[i], hy_skip[i], b, l, b)
        y_h = _hy_conv(u_h, fwd_l, inv_l, hre, him)
        if with_ctx:
            hre_c, him_c = _hy_filter(lc, hy_fw1[i], hy_fb1[i], hy_fw2[i], hy_fb2[i], hy_fw3[i], hy_fb3[i], fwd_c)
            u_c, x0_c, t_c = _hy_pre(hy, hy_conv_w[i], hy_conv_b[i], hy_skip[i], b, lc, b)
            y_c = _hy_conv(u_c, fwd_c, inv_c, hre_c, him_c)
            fold = lambda a_ctx: a_ctx.reshape(pc, l, a_ctx.shape[-1])
            pairs = [(o_b, fold(o_b_ctx)), (y_h, fold(y_c)), (x0_h, fold(x0_c)), (t_h, fold(t_c))]
        else:
            pairs = [(a, a) for a in (o_b, y_h, x0_h, t_h)]

        wa_rows = np.concatenate([np.concatenate([64 * j + np.arange(64), 64 * (j + 4) + np.arange(64)])
                                  for j in range(4)])
        x1 = _merge(x_pair, o_a, *pairs, gates, g1,
                    w_branch_a[i][wa_rows].astype(BF16), w_branch_b[i].astype(BF16),
                    w_branch_c[i].astype(BF16), w_out[i].astype(BF16),
                    ln1_g[i].reshape(1, d), ln1_b[i].reshape(1, d), npb)

        rw = jnp.pad(router_w[i], ((0, 0), (0, LANES - N_EXPERTS)))
        rb = jnp.concatenate([router_b[i], jnp.full((LANES - N_EXPERTS,), NEG_BIG, F32)]).reshape(1, LANES)
        b1p = jnp.concatenate([exp_b1[i][..., 0::2], exp_b1[i][..., 1::2]], axis=-1)[:, None, :]
        x_all = _moe(x1, sh2, sc2, g2, rw, rb, exp_w1[i], b1p, exp_w2[i], exp_b2[i][:, None, :], perm,
                     ln2_g[i].reshape(1, d), ln2_b[i].reshape(1, d), npb)
        x_pair = (x_all, x_all)
    return x_all
```

```python
import functools
import math

import numpy as np
import jax
import jax.numpy as jnp
from jax import lax
from jax.experimental import pallas as pl
from jax.experimental.pallas import tpu as pltpu

F32 = jnp.float32
BF16 = jnp.bfloat16

D_MODEL = 1024
DEPTH = 2
GRID_W = 64
ATT_HEADS = 8
ATT_KV_HEADS = 2
ATT_HEAD_DIM = 64
ROPE_THETA = 10000.0
GLA_HEADS = 4
GLA_DK = 64
GLA_DV = 128
GLA_RANK = 16
GLA_TAU = 16.0
GLA_CHUNK = 64
HY_WIDTH = 512
HY_POS_FREQS = 16
HY_DECAY_SLOW = math.log(1e-2) / 1.5
HY_DECAY_FAST = math.log(1e-2) / 0.3
N_EXPERTS = 32
TOP_K = 4
EXPERT_HIDDEN = D_MODEL
SWIGLU_LIMIT = 7.0
SWIGLU_ALPHA = 1.702
DEEPNORM_ALPHA = (2 * DEPTH) ** 0.25
LN_EPS = 1e-5
RMS_EPS = 1e-6

ATT_Q_W = ATT_HEADS * ATT_HEAD_DIM
ATT_KV_W = ATT_KV_HEADS * ATT_HEAD_DIM
GLA_K_W = GLA_HEADS * GLA_DK
GLA_V_W = GLA_HEADS * GLA_DV
GLA_GROUP_W = 1152
LANES = 128
NEG_BIG = -3.0e38

VMEM_LIMIT = 56 * 1024 * 1024


def _cparams(*sem):
    return pltpu.CompilerParams(dimension_semantics=sem, vmem_limit_bytes=VMEM_LIMIT)


def _dot(a, b):
    return jnp.dot(a, b, preferred_element_type=F32)


def _dot_nt(a, b):
    return lax.dot_general(a, b, (((1,), (1,)), ((), ())), preferred_element_type=F32)


def _dot_tn(a, b):
    return lax.dot_general(a, b, (((0,), (0,)), ((), ())), preferred_element_type=F32)


def _split2(x):
    hi = x.astype(BF16)
    lo = (x - hi.astype(F32)).astype(BF16)
    return hi, lo


def _split3(x):
    hi = x.astype(BF16)
    r = x - hi.astype(F32)
    mid = r.astype(BF16)
    lo = (r - mid.astype(F32)).astype(BF16)
    return hi, mid, lo


def _dot3(a, b):
    ah, al = _split2(a)
    bh, bl = _split2(b)
    return _dot(ah, bh) + (_dot(ah, bl) + _dot(al, bh))


def _dot_exact_lhs(a_bf16, b):
    b0, b1, b2 = _split3(b)
    return _dot(a_bf16, b0) + (_dot(a_bf16, b1) + _dot(a_bf16, b2))


def _ln(x):
    mu = jnp.mean(x, axis=-1, keepdims=True)
    xc = x - mu
    var = jnp.mean(xc * xc, axis=-1, keepdims=True)
    return xc * lax.rsqrt(var + LN_EPS)


def _sigmoid(x):
    return 1.0 / (1.0 + jnp.exp(-x))


def _mod_kernel(c_ref, w_ref, b_ref, o_ref):
    c = c_ref[...]
    o_ref[...] = _dot3(c * _sigmoid(c), w_ref[...]) + b_ref[...]


def _modulation(cc, w_mod, b_mod):
    rows, d = cc.shape
    n = w_mod.shape[1]
    tn = 1024
    return pl.pallas_call(
        _mod_kernel,
        grid=(n // tn,),
        in_specs=[pl.BlockSpec((rows, d), lambda j: (0, 0)),
                  pl.BlockSpec((d, tn), lambda j: (0, j)),
                  pl.BlockSpec((1, tn), lambda j: (0, j))],
        out_specs=pl.BlockSpec((rows, tn), lambda j: (0, j)),
        out_shape=jax.ShapeDtypeStruct((rows, n), F32),
        compiler_params=_cparams("arbitrary"),
        name="modulation",
    )(cc, w_mod, b_mod.reshape(1, n))


def _head_norm_rope(t, gain, rc, rs1, rs2, bd):
    sh, sl = _split2(t * t)
    ss = _dot(sh, bd) + _dot(sl, bd)
    tn = t * lax.rsqrt(ss * (1.0 / ATT_HEAD_DIM) + RMS_EPS) * gain
    return tn * rc + pltpu.roll(tn, 96, 1) * rs1 + pltpu.roll(tn, 32, 1) * rs2


def _pair_specs(tm, w, n_split, n_i):
    first = pl.BlockSpec((1, tm, w), lambda b, i: (jnp.minimum(b, n_split - 1), jnp.where(b < n_split, i, n_i - 1), 0))
    rest = pl.BlockSpec((1, tm, w), lambda b, i: (jnp.maximum(b - n_split, 0), jnp.where(b >= n_split, i, 0), 0))
    return [first, rest]


def _pick(use_rest, first_ref, rest_ref):
    return jnp.where(use_rest, rest_ref[0], first_ref[0])


def _inproj_kernel(x_ref, xr_ref, sh_ref, sc_ref, watt_ref, wgla_ref, wog_ref, why_ref, wgt_ref,
                   qg_ref, kg_ref, rc_ref, rs1_ref, rs2_ref, bd_ref,
                   qn_ref, kn_ref, v_ref, gla_ref, og_ref, hy_ref, gt_ref, *, n_split):
    x = _pick(pl.program_id(0) >= n_split, x_ref, xr_ref)
    u = _ln(x) * (1.0 + sc_ref[0]) + sh_ref[0]
    ub = u.astype(BF16)
    rc, rs1, rs2, bd = rc_ref[0], rs1_ref[0], rs2_ref[0], bd_ref[...]
    att = _dot(ub, watt_ref[...])
    for j in range(ATT_Q_W // LANES):
        t = att[:, LANES * j:LANES * (j + 1)]
        qn = _head_norm_rope(t, qg_ref[...], rc, rs1, rs2, bd) * (ATT_HEAD_DIM ** -0.5)
        qn_ref[0, :, LANES * j:LANES * (j + 1)] = qn.astype(BF16)
    kn = _head_norm_rope(att[:, ATT_Q_W:ATT_Q_W + LANES], kg_ref[...], rc, rs1, rs2, bd)
    kn_ref[0] = kn.astype(BF16)
    v_ref[0] = att[:, ATT_Q_W + LANES:].astype(BF16)
    gla_ref[0] = _dot(ub, wgla_ref[...])
    og_ref[0] = _dot(ub, wog_ref[...]).astype(BF16)
    hy_ref[0] = _dot(ub, why_ref[...]).astype(BF16)
    gt_ref[0] = _dot(ub, wgt_ref[...]).astype(BF16)


def _inproj(x_first, x_rest, sh, sc, wts, qg, kg, rope, bd, n_lat):
    n_split, l, d = x_first.shape
    p = sh.shape[0]
    tm = min(256, l)
    watt, wgla, wog, why, wgt = wts
    rc, rs1, rs2 = rope
    const = lambda a: pl.BlockSpec(a.shape, lambda b, i: (0,) * a.ndim)
    row = lambda w: pl.BlockSpec((1, tm, w), lambda b, i: (b, i, 0))
    tab = pl.BlockSpec((1, tm, LANES), lambda b, i: (jnp.where(b >= n_lat, 1, 0), i, 0))
    vec = pl.BlockSpec((1, 1, d), lambda b, i: (b, 0, 0))
    widths = (ATT_Q_W, LANES, LANES, GLA_GROUP_W, GLA_V_W, 3 * HY_WIDTH, 3 * d)
    dtypes = (BF16, BF16, BF16, F32, BF16, BF16, BF16)
    return pl.pallas_call(
        functools.partial(_inproj_kernel, n_split=n_split),
        grid=(p, l // tm),
        in_specs=_pair_specs(tm, d, n_split, l // tm) + [
            vec, vec, const(watt), const(wgla), const(wog), const(why), const(wgt),
            const(qg), const(kg), tab, tab, tab, const(bd)],
        out_specs=[row(w) for w in widths],
        out_shape=[jax.ShapeDtypeStruct((p, l, w), dt) for w, dt in zip(widths, dtypes)],
        compiler_params=_cparams("parallel", "arbitrary"),
        name="inproj",
    )(x_first, x_rest, sh, sc, watt, wgla, wog, why, wgt, qg, kg, rc, rs1, rs2, bd)


def _attend_group(qg, ks, vs, lo):
    zero = jnp.zeros_like(qg)
    outs = []
    for qh in (jnp.where(lo, qg, zero), jnp.where(lo, zero, qg)):
        ss = [_dot_nt(qh, k) for k in ks]
        m = ss[0].max(axis=-1, keepdims=True)
        for s in ss[1:]:
            m = jnp.maximum(m, s.max(axis=-1, keepdims=True))
        den = 0.0
        acc = 0.0
        for s, v in zip(ss, vs):
            p = jnp.exp(s - m)
            den = den + p.sum(axis=-1, keepdims=True)
            acc = acc + _dot(p.astype(BF16), v)
        outs.append(acc / den)
    return jnp.where(lo, outs[0], outs[1])


def _attn_kernel(q_ref, k_ref, v_ref, kc_ref, vc_ref, o_ref, *, n_lat_tiles, with_ctx):
    i = pl.program_id(1)
    tq = q_ref.shape[1]
    lo = lax.broadcasted_iota(jnp.int32, (tq, LANES), 1) < ATT_HEAD_DIM

    def run(ks, vs):
        for j in range(ATT_Q_W // LANES):
            o = _attend_group(q_ref[0, :, LANES * j:LANES * (j + 1)], ks, vs, lo)
            o_ref[0, :, LANES * j:LANES * (j + 1)] = o.astype(BF16)

    if with_ctx:
        @pl.when(i < n_lat_tiles)
        def _():
            run([k_ref[0], kc_ref[0]], [v_ref[0], vc_ref[0]])

        @pl.when(i == n_lat_tiles)
        def _():
            run([kc_ref[0]], [vc_ref[0]])
    else:
        run([k_ref[0], kc_ref[0]], [v_ref[0], vc_ref[0]])


def _attention(qn, kn, v, n_lat, lc, with_ctx):
    p, l, _ = qn.shape
    tq = lc
    r = l // lc
    nt = l // tq
    ctx_idx = lambda b: (n_lat + b // r, b % r, 0)
    if with_ctx:
        qmap = lambda b, i: (jnp.where(i < nt, b, n_lat + b // r), jnp.where(i < nt, i, b % r), 0)
    else:
        qmap = lambda b, i: (b, i, 0)
    return pl.pallas_call(
        functools.partial(_attn_kernel, n_lat_tiles=nt, with_ctx=with_ctx),
        grid=(n_lat, nt + (1 if with_ctx else 0)),
        in_specs=[pl.BlockSpec((1, tq, ATT_Q_W), qmap),
                  pl.BlockSpec((1, l, LANES), lambda b, i: (b, 0, 0)),
                  pl.BlockSpec((1, l, LANES), lambda b, i: (b, 0, 0)),
                  pl.BlockSpec((1, lc, LANES), lambda b, i: ctx_idx(b)),
                  pl.BlockSpec((1, lc, LANES), lambda b, i: ctx_idx(b))],
        out_specs=pl.BlockSpec((1, tq, ATT_Q_W), qmap),
        out_shape=jax.ShapeDtypeStruct((p if with_ctx else n_lat, l, ATT_Q_W), BF16),
        compiler_params=_cparams("parallel", "arbitrary"),
        name="attention",
    )(qn, kn, v, kn, v)


def _gla_segment(src_ref, og_ref, out_ref, acc_ref, st_ref, qd_ref, ke_ref, vb_ref, dec_ref,
                 wdec_ref, bdec_ref, gn_ref, ls):
    c = GLA_CHUNK
    nc = ls // c
    rb = min(256, ls)

    bi = lax.broadcasted_iota(jnp.int32, (rb, rb), 0)
    bj = lax.broadcasted_iota(jnp.int32, (rb, rb), 1)
    same = (bi // c) == (bj // c)
    att_masks = (same & (bi >= bj), same & (bi <= bj))
    tri_blk = tuple(jnp.where(m, 1.0, 0.0).astype(BF16) for m in att_masks)
    ones_blk = jnp.where(same, 1.0, 0.0).astype(BF16)
    lo_b = lax.broadcasted_iota(jnp.int32, (rb, LANES), 1) < GLA_DK

    def prep(t, carry):
        r0 = pl.multiple_of(t * rb, rb)
        q = src_ref[0, pl.ds(r0, rb), 0:GLA_K_W] * (GLA_DK ** -0.5)
        k = src_ref[0, pl.ds(r0, rb), GLA_K_W:2 * GLA_K_W]
        vb = src_ref[0, pl.ds(r0, rb), 2 * GLA_K_W:2 * GLA_K_W + GLA_V_W].astype(BF16)
        vb_ref[pl.ds(r0, rb), :] = vb
        lr = src_ref[0, pl.ds(r0, rb), 2 * GLA_K_W + GLA_V_W:GLA_GROUP_W]
        la = jax.nn.log_sigmoid(_dot3(lr, wdec_ref[...]) + bdec_ref[...]) * (1.0 / GLA_TAU)
        for d in range(2):
            l0, l1, l2 = _split3(la[:, d * GLA_K_W:(d + 1) * GLA_K_W])
            cum = _dot(tri_blk[d], l0) + (_dot(tri_blk[d], l1) + _dot(tri_blk[d], l2))
            tot = _dot(ones_blk, l0) + (_dot(ones_blk, l1) + _dot(ones_blk, l2))
            qd = (q * jnp.exp(cum)).astype(BF16)
            ki = (k * jnp.exp(-cum)).astype(BF16)
            qd_ref[d, pl.ds(r0, rb), :] = qd
            ke_ref[d, pl.ds(r0, rb), :] = (k * jnp.exp(tot - cum)).astype(BF16)
            dec_ref[d, pl.ds(r0, rb), :] = jnp.exp(tot)
            for h in range(GLA_HEADS):
                g, half = h // 2, h % 2
                sl = slice(LANES * g, LANES * (g + 1))
                zero = jnp.zeros_like(qd[:, sl])
                qh = jnp.where(lo_b, qd[:, sl], zero) if half == 0 else jnp.where(lo_b, zero, qd[:, sl])
                att = jnp.where(att_masks[d], _dot_nt(qh, ki[:, sl]), 0.0)
                o = _dot(att.astype(BF16), vb[:, GLA_DV * h:GLA_DV * (h + 1)])
                if d == 0:
                    acc_ref[pl.ds(r0, rb), GLA_DV * h:GLA_DV * (h + 1)] = o
                else:
                    acc_ref[pl.ds(r0, rb), GLA_DV * h:GLA_DV * (h + 1)] += o
        return carry

    lax.fori_loop(0, ls // rb, prep, 0)

    lo = lax.broadcasted_iota(jnp.int32, (c, LANES), 1) < GLA_DK

    def body(n, carry):
        for d, cn in ((0, n), (1, nc - 1 - n)):
            r0 = pl.multiple_of(cn * c, c)
            dec = dec_ref[d, pl.ds(r0, 1), :]
            for h in range(GLA_HEADS):
                g, half = h // 2, h % 2
                sl = slice(LANES * g, LANES * (g + 1))
                qd = qd_ref[d, pl.ds(r0, c), sl]
                zero = jnp.zeros_like(qd)
                qd = jnp.where(lo, qd, zero) if half == 0 else jnp.where(lo, zero, qd)
                vh = vb_ref[pl.ds(r0, c), GLA_DV * h:GLA_DV * (h + 1)]
                st = st_ref[d * GLA_HEADS + h]
                acc_ref[pl.ds(r0, c), GLA_DV * h:GLA_DV * (h + 1)] += _dot_nt(qd, st.astype(BF16))
                st_ref[d * GLA_HEADS + h] = dec[:, sl] * st + _dot_tn(vh, ke_ref[d, pl.ds(r0, c), sl])
        return carry

    lax.fori_loop(0, nc, body, 0, unroll=2 if nc % 2 == 0 else 1)

    tr = min(256, ls)

    def fin(t, carry):
        r0 = pl.multiple_of(t * tr, tr)
        for h in range(GLA_HEADS):
            sl = slice(GLA_DV * h, GLA_DV * (h + 1))
            o = acc_ref[pl.ds(r0, tr), sl]
            y = o * lax.rsqrt(jnp.mean(o * o, axis=-1, keepdims=True) + RMS_EPS)
            y = y * gn_ref[...]
            og = og_ref[0, pl.ds(r0, tr), sl].astype(F32)
            out_ref[0, pl.ds(r0, tr), sl] = (y * (og * _sigmoid(og))).astype(BF16)
        return carry

    lax.fori_loop(0, ls // tr, fin, 0)


def _gla_kernel(gl_ref, glc_ref, og_ref, ogc_ref, wdec_ref, bdec_ref, gn_ref, o_ref, oc_ref,
                acc_ref, st_ref, qd_ref, ke_ref, vb_ref, dec_ref):
    st_ref[...] = jnp.zeros(st_ref.shape, F32)
    scratch = (acc_ref, st_ref, qd_ref, ke_ref, vb_ref, dec_ref)
    _gla_segment(glc_ref, ogc_ref, oc_ref, *scratch, wdec_ref, bdec_ref, gn_ref, glc_ref.shape[1])
    _gla_segment(gl_ref, og_ref, o_ref, *scratch, wdec_ref, bdec_ref, gn_ref, gl_ref.shape[1])


def _gla(gla, og, wdec, bdec, gn, n_lat, lc):
    p, l, _ = gla.shape
    r = l // lc
    ctx_idx = lambda b: (n_lat + b // r, b % r, 0)
    const = lambda a: pl.BlockSpec(a.shape, lambda b: (0,) * a.ndim)
    return pl.pallas_call(
        _gla_kernel,
        grid=(n_lat,),
        in_specs=[pl.BlockSpec((1, l, GLA_GROUP_W), lambda b: (b, 0, 0)),
                  pl.BlockSpec((1, lc, GLA_GROUP_W), ctx_idx),
                  pl.BlockSpec((1, l, GLA_V_W), lambda b: (b, 0, 0)),
                  pl.BlockSpec((1, lc, GLA_V_W), ctx_idx),
                  const(wdec), const(bdec), const(gn)],
        out_specs=[pl.BlockSpec((1, l, GLA_V_W), lambda b: (b, 0, 0)),
                   pl.BlockSpec((1, lc, GLA_V_W), lambda b: (b, 0, 0))],
        out_shape=[jax.ShapeDtypeStruct((n_lat, l, GLA_V_W), BF16),
                   jax.ShapeDtypeStruct((n_lat, lc, GLA_V_W), BF16)],
        scratch_shapes=[pltpu.VMEM((l, GLA_V_W), F32),
                        pltpu.VMEM((2 * GLA_HEADS, GLA_DV, LANES), F32),
                        pltpu.VMEM((2, l, GLA_K_W), BF16),
                        pltpu.VMEM((2, l, GLA_K_W), BF16), pltpu.VMEM((l, GLA_V_W), BF16),
                        pltpu.VMEM((2, l, GLA_K_W), F32)],
        compiler_params=_cparams("arbitrary"),
        name="gla",
    )(gla, gla, og, og, wdec, bdec, gn)


def _dft_tables(l):
    n = 2 * l
    k = np.arange(l, dtype=np.int64)[:, None]
    t = np.arange(l, dtype=np.int64)[None, :]
    ang = 2.0 * np.pi * ((k * t) % n).astype(np.float64) / n
    fre = np.cos(ang)
    fim = -np.sin(ang)
    fim[0, :] = np.where(np.arange(l) % 2 == 0, 1.0, -1.0)
    fwd = np.concatenate([fre, fim], axis=0)
    return jnp.asarray(fwd, dtype=BF16), jnp.asarray(fwd.T, dtype=BF16)


def _hy_filter_kernel(z_ref, win_ref, w1_ref, b1_ref, w2_ref, b2_ref, w3_ref, b3_ref, fre_ref, fim_ref,
                      hre_ref, him_ref, hf_ref):
    j = pl.program_id(0)
    l = z_ref.shape[0]

    @pl.when(j == 0)
    def _():
        h = jnp.sin(_dot3(z_ref[...], w1_ref[...]) + b1_ref[...])
        h = jnp.sin(_dot3(h, w2_ref[...]) + b2_ref[...])
        h = _dot3(h, w3_ref[...]) + b3_ref[...]
        win = win_ref[...]
        hf = h[:, :HY_WIDTH] * win
        hb = h[:, HY_WIDTH:] * win
        row = lax.broadcasted_iota(jnp.int32, (l, HY_WIDTH), 0)
        hf_ref[:, :HY_WIDTH] = hf.astype(BF16)
        hf_ref[:, HY_WIDTH:] = jnp.where(row == 0, 0.0, hb).astype(BF16)

    tk = fre_ref.shape[0]
    a = _dot(fre_ref[...], hf_ref[...])
    b = _dot(fim_ref[...], hf_ref[...])
    first = (lax.broadcasted_iota(jnp.int32, (tk, HY_WIDTH), 0) == 0) & (j == 0)
    scl = jnp.where(first, 1.0 / (2 * l), 2.0 / (2 * l))
    hre_ref[...] = (a[:, :HY_WIDTH] + a[:, HY_WIDTH:]) * scl
    him_ref[...] = (b[:, :HY_WIDTH] + jnp.where(first, 1.0, -1.0) * b[:, HY_WIDTH:]) * scl


def _hy_filter(l, fw1, fb1, fw2, fb2, fw3, fb3, fwd):
    t = np.arange(l, dtype=np.float32)[:, None] / np.float32(l)
    freqs = np.arange(1, HY_POS_FREQS + 1, dtype=np.float32)
    z = np.concatenate([t, np.cos(2.0 * math.pi * freqs * t), np.sin(2.0 * math.pi * freqs * t)], axis=-1)
    z = np.pad(z, ((0, 0), (0, LANES - z.shape[1])))
    hid = fw2.shape[0]
    fw1 = jnp.pad(fw1, ((0, LANES - fw1.shape[0]), (0, LANES - hid)))
    fb1 = jnp.pad(fb1, (0, LANES - hid))
    fw2 = jnp.pad(fw2, ((0, LANES - hid), (0, LANES - hid)))
    fb2 = jnp.pad(fb2, (0, LANES - hid))
    fw3 = jnp.pad(fw3, ((0, LANES - hid), (0, 0)))
    deltas = np.abs(np.linspace(HY_DECAY_SLOW, HY_DECAY_FAST, HY_WIDTH, dtype=np.float32))
    win = np.exp(-t * deltas).astype(np.float32)
    tk = min(512, l)
    nk = l // tk
    const = lambda a: pl.BlockSpec(a.shape, lambda j: (0,) * a.ndim)
    args = (jnp.asarray(z), jnp.asarray(win), fw1, fb1.reshape(1, -1), fw2, fb2.reshape(1, -1),
            fw3, fb3.reshape(1, -1))
    return pl.pallas_call(
        _hy_filter_kernel,
        grid=(nk,),
        in_specs=[const(a) for a in args] + [pl.BlockSpec((tk, l), lambda j: (j, 0)),
                                              pl.BlockSpec((tk, l), lambda j: (nk + j, 0))],
        out_specs=[pl.BlockSpec((tk, HY_WIDTH), lambda j: (j, 0))] * 2,
        out_shape=[jax.ShapeDtypeStruct((l, HY_WIDTH), F32)] * 2,
        scratch_shapes=[pltpu.VMEM((l, 2 * HY_WIDTH), BF16)],
        compiler_params=_cparams("arbitrary"),
        name="hyena_filter",
    )(*args, fwd, fwd)


def _hy_pre_kernel(p0_ref, p1_ref, p2_ref, w0_ref, w1_ref, w2_ref, b0_ref, b1_ref, b2_ref, skip_ref,
                   u_ref, x0_ref, t_ref):
    ls = p0_ref.shape[1]
    row = lax.broadcasted_iota(jnp.int32, (ls, LANES), 0)

    def conv(p_ref, w_ref, b_ref):
        p = p_ref[0].astype(F32)
        prev = jnp.where(row == 0, 0.0, pltpu.roll(p, 1, 0))
        nxt = jnp.where(row == ls - 1, 0.0, pltpu.roll(p, ls - 1, 0))
        return prev * w_ref[0:1, :] + p * w_ref[1:2, :] + nxt * w_ref[2:3, :] + b_ref[...]

    x0 = conv(p0_ref, w0_ref, b0_ref)
    u = conv(p1_ref, w1_ref, b1_ref) * conv(p2_ref, w2_ref, b2_ref)
    u_ref[0] = u.astype(BF16)
    x0_ref[0] = x0.astype(BF16)
    t_ref[0] = (x0 * u * skip_ref[...]).astype(BF16)


def _hy_pre(hy, conv_w, conv_b, skip, nseq, ls, n_lat):
    p, l, _ = hy.shape
    r = l // ls
    nj = HY_WIDTH // LANES
    base = 0 if ls == l else n_lat
    src = lambda off: pl.BlockSpec((1, ls, LANES), lambda b, j: (base + b // r, b % r, off * nj + j))
    wsp = lambda off: pl.BlockSpec((3, LANES), lambda b, j: (0, off * nj + j))
    bsp = lambda off: pl.BlockSpec((1, LANES), lambda b, j: (0, off * nj + j))
    out = pl.BlockSpec((1, ls, LANES), lambda b, j: (b, 0, j))
    cb = conv_b.reshape(1, -1)
    return pl.pallas_call(
        _hy_pre_kernel,
        grid=(nseq, nj),
        in_specs=[src(0), src(1), src(2), wsp(0), wsp(1), wsp(2), bsp(0), bsp(1), bsp(2), bsp(0)],
        out_specs=[out] * 3,
        out_shape=[jax.ShapeDtypeStruct((nseq, ls, HY_WIDTH), BF16)] * 3,
        compiler_params=_cparams("parallel", "arbitrary"),
        name="hyena_pre",
    )(hy, hy, hy, conv_w, conv_w, conv_w, cb, cb, cb, skip.reshape(1, -1))


def _hy_conv_kernel(u_ref, fre_ref, fim_ref, ire_ref, iim_ref, hre_ref, him_ref, y_ref):
    j = pl.program_id(1)
    u = u_ref[0]
    ure = _dot(fre_ref[...], u)
    uim = _dot(fim_ref[...], u)
    hre, him = hre_ref[...], him_ref[...]
    first = (lax.broadcasted_iota(jnp.int32, ure.shape, 0) == 0) & (j == 0)
    yre = jnp.where(first, ure * hre, ure * hre - uim * him)
    yim = jnp.where(first, uim * him, ure * him + uim * hre)
    y = _dot(ire_ref[...], yre.astype(BF16)) + _dot(iim_ref[...], yim.astype(BF16))

    @pl.when(j == 0)
    def _():
        y_ref[0] = y

    @pl.when(j > 0)
    def _():
        y_ref[0] += y


def _hy_conv(u, fwd, inv, hre, him):
    nseq, ls, _ = u.shape
    tk = min(512, ls)
    nk = ls // tk
    return pl.pallas_call(
        _hy_conv_kernel,
        grid=(nseq, nk),
        in_specs=[pl.BlockSpec((1, ls, HY_WIDTH), lambda b, j: (b, 0, 0)),
                  pl.BlockSpec((tk, ls), lambda b, j: (j, 0)),
                  pl.BlockSpec((tk, ls), lambda b, j: (nk + j, 0)),
                  pl.BlockSpec((ls, tk), lambda b, j: (0, j)),
                  pl.BlockSpec((ls, tk), lambda b, j: (0, nk + j)),
                  pl.BlockSpec((tk, HY_WIDTH), lambda b, j: (j, 0)),
                  pl.BlockSpec((tk, HY_WIDTH), lambda b, j: (j, 0))],
        out_specs=pl.BlockSpec((1, ls, HY_WIDTH), lambda b, j: (b, 0, 0)),
        out_shape=jax.ShapeDtypeStruct((nseq, ls, HY_WIDTH), F32),
        compiler_params=_cparams("parallel", "arbitrary"),
        name="hyena_conv",
    )(u, fwd, fwd, inv, inv, hre, him)


def _merge_kernel(x_ref, xr_ref, oa_ref, ob_ref, obr_ref, y_ref, yr_ref, x0_ref, x0r_ref, t_ref, tr_ref,
                  gt_ref, g1_ref, wa_ref, wb_ref, wc_ref, wo_ref, lg_ref, lb_ref, o_ref, *, n_split):
    d = x_ref.shape[2]
    rest = pl.program_id(0) >= n_split
    x = _pick(rest, x_ref, xr_ref)
    ob = _pick(rest, ob_ref, obr_ref)
    oc = (_pick(rest, x0_ref, x0r_ref).astype(F32) * _pick(rest, y_ref, yr_ref)
          + _pick(rest, t_ref, tr_ref).astype(F32)).astype(BF16)
    gt = gt_ref[0]
    m = (_sigmoid(gt[:, 0:d].astype(F32)) * _dot(oa_ref[0], wa_ref[...])
         + _sigmoid(gt[:, d:2 * d].astype(F32)) * _dot(ob, wb_ref[...])
         + _sigmoid(gt[:, 2 * d:3 * d].astype(F32)) * _dot(oc, wc_ref[...]))
    mix = _dot(m.astype(BF16), wo_ref[...])
    o_ref[0] = _ln(DEEPNORM_ALPHA * x + g1_ref[0] * mix) * lg_ref[...] + lb_ref[...]


def _merge(x, o_a, o_b, y, x0, t, gates, g1, wa, wb, wc, wo, lg, lb, npb):
    n_split, l, d = x[0].shape
    tm = min(512, l)
    n_i = l // tm
    row = lambda w: pl.BlockSpec((1, tm, w), lambda b, i: (b, i, 0))
    const = lambda a: pl.BlockSpec(a.shape, lambda b, i: (0,) * a.ndim)
    pair = lambda w: _pair_specs(tm, w, n_split, n_i)
    return pl.pallas_call(
        functools.partial(_merge_kernel, n_split=n_split),
        grid=(npb, n_i),
        in_specs=pair(d) + [row(ATT_Q_W)] + pair(GLA_V_W) + pair(HY_WIDTH) + pair(HY_WIDTH) + pair(HY_WIDTH) + [
            row(3 * d), pl.BlockSpec((1, 1, d), lambda b, i: (b, 0, 0)),
            const(wa), const(wb), const(wc), const(wo), const(lg), const(lb)],
        out_specs=row(d),
        out_shape=jax.ShapeDtypeStruct((npb, l, d), F32),
        compiler_params=_cparams("parallel", "arbitrary"),
        name="merge",
    )(*x, o_a, *o_b, *y, *x0, *t, gates, g1, wa, wb, wc, wo, lg, lb)


MOE_T = 256
MOE_G = 16
MOE_TM = 1024
MOE_SLOTS = TOP_K * MOE_T + 512
AUG_W = D_MODEL + LANES


def _route_kernel(x_ref, sh_ref, sc_ref, rw_ref, rb_ref, xa_ref, sel_ref, cnt_ref):
    d = x_ref.shape[2]
    u = _ln(x_ref[0]) * (1.0 + sc_ref[0]) + sh_ref[0]
    xa_ref[0, :, 0:d] = u.astype(BF16)
    logits = _dot3(u, rw_ref[...]) + rb_ref[...]
    lane = lax.broadcasted_iota(jnp.int32, logits.shape, 1).astype(F32)
    work = logits
    hots, vals = [], []
    for _ in range(TOP_K):
        m = work.max(axis=-1, keepdims=True)
        idx = jnp.where(work == m, lane, float(LANES)).min(axis=-1, keepdims=True)
        hot = lane == idx
        hots.append(hot)
        vals.append(m)
        work = jnp.where(hot, NEG_BIG, work)
    es = [jnp.exp(v - vals[0]) for v in vals]
    den = es[0] + es[1] + es[2] + es[3]
    comb = jnp.zeros_like(logits)
    sel = jnp.zeros_like(logits)
    for hot, e in zip(hots, es):
        comb = comb + jnp.where(hot, e / den, 0.0)
        sel = sel + jnp.where(hot, 1.0, 0.0)
    c0, c1, c2 = _split3(comb)
    aug = c0.astype(F32) + pltpu.roll(c1.astype(F32), 32, 1) + pltpu.roll(c2.astype(F32), 64, 1)
    xa_ref[0, :, d:d + LANES] = aug.astype(BF16)
    sel_ref[0] = sel.astype(BF16)
    cnt_ref[0] = sel.sum(axis=0, keepdims=True)


def _route(x_all, sh, sc, rw, rb, npb):
    _, l, d = x_all.shape
    t = MOE_T
    nt_b = l // t
    row = lambda w: pl.BlockSpec((1, t, w), lambda b, i: (b, i, 0))
    vec = pl.BlockSpec((1, 1, d), lambda b, i: (b, 0, 0))
    const = lambda a: pl.BlockSpec(a.shape, lambda b, i: (0,) * a.ndim)
    return pl.pallas_call(
        _route_kernel,
        grid=(npb, nt_b),
        in_specs=[row(d), vec, vec, const(rw), const(rb)],
        out_specs=[row(AUG_W), row(LANES), pl.BlockSpec((1, 1, LANES), lambda b, i: (b * nt_b + i, 0, 0))],
        out_shape=[jax.ShapeDtypeStruct((npb, l, AUG_W), BF16), jax.ShapeDtypeStruct((npb, l, LANES), BF16),
                   jax.ShapeDtypeStruct((npb * nt_b, 1, LANES), F32)],
        compiler_params=_cparams("parallel", "arbitrary"),
        name="route",
    )(x_all, sh, sc, rw, rb)


def _group_copies(j, cntp_ref, loc_ref, goff_ref, make_copy):
    for e in range(N_EXPERTS):
        n = lax.shift_right_logical(cntp_ref[j * N_EXPERTS + e], int(math.log2(MOE_G)))
        loc = loc_ref[j * N_EXPERTS + e]
        off = goff_ref[j * N_EXPERTS + e]
        n2 = lax.shift_right_logical(n, 1)

        def per_pair(c, carry, loc=loc, off=off):
            make_copy(pl.multiple_of(loc + c * (2 * MOE_G), MOE_G),
                      pl.multiple_of(off + c * (2 * MOE_G), MOE_G), 2 * MOE_G).start()
            return carry

        lax.fori_loop(0, n2, per_pair, 0)

        @pl.when((n & 1) == 1)
        def _(loc=loc, off=off, n2=n2):
            make_copy(pl.multiple_of(loc + n2 * (2 * MOE_G), MOE_G),
                      pl.multiple_of(off + n2 * (2 * MOE_G), MOE_G), MOE_G).start()


def _wait_copies(nch_ref, j, make_copy):
    for k, rows in ((0, 2 * MOE_G), (1, MOE_G)):
        def wait_one(c, carry, rows=rows):
            make_copy(0, 0, rows).wait()
            return carry

        lax.fori_loop(0, nch_ref[2 * j + k], wait_one, 0)


def _dispatch_kernel(cntp_ref, loc_ref, goff_ref, nch_ref, tail_ref, tailn_ref,
                     xa_ref, sel_ref, locv_ref, pt_ref, xs_ref, stage_ref, zero_ref, sems):
    j = pl.program_id(0)
    nt = pl.num_programs(0)
    buf = j % 2
    sem = sems.at[buf]
    stage = stage_ref.at[buf]
    t = MOE_T

    def out_copy(src, dst, rows):
        return pltpu.make_async_copy(stage.at[pl.ds(src, rows)], xs_ref.at[pl.ds(dst, rows)], sem)

    @pl.when(j >= 2)
    def _():
        _wait_copies(nch_ref, jnp.maximum(j - 2, 0), out_copy)

    sel = sel_ref[0]
    ri = lax.broadcasted_iota(jnp.int32, (t, t), 0)
    ci = lax.broadcasted_iota(jnp.int32, (t, t), 1)
    rank = _dot(jnp.where(ci < ri, 1.0, 0.0).astype(BF16), sel)
    dest = rank + locv_ref[0]
    lane = lax.broadcasted_iota(jnp.int32, (t, LANES), 1).astype(F32)
    slot = lax.broadcasted_iota(jnp.int32, (t, MOE_SLOTS), 1)
    avail = sel.astype(F32)
    pt = jnp.zeros((t, MOE_SLOTS), F32)
    for _ in range(TOP_K):
        ek = jnp.where(avail > 0.0, lane, float(LANES)).min(axis=-1, keepdims=True)
        hot = lane == ek
        dk = jnp.where(hot, dest, 0.0).sum(axis=-1, keepdims=True).astype(jnp.int32)
        pt = pt + jnp.where(slot == dk, 1.0, 0.0)
        avail = jnp.where(hot, 0.0, avail)
    ptb = pt.astype(BF16)
    pt_ref[0] = ptb
    stage[...] = _dot_tn(ptb, xa_ref[0]).astype(BF16)
    _group_copies(j, cntp_ref, loc_ref, goff_ref, out_copy)

    @pl.when(j == nt - 1)
    def _():
        _wait_copies(nch_ref, j, out_copy)

        @pl.when(j >= 1)
        def _():
            other = sems.at[1 - buf]
            _wait_copies(nch_ref, jnp.maximum(j - 1, 0), lambda s, d, rows: pltpu.make_async_copy(
                stage_ref.at[1 - buf].at[pl.ds(s, rows)], xs_ref.at[pl.ds(d, rows)], other))

        zero_ref[...] = jnp.zeros(zero_ref.shape, BF16)

        def zcopy(dst):
            return pltpu.make_async_copy(zero_ref.at[pl.ds(0, MOE_G)], xs_ref.at[pl.ds(dst, MOE_G)], sem)

        def zcopy_big(dst):
            return pltpu.make_async_copy(zero_ref, xs_ref.at[pl.ds(dst, MOE_TM)], sem)

        def per_expert(e, tot):
            n = tailn_ref[e]
            base = tail_ref[e]

            def per_chunk(c, carry):
                zcopy(pl.multiple_of(base + c * MOE_G, MOE_G)).start()
                return carry

            lax.fori_loop(0, n, per_chunk, 0)
            return tot + n

        ztotal = lax.fori_loop(0, N_EXPERTS, per_expert, 0)

        def zwait(c, carry):
            zcopy(0).wait()
            return carry

        lax.fori_loop(0, ztotal, zwait, 0)

        nbig = tailn_ref[N_EXPERTS]
        big0 = tail_ref[N_EXPERTS]

        def big_start(c, carry):
            zcopy_big(pl.multiple_of(big0 + c * MOE_TM, MOE_TM)).start()
            return carry

        def big_wait(c, carry):
            zcopy_big(0).wait()
            return carry

        lax.fori_loop(0, nbig, big_start, 0)
        lax.fori_loop(0, nbig, big_wait, 0)


def _dispatch(xaug, sel, locv, cntp, loc, goff, nch, tail, tailn, s_rows):
    npb, l, _ = xaug.shape
    t = MOE_T
    nt_b = l // t
    nt = npb * nt_b
    tile = lambda w: pl.BlockSpec((1, t, w), lambda j, *_: (j // nt_b, j % nt_b, 0))
    return pl.pallas_call(
        _dispatch_kernel,
        grid_spec=pltpu.PrefetchScalarGridSpec(
            num_scalar_prefetch=6,
            grid=(nt,),
            in_specs=[tile(AUG_W), tile(LANES), pl.BlockSpec((1, 1, LANES), lambda j, *_: (j, 0, 0))],
            out_specs=[pl.BlockSpec((1, t, MOE_SLOTS), lambda j, *_: (j, 0, 0)),
                       pl.BlockSpec(memory_space=pl.ANY)],
            scratch_shapes=[pltpu.VMEM((2, MOE_SLOTS, AUG_W), BF16), pltpu.VMEM((MOE_TM, AUG_W), BF16),
                            pltpu.SemaphoreType.DMA((2,))]),
        out_shape=[jax.ShapeDtypeStruct((nt, t, MOE_SLOTS), BF16), jax.ShapeDtypeStruct((s_rows, AUG_W), BF16)],
        compiler_params=_cparams("arbitrary"),
        name="moe_dispatch",
    )(cntp, loc, goff, nch, tail, tailn, xaug, sel, locv)


def _ffn_kernel(te_ref, na_ref, xs_ref, w1_ref, b1_ref, w2_ref, b2_ref, perm_ref, ys_ref, w1p_ref, w2b_ref):
    i = pl.program_id(0)
    e = te_ref[i]
    active = i < na_ref[0]
    fresh = jnp.logical_or(i == 0, e != te_ref[jnp.maximum(i - 1, 0)])
    hh = EXPERT_HIDDEN // 2

    @pl.when(jnp.logical_and(active, fresh))
    def _():
        for c in range(2):
            blk = w1_ref[0, 0, :, 2 * hh * c:2 * hh * (c + 1)].astype(BF16)
            r = _dot(blk, perm_ref[...])
            w1p_ref[:, hh * c:hh * (c + 1)] = r[:, :hh].astype(BF16)
            w1p_ref[:, EXPERT_HIDDEN + hh * c:EXPERT_HIDDEN + hh * (c + 1)] = r[:, hh:].astype(BF16)
        w2b_ref[...] = w2_ref[0, 0].astype(BF16)

    @pl.when(active)
    def _():
        x = xs_ref[:, 0:D_MODEL]
        extra = xs_ref[:, D_MODEL:AUG_W].astype(F32)
        lane = lax.broadcasted_iota(jnp.int32, extra.shape, 1)
        mine = jnp.logical_and(lane % N_EXPERTS == e, lane < 3 * N_EXPERTS)
        wslot = jnp.where(mine, extra, 0.0).sum(axis=-1, keepdims=True)
        h = _dot(x, w1p_ref[...]) + b1_ref[0]
        g = jnp.minimum(h[:, :EXPERT_HIDDEN], SWIGLU_LIMIT)
        up = jnp.clip(h[:, EXPERT_HIDDEN:], -SWIGLU_LIMIT, SWIGLU_LIMIT)
        act = g * _sigmoid(SWIGLU_ALPHA * g) * (up + 1.0)
        y = _dot(act.astype(BF16), w2b_ref[...]) + b2_ref[0]
        ys_ref[...] = (wslot * y).astype(BF16)

    @pl.when(jnp.logical_not(active))
    def _():
        ys_ref[...] = jnp.zeros(ys_ref.shape, BF16)


def _ffn(xs, tile_e, n_active, w1, b1p, w2, b2, perm, n_tiles, layer):
    tm = MOE_TM
    d, h2 = w1.shape[2], w1.shape[3]
    once = pl.Buffered(1)
    return pl.pallas_call(
        _ffn_kernel,
        grid_spec=pltpu.PrefetchScalarGridSpec(
            num_scalar_prefetch=2,
            grid=(n_tiles,),
            in_specs=[pl.BlockSpec((tm, AUG_W), lambda i, te, na: (jnp.where(i < na[0], i, 0), 0)),
                      pl.BlockSpec((1, 1, d, h2), lambda i, te, na: (layer, te[i], 0, 0), pipeline_mode=once),
                      pl.BlockSpec((1, 1, h2), lambda i, te, na: (te[i], 0, 0)),
                      pl.BlockSpec((1, 1, h2 // 2, d), lambda i, te, na: (layer, te[i], 0, 0), pipeline_mode=once),
                      pl.BlockSpec((1, 1, d), lambda i, te, na: (te[i], 0, 0)),
                      pl.BlockSpec(perm.shape, lambda i, te, na: (0, 0), pipeline_mode=once)],
            out_specs=pl.BlockSpec((tm, d), lambda i, te, na: (i, 0)),
            scratch_shapes=[pltpu.VMEM((d, h2), BF16), pltpu.VMEM((h2 // 2, d), BF16)]),
        out_shape=jax.ShapeDtypeStruct((n_tiles * tm, d), BF16),
        compiler_params=_cparams("arbitrary"),
        name="moe_ffn",
    )(tile_e, n_active, xs, w1, b1p, w2, b2, perm)


def _combine_kernel(cntp_ref, loc_ref, goff_ref, nch_ref, pt_ref, x_ref, g2_ref, lg_ref, lb_ref, ys_ref,
                    o_ref, stage_ref, sems):
    j = pl.program_id(0)
    nt = pl.num_programs(0)
    buf = j % 2

    def in_copy(b):
        return lambda dst, src, rows: pltpu.make_async_copy(
            ys_ref.at[pl.ds(src, rows)], stage_ref.at[b].at[pl.ds(dst, rows)], sems.at[b])

    @pl.when(j == 0)
    def _():
        stage_ref[...] = jnp.zeros(stage_ref.shape, BF16)
        _group_copies(j, cntp_ref, loc_ref, goff_ref, in_copy(buf))

    @pl.when(j + 1 < nt)
    def _():
        _group_copies(j + 1, cntp_ref, loc_ref, goff_ref, in_copy(1 - buf))

    _wait_copies(nch_ref, j, in_copy(buf))
    y = _dot(pt_ref[0], stage_ref[buf])
    o_ref[0] = _ln(DEEPNORM_ALPHA * x_ref[0] + g2_ref[0] * y) * lg_ref[...] + lb_ref[...]


def _combine(pt, ys, x1, g2, lg, lb, cntp, loc, goff, nch):
    npb, l, d = x1.shape
    t = MOE_T
    nt_b = l // t
    tile = lambda w: pl.BlockSpec((1, t, w), lambda j, *_: (j // nt_b, j % nt_b, 0))
    const = lambda a: pl.BlockSpec(a.shape, lambda j, *_: (0,) * a.ndim)
    return pl.pallas_call(
        _combine_kernel,
        grid_spec=pltpu.PrefetchScalarGridSpec(
            num_scalar_prefetch=4,
            grid=(npb * nt_b,),
            in_specs=[pl.BlockSpec((1, t, MOE_SLOTS), lambda j, *_: (j, 0, 0)), tile(d),
                      pl.BlockSpec((1, 1, d), lambda j, *_: (j // nt_b, 0, 0)), const(lg), const(lb),
                      pl.BlockSpec(memory_space=pl.ANY)],
            out_specs=tile(d),
            scratch_shapes=[pltpu.VMEM((2, MOE_SLOTS, d), BF16), pltpu.SemaphoreType.DMA((2,))]),
        out_shape=jax.ShapeDtypeStruct((npb, l, d), F32),
        compiler_params=_cparams("arbitrary"),
        name="moe_combine",
    )(cntp, loc, goff, nch, pt, x1, g2, lg, lb, ys)


def _moe(x1, sh2, sc2, g2, rw, rb, w1, b1p, w2, b2, perm, lg, lb, npb, layer):
    _, l, d = x1.shape
    n_tok = npb * l
    nt = n_tok // MOE_T
    xaug, sel, cnt = _route(x1, sh2, sc2, rw, rb, npb)

    i32 = jnp.int32
    cnt = cnt[:, 0, :N_EXPERTS].astype(i32)
    cntp = (cnt + (MOE_G - 1)) // MOE_G * MOE_G
    loc = jnp.cumsum(cntp, axis=1) - cntp
    tot = cntp.sum(axis=0)
    seg = (tot + (MOE_TM - 1)) // MOE_TM * MOE_TM
    seg_start = jnp.cumsum(seg) - seg
    goff = seg_start[None, :] + jnp.cumsum(cntp, axis=0) - cntp
    s_max = TOP_K * n_tok + nt * N_EXPERTS * (MOE_G - 1) + N_EXPERTS * (MOE_TM - MOE_G)
    n_tiles = -(-s_max // MOE_TM)
    cum_tiles = jnp.cumsum(seg // MOE_TM)
    tile_e = jnp.minimum((jnp.arange(n_tiles, dtype=i32)[:, None] >= cum_tiles[None, :]).astype(i32).sum(axis=1),
                         N_EXPERTS - 1)
    n_active = cum_tiles[-1:].astype(i32)
    gran = cntp // MOE_G
    nch = jnp.stack([(gran // 2).sum(axis=1), (gran % 2).sum(axis=1)], axis=1).reshape(-1).astype(i32)
    locv = jnp.pad(loc.astype(F32), ((0, 0), (0, LANES - N_EXPERTS)))[:, None, :]
    flat = lambda a: a.reshape(-1).astype(i32)

    tail = jnp.concatenate([seg_start + tot, n_active * MOE_TM]).astype(i32)
    tailn = jnp.concatenate([(seg - tot) // MOE_G, n_tiles - n_active]).astype(i32)
    pt, xs = _dispatch(xaug, sel, locv, flat(cntp), flat(loc), flat(goff), nch, tail, tailn, n_tiles * MOE_TM)
    ys = _ffn(xs, tile_e, n_active, w1, b1p, w2, b2, perm, n_tiles, layer)
    return _combine(pt, ys, x1, g2, lg, lb, flat(cntp), flat(loc), flat(goff), nch)


def _deinterleave(n):
    return np.concatenate([np.arange(0, n, 2), np.arange(1, n, 2)])


def _rope_tables(l):
    rows = l // GRID_W
    r = np.repeat(np.arange(rows, dtype=np.float32), GRID_W)
    col = np.tile(np.arange(GRID_W, dtype=np.float32), rows)
    axis_dim = ATT_HEAD_DIM // 2
    inv_freq = (ROPE_THETA ** (-np.arange(0, axis_dim, 2, dtype=np.float32) / axis_dim)).astype(np.float32)
    ang = np.concatenate([r[:, None] * inv_freq, col[:, None] * inv_freq], axis=-1)
    c, s = np.cos(ang).astype(np.float32), np.sin(ang).astype(np.float32)
    z = np.zeros_like(s)
    rc = np.tile(np.concatenate([c, c], -1), (1, 2))
    rs1 = np.tile(np.concatenate([-s, z], -1), (1, 2))
    rs2 = np.tile(np.concatenate([z, s], -1), (1, 2))
    ident = (np.ones_like(rc), np.zeros_like(rc), np.zeros_like(rc))
    return tuple(jnp.asarray(np.stack([a, b])) for a, b in zip((rc, rs1, rs2), ident))


def kernel(x, c, ctx, c_ctx, w_mod, b_mod, w_in, att_q_norm, att_k_norm, gla_wdec_f, gla_bdec_f, gla_wdec_b, gla_bdec_b, gla_out_norm, hy_conv_w, hy_conv_b, hy_fw1, hy_fb1, hy_fw2, hy_fb2, hy_fw3, hy_fb3, hy_skip, w_branch_a, w_branch_b, w_branch_c, w_out, ln1_g, ln1_b, router_w, router_b, exp_w1, exp_b1, exp_w2, exp_b2, ln2_g, ln2_b):
    b, l, d = x.shape
    lc = ctx.shape[1]
    assert d == D_MODEL and (b * lc) % l == 0 and l % lc == 0 and lc % GLA_CHUNK == 0
    pc = (b * lc) // l
    p = b + pc

    perm64 = _deinterleave(ATT_HEAD_DIM)
    q_cols = np.concatenate([np.concatenate([64 * j + perm64, 64 * (j + 4) + perm64]) for j in range(4)])
    k_cols = ATT_Q_W + np.concatenate([perm64, 64 + perm64])
    v_cols = ATT_Q_W + ATT_KV_W + np.arange(ATT_KV_W)
    att_cols = np.concatenate([q_cols, k_cols, v_cols])
    o0 = ATT_Q_W + 2 * ATT_KV_W
    gla_cols = o0 + np.concatenate([np.arange(0, 2 * GLA_K_W + GLA_V_W + 2 * GLA_RANK)])
    og0 = o0 + 2 * GLA_K_W + GLA_V_W + 2 * GLA_RANK
    hy0 = og0 + GLA_V_W
    gt0 = hy0 + 3 * HY_WIDTH
    gain_perm = np.concatenate([perm64, perm64])
    bd = jnp.asarray(np.kron(np.eye(2), np.ones((64, 64))), dtype=BF16)
    rope = _rope_tables(l)
    fwd_l, inv_l = _dft_tables(l)
    fwd_c, inv_c = _dft_tables(lc)
    cmap = np.concatenate([np.arange(b), np.full(pc, b)])
    hh = EXPERT_HIDDEN // 2
    perm_np = np.zeros((2 * hh, 2 * hh), np.float32)
    perm_np[2 * np.arange(hh), np.arange(hh)] = 1.0
    perm_np[2 * np.arange(hh) + 1, hh + np.arange(hh)] = 1.0
    perm = jnp.asarray(perm_np, dtype=BF16)

    x_pair = (x, ctx.reshape(pc, l, d))
    cc = jnp.concatenate([c, c_ctx[None], jnp.zeros((7, d), F32)], axis=0)

    for i in range(DEPTH):
        with_ctx = i < DEPTH - 1
        npb = p if with_ctx else b
        mod = _modulation(cc, w_mod[i], b_mod[i])[cmap]
        sh1, sc1, g1, sh2, sc2, g2 = [m[:, None, :] for m in jnp.split(mod, 6, axis=-1)]

        wi = w_in[i]
        watt = wi[:, att_cols].astype(BF16)
        wgla = jnp.pad(wi[:, gla_cols], ((0, 0), (0, GLA_GROUP_W - gla_cols.size))).astype(BF16)
        wog = wi[:, og0:hy0].astype(BF16)
        why = wi[:, hy0:gt0].astype(BF16)
        wgt = wi[:, gt0:].astype(BF16)
        qg = att_q_norm[i][gain_perm].reshape(1, LANES)
        kg = att_k_norm[i][gain_perm].reshape(1, LANES)
        qn, kn, v, gla, og, hy, gates = _inproj(*x_pair, sh1, sc1, (watt, wgla, wog, why, wgt), qg, kg, rope, bd, b)

        o_a = _attention(qn, kn, v, b, lc, with_ctx)

        wdec = jnp.zeros((LANES, 2 * GLA_K_W), F32)
        wdec = wdec.at[0:GLA_RANK, 0:GLA_K_W].set(gla_wdec_f[i])
        wdec = wdec.at[GLA_RANK:2 * GLA_RANK, GLA_K_W:].set(gla_wdec_b[i])
        bdec = jnp.concatenate([gla_bdec_f[i], gla_bdec_b[i]]).reshape(1, -1)
        o_b, o_b_ctx = _gla(gla, og, wdec, bdec, gla_out_norm[i].reshape(1, -1), b, lc)

        hre, him = _hy_filter(l, hy_fw1[i], hy_fb1[i], hy_fw2[i], hy_fb2[i], hy_fw3[i], hy_fb3[i], fwd_l)
        u_h, x0_h, t_h = _hy_pre(hy, hy_conv_w[i], hy_conv_b[i], hy_skip[i], b, l, b)
        y_h = _hy_conv(u_h, fwd_l, inv_l, hre, him)
        if with_ctx:
            hre_c, him_c = _hy_filter(lc, hy_fw1[i], hy_fb1[i], hy_fw2[i], hy_fb2[i], hy_fw3[i], hy_fb3[i], fwd_c)
            u_c, x0_c, t_c = _hy_pre(hy, hy_conv_w[i], hy_conv_b[i], hy_skip[i], b, lc, b)
            y_c = _hy_conv(u_c, fwd_c, inv_c, hre_c, him_c)
            fold = lambda a_ctx: a_ctx.reshape(pc, l, a_ctx.shape[-1])
            pairs = [(o_b, fold(o_b_ctx)), (y_h, fold(y_c)), (x0_h, fold(x0_c)), (t_h, fold(t_c))]
        else:
            pairs = [(a, a) for a in (o_b, y_h, x0_h, t_h)]

        wa_rows = np.concatenate([np.concatenate([64 * j + np.arange(64), 64 * (j + 4) + np.arange(64)])
                                  for j in range(4)])
        x1 = _merge(x_pair, o_a, *pairs, gates, g1,
                    w_branch_a[i][wa_rows].astype(BF16), w_branch_b[i].astype(BF16),
                    w_branch_c[i].astype(BF16), w_out[i].astype(BF16),
                    ln1_g[i].reshape(1, d), ln1_b[i].reshape(1, d), npb)

        rw = jnp.pad(router_w[i], ((0, 0), (0, LANES - N_EXPERTS)))
        rb = jnp.concatenate([router_b[i], jnp.full((LANES - N_EXPERTS,), NEG_BIG, F32)]).reshape(1, LANES)
        b1p = jnp.concatenate([exp_b1[i][..., 0::2], exp_b1[i][..., 1::2]], axis=-1)[:, None, :]
        x_all = _moe(x1, sh2, sc2, g2, rw, rb, exp_w1, b1p, exp_w2, exp_b2[i][:, None, :], perm,
                     ln2_g[i].reshape(1, d), ln2_b[i].reshape(1, d), npb, i)
        x_pair = (x_all, x_all)
    return x_all
```

```python
import functools
import math

import numpy as np
import jax
import jax.numpy as jnp
from jax import lax
from jax.experimental import pallas as pl
from jax.experimental.pallas import tpu as pltpu

F32 = jnp.float32
BF16 = jnp.bfloat16

D_MODEL = 1024
DEPTH = 2
GRID_W = 64
ATT_HEADS = 8
ATT_KV_HEADS = 2
ATT_HEAD_DIM = 64
ROPE_THETA = 10000.0
GLA_HEADS = 4
GLA_DK = 64
GLA_DV = 128
GLA_RANK = 16
GLA_TAU = 16.0
GLA_CHUNK = 64
HY_WIDTH = 512
HY_POS_FREQS = 16
HY_DECAY_SLOW = math.log(1e-2) / 1.5
HY_DECAY_FAST = math.log(1e-2) / 0.3
N_EXPERTS = 32
TOP_K = 4
EXPERT_HIDDEN = D_MODEL
SWIGLU_LIMIT = 7.0
SWIGLU_ALPHA = 1.702
DEEPNORM_ALPHA = (2 * DEPTH) ** 0.25
LN_EPS = 1e-5
RMS_EPS = 1e-6

ATT_Q_W = ATT_HEADS * ATT_HEAD_DIM
ATT_KV_W = ATT_KV_HEADS * ATT_HEAD_DIM
GLA_K_W = GLA_HEADS * GLA_DK
GLA_V_W = GLA_HEADS * GLA_DV
GLA_GROUP_W = 1152
LANES = 128
NEG_BIG = -3.0e38

VMEM_LIMIT = 56 * 1024 * 1024


def _cparams(*sem):
    return pltpu.CompilerParams(dimension_semantics=sem, vmem_limit_bytes=VMEM_LIMIT)


def _dot(a, b):
    return jnp.dot(a, b, preferred_element_type=F32)


def _dot_nt(a, b):
    return lax.dot_general(a, b, (((1,), (1,)), ((), ())), preferred_element_type=F32)


def _dot_tn(a, b):
    return lax.dot_general(a, b, (((0,), (0,)), ((), ())), preferred_element_type=F32)


def _split2(x):
    hi = x.astype(BF16)
    lo = (x - hi.astype(F32)).astype(BF16)
    return hi, lo


def _split3(x):
    hi = x.astype(BF16)
    r = x - hi.astype(F32)
    mid = r.astype(BF16)
    lo = (r - mid.astype(F32)).astype(BF16)
    return hi, mid, lo


def _dot3(a, b):
    ah, al = _split2(a)
    bh, bl = _split2(b)
    return _dot(ah, bh) + (_dot(ah, bl) + _dot(al, bh))


def _dot_exact_lhs(a_bf16, b):
    b0, b1, b2 = _split3(b)
    return _dot(a_bf16, b0) + (_dot(a_bf16, b1) + _dot(a_bf16, b2))


def _ln(x):
    mu = jnp.mean(x, axis=-1, keepdims=True)
    xc = x - mu
    var = jnp.mean(xc * xc, axis=-1, keepdims=True)
    return xc * lax.rsqrt(var + LN_EPS)


def _sigmoid(x):
    return 1.0 / (1.0 + jnp.exp(-x))


def _mod_kernel(c_ref, w_ref, b_ref, o_ref):
    c = c_ref[...]
    o_ref[...] = _dot3(c * _sigmoid(c), w_ref[...]) + b_ref[...]


def _modulation(cc, w_mod, b_mod):
    rows, d = cc.shape
    n = w_mod.shape[1]
    tn = 1024
    return pl.pallas_call(
        _mod_kernel,
        grid=(n // tn,),
        in_specs=[pl.BlockSpec((rows, d), lambda j: (0, 0)),
                  pl.BlockSpec((d, tn), lambda j: (0, j)),
                  pl.BlockSpec((1, tn), lambda j: (0, j))],
        out_specs=pl.BlockSpec((rows, tn), lambda j: (0, j)),
        out_shape=jax.ShapeDtypeStruct((rows, n), F32),
        compiler_params=_cparams("arbitrary"),
        name="modulation",
    )(cc, w_mod, b_mod.reshape(1, n))


def _head_norm_rope(t, gain, rc, rs1, rs2, bd):
    sh, sl = _split2(t * t)
    ss = _dot(sh, bd) + _dot(sl, bd)
    tn = t * lax.rsqrt(ss * (1.0 / ATT_HEAD_DIM) + RMS_EPS) * gain
    return tn * rc + pltpu.roll(tn, 96, 1) * rs1 + pltpu.roll(tn, 32, 1) * rs2


def _pair_specs(tm, w, n_split, n_i):
    first = pl.BlockSpec((1, tm, w), lambda b, i: (jnp.minimum(b, n_split - 1), jnp.where(b < n_split, i, n_i - 1), 0))
    rest = pl.BlockSpec((1, tm, w), lambda b, i: (jnp.maximum(b - n_split, 0), jnp.where(b >= n_split, i, 0), 0))
    return [first, rest]


def _pick(use_rest, first_ref, rest_ref):
    return jnp.where(use_rest, rest_ref[0], first_ref[0])


def _inproj_kernel(x_ref, xr_ref, sh_ref, sc_ref, watt_ref, wgla_ref, wog_ref, why_ref, wgt_ref,
                   qg_ref, kg_ref, rc_ref, rs1_ref, rs2_ref, bd_ref,
                   qn_ref, kn_ref, v_ref, gla_ref, og_ref, hy_ref, gt_ref, *, n_split):
    x = _pick(pl.program_id(0) >= n_split, x_ref, xr_ref)
    u = _ln(x) * (1.0 + sc_ref[0]) + sh_ref[0]
    ub = u.astype(BF16)
    rc, rs1, rs2, bd = rc_ref[0], rs1_ref[0], rs2_ref[0], bd_ref[...]
    att = _dot(ub, watt_ref[...])
    for j in range(ATT_Q_W // LANES):
        t = att[:, LANES * j:LANES * (j + 1)]
        qn = _head_norm_rope(t, qg_ref[...], rc, rs1, rs2, bd) * (ATT_HEAD_DIM ** -0.5)
        qn_ref[0, :, LANES * j:LANES * (j + 1)] = qn.astype(BF16)
    kn = _head_norm_rope(att[:, ATT_Q_W:ATT_Q_W + LANES], kg_ref[...], rc, rs1, rs2, bd)
    kn_ref[0] = kn.astype(BF16)
    v_ref[0, :, 0:LANES] = att[:, ATT_Q_W + LANES:].astype(BF16)
    v_ref[0, :, LANES:2 * LANES] = jnp.ones((att.shape[0], LANES), BF16)
    gla_ref[0] = _dot(ub, wgla_ref[...])
    og_ref[0] = _dot(ub, wog_ref[...]).astype(BF16)
    hy_ref[0] = _dot(ub, why_ref[...]).astype(BF16)
    gt_ref[0] = _dot(ub, wgt_ref[...]).astype(BF16)


def _inproj(x_first, x_rest, sh, sc, wts, qg, kg, rope, bd, n_lat):
    n_split, l, d = x_first.shape
    p = sh.shape[0]
    tm = min(256, l)
    watt, wgla, wog, why, wgt = wts
    rc, rs1, rs2 = rope
    const = lambda a: pl.BlockSpec(a.shape, lambda b, i: (0,) * a.ndim)
    row = lambda w: pl.BlockSpec((1, tm, w), lambda b, i: (b, i, 0))
    tab = pl.BlockSpec((1, tm, LANES), lambda b, i: (jnp.where(b >= n_lat, 1, 0), i, 0))
    vec = pl.BlockSpec((1, 1, d), lambda b, i: (b, 0, 0))
    widths = (ATT_Q_W, LANES, 2 * LANES, GLA_GROUP_W, GLA_V_W, 3 * HY_WIDTH, 3 * d)
    dtypes = (BF16, BF16, BF16, F32, BF16, BF16, BF16)
    return pl.pallas_call(
        functools.partial(_inproj_kernel, n_split=n_split),
        grid=(p, l // tm),
        in_specs=_pair_specs(tm, d, n_split, l // tm) + [
            vec, vec, const(watt), const(wgla), const(wog), const(why), const(wgt),
            const(qg), const(kg), tab, tab, tab, const(bd)],
        out_specs=[row(w) for w in widths],
        out_shape=[jax.ShapeDtypeStruct((p, l, w), dt) for w, dt in zip(widths, dtypes)],
        compiler_params=_cparams("parallel", "arbitrary"),
        name="inproj",
    )(x_first, x_rest, sh, sc, watt, wgla, wog, why, wgt, qg, kg, rc, rs1, rs2, bd)


def _attend_group(qg, ks, vs, lo):
    zero = jnp.zeros_like(qg)
    outs = []
    for qh in (jnp.where(lo, qg, zero), jnp.where(lo, zero, qg)):
        ss = [_dot_nt(qh, k) for k in ks]
        m = ss[0].max(axis=-1, keepdims=True)
        for s in ss[1:]:
            m = jnp.maximum(m, s.max(axis=-1, keepdims=True))
        acc = 0.0
        for s, v in zip(ss, vs):
            acc = acc + _dot(jnp.exp((s - m).astype(BF16)), v)
        outs.append(acc[:, 0:LANES] / acc[:, LANES:2 * LANES])
    return jnp.where(lo, outs[0], outs[1])


def _attn_kernel(q_ref, k_ref, v_ref, kc_ref, vc_ref, o_ref, *, n_lat_tiles, with_ctx):
    i = pl.program_id(1)
    tq = q_ref.shape[1]
    lo = lax.broadcasted_iota(jnp.int32, (tq, LANES), 1) < ATT_HEAD_DIM

    def run(ks, vs):
        for j in range(ATT_Q_W // LANES):
            o = _attend_group(q_ref[0, :, LANES * j:LANES * (j + 1)], ks, vs, lo)
            o_ref[0, :, LANES * j:LANES * (j + 1)] = o.astype(BF16)

    if with_ctx:
        @pl.when(i < n_lat_tiles)
        def _():
            run([k_ref[0], kc_ref[0]], [v_ref[0], vc_ref[0]])

        @pl.when(i == n_lat_tiles)
        def _():
            run([kc_ref[0]], [vc_ref[0]])
    else:
        run([k_ref[0], kc_ref[0]], [v_ref[0], vc_ref[0]])


def _attention(qn, kn, v, n_lat, lc, with_ctx):
    p, l, _ = qn.shape
    tq = lc
    r = l // lc
    nt = l // tq
    ctx_idx = lambda b: (n_lat + b // r, b % r, 0)
    if with_ctx:
        qmap = lambda b, i: (jnp.where(i < nt, b, n_lat + b // r), jnp.where(i < nt, i, b % r), 0)
    else:
        qmap = lambda b, i: (b, i, 0)
    return pl.pallas_call(
        functools.partial(_attn_kernel, n_lat_tiles=nt, with_ctx=with_ctx),
        grid=(n_lat, nt + (1 if with_ctx else 0)),
        in_specs=[pl.BlockSpec((1, tq, ATT_Q_W), qmap),
                  pl.BlockSpec((1, l, LANES), lambda b, i: (b, 0, 0)),
                  pl.BlockSpec((1, l, 2 * LANES), lambda b, i: (b, 0, 0)),
                  pl.BlockSpec((1, lc, LANES), lambda b, i: ctx_idx(b)),
                  pl.BlockSpec((1, lc, 2 * LANES), lambda b, i: ctx_idx(b))],
        out_specs=pl.BlockSpec((1, tq, ATT_Q_W), qmap),
        out_shape=jax.ShapeDtypeStruct((p if with_ctx else n_lat, l, ATT_Q_W), BF16),
        compiler_params=_cparams("parallel", "arbitrary"),
        name="attention",
    )(qn, kn, v, kn, v)


def _gla_segment(src_ref, og_ref, out_ref, acc_ref, st_ref, qd_ref, ke_ref, vb_ref, dec_ref,
                 wdec_ref, bdec_ref, gn_ref, ls):
    c = GLA_CHUNK
    nc = ls // c
    rb = min(256, ls)

    bi = lax.broadcasted_iota(jnp.int32, (rb, rb), 0)
    bj = lax.broadcasted_iota(jnp.int32, (rb, rb), 1)
    same = (bi // c) == (bj // c)
    att_masks = (same & (bi >= bj), same & (bi <= bj))
    tri_blk = tuple(jnp.where(m, 1.0, 0.0).astype(BF16) for m in att_masks)
    ones_blk = jnp.where(same, 1.0, 0.0).astype(BF16)
    lo_b = lax.broadcasted_iota(jnp.int32, (rb, LANES), 1) < GLA_DK

    def prep(t, carry):
        r0 = pl.multiple_of(t * rb, rb)
        q = src_ref[0, pl.ds(r0, rb), 0:GLA_K_W] * (GLA_DK ** -0.5)
        k = src_ref[0, pl.ds(r0, rb), GLA_K_W:2 * GLA_K_W]
        vb = src_ref[0, pl.ds(r0, rb), 2 * GLA_K_W:2 * GLA_K_W + GLA_V_W].astype(BF16)
        vb_ref[pl.ds(r0, rb), :] = vb
        lr = src_ref[0, pl.ds(r0, rb), 2 * GLA_K_W + GLA_V_W:GLA_GROUP_W]
        la = jax.nn.log_sigmoid(_dot3(lr, wdec_ref[...]) + bdec_ref[...]) * (1.0 / GLA_TAU)
        for d in range(2):
            l0, l1, l2 = _split3(la[:, d * GLA_K_W:(d + 1) * GLA_K_W])
            cum = _dot(tri_blk[d], l0) + (_dot(tri_blk[d], l1) + _dot(tri_blk[d], l2))
            tot = _dot(ones_blk, l0) + (_dot(ones_blk, l1) + _dot(ones_blk, l2))
            qd = (q * jnp.exp(cum)).astype(BF16)
            ki = (k * jnp.exp(-cum)).astype(BF16)
            qd_ref[d, pl.ds(r0, rb), :] = qd
            ke_ref[d, pl.ds(r0, rb), :] = (k * jnp.exp(tot - cum)).astype(BF16)
            dec_ref[d, pl.ds(r0, rb), :] = jnp.exp(tot)
            for h in range(GLA_HEADS):
                g, half = h // 2, h % 2
                sl = slice(LANES * g, LANES * (g + 1))
                zero = jnp.zeros_like(qd[:, sl])
                qh = jnp.where(lo_b, qd[:, sl], zero) if half == 0 else jnp.where(lo_b, zero, qd[:, sl])
                att = jnp.where(att_masks[d], _dot_nt(qh, ki[:, sl]), 0.0)
                o = _dot(att.astype(BF16), vb[:, GLA_DV * h:GLA_DV * (h + 1)])
                if d == 0:
                    acc_ref[pl.ds(r0, rb), GLA_DV * h:GLA_DV * (h + 1)] = o
                else:
                    acc_ref[pl.ds(r0, rb), GLA_DV * h:GLA_DV * (h + 1)] += o
        return carry

    lax.fori_loop(0, ls // rb, prep, 0)

    lo = lax.broadcasted_iota(jnp.int32, (c, LANES), 1) < GLA_DK

    def body(n, carry):
        for d, cn in ((0, n), (1, nc - 1 - n)):
            r0 = pl.multiple_of(cn * c, c)
            dec = dec_ref[d, pl.ds(r0, 1), :]
            for h in range(GLA_HEADS):
                g, half = h // 2, h % 2
                sl = slice(LANES * g, LANES * (g + 1))
                qd = qd_ref[d, pl.ds(r0, c), sl]
                zero = jnp.zeros_like(qd)
                qd = jnp.where(lo, qd, zero) if half == 0 else jnp.where(lo, zero, qd)
                vh = vb_ref[pl.ds(r0, c), GLA_DV * h:GLA_DV * (h + 1)]
                st = st_ref[d * GLA_HEADS + h]
                acc_ref[pl.ds(r0, c), GLA_DV * h:GLA_DV * (h + 1)] += _dot_nt(qd, st.astype(BF16))
                st_ref[d * GLA_HEADS + h] = dec[:, sl] * st + _dot_tn(vh, ke_ref[d, pl.ds(r0, c), sl])
        return carry

    lax.fori_loop(0, nc, body, 0, unroll=4 if nc % 4 == 0 else (2 if nc % 2 == 0 else 1))

    tr = min(256, ls)

    def fin(t, carry):
        r0 = pl.multiple_of(t * tr, tr)
        for h in range(GLA_HEADS):
            sl = slice(GLA_DV * h, GLA_DV * (h + 1))
            o = acc_ref[pl.ds(r0, tr), sl]
            y = o * lax.rsqrt(jnp.mean(o * o, axis=-1, keepdims=True) + RMS_EPS)
            y = y * gn_ref[...]
            og = og_ref[0, pl.ds(r0, tr), sl].astype(F32)
            out_ref[0, pl.ds(r0, tr), sl] = (y * (og * _sigmoid(og))).astype(BF16)
        return carry

    lax.fori_loop(0, ls // tr, fin, 0)


def _gla_kernel(gl_ref, glc_ref, og_ref, ogc_ref, wdec_ref, bdec_ref, gn_ref, o_ref, oc_ref,
                acc_ref, st_ref, qd_ref, ke_ref, vb_ref, dec_ref):
    st_ref[...] = jnp.zeros(st_ref.shape, F32)
    scratch = (acc_ref, st_ref, qd_ref, ke_ref, vb_ref, dec_ref)
    _gla_segment(glc_ref, ogc_ref, oc_ref, *scratch, wdec_ref, bdec_ref, gn_ref, glc_ref.shape[1])
    _gla_segment(gl_ref, og_ref, o_ref, *scratch, wdec_ref, bdec_ref, gn_ref, gl_ref.shape[1])


def _gla(gla, og, wdec, bdec, gn, n_lat, lc):
    p, l, _ = gla.shape
    r = l // lc
    ctx_idx = lambda b: (n_lat + b // r, b % r, 0)
    const = lambda a: pl.BlockSpec(a.shape, lambda b: (0,) * a.ndim)
    return pl.pallas_call(
        _gla_kernel,
        grid=(n_lat,),
        in_specs=[pl.BlockSpec((1, l, GLA_GROUP_W), lambda b: (b, 0, 0)),
                  pl.BlockSpec((1, lc, GLA_GROUP_W), ctx_idx),
                  pl.BlockSpec((1, l, GLA_V_W), lambda b: (b, 0, 0)),
                  pl.BlockSpec((1, lc, GLA_V_W), ctx_idx),
                  const(wdec), const(bdec), const(gn)],
        out_specs=[pl.BlockSpec((1, l, GLA_V_W), lambda b: (b, 0, 0)),
                   pl.BlockSpec((1, lc, GLA_V_W), lambda b: (b, 0, 0))],
        out_shape=[jax.ShapeDtypeStruct((n_lat, l, GLA_V_W), BF16),
                   jax.ShapeDtypeStruct((n_lat, lc, GLA_V_W), BF16)],
        scratch_shapes=[pltpu.VMEM((l, GLA_V_W), F32),
                        pltpu.VMEM((2 * GLA_HEADS, GLA_DV, LANES), F32),
                        pltpu.VMEM((2, l, GLA_K_W), BF16),
                        pltpu.VMEM((2, l, GLA_K_W), BF16), pltpu.VMEM((l, GLA_V_W), BF16),
                        pltpu.VMEM((2, l, GLA_K_W), F32)],
        compiler_params=_cparams("arbitrary"),
        name="gla",
    )(gla, gla, og, og, wdec, bdec, gn)


def _dft_tables(l):
    n = 2 * l
    k = np.arange(l, dtype=np.int64)[:, None]
    t = np.arange(l, dtype=np.int64)[None, :]
    ang = 2.0 * np.pi * ((k * t) % n).astype(np.float64) / n
    fre = np.cos(ang)
    fim = -np.sin(ang)
    fim[0, :] = np.where(np.arange(l) % 2 == 0, 1.0, -1.0)
    fwd = np.concatenate([fre, fim], axis=0)
    return jnp.asarray(fwd, dtype=BF16), jnp.asarray(fwd.T, dtype=BF16)


def _hy_filter_kernel(z_ref, win_ref, w1_ref, b1_ref, w2_ref, b2_ref, w3_ref, b3_ref, fre_ref, fim_ref,
                      hre_ref, him_ref, hf_ref):
    j = pl.program_id(0)
    l = z_ref.shape[0]

    @pl.when(j == 0)
    def _():
        h = jnp.sin(_dot3(z_ref[...], w1_ref[...]) + b1_ref[...])
        h = jnp.sin(_dot3(h, w2_ref[...]) + b2_ref[...])
        h = _dot3(h, w3_ref[...]) + b3_ref[...]
        win = win_ref[...]
        hf = h[:, :HY_WIDTH] * win
        hb = h[:, HY_WIDTH:] * win
        row = lax.broadcasted_iota(jnp.int32, (l, HY_WIDTH), 0)
        hf_ref[:, :HY_WIDTH] = hf.astype(BF16)
        hf_ref[:, HY_WIDTH:] = jnp.where(row == 0, 0.0, hb).astype(BF16)

    tk = fre_ref.shape[0]
    a = _dot(fre_ref[...], hf_ref[...])
    b = _dot(fim_ref[...], hf_ref[...])
    first = (lax.broadcasted_iota(jnp.int32, (tk, HY_WIDTH), 0) == 0) & (j == 0)
    scl = jnp.where(first, 1.0 / (2 * l), 2.0 / (2 * l))
    hre_ref[...] = (a[:, :HY_WIDTH] + a[:, HY_WIDTH:]) * scl
    him_ref[...] = (b[:, :HY_WIDTH] + jnp.where(first, 1.0, -1.0) * b[:, HY_WIDTH:]) * scl


def _hy_filter(l, fw1, fb1, fw2, fb2, fw3, fb3, fwd):
    t = np.arange(l, dtype=np.float32)[:, None] / np.float32(l)
    freqs = np.arange(1, HY_POS_FREQS + 1, dtype=np.float32)
    z = np.concatenate([t, np.cos(2.0 * math.pi * freqs * t), np.sin(2.0 * math.pi * freqs * t)], axis=-1)
    z = np.pad(z, ((0, 0), (0, LANES - z.shape[1])))
    hid = fw2.shape[0]
    fw1 = jnp.pad(fw1, ((0, LANES - fw1.shape[0]), (0, LANES - hid)))
    fb1 = jnp.pad(fb1, (0, LANES - hid))
    fw2 = jnp.pad(fw2, ((0, LANES - hid), (0, LANES - hid)))
    fb2 = jnp.pad(fb2, (0, LANES - hid))
    fw3 = jnp.pad(fw3, ((0, LANES - hid), (0, 0)))
    deltas = np.abs(np.linspace(HY_DECAY_SLOW, HY_DECAY_FAST, HY_WIDTH, dtype=np.float32))
    win = np.exp(-t * deltas).astype(np.float32)
    tk = min(512, l)
    nk = l // tk
    const = lambda a: pl.BlockSpec(a.shape, lambda j: (0,) * a.ndim)
    args = (jnp.asarray(z), jnp.asarray(win), fw1, fb1.reshape(1, -1), fw2, fb2.reshape(1, -1),
            fw3, fb3.reshape(1, -1))
    return pl.pallas_call(
        _hy_filter_kernel,
        grid=(nk,),
        in_specs=[const(a) for a in args] + [pl.BlockSpec((tk, l), lambda j: (j, 0)),
                                              pl.BlockSpec((tk, l), lambda j: (nk + j, 0))],
        out_specs=[pl.BlockSpec((tk, HY_WIDTH), lambda j: (j, 0))] * 2,
        out_shape=[jax.ShapeDtypeStruct((l, HY_WIDTH), F32)] * 2,
        scratch_shapes=[pltpu.VMEM((l, 2 * HY_WIDTH), BF16)],
        compiler_params=_cparams("arbitrary"),
        name="hyena_filter",
    )(*args, fwd, fwd)


def _hy_pre_kernel(p0_ref, p1_ref, p2_ref, w0_ref, w1_ref, w2_ref, b0_ref, b1_ref, b2_ref, skip_ref,
                   u_ref, x0_ref, t_ref):
    ls = p0_ref.shape[1]
    row = lax.broadcasted_iota(jnp.int32, (ls, LANES), 0)

    def conv(p_ref, w_ref, b_ref):
        p = p_ref[0].astype(F32)
        prev = jnp.where(row == 0, 0.0, pltpu.roll(p, 1, 0))
        nxt = jnp.where(row == ls - 1, 0.0, pltpu.roll(p, ls - 1, 0))
        return prev * w_ref[0:1, :] + p * w_ref[1:2, :] + nxt * w_ref[2:3, :] + b_ref[...]

    x0 = conv(p0_ref, w0_ref, b0_ref)
    u = conv(p1_ref, w1_ref, b1_ref) * conv(p2_ref, w2_ref, b2_ref)
    u_ref[0] = u.astype(BF16)
    x0_ref[0] = x0.astype(BF16)
    t_ref[0] = (x0 * u * skip_ref[...]).astype(BF16)


def _hy_pre(hy, conv_w, conv_b, skip, nseq, ls, n_lat):
    p, l, _ = hy.shape
    r = l // ls
    nj = HY_WIDTH // LANES
    base = 0 if ls == l else n_lat
    src = lambda off: pl.BlockSpec((1, ls, LANES), lambda b, j: (base + b // r, b % r, off * nj + j))
    wsp = lambda off: pl.BlockSpec((3, LANES), lambda b, j: (0, off * nj + j))
    bsp = lambda off: pl.BlockSpec((1, LANES), lambda b, j: (0, off * nj + j))
    out = pl.BlockSpec((1, ls, LANES), lambda b, j: (b, 0, j))
    cb = conv_b.reshape(1, -1)
    return pl.pallas_call(
        _hy_pre_kernel,
        grid=(nseq, nj),
        in_specs=[src(0), src(1), src(2), wsp(0), wsp(1), wsp(2), bsp(0), bsp(1), bsp(2), bsp(0)],
        out_specs=[out] * 3,
        out_shape=[jax.ShapeDtypeStruct((nseq, ls, HY_WIDTH), BF16)] * 3,
        compiler_params=_cparams("parallel", "arbitrary"),
        name="hyena_pre",
    )(hy, hy, hy, conv_w, conv_w, conv_w, cb, cb, cb, skip.reshape(1, -1))


def _hy_conv_kernel(u_ref, fre_ref, fim_ref, ire_ref, iim_ref, hre_ref, him_ref, y_ref):
    j = pl.program_id(1)
    u = u_ref[0]
    ure = _dot(fre_ref[...], u)
    uim = _dot(fim_ref[...], u)
    hre, him = hre_ref[...], him_ref[...]
    first = (lax.broadcasted_iota(jnp.int32, ure.shape, 0) == 0) & (j == 0)
    yre = jnp.where(first, ure * hre, ure * hre - uim * him)
    yim = jnp.where(first, uim * him, ure * him + uim * hre)
    y = _dot(ire_ref[...], yre.astype(BF16)) + _dot(iim_ref[...], yim.astype(BF16))

    @pl.when(j == 0)
    def _():
        y_ref[0] = y

    @pl.when(j > 0)
    def _():
        y_ref[0] += y


def _hy_conv(u, fwd, inv, hre, him):
    nseq, ls, _ = u.shape
    tk = min(512, ls)
    nk = ls // tk
    return pl.pallas_call(
        _hy_conv_kernel,
        grid=(nseq, nk),
        in_specs=[pl.BlockSpec((1, ls, HY_WIDTH), lambda b, j: (b, 0, 0)),
                  pl.BlockSpec((tk, ls), lambda b, j: (j, 0)),
                  pl.BlockSpec((tk, ls), lambda b, j: (nk + j, 0)),
                  pl.BlockSpec((ls, tk), lambda b, j: (0, j)),
                  pl.BlockSpec((ls, tk), lambda b, j: (0, nk + j)),
                  pl.BlockSpec((tk, HY_WIDTH), lambda b, j: (j, 0)),
                  pl.BlockSpec((tk, HY_WIDTH), lambda b, j: (j, 0))],
        out_specs=pl.BlockSpec((1, ls, HY_WIDTH), lambda b, j: (b, 0, 0)),
        out_shape=jax.ShapeDtypeStruct((nseq, ls, HY_WIDTH), F32),
        compiler_params=_cparams("parallel", "arbitrary"),
        name="hyena_conv",
    )(u, fwd, fwd, inv, inv, hre, him)


def _merge_kernel(x_ref, xr_ref, oa_ref, ob_ref, obr_ref, y_ref, yr_ref, x0_ref, x0r_ref, t_ref, tr_ref,
                  gt_ref, g1_ref, wa_ref, wb_ref, wc_ref, wo_ref, lg_ref, lb_ref, o_ref, *, n_split):
    d = x_ref.shape[2]
    rest = pl.program_id(0) >= n_split
    x = _pick(rest, x_ref, xr_ref)
    ob = _pick(rest, ob_ref, obr_ref)
    oc = (_pick(rest, x0_ref, x0r_ref).astype(F32) * _pick(rest, y_ref, yr_ref)
          + _pick(rest, t_ref, tr_ref).astype(F32)).astype(BF16)
    gt = gt_ref[0]
    m = (_sigmoid(gt[:, 0:d].astype(F32)) * _dot(oa_ref[0], wa_ref[...])
         + _sigmoid(gt[:, d:2 * d].astype(F32)) * _dot(ob, wb_ref[...])
         + _sigmoid(gt[:, 2 * d:3 * d].astype(F32)) * _dot(oc, wc_ref[...]))
    mix = _dot(m.astype(BF16), wo_ref[...])
    o_ref[0] = _ln(DEEPNORM_ALPHA * x + g1_ref[0] * mix) * lg_ref[...] + lb_ref[...]


def _merge(x, o_a, o_b, y, x0, t, gates, g1, wa, wb, wc, wo, lg, lb, npb):
    n_split, l, d = x[0].shape
    tm = min(512, l)
    n_i = l // tm
    row = lambda w: pl.BlockSpec((1, tm, w), lambda b, i: (b, i, 0))
    const = lambda a: pl.BlockSpec(a.shape, lambda b, i: (0,) * a.ndim)
    pair = lambda w: _pair_specs(tm, w, n_split, n_i)
    return pl.pallas_call(
        functools.partial(_merge_kernel, n_split=n_split),
        grid=(npb, n_i),
        in_specs=pair(d) + [row(ATT_Q_W)] + pair(GLA_V_W) + pair(HY_WIDTH) + pair(HY_WIDTH) + pair(HY_WIDTH) + [
            row(3 * d), pl.BlockSpec((1, 1, d), lambda b, i: (b, 0, 0)),
            const(wa), const(wb), const(wc), const(wo), const(lg), const(lb)],
        out_specs=row(d),
        out_shape=jax.ShapeDtypeStruct((npb, l, d), F32),
        compiler_params=_cparams("parallel", "arbitrary"),
        name="merge",
    )(*x, o_a, *o_b, *y, *x0, *t, gates, g1, wa, wb, wc, wo, lg, lb)


MOE_T = 256
MOE_G = 16
MOE_TM = 1024
MOE_SLOTS = TOP_K * MOE_T + 512
AUG_W = D_MODEL + LANES


def _route_kernel(x_ref, sh_ref, sc_ref, rw_ref, rb_ref, xa_ref, sel_ref, cnt_ref):
    d = x_ref.shape[2]
    u = _ln(x_ref[0]) * (1.0 + sc_ref[0]) + sh_ref[0]
    xa_ref[0, :, 0:d] = u.astype(BF16)
    logits = _dot3(u, rw_ref[...]) + rb_ref[...]
    lane = lax.broadcasted_iota(jnp.int32, logits.shape, 1).astype(F32)
    work = logits
    hots, vals = [], []
    for _ in range(TOP_K):
        m = work.max(axis=-1, keepdims=True)
        idx = jnp.where(work == m, lane, float(LANES)).min(axis=-1, keepdims=True)
        hot = lane == idx
        hots.append(hot)
        vals.append(m)
        work = jnp.where(hot, NEG_BIG, work)
    es = [jnp.exp(v - vals[0]) for v in vals]
    den = es[0] + es[1] + es[2] + es[3]
    comb = jnp.zeros_like(logits)
    sel = jnp.zeros_like(logits)
    for hot, e in zip(hots, es):
        comb = comb + jnp.where(hot, e / den, 0.0)
        sel = sel + jnp.where(hot, 1.0, 0.0)
    c0, c1, c2 = _split3(comb)
    aug = c0.astype(F32) + pltpu.roll(c1.astype(F32), 32, 1) + pltpu.roll(c2.astype(F32), 64, 1)
    xa_ref[0, :, d:d + LANES] = aug.astype(BF16)
    sel_ref[0] = sel.astype(BF16)
    cnt_ref[0] = sel.sum(axis=0, keepdims=True)


def _route(x_all, sh, sc, rw, rb, npb):
    _, l, d = x_all.shape
    t = MOE_T
    nt_b = l // t
    row = lambda w: pl.BlockSpec((1, t, w), lambda b, i: (b, i, 0))
    vec = pl.BlockSpec((1, 1, d), lambda b, i: (b, 0, 0))
    const = lambda a: pl.BlockSpec(a.shape, lambda b, i: (0,) * a.ndim)
    return pl.pallas_call(
        _route_kernel,
        grid=(npb, nt_b),
        in_specs=[row(d), vec, vec, const(rw), const(rb)],
        out_specs=[row(AUG_W), row(LANES), pl.BlockSpec((1, 1, LANES), lambda b, i: (b * nt_b + i, 0, 0))],
        out_shape=[jax.ShapeDtypeStruct((npb, l, AUG_W), BF16), jax.ShapeDtypeStruct((npb, l, LANES), BF16),
                   jax.ShapeDtypeStruct((npb * nt_b, 1, LANES), F32)],
        compiler_params=_cparams("parallel", "arbitrary"),
        name="route",
    )(x_all, sh, sc, rw, rb)


def _group_copies(j, cntp_ref, loc_ref, goff_ref, make_copy):
    for e in range(N_EXPERTS):
        n = lax.shift_right_logical(cntp_ref[j * N_EXPERTS + e], int(math.log2(MOE_G)))
        loc = loc_ref[j * N_EXPERTS + e]
        off = goff_ref[j * N_EXPERTS + e]
        n2 = lax.shift_right_logical(n, 1)

        def per_pair(c, carry, loc=loc, off=off):
            make_copy(pl.multiple_of(loc + c * (2 * MOE_G), MOE_G),
                      pl.multiple_of(off + c * (2 * MOE_G), MOE_G), 2 * MOE_G).start()
            return carry

        lax.fori_loop(0, n2, per_pair, 0)

        @pl.when((n & 1) == 1)
        def _(loc=loc, off=off, n2=n2):
            make_copy(pl.multiple_of(loc + n2 * (2 * MOE_G), MOE_G),
                      pl.multiple_of(off + n2 * (2 * MOE_G), MOE_G), MOE_G).start()


def _wait_copies(nch_ref, j, make_copy):
    for k, rows in ((0, 2 * MOE_G), (1, MOE_G)):
        def wait_one(c, carry, rows=rows):
            make_copy(0, 0, rows).wait()
            return carry

        lax.fori_loop(0, nch_ref[2 * j + k], wait_one, 0)


def _dispatch_kernel(cntp_ref, loc_ref, goff_ref, nch_ref, tail_ref, tailn_ref,
                     xa_ref, sel_ref, locv_ref, pt_ref, xs_ref, stage_ref, zero_ref, sems):
    j = pl.program_id(0)
    nt = pl.num_programs(0)
    buf = j % 2
    sem = sems.at[buf]
    stage = stage_ref.at[buf]
    t = MOE_T

    def out_copy(src, dst, rows):
        return pltpu.make_async_copy(stage.at[pl.ds(src, rows)], xs_ref.at[pl.ds(dst, rows)], sem)

    @pl.when(j >= 2)
    def _():
        _wait_copies(nch_ref, jnp.maximum(j - 2, 0), out_copy)

    sel = sel_ref[0]
    ri = lax.broadcasted_iota(jnp.int32, (t, t), 0)
    ci = lax.broadcasted_iota(jnp.int32, (t, t), 1)
    rank = _dot(jnp.where(ci < ri, 1.0, 0.0).astype(BF16), sel)
    dest = rank + locv_ref[0]
    lane = lax.broadcasted_iota(jnp.int32, (t, LANES), 1).astype(F32)
    slot = lax.broadcasted_iota(jnp.int32, (t, MOE_SLOTS), 1)
    avail = sel.astype(F32)
    pt = jnp.zeros((t, MOE_SLOTS), F32)
    for _ in range(TOP_K):
        ek = jnp.where(avail > 0.0, lane, float(LANES)).min(axis=-1, keepdims=True)
        hot = lane == ek
        dk = jnp.where(hot, dest, 0.0).sum(axis=-1, keepdims=True).astype(jnp.int32)
        pt = pt + jnp.where(slot == dk, 1.0, 0.0)
        avail = jnp.where(hot, 0.0, avail)
    ptb = pt.astype(BF16)
    pt_ref[0] = ptb
    stage[...] = _dot_tn(ptb, xa_ref[0]).astype(BF16)
    _group_copies(j, cntp_ref, loc_ref, goff_ref, out_copy)

    @pl.when(j == nt - 1)
    def _():
        _wait_copies(nch_ref, j, out_copy)

        @pl.when(j >= 1)
        def _():
            other = sems.at[1 - buf]
            _wait_copies(nch_ref, jnp.maximum(j - 1, 0), lambda s, d, rows: pltpu.make_async_copy(
                stage_ref.at[1 - buf].at[pl.ds(s, rows)], xs_ref.at[pl.ds(d, rows)], other))

        zero_ref[...] = jnp.zeros(zero_ref.shape, BF16)

        def zcopy(dst):
            return pltpu.make_async_copy(zero_ref.at[pl.ds(0, MOE_G)], xs_ref.at[pl.ds(dst, MOE_G)], sem)

        def zcopy_big(dst):
            return pltpu.make_async_copy(zero_ref, xs_ref.at[pl.ds(dst, MOE_TM)], sem)

        def per_expert(e, tot):
            n = tailn_ref[e]
            base = tail_ref[e]

            def per_chunk(c, carry):
                zcopy(pl.multiple_of(base + c * MOE_G, MOE_G)).start()
                return carry

            lax.fori_loop(0, n, per_chunk, 0)
            return tot + n

        ztotal = lax.fori_loop(0, N_EXPERTS, per_expert, 0)

        def zwait(c, carry):
            zcopy(0).wait()
            return carry

        lax.fori_loop(0, ztotal, zwait, 0)

        nbig = tailn_ref[N_EXPERTS]
        big0 = tail_ref[N_EXPERTS]

        def big_start(c, carry):
            zcopy_big(pl.multiple_of(big0 + c * MOE_TM, MOE_TM)).start()
            return carry

        def big_wait(c, carry):
            zcopy_big(0).wait()
            return carry

        lax.fori_loop(0, nbig, big_start, 0)
        lax.fori_loop(0, nbig, big_wait, 0)


def _dispatch(xaug, sel, locv, cntp, loc, goff, nch, tail, tailn, s_rows):
    npb, l, _ = xaug.shape
    t = MOE_T
    nt_b = l // t
    nt = npb * nt_b
    tile = lambda w: pl.BlockSpec((1, t, w), lambda j, *_: (j // nt_b, j % nt_b, 0))
    return pl.pallas_call(
        _dispatch_kernel,
        grid_spec=pltpu.PrefetchScalarGridSpec(
            num_scalar_prefetch=6,
            grid=(nt,),
            in_specs=[tile(AUG_W), tile(LANES), pl.BlockSpec((1, 1, LANES), lambda j, *_: (j, 0, 0))],
            out_specs=[pl.BlockSpec((1, t, MOE_SLOTS), lambda j, *_: (j, 0, 0)),
                       pl.BlockSpec(memory_space=pl.ANY)],
            scratch_shapes=[pltpu.VMEM((2, MOE_SLOTS, AUG_W), BF16), pltpu.VMEM((MOE_TM, AUG_W), BF16),
                            pltpu.SemaphoreType.DMA((2,))]),
        out_shape=[jax.ShapeDtypeStruct((nt, t, MOE_SLOTS), BF16), jax.ShapeDtypeStruct((s_rows, AUG_W), BF16)],
        compiler_params=_cparams("arbitrary"),
        name="moe_dispatch",
    )(cntp, loc, goff, nch, tail, tailn, xaug, sel, locv)


def _ffn_kernel(te_ref, na_ref, xs_ref, w1_ref, b1_ref, w2_ref, b2_ref, perm_ref, ys_ref, w1p_ref, w2b_ref):
    i = pl.program_id(0)
    e = te_ref[i]
    active = i < na_ref[0]
    fresh = jnp.logical_or(i == 0, e != te_ref[jnp.maximum(i - 1, 0)])
    hh = EXPERT_HIDDEN // 2

    @pl.when(jnp.logical_and(active, fresh))
    def _():
        for c in range(2):
            blk = w1_ref[0, 0, :, 2 * hh * c:2 * hh * (c + 1)].astype(BF16)
            r = _dot(blk, perm_ref[...])
            w1p_ref[:, hh * c:hh * (c + 1)] = r[:, :hh].astype(BF16)
            w1p_ref[:, EXPERT_HIDDEN + hh * c:EXPERT_HIDDEN + hh * (c + 1)] = r[:, hh:].astype(BF16)
        w2b_ref[...] = w2_ref[0, 0].astype(BF16)

    @pl.when(active)
    def _():
        x = xs_ref[:, 0:D_MODEL]
        extra = xs_ref[:, D_MODEL:AUG_W].astype(F32)
        lane = lax.broadcasted_iota(jnp.int32, extra.shape, 1)
        mine = jnp.logical_and(lane % N_EXPERTS == e, lane < 3 * N_EXPERTS)
        wslot = jnp.where(mine, extra, 0.0).sum(axis=-1, keepdims=True)
        h = _dot(x, w1p_ref[...]) + b1_ref[0]
        g = jnp.minimum(h[:, :EXPERT_HIDDEN], SWIGLU_LIMIT)
        up = jnp.clip(h[:, EXPERT_HIDDEN:], -SWIGLU_LIMIT, SWIGLU_LIMIT)
        act = g * _sigmoid(SWIGLU_ALPHA * g) * (up + 1.0)
        y = _dot(act.astype(BF16), w2b_ref[...]) + b2_ref[0]
        ys_ref[...] = (wslot * y).astype(BF16)

    @pl.when(jnp.logical_not(active))
    def _():
        ys_ref[...] = jnp.zeros(ys_ref.shape, BF16)


def _ffn(xs, tile_e, n_active, w1, b1p, w2, b2, perm, n_tiles, layer):
    tm = MOE_TM
    d, h2 = w1.shape[2], w1.shape[3]
    once = pl.Buffered(1)
    return pl.pallas_call(
        _ffn_kernel,
        grid_spec=pltpu.PrefetchScalarGridSpec(
            num_scalar_prefetch=2,
            grid=(n_tiles,),
            in_specs=[pl.BlockSpec((tm, AUG_W), lambda i, te, na: (jnp.where(i < na[0], i, 0), 0)),
                      pl.BlockSpec((1, 1, d, h2), lambda i, te, na: (layer, te[i], 0, 0), pipeline_mode=once),
                      pl.BlockSpec((1, 1, h2), lambda i, te, na: (te[i], 0, 0)),
                      pl.BlockSpec((1, 1, h2 // 2, d), lambda i, te, na: (layer, te[i], 0, 0), pipeline_mode=once),
                      pl.BlockSpec((1, 1, d), lambda i, te, na: (te[i], 0, 0)),
                      pl.BlockSpec(perm.shape, lambda i, te, na: (0, 0), pipeline_mode=once)],
            out_specs=pl.BlockSpec((tm, d), lambda i, te, na: (i, 0)),
            scratch_shapes=[pltpu.VMEM((d, h2), BF16), pltpu.VMEM((h2 // 2, d), BF16)]),
        out_shape=jax.ShapeDtypeStruct((n_tiles * tm, d), BF16),
        compiler_params=_cparams("arbitrary"),
        name="moe_ffn",
    )(tile_e, n_active, xs, w1, b1p, w2, b2, perm)


def _combine_kernel(cntp_ref, loc_ref, goff_ref, nch_ref, pt_ref, x_ref, g2_ref, lg_ref, lb_ref, ys_ref,
                    o_ref, stage_ref, sems):
    j = pl.program_id(0)
    nt = pl.num_programs(0)
    buf = j % 2

    def in_copy(b):
        return lambda dst, src, rows: pltpu.make_async_copy(
            ys_ref.at[pl.ds(src, rows)], stage_ref.at[b].at[pl.ds(dst, rows)], sems.at[b])

    @pl.when(j == 0)
    def _():
        stage_ref[...] = jnp.zeros(stage_ref.shape, BF16)
        _group_copies(j, cntp_ref, loc_ref, goff_ref, in_copy(buf))

    @pl.when(j + 1 < nt)
    def _():
        _group_copies(j + 1, cntp_ref, loc_ref, goff_ref, in_copy(1 - buf))

    _wait_copies(nch_ref, j, in_copy(buf))
    y = _dot(pt_ref[0], stage_ref[buf])
    o_ref[0] = _ln(DEEPNORM_ALPHA * x_ref[0] + g2_ref[0] * y) * lg_ref[...] + lb_ref[...]


def _combine(pt, ys, x1, g2, lg, lb, cntp, loc, goff, nch):
    npb, l, d = x1.shape
    t = MOE_T
    nt_b = l // t
    tile = lambda w: pl.BlockSpec((1, t, w), lambda j, *_: (j // nt_b, j % nt_b, 0))
    const = lambda a: pl.BlockSpec(a.shape, lambda j, *_: (0,) * a.ndim)
    return pl.pallas_call(
        _combine_kernel,
        grid_spec=pltpu.PrefetchScalarGridSpec(
            num_scalar_prefetch=4,
            grid=(npb * nt_b,),
            in_specs=[pl.BlockSpec((1, t, MOE_SLOTS), lambda j, *_: (j, 0, 0)), tile(d),
                      pl.BlockSpec((1, 1, d), lambda j, *_: (j // nt_b, 0, 0)), const(lg), const(lb),
                      pl.BlockSpec(memory_space=pl.ANY)],
            out_specs=tile(d),
            scratch_shapes=[pltpu.VMEM((2, MOE_SLOTS, d), BF16), pltpu.SemaphoreType.DMA((2,))]),
        out_shape=jax.ShapeDtypeStruct((npb, l, d), F32),
        compiler_params=_cparams("arbitrary"),
        name="moe_combine",
    )(cntp, loc, goff, nch, pt, x1, g2, lg, lb, ys)


def _moe(x1, sh2, sc2, g2, rw, rb, w1, b1p, w2, b2, perm, lg, lb, npb, layer):
    _, l, d = x1.shape
    n_tok = npb * l
    nt = n_tok // MOE_T
    xaug, sel, cnt = _route(x1, sh2, sc2, rw, rb, npb)

    i32 = jnp.int32
    cnt = cnt[:, 0, :N_EXPERTS].astype(i32)
    cntp = (cnt + (MOE_G - 1)) // MOE_G * MOE_G
    loc = jnp.cumsum(cntp, axis=1) - cntp
    tot = cntp.sum(axis=0)
    seg = (tot + (MOE_TM - 1)) // MOE_TM * MOE_TM
    seg_start = jnp.cumsum(seg) - seg
    goff = seg_start[None, :] + jnp.cumsum(cntp, axis=0) - cntp
    s_max = TOP_K * n_tok + nt * N_EXPERTS * (MOE_G - 1) + N_EXPERTS * (MOE_TM - MOE_G)
    n_tiles = -(-s_max // MOE_TM)
    cum_tiles = jnp.cumsum(seg // MOE_TM)
    tile_e = jnp.minimum((jnp.arange(n_tiles, dtype=i32)[:, None] >= cum_tiles[None, :]).astype(i32).sum(axis=1),
                         N_EXPERTS - 1)
    n_active = cum_tiles[-1:].astype(i32)
    gran = cntp // MOE_G
    nch = jnp.stack([(gran // 2).sum(axis=1), (gran % 2).sum(axis=1)], axis=1).reshape(-1).astype(i32)
    locv = jnp.pad(loc.astype(F32), ((0, 0), (0, LANES - N_EXPERTS)))[:, None, :]
    flat = lambda a: a.reshape(-1).astype(i32)

    tail = jnp.concatenate([seg_start + tot, n_active * MOE_TM]).astype(i32)
    tailn = jnp.concatenate([(seg - tot) // MOE_G, n_tiles - n_active]).astype(i32)
    pt, xs = _dispatch(xaug, sel, locv, flat(cntp), flat(loc), flat(goff), nch, tail, tailn, n_tiles * MOE_TM)
    ys = _ffn(xs, tile_e, n_active, w1, b1p, w2, b2, perm, n_tiles, layer)
    return _combine(pt, ys, x1, g2, lg, lb, flat(cntp), flat(loc), flat(goff), nch)


def _deinterleave(n):
    return np.concatenate([np.arange(0, n, 2), np.arange(1, n, 2)])


def _rope_tables(l):
    rows = l // GRID_W
    r = np.repeat(np.arange(rows, dtype=np.float32), GRID_W)
    col = np.tile(np.arange(GRID_W, dtype=np.float32), rows)
    axis_dim = ATT_HEAD_DIM // 2
    inv_freq = (ROPE_THETA ** (-np.arange(0, axis_dim, 2, dtype=np.float32) / axis_dim)).astype(np.float32)
    ang = np.concatenate([r[:, None] * inv_freq, col[:, None] * inv_freq], axis=-1)
    c, s = np.cos(ang).astype(np.float32), np.sin(ang).astype(np.float32)
    z = np.zeros_like(s)
    rc = np.tile(np.concatenate([c, c], -1), (1, 2))
    rs1 = np.tile(np.concatenate([-s, z], -1), (1, 2))
    rs2 = np.tile(np.concatenate([z, s], -1), (1, 2))
    ident = (np.ones_like(rc), np.zeros_like(rc), np.zeros_like(rc))
    return tuple(jnp.asarray(np.stack([a, b])) for a, b in zip((rc, rs1, rs2), ident))


def kernel(x, c, ctx, c_ctx, w_mod, b_mod, w_in, att_q_norm, att_k_norm, gla_wdec_f, gla_bdec_f, gla_wdec_b, gla_bdec_b, gla_out_norm, hy_conv_w, hy_conv_b, hy_fw1, hy_fb1, hy_fw2, hy_fb2, hy_fw3, hy_fb3, hy_skip, w_branch_a, w_branch_b, w_branch_c, w_out, ln1_g, ln1_b, router_w, router_b, exp_w1, exp_b1, exp_w2, exp_b2, ln2_g, ln2_b):
    b, l, d = x.shape
    lc = ctx.shape[1]
    assert d == D_MODEL and (b * lc) % l == 0 and l % lc == 0 and lc % GLA_CHUNK == 0
    pc = (b * lc) // l
    p = b + pc

    perm64 = _deinterleave(ATT_HEAD_DIM)
    q_cols = np.concatenate([np.concatenate([64 * j + perm64, 64 * (j + 4) + perm64]) for j in range(4)])
    k_cols = ATT_Q_W + np.concatenate([perm64, 64 + perm64])
    v_cols = ATT_Q_W + ATT_KV_W + np.arange(ATT_KV_W)
    att_cols = np.concatenate([q_cols, k_cols, v_cols])
    o0 = ATT_Q_W + 2 * ATT_KV_W
    gla_cols = o0 + np.concatenate([np.arange(0, 2 * GLA_K_W + GLA_V_W + 2 * GLA_RANK)])
    og0 = o0 + 2 * GLA_K_W + GLA_V_W + 2 * GLA_RANK
    hy0 = og0 + GLA_V_W
    gt0 = hy0 + 3 * HY_WIDTH
    gain_perm = np.concatenate([perm64, perm64])
    bd = jnp.asarray(np.kron(np.eye(2), np.ones((64, 64))), dtype=BF16)
    rope = _rope_tables(l)
    fwd_l, inv_l = _dft_tables(l)
    fwd_c, inv_c = _dft_tables(lc)
    cmap = np.concatenate([np.arange(b), np.full(pc, b)])
    hh = EXPERT_HIDDEN // 2
    perm_np = np.zeros((2 * hh, 2 * hh), np.float32)
    perm_np[2 * np.arange(hh), np.arange(hh)] = 1.0
    perm_np[2 * np.arange(hh) + 1, hh + np.arange(hh)] = 1.0
    perm = jnp.asarray(perm_np, dtype=BF16)

    x_pair = (x, ctx.reshape(pc, l, d))
    cc = jnp.concatenate([c, c_ctx[None], jnp.zeros((7, d), F32)], axis=0)

    for i in range(DEPTH):
        with_ctx = i < DEPTH - 1
        npb = p if with_ctx else b
        mod = _modulation(cc, w_mod[i], b_mod[i])[cmap]
        sh1, sc1, g1, sh2, sc2, g2 = [m[:, None, :] for m in jnp.split(mod, 6, axis=-1)]

        wi = w_in[i]
        watt = wi[:, att_cols].astype(BF16)
        wgla = jnp.pad(wi[:, gla_cols], ((0, 0), (0, GLA_GROUP_W - gla_cols.size))).astype(BF16)
        wog = wi[:, og0:hy0].astype(BF16)
        why = wi[:, hy0:gt0].astype(BF16)
        wgt = wi[:, gt0:].astype(BF16)
        qg = att_q_norm[i][gain_perm].reshape(1, LANES)
        kg = att_k_norm[i][gain_perm].reshape(1, LANES)
        qn, kn, v, gla, og, hy, gates = _inproj(*x_pair, sh1, sc1, (watt, wgla, wog, why, wgt), qg, kg, rope, bd, b)

        o_a = _attention(qn, kn, v, b, lc, with_ctx)

        wdec = jnp.zeros((LANES, 2 * GLA_K_W), F32)
        wdec = wdec.at[0:GLA_RANK, 0:GLA_K_W].set(gla_wdec_f[i])
        wdec = wdec.at[GLA_RANK:2 * GLA_RANK, GLA_K_W:].set(gla_wdec_b[i])
        bdec = jnp.concatenate([gla_bdec_f[i], gla_bdec_b[i]]).reshape(1, -1)
        o_b, o_b_ctx = _gla(gla, og, wdec, bdec, gla_out_norm[i].reshape(1, -1), b, lc)

        hre, him = _hy_filter(l, hy_fw1[i], hy_fb1[i], hy_fw2[i], hy_fb2[i], hy_fw3[i], hy_fb3[i], fwd_l)
        u_h, x0_h, t_h = _hy_pre(hy, hy_conv_w[i], hy_conv_b[i], hy_skip[i], b, l, b)
        y_h = _hy_conv(u_h, fwd_l, inv_l, hre, him)
        if with_ctx:
            hre_c, him_c = _hy_filter(lc, hy_fw1[i], hy_fb1[i], hy_fw2[i], hy_fb2[i], hy_fw3[i], hy_fb3[i], fwd_c)
            u_c, x0_c, t_c = _hy_pre(hy, hy_conv_w[i], hy_conv_b[i], hy_skip[i], b, lc, b)
            y_c = _hy_conv(u_c, fwd_c, inv_c, hre_c, him_c)
            fold = lambda a_ctx: a_ctx.reshape(pc, l, a_ctx.shape[-1])
            pairs = [(o_b, fold(o_b_ctx)), (y_h, fold(y_c)), (x0_h, fold(x0_c)), (t_h, fold(t_c))]
        else:
            pairs = [(a, a) for a in (o_b, y_h, x0_h, t_h)]

        wa_rows = np.concatenate([np.concatenate([64 * j + np.arange(64), 64 * (j + 4) + np.arange(64)])
                                  for j in range(4)])
        x1 = _merge(x_pair, o_a, *pairs, gates, g1,
                    w_branch_a[i][wa_rows].astype(BF16), w_branch_b[i].astype(BF16),
                    w_branch_c[i].astype(BF16), w_out[i].astype(BF16),
                    ln1_g[i].reshape(1, d), ln1_b[i].reshape(1, d), npb)

        rw = jnp.pad(router_w[i], ((0, 0), (0, LANES - N_EXPERTS)))
        rb = jnp.concatenate([router_b[i], jnp.full((LANES - N_EXPERTS,), NEG_BIG, F32)]).reshape(1, LANES)
        b1p = jnp.concatenate([exp_b1[i][..., 0::2], exp_b1[i][..., 1::2]], axis=-1)[:, None, :]
        x_all = _moe(x1, sh2, sc2, g2, rw, rb, exp_w1, b1p, exp_w2, exp_b2[i][:, None, :], perm,
                     ln2_g[i].reshape(1, d), ln2_b[i].reshape(1, d), npb, i)
        x_pair = (x_all, x_all)
    return x_all
```

```python
import functools
import math

import numpy as np
import jax
import jax.numpy as jnp
from jax import lax
from jax.experimental import pallas as pl
from jax.experimental.pallas import tpu as pltpu

F32 = jnp.float32
BF16 = jnp.bfloat16

D_MODEL = 1024
DEPTH = 2
GRID_W = 64
ATT_HEADS = 8
ATT_KV_HEADS = 2
ATT_HEAD_DIM = 64
ROPE_THETA = 10000.0
GLA_HEADS = 4
GLA_DK = 64
GLA_DV = 128
GLA_RANK = 16
GLA_TAU = 16.0
GLA_CHUNK = 64
HY_WIDTH = 512
HY_POS_FREQS = 16
HY_DECAY_SLOW = math.log(1e-2) / 1.5
HY_DECAY_FAST = math.log(1e-2) / 0.3
N_EXPERTS = 32
TOP_K = 4
EXPERT_HIDDEN = D_MODEL
SWIGLU_LIMIT = 7.0
SWIGLU_ALPHA = 1.702
DEEPNORM_ALPHA = (2 * DEPTH) ** 0.25
LN_EPS = 1e-5
RMS_EPS = 1e-6

ATT_Q_W = ATT_HEADS * ATT_HEAD_DIM
ATT_KV_W = ATT_KV_HEADS * ATT_HEAD_DIM
GLA_K_W = GLA_HEADS * GLA_DK
GLA_V_W = GLA_HEADS * GLA_DV
GLA_GROUP_W = 1152
LANES = 128
NEG_BIG = -3.0e38

VMEM_LIMIT = 56 * 1024 * 1024


def _cparams(*sem):
    return pltpu.CompilerParams(dimension_semantics=sem, vmem_limit_bytes=VMEM_LIMIT)


def _dot(a, b):
    return jnp.dot(a, b, preferred_element_type=F32)


def _dot_nt(a, b):
    return lax.dot_general(a, b, (((1,), (1,)), ((), ())), preferred_element_type=F32)


def _dot_tn(a, b):
    return lax.dot_general(a, b, (((0,), (0,)), ((), ())), preferred_element_type=F32)


def _split2(x):
    hi = x.astype(BF16)
    lo = (x - hi.astype(F32)).astype(BF16)
    return hi, lo


def _split3(x):
    hi = x.astype(BF16)
    r = x - hi.astype(F32)
    mid = r.astype(BF16)
    lo = (r - mid.astype(F32)).astype(BF16)
    return hi, mid, lo


def _dot3(a, b):
    ah, al = _split2(a)
    bh, bl = _split2(b)
    return _dot(ah, bh) + (_dot(ah, bl) + _dot(al, bh))


def _dot_exact_lhs(a_bf16, b):
    b0, b1, b2 = _split3(b)
    return _dot(a_bf16, b0) + (_dot(a_bf16, b1) + _dot(a_bf16, b2))


def _ln(x):
    mu = jnp.mean(x, axis=-1, keepdims=True)
    xc = x - mu
    var = jnp.mean(xc * xc, axis=-1, keepdims=True)
    return xc * lax.rsqrt(var + LN_EPS)


def _sigmoid(x):
    return 1.0 / (1.0 + jnp.exp(-x))


def _mod_kernel(c_ref, w_ref, b_ref, o_ref):
    c = c_ref[...]
    o_ref[...] = _dot3(c * _sigmoid(c), w_ref[...]) + b_ref[...]


def _modulation(cc, w_mod, b_mod):
    rows, d = cc.shape
    n = w_mod.shape[1]
    tn = 1024
    return pl.pallas_call(
        _mod_kernel,
        grid=(n // tn,),
        in_specs=[pl.BlockSpec((rows, d), lambda j: (0, 0)),
                  pl.BlockSpec((d, tn), lambda j: (0, j)),
                  pl.BlockSpec((1, tn), lambda j: (0, j))],
        out_specs=pl.BlockSpec((rows, tn), lambda j: (0, j)),
        out_shape=jax.ShapeDtypeStruct((rows, n), F32),
        compiler_params=_cparams("arbitrary"),
        name="modulation",
    )(cc, w_mod, b_mod.reshape(1, n))


def _head_norm_rope(t, gain, rc, rs1, rs2, bd):
    sh, sl = _split2(t * t)
    ss = _dot(sh, bd) + _dot(sl, bd)
    tn = t * lax.rsqrt(ss * (1.0 / ATT_HEAD_DIM) + RMS_EPS) * gain
    return tn * rc + pltpu.roll(tn, 96, 1) * rs1 + pltpu.roll(tn, 32, 1) * rs2


def _pair_specs(tm, w, n_split, n_i):
    first = pl.BlockSpec((1, tm, w), lambda b, i: (jnp.minimum(b, n_split - 1), jnp.where(b < n_split, i, n_i - 1), 0))
    rest = pl.BlockSpec((1, tm, w), lambda b, i: (jnp.maximum(b - n_split, 0), jnp.where(b >= n_split, i, 0), 0))
    return [first, rest]


def _pick(use_rest, first_ref, rest_ref):
    return jnp.where(use_rest, rest_ref[0], first_ref[0])


def _inproj_kernel(x_ref, xr_ref, sh_ref, sc_ref, watt_ref, wgla_ref, wog_ref, why_ref, wgt_ref,
                   qg_ref, kg_ref, rc_ref, rs1_ref, rs2_ref, bd_ref,
                   qn_ref, kn_ref, v_ref, gla_ref, og_ref, hy_ref, gt_ref, *, n_split):
    x = _pick(pl.program_id(0) >= n_split, x_ref, xr_ref)
    u = _ln(x) * (1.0 + sc_ref[0]) + sh_ref[0]
    ub = u.astype(BF16)
    rc, rs1, rs2, bd = rc_ref[0], rs1_ref[0], rs2_ref[0], bd_ref[...]
    att = _dot(ub, watt_ref[...])
    for j in range(ATT_Q_W // LANES):
        t = att[:, LANES * j:LANES * (j + 1)]
        qn = _head_norm_rope(t, qg_ref[...], rc, rs1, rs2, bd) * (ATT_HEAD_DIM ** -0.5)
        qn_ref[0, :, LANES * j:LANES * (j + 1)] = qn.astype(BF16)
    kn = _head_norm_rope(att[:, ATT_Q_W:ATT_Q_W + LANES], kg_ref[...], rc, rs1, rs2, bd)
    kn_ref[0] = kn.astype(BF16)
    v_ref[0, :, 0:LANES] = att[:, ATT_Q_W + LANES:].astype(BF16)
    v_ref[0, :, LANES:2 * LANES] = jnp.ones((att.shape[0], LANES), BF16)
    gla_ref[0] = _dot(ub, wgla_ref[...])
    og_ref[0] = _dot(ub, wog_ref[...]).astype(BF16)
    hy_ref[0] = _dot(ub, why_ref[...]).astype(BF16)
    gt_ref[0] = _dot(ub, wgt_ref[...]).astype(BF16)


def _inproj(x_first, x_rest, sh, sc, wts, qg, kg, rope, bd, n_lat):
    n_split, l, d = x_first.shape
    p = sh.shape[0]
    tm = min(256, l)
    watt, wgla, wog, why, wgt = wts
    rc, rs1, rs2 = rope
    const = lambda a: pl.BlockSpec(a.shape, lambda b, i: (0,) * a.ndim)
    row = lambda w: pl.BlockSpec((1, tm, w), lambda b, i: (b, i, 0))
    tab = pl.BlockSpec((1, tm, LANES), lambda b, i: (jnp.where(b >= n_lat, 1, 0), i, 0))
    vec = pl.BlockSpec((1, 1, d), lambda b, i: (b, 0, 0))
    widths = (ATT_Q_W, LANES, 2 * LANES, GLA_GROUP_W, GLA_V_W, 3 * HY_WIDTH, 3 * d)
    dtypes = (BF16, BF16, BF16, F32, BF16, BF16, BF16)
    return pl.pallas_call(
        functools.partial(_inproj_kernel, n_split=n_split),
        grid=(p, l // tm),
        in_specs=_pair_specs(tm, d, n_split, l // tm) + [
            vec, vec, const(watt), const(wgla), const(wog), const(why), const(wgt),
            const(qg), const(kg), tab, tab, tab, const(bd)],
        out_specs=[row(w) for w in widths],
        out_shape=[jax.ShapeDtypeStruct((p, l, w), dt) for w, dt in zip(widths, dtypes)],
        compiler_params=_cparams("parallel", "arbitrary"),
        name="inproj",
    )(x_first, x_rest, sh, sc, watt, wgla, wog, why, wgt, qg, kg, rc, rs1, rs2, bd)


def _attend_group(qg, ks, vs, lo):
    zero = jnp.zeros_like(qg)
    outs = []
    for qh in (jnp.where(lo, qg, zero), jnp.where(lo, zero, qg)):
        ss = [_dot_nt(qh, k) for k in ks]
        m = ss[0].max(axis=-1, keepdims=True)
        for s in ss[1:]:
            m = jnp.maximum(m, s.max(axis=-1, keepdims=True))
        acc = 0.0
        for s, v in zip(ss, vs):
            acc = acc + _dot(jnp.exp((s - m).astype(BF16)), v)
        outs.append(acc[:, 0:LANES] / acc[:, LANES:2 * LANES])
    return jnp.where(lo, outs[0], outs[1])


def _attn_kernel(q_ref, k_ref, v_ref, kc_ref, vc_ref, o_ref, *, n_lat_tiles, with_ctx):
    i = pl.program_id(1)
    tq = q_ref.shape[1]
    lo = lax.broadcasted_iota(jnp.int32, (tq, LANES), 1) < ATT_HEAD_DIM

    def run(ks, vs):
        for j in range(ATT_Q_W // LANES):
            o = _attend_group(q_ref[0, :, LANES * j:LANES * (j + 1)], ks, vs, lo)
            o_ref[0, :, LANES * j:LANES * (j + 1)] = o.astype(BF16)

    if with_ctx:
        @pl.when(i < n_lat_tiles)
        def _():
            run([k_ref[0], kc_ref[0]], [v_ref[0], vc_ref[0]])

        @pl.when(i == n_lat_tiles)
        def _():
            run([kc_ref[0]], [vc_ref[0]])
    else:
        run([k_ref[0], kc_ref[0]], [v_ref[0], vc_ref[0]])


def _attention(qn, kn, v, n_lat, lc, with_ctx):
    p, l, _ = qn.shape
    tq = lc
    r = l // lc
    nt = l // tq
    ctx_idx = lambda b: (n_lat + b // r, b % r, 0)
    if with_ctx:
        qmap = lambda b, i: (jnp.where(i < nt, b, n_lat + b // r), jnp.where(i < nt, i, b % r), 0)
    else:
        qmap = lambda b, i: (b, i, 0)
    return pl.pallas_call(
        functools.partial(_attn_kernel, n_lat_tiles=nt, with_ctx=with_ctx),
        grid=(n_lat, nt + (1 if with_ctx else 0)),
        in_specs=[pl.BlockSpec((1, tq, ATT_Q_W), qmap),
                  pl.BlockSpec((1, l, LANES), lambda b, i: (b, 0, 0)),
                  pl.BlockSpec((1, l, 2 * LANES), lambda b, i: (b, 0, 0)),
                  pl.BlockSpec((1, lc, LANES), lambda b, i: ctx_idx(b)),
                  pl.BlockSpec((1, lc, 2 * LANES), lambda b, i: ctx_idx(b))],
        out_specs=pl.BlockSpec((1, tq, ATT_Q_W), qmap),
        out_shape=jax.ShapeDtypeStruct((p if with_ctx else n_lat, l, ATT_Q_W), BF16),
        compiler_params=_cparams("parallel", "arbitrary"),
        name="attention",
    )(qn, kn, v, kn, v)


def _gla_segment(src_ref, og_ref, out_ref, acc_ref, st_ref, qd_ref, ke_ref, vb_ref, dec_ref,
                 wdec_ref, bdec_ref, gn_ref, ls):
    c = GLA_CHUNK
    nc = ls // c
    rb = min(256, ls)

    bi = lax.broadcasted_iota(jnp.int32, (rb, rb), 0)
    bj = lax.broadcasted_iota(jnp.int32, (rb, rb), 1)
    same = (bi // c) == (bj // c)
    att_masks = (same & (bi >= bj), same & (bi <= bj))
    tri_blk = tuple(jnp.where(m, 1.0, 0.0).astype(BF16) for m in att_masks)
    ones_blk = jnp.where(same, 1.0, 0.0).astype(BF16)
    lo_b = lax.broadcasted_iota(jnp.int32, (rb, LANES), 1) < GLA_DK

    def prep(t, carry):
        r0 = pl.multiple_of(t * rb, rb)
        q = src_ref[0, pl.ds(r0, rb), 0:GLA_K_W] * (GLA_DK ** -0.5)
        k = src_ref[0, pl.ds(r0, rb), GLA_K_W:2 * GLA_K_W]
        vb = src_ref[0, pl.ds(r0, rb), 2 * GLA_K_W:2 * GLA_K_W + GLA_V_W].astype(BF16)
        vb_ref[pl.ds(r0, rb), :] = vb
        lr = src_ref[0, pl.ds(r0, rb), 2 * GLA_K_W + GLA_V_W:GLA_GROUP_W]
        la = jax.nn.log_sigmoid(_dot3(lr, wdec_ref[...]) + bdec_ref[...]) * (1.0 / GLA_TAU)
        for d in range(2):
            l0, l1, l2 = _split3(la[:, d * GLA_K_W:(d + 1) * GLA_K_W])
            cum = _dot(tri_blk[d], l0) + (_dot(tri_blk[d], l1) + _dot(tri_blk[d], l2))
            tot = _dot(ones_blk, l0) + (_dot(ones_blk, l1) + _dot(ones_blk, l2))
            qd = (q * jnp.exp(cum)).astype(BF16)
            ki = (k * jnp.exp(-cum)).astype(BF16)
            qd_ref[d, pl.ds(r0, rb), :] = qd
            ke_ref[d, pl.ds(r0, rb), :] = (k * jnp.exp(tot - cum)).astype(BF16)
            dec_ref[d, pl.ds(r0, rb), :] = jnp.exp(tot)
            for h in range(GLA_HEADS):
                g, half = h // 2, h % 2
                sl = slice(LANES * g, LANES * (g + 1))
                zero = jnp.zeros_like(qd[:, sl])
                qh = jnp.where(lo_b, qd[:, sl], zero) if half == 0 else jnp.where(lo_b, zero, qd[:, sl])
                att = jnp.where(att_masks[d], _dot_nt(qh, ki[:, sl]), 0.0)
                o = _dot(att.astype(BF16), vb[:, GLA_DV * h:GLA_DV * (h + 1)])
                if d == 0:
                    acc_ref[pl.ds(r0, rb), GLA_DV * h:GLA_DV * (h + 1)] = o
                else:
                    acc_ref[pl.ds(r0, rb), GLA_DV * h:GLA_DV * (h + 1)] += o
        return carry

    lax.fori_loop(0, ls // rb, prep, 0)

    lo = lax.broadcasted_iota(jnp.int32, (c, LANES), 1) < GLA_DK

    def body(n, carry):
        for d, cn in ((0, n), (1, nc - 1 - n)):
            r0 = pl.multiple_of(cn * c, c)
            dec = dec_ref[d, pl.ds(r0, 1), :]
            for h in range(GLA_HEADS):
                g, half = h // 2, h % 2
                sl = slice(LANES * g, LANES * (g + 1))
                qd = qd_ref[d, pl.ds(r0, c), sl]
                zero = jnp.zeros_like(qd)
                qd = jnp.where(lo, qd, zero) if half == 0 else jnp.where(lo, zero, qd)
                vh = vb_ref[pl.ds(r0, c), GLA_DV * h:GLA_DV * (h + 1)]
                st = st_ref[d * GLA_HEADS + h]
                acc_ref[pl.ds(r0, c), GLA_DV * h:GLA_DV * (h + 1)] += _dot_nt(qd, st.astype(BF16))
                st_ref[d * GLA_HEADS + h] = dec[:, sl] * st + _dot_tn(vh, ke_ref[d, pl.ds(r0, c), sl])
        return carry

    lax.fori_loop(0, nc, body, 0, unroll=4 if nc % 4 == 0 else (2 if nc % 2 == 0 else 1))

    tr = min(256, ls)

    def fin(t, carry):
        r0 = pl.multiple_of(t * tr, tr)
        for h in range(GLA_HEADS):
            sl = slice(GLA_DV * h, GLA_DV * (h + 1))
            o = acc_ref[pl.ds(r0, tr), sl]
            y = o * lax.rsqrt(jnp.mean(o * o, axis=-1, keepdims=True) + RMS_EPS)
            y = y * gn_ref[...]
            og = og_ref[0, pl.ds(r0, tr), sl].astype(F32)
            out_ref[0, pl.ds(r0, tr), sl] = (y * (og * _sigmoid(og))).astype(BF16)
        return carry

    lax.fori_loop(0, ls // tr, fin, 0)


def _gla_kernel(gl_ref, glc_ref, og_ref, ogc_ref, wdec_ref, bdec_ref, gn_ref, o_ref, oc_ref,
                acc_ref, st_ref, qd_ref, ke_ref, vb_ref, dec_ref):
    st_ref[...] = jnp.zeros(st_ref.shape, F32)
    scratch = (acc_ref, st_ref, qd_ref, ke_ref, vb_ref, dec_ref)
    _gla_segment(glc_ref, ogc_ref, oc_ref, *scratch, wdec_ref, bdec_ref, gn_ref, glc_ref.shape[1])
    _gla_segment(gl_ref, og_ref, o_ref, *scratch, wdec_ref, bdec_ref, gn_ref, gl_ref.shape[1])


def _gla(gla, og, wdec, bdec, gn, n_lat, lc):
    p, l, _ = gla.shape
    r = l // lc
    ctx_idx = lambda b: (n_lat + b // r, b % r, 0)
    const = lambda a: pl.BlockSpec(a.shape, lambda b: (0,) * a.ndim)
    return pl.pallas_call(
        _gla_kernel,
        grid=(n_lat,),
        in_specs=[pl.BlockSpec((1, l, GLA_GROUP_W), lambda b: (b, 0, 0)),
                  pl.BlockSpec((1, lc, GLA_GROUP_W), ctx_idx),
                  pl.BlockSpec((1, l, GLA_V_W), lambda b: (b, 0, 0)),
                  pl.BlockSpec((1, lc, GLA_V_W), ctx_idx),
                  const(wdec), const(bdec), const(gn)],
        out_specs=[pl.BlockSpec((1, l, GLA_V_W), lambda b: (b, 0, 0)),
                   pl.BlockSpec((1, lc, GLA_V_W), lambda b: (b, 0, 0))],
        out_shape=[jax.ShapeDtypeStruct((n_lat, l, GLA_V_W), BF16),
                   jax.ShapeDtypeStruct((n_lat, lc, GLA_V_W), BF16)],
        scratch_shapes=[pltpu.VMEM((l, GLA_V_W), F32),
                        pltpu.VMEM((2 * GLA_HEADS, GLA_DV, LANES), F32),
                        pltpu.VMEM((2, l, GLA_K_W), BF16),
                        pltpu.VMEM((2, l, GLA_K_W), BF16), pltpu.VMEM((l, GLA_V_W), BF16),
                        pltpu.VMEM((2, l, GLA_K_W), F32)],
        compiler_params=_cparams("arbitrary"),
        name="gla",
    )(gla, gla, og, og, wdec, bdec, gn)


HY_FREQ_TILE = 512


def _dft_tables(l):
    n = 2 * l
    k = np.arange(l, dtype=np.int64)[:, None]
    t = np.arange(l, dtype=np.int64)[None, :]
    ang = 2.0 * np.pi * ((k * t) % n).astype(np.float64) / n
    fre = np.cos(ang)
    fim = -np.sin(ang)
    fim[0, :] = np.where(np.arange(l) % 2 == 0, 1.0, -1.0)
    tk = min(HY_FREQ_TILE, l)
    nk = l // tk
    fwd = np.concatenate([fre.reshape(nk, tk, l), fim.reshape(nk, tk, l)], axis=1)
    return jnp.asarray(fwd, dtype=BF16), jnp.asarray(fwd.transpose(0, 2, 1), dtype=BF16)


def _hy_filter_kernel(z_ref, win_ref, w1_ref, b1_ref, w2_ref, b2_ref, w3_ref, b3_ref, f_ref,
                      hre_ref, him_ref, hf_ref):
    j = pl.program_id(0)
    l = z_ref.shape[0]

    @pl.when(j == 0)
    def _():
        h = jnp.sin(_dot3(z_ref[...], w1_ref[...]) + b1_ref[...])
        h = jnp.sin(_dot3(h, w2_ref[...]) + b2_ref[...])
        h = _dot3(h, w3_ref[...]) + b3_ref[...]
        win = win_ref[...]
        hf = h[:, :HY_WIDTH] * win
        hb = h[:, HY_WIDTH:] * win
        row = lax.broadcasted_iota(jnp.int32, (l, HY_WIDTH), 0)
        hf_ref[:, :HY_WIDTH] = hf.astype(BF16)
        hf_ref[:, HY_WIDTH:] = jnp.where(row == 0, 0.0, hb).astype(BF16)

    tk = f_ref.shape[1] // 2
    ab = _dot(f_ref[0], hf_ref[...])
    a, b = ab[:tk], ab[tk:]
    first = (lax.broadcasted_iota(jnp.int32, (tk, HY_WIDTH), 0) == 0) & (j == 0)
    scl = jnp.where(first, 1.0 / (2 * l), 2.0 / (2 * l))
    hre_ref[...] = (a[:, :HY_WIDTH] + a[:, HY_WIDTH:]) * scl
    him_ref[...] = (b[:, :HY_WIDTH] + jnp.where(first, 1.0, -1.0) * b[:, HY_WIDTH:]) * scl


def _hy_filter(l, fw1, fb1, fw2, fb2, fw3, fb3, fwd):
    t = np.arange(l, dtype=np.float32)[:, None] / np.float32(l)
    freqs = np.arange(1, HY_POS_FREQS + 1, dtype=np.float32)
    z = np.concatenate([t, np.cos(2.0 * math.pi * freqs * t), np.sin(2.0 * math.pi * freqs * t)], axis=-1)
    z = np.pad(z, ((0, 0), (0, LANES - z.shape[1])))
    hid = fw2.shape[0]
    fw1 = jnp.pad(fw1, ((0, LANES - fw1.shape[0]), (0, LANES - hid)))
    fb1 = jnp.pad(fb1, (0, LANES - hid))
    fw2 = jnp.pad(fw2, ((0, LANES - hid), (0, LANES - hid)))
    fb2 = jnp.pad(fb2, (0, LANES - hid))
    fw3 = jnp.pad(fw3, ((0, LANES - hid), (0, 0)))
    deltas = np.abs(np.linspace(HY_DECAY_SLOW, HY_DECAY_FAST, HY_WIDTH, dtype=np.float32))
    win = np.exp(-t * deltas).astype(np.float32)
    nk, tk2, _ = fwd.shape
    tk = tk2 // 2
    const = lambda a: pl.BlockSpec(a.shape, lambda j: (0,) * a.ndim)
    args = (jnp.asarray(z), jnp.asarray(win), fw1, fb1.reshape(1, -1), fw2, fb2.reshape(1, -1),
            fw3, fb3.reshape(1, -1))
    return pl.pallas_call(
        _hy_filter_kernel,
        grid=(nk,),
        in_specs=[const(a) for a in args] + [pl.BlockSpec((1, tk2, l), lambda j: (j, 0, 0))],
        out_specs=[pl.BlockSpec((tk, HY_WIDTH), lambda j: (j, 0))] * 2,
        out_shape=[jax.ShapeDtypeStruct((l, HY_WIDTH), F32)] * 2,
        scratch_shapes=[pltpu.VMEM((l, 2 * HY_WIDTH), BF16)],
        compiler_params=_cparams("arbitrary"),
        name="hyena_filter",
    )(*args, fwd)


def _hy_pre_kernel(p0_ref, p1_ref, p2_ref, w0_ref, w1_ref, w2_ref, b0_ref, b1_ref, b2_ref, skip_ref,
                   u_ref, x0_ref, t_ref):
    ls = p0_ref.shape[1]
    row = lax.broadcasted_iota(jnp.int32, (ls, LANES), 0)

    def conv(p_ref, w_ref, b_ref):
        p = p_ref[0].astype(F32)
        prev = jnp.where(row == 0, 0.0, pltpu.roll(p, 1, 0))
        nxt = jnp.where(row == ls - 1, 0.0, pltpu.roll(p, ls - 1, 0))
        return prev * w_ref[0:1, :] + p * w_ref[1:2, :] + nxt * w_ref[2:3, :] + b_ref[...]

    x0 = conv(p0_ref, w0_ref, b0_ref)
    u = conv(p1_ref, w1_ref, b1_ref) * conv(p2_ref, w2_ref, b2_ref)
    u_ref[0] = u.astype(BF16)
    x0_ref[0] = x0.astype(BF16)
    t_ref[0] = (x0 * u * skip_ref[...]).astype(BF16)


def _hy_pre(hy, conv_w, conv_b, skip, nseq, ls, n_lat):
    p, l, _ = hy.shape
    r = l // ls
    nj = HY_WIDTH // LANES
    base = 0 if ls == l else n_lat
    src = lambda off: pl.BlockSpec((1, ls, LANES), lambda b, j: (base + b // r, b % r, off * nj + j))
    wsp = lambda off: pl.BlockSpec((3, LANES), lambda b, j: (0, off * nj + j))
    bsp = lambda off: pl.BlockSpec((1, LANES), lambda b, j: (0, off * nj + j))
    out = pl.BlockSpec((1, ls, LANES), lambda b, j: (b, 0, j))
    cb = conv_b.reshape(1, -1)
    return pl.pallas_call(
        _hy_pre_kernel,
        grid=(nseq, nj),
        in_specs=[src(0), src(1), src(2), wsp(0), wsp(1), wsp(2), bsp(0), bsp(1), bsp(2), bsp(0)],
        out_specs=[out] * 3,
        out_shape=[jax.ShapeDtypeStruct((nseq, ls, HY_WIDTH), BF16)] * 3,
        compiler_params=_cparams("parallel", "arbitrary"),
        name="hyena_pre",
    )(hy, hy, hy, conv_w, conv_w, conv_w, cb, cb, cb, skip.reshape(1, -1))


def _hy_conv_kernel(u_ref, f_ref, i_ref, hre_ref, him_ref, y_ref):
    j = pl.program_id(1)
    tk = hre_ref.shape[0]
    uu = _dot(f_ref[0], u_ref[0])
    ure, uim = uu[:tk], uu[tk:]
    hre, him = hre_ref[...], him_ref[...]
    first = (lax.broadcasted_iota(jnp.int32, ure.shape, 0) == 0) & (j == 0)
    yre = jnp.where(first, ure * hre, ure * hre - uim * him)
    yim = jnp.where(first, uim * him, ure * him + uim * hre)
    y = _dot(i_ref[0], jnp.concatenate([yre.astype(BF16), yim.astype(BF16)], axis=0))

    @pl.when(j == 0)
    def _():
        y_ref[0] = y

    @pl.when(j > 0)
    def _():
        y_ref[0] += y


def _hy_conv(u, fwd, inv, hre, him):
    nseq, ls, _ = u.shape
    nk, tk2, _ = fwd.shape
    tk = tk2 // 2
    return pl.pallas_call(
        _hy_conv_kernel,
        grid=(nseq, nk),
        in_specs=[pl.BlockSpec((1, ls, HY_WIDTH), lambda b, j: (b, 0, 0)),
                  pl.BlockSpec((1, tk2, ls), lambda b, j: (j, 0, 0)),
                  pl.BlockSpec((1, ls, tk2), lambda b, j: (j, 0, 0)),
                  pl.BlockSpec((tk, HY_WIDTH), lambda b, j: (j, 0)),
                  pl.BlockSpec((tk, HY_WIDTH), lambda b, j: (j, 0))],
        out_specs=pl.BlockSpec((1, ls, HY_WIDTH), lambda b, j: (b, 0, 0)),
        out_shape=jax.ShapeDtypeStruct((nseq, ls, HY_WIDTH), F32),
        compiler_params=_cparams("parallel", "arbitrary"),
        name="hyena_conv",
    )(u, fwd, inv, hre, him)


def _merge_kernel(x_ref, xr_ref, oa_ref, ob_ref, obr_ref, y_ref, yr_ref, x0_ref, x0r_ref, t_ref, tr_ref,
                  gt_ref, g1_ref, wa_ref, wb_ref, wc_ref, wo_ref, lg_ref, lb_ref, o_ref, *, n_split):
    d = x_ref.shape[2]
    rest = pl.program_id(0) >= n_split
    x = _pick(rest, x_ref, xr_ref)
    ob = _pick(rest, ob_ref, obr_ref)
    oc = (_pick(rest, x0_ref, x0r_ref).astype(F32) * _pick(rest, y_ref, yr_ref)
          + _pick(rest, t_ref, tr_ref).astype(F32)).astype(BF16)
    gt = gt_ref[0]
    m = (_sigmoid(gt[:, 0:d].astype(F32)) * _dot(oa_ref[0], wa_ref[...])
         + _sigmoid(gt[:, d:2 * d].astype(F32)) * _dot(ob, wb_ref[...])
         + _sigmoid(gt[:, 2 * d:3 * d].astype(F32)) * _dot(oc, wc_ref[...]))
    mix = _dot(m.astype(BF16), wo_ref[...])
    o_ref[0] = _ln(DEEPNORM_ALPHA * x + g1_ref[0] * mix) * lg_ref[...] + lb_ref[...]


def _merge(x, o_a, o_b, y, x0, t, gates, g1, wa, wb, wc, wo, lg, lb, npb):
    n_split, l, d = x[0].shape
    tm = min(512, l)
    n_i = l // tm
    row = lambda w: pl.BlockSpec((1, tm, w), lambda b, i: (b, i, 0))
    const = lambda a: pl.BlockSpec(a.shape, lambda b, i: (0,) * a.ndim)
    pair = lambda w: _pair_specs(tm, w, n_split, n_i)
    return pl.pallas_call(
        functools.partial(_merge_kernel, n_split=n_split),
        grid=(npb, n_i),
        in_specs=pair(d) + [row(ATT_Q_W)] + pair(GLA_V_W) + pair(HY_WIDTH) + pair(HY_WIDTH) + pair(HY_WIDTH) + [
            row(3 * d), pl.BlockSpec((1, 1, d), lambda b, i: (b, 0, 0)),
            const(wa), const(wb), const(wc), const(wo), const(lg), const(lb)],
        out_specs=row(d),
        out_shape=jax.ShapeDtypeStruct((npb, l, d), F32),
        compiler_params=_cparams("parallel", "arbitrary"),
        name="merge",
    )(*x, o_a, *o_b, *y, *x0, *t, gates, g1, wa, wb, wc, wo, lg, lb)


MOE_T = 256
MOE_G = 16
MOE_TM = 1024
MOE_SLOTS = TOP_K * MOE_T + 512
AUG_W = D_MODEL + LANES


def _route_kernel(x_ref, sh_ref, sc_ref, rw_ref, rb_ref, xa_ref, sel_ref, cnt_ref):
    d = x_ref.shape[2]
    for h in range(x_ref.shape[1] // MOE_T):
        rows = slice(h * MOE_T, (h + 1) * MOE_T)
        u = _ln(x_ref[0, rows, :]) * (1.0 + sc_ref[0]) + sh_ref[0]
        xa_ref[0, rows, 0:d] = u.astype(BF16)
        logits = _dot3(u, rw_ref[...]) + rb_ref[...]
        lane = lax.broadcasted_iota(jnp.int32, logits.shape, 1).astype(F32)
        work = logits
        hots, vals = [], []
        for _ in range(TOP_K):
            m = work.max(axis=-1, keepdims=True)
            idx = jnp.where(work == m, lane, float(LANES)).min(axis=-1, keepdims=True)
            hot = lane == idx
            hots.append(hot)
            vals.append(m)
            work = jnp.where(hot, NEG_BIG, work)
        es = [jnp.exp(v - vals[0]) for v in vals]
        den = es[0] + es[1] + es[2] + es[3]
        comb = jnp.zeros_like(logits)
        sel = jnp.zeros_like(logits)
        for hot, e in zip(hots, es):
            comb = comb + jnp.where(hot, e / den, 0.0)
            sel = sel + jnp.where(hot, 1.0, 0.0)
        c0, c1, c2 = _split3(comb)
        aug = c0.astype(F32) + pltpu.roll(c1.astype(F32), 32, 1) + pltpu.roll(c2.astype(F32), 64, 1)
        xa_ref[0, rows, d:d + LANES] = aug.astype(BF16)
        sel_ref[0, rows, :] = sel.astype(BF16)
        cnt_ref[h] = sel.sum(axis=0, keepdims=True)


def _route(x_all, sh, sc, rw, rb, npb):
    _, l, d = x_all.shape
    per = 2 if l % (2 * MOE_T) == 0 else 1
    t = per * MOE_T
    n_i = l // t
    row = lambda w: pl.BlockSpec((1, t, w), lambda b, i: (b, i, 0))
    vec = pl.BlockSpec((1, 1, d), lambda b, i: (b, 0, 0))
    const = lambda a: pl.BlockSpec(a.shape, lambda b, i: (0,) * a.ndim)
    return pl.pallas_call(
        _route_kernel,
        grid=(npb, n_i),
        in_specs=[row(d), vec, vec, const(rw), const(rb)],
        out_specs=[row(AUG_W), row(LANES), pl.BlockSpec((per, 1, LANES), lambda b, i: (b * n_i + i, 0, 0))],
        out_shape=[jax.ShapeDtypeStruct((npb, l, AUG_W), BF16), jax.ShapeDtypeStruct((npb, l, LANES), BF16),
                   jax.ShapeDtypeStruct((npb * n_i * per, 1, LANES), F32)],
        compiler_params=_cparams("parallel", "arbitrary"),
        name="route",
    )(x_all, sh, sc, rw, rb)


def _group_copies(j, cntp_ref, loc_ref, goff_ref, make_copy):
    for e in range(N_EXPERTS):
        n = lax.shift_right_logical(cntp_ref[j * N_EXPERTS + e], int(math.log2(MOE_G)))
        loc = loc_ref[j * N_EXPERTS + e]
        off = goff_ref[j * N_EXPERTS + e]
        n2 = lax.shift_right_logical(n, 1)

        def per_pair(c, carry, loc=loc, off=off):
            make_copy(pl.multiple_of(loc + c * (2 * MOE_G), MOE_G),
                      pl.multiple_of(off + c * (2 * MOE_G), MOE_G), 2 * MOE_G).start()
            return carry

        lax.fori_loop(0, n2, per_pair, 0)

        @pl.when((n & 1) == 1)
        def _(loc=loc, off=off, n2=n2):
            make_copy(pl.multiple_of(loc + n2 * (2 * MOE_G), MOE_G),
                      pl.multiple_of(off + n2 * (2 * MOE_G), MOE_G), MOE_G).start()


def _wait_copies(nch_ref, j, make_copy):
    for k, rows in ((0, 2 * MOE_G), (1, MOE_G)):
        def wait_one(c, carry, rows=rows):
            make_copy(0, 0, rows).wait()
            return carry

        lax.fori_loop(0, nch_ref[2 * j + k], wait_one, 0)


def _dispatch_kernel(cntp_ref, loc_ref, goff_ref, nch_ref, tail_ref, tailn_ref,
                     xa_ref, sel_ref, locv_ref, pt_ref, xs_ref, stage_ref, zero_ref, sems):
    j = pl.program_id(0)
    nt = pl.num_programs(0)
    buf = j % 2
    sem = sems.at[buf]
    stage = stage_ref.at[buf]
    t = MOE_T

    def out_copy(src, dst, rows):
        return pltpu.make_async_copy(stage.at[pl.ds(src, rows)], xs_ref.at[pl.ds(dst, rows)], sem)

    @pl.when(j >= 2)
    def _():
        _wait_copies(nch_ref, jnp.maximum(j - 2, 0), out_copy)

    sel = sel_ref[0]
    ri = lax.broadcasted_iota(jnp.int32, (t, t), 0)
    ci = lax.broadcasted_iota(jnp.int32, (t, t), 1)
    rank = _dot(jnp.where(ci < ri, 1.0, 0.0).astype(BF16), sel)
    dest = rank + locv_ref[0]
    lane = lax.broadcasted_iota(jnp.int32, (t, LANES), 1).astype(F32)
    slot = lax.broadcasted_iota(jnp.int32, (t, MOE_SLOTS), 1)
    avail = sel.astype(F32)
    pt = jnp.zeros((t, MOE_SLOTS), F32)
    for _ in range(TOP_K):
        ek = jnp.where(avail > 0.0, lane, float(LANES)).min(axis=-1, keepdims=True)
        hot = lane == ek
        dk = jnp.where(hot, dest, 0.0).sum(axis=-1, keepdims=True).astype(jnp.int32)
        pt = pt + jnp.where(slot == dk, 1.0, 0.0)
        avail = jnp.where(hot, 0.0, avail)
    ptb = pt.astype(BF16)
    pt_ref[0] = ptb
    stage[...] = _dot_tn(ptb, xa_ref[0]).astype(BF16)
    _group_copies(j, cntp_ref, loc_ref, goff_ref, out_copy)

    @pl.when(j == nt - 1)
    def _():
        _wait_copies(nch_ref, j, out_copy)

        @pl.when(j >= 1)
        def _():
            other = sems.at[1 - buf]
            _wait_copies(nch_ref, jnp.maximum(j - 1, 0), lambda s, d, rows: pltpu.make_async_copy(
                stage_ref.at[1 - buf].at[pl.ds(s, rows)], xs_ref.at[pl.ds(d, rows)], other))

        zero_ref[...] = jnp.zeros(zero_ref.shape, BF16)

        def zcopy(dst):
            return pltpu.make_async_copy(zero_ref.at[pl.ds(0, MOE_G)], xs_ref.at[pl.ds(dst, MOE_G)], sem)

        def zcopy_big(dst):
            return pltpu.make_async_copy(zero_ref, xs_ref.at[pl.ds(dst, MOE_TM)], sem)

        def per_expert(e, tot):
            n = tailn_ref[e]
            base = tail_ref[e]

            def per_chunk(c, carry):
                zcopy(pl.multiple_of(base + c * MOE_G, MOE_G)).start()
                return carry

            lax.fori_loop(0, n, per_chunk, 0)
            return tot + n

        ztotal = lax.fori_loop(0, N_EXPERTS, per_expert, 0)

        def zwait(c, carry):
            zcopy(0).wait()
            return carry

        lax.fori_loop(0, ztotal, zwait, 0)

        nbig = tailn_ref[N_EXPERTS]
        big0 = tail_ref[N_EXPERTS]

        def big_start(c, carry):
            zcopy_big(pl.multiple_of(big0 + c * MOE_TM, MOE_TM)).start()
            return carry

        def big_wait(c, carry):
            zcopy_big(0).wait()
            return carry

        lax.fori_loop(0, nbig, big_start, 0)
        lax.fori_loop(0, nbig, big_wait, 0)


def _dispatch(xaug, sel, locv, cntp, loc, goff, nch, tail, tailn, s_rows):
    npb, l, _ = xaug.shape
    t = MOE_T
    nt_b = l // t
    nt = npb * nt_b
    tile = lambda w: pl.BlockSpec((1, t, w), lambda j, *_: (j // nt_b, j % nt_b, 0))
    return pl.pallas_call(
        _dispatch_kernel,
        grid_spec=pltpu.PrefetchScalarGridSpec(
            num_scalar_prefetch=6,
            grid=(nt,),
            in_specs=[tile(AUG_W), tile(LANES), pl.BlockSpec((1, 1, LANES), lambda j, *_: (j, 0, 0))],
            out_specs=[pl.BlockSpec((1, t, MOE_SLOTS), lambda j, *_: (j, 0, 0)),
                       pl.BlockSpec(memory_space=pl.ANY)],
            scratch_shapes=[pltpu.VMEM((2, MOE_SLOTS, AUG_W), BF16), pltpu.VMEM((MOE_TM, AUG_W), BF16),
                            pltpu.SemaphoreType.DMA((2,))]),
        out_shape=[jax.ShapeDtypeStruct((nt, t, MOE_SLOTS), BF16), jax.ShapeDtypeStruct((s_rows, AUG_W), BF16)],
        compiler_params=_cparams("arbitrary"),
        name="moe_dispatch",
    )(cntp, loc, goff, nch, tail, tailn, xaug, sel, locv)


def _ffn_kernel(te_ref, na_ref, xs_ref, w1_ref, b1_ref, w2_ref, b2_ref, perm_ref, ys_ref, w1p_ref, w2b_ref):
    i = pl.program_id(0)
    e = te_ref[i]
    active = i < na_ref[0]
    fresh = jnp.logical_or(i == 0, e != te_ref[jnp.maximum(i - 1, 0)])
    hh = EXPERT_HIDDEN // 2

    @pl.when(jnp.logical_and(active, fresh))
    def _():
        for c in range(2):
            blk = w1_ref[0, 0, :, 2 * hh * c:2 * hh * (c + 1)].astype(BF16)
            r = _dot(blk, perm_ref[...])
            w1p_ref[:, hh * c:hh * (c + 1)] = r[:, :hh].astype(BF16)
            w1p_ref[:, EXPERT_HIDDEN + hh * c:EXPERT_HIDDEN + hh * (c + 1)] = r[:, hh:].astype(BF16)
        w2b_ref[...] = w2_ref[0, 0].astype(BF16)

    @pl.when(active)
    def _():
        x = xs_ref[:, 0:D_MODEL]
        extra = xs_ref[:, D_MODEL:AUG_W].astype(F32)
        lane = lax.broadcasted_iota(jnp.int32, extra.shape, 1)
        mine = jnp.logical_and(lane % N_EXPERTS == e, lane < 3 * N_EXPERTS)
        wslot = jnp.where(mine, extra, 0.0).sum(axis=-1, keepdims=True)
        h = _dot(x, w1p_ref[...]) + b1_ref[0]
        g = jnp.minimum(h[:, :EXPERT_HIDDEN], SWIGLU_LIMIT)
        up = jnp.clip(h[:, EXPERT_HIDDEN:], -SWIGLU_LIMIT, SWIGLU_LIMIT)
        act = g * _sigmoid(SWIGLU_ALPHA * g) * (up + 1.0)
        y = _dot(act.astype(BF16), w2b_ref[...]) + b2_ref[0]
        ys_ref[...] = (wslot * y).astype(BF16)

    @pl.when(jnp.logical_not(active))
    def _():
        ys_ref[...] = jnp.zeros(ys_ref.shape, BF16)


def _ffn(xs, tile_e, n_active, w1, b1p, w2, b2, perm, n_tiles, layer):
    tm = MOE_TM
    d, h2 = w1.shape[2], w1.shape[3]
    once = pl.Buffered(1)
    return pl.pallas_call(
        _ffn_kernel,
        grid_spec=pltpu.PrefetchScalarGridSpec(
            num_scalar_prefetch=2,
            grid=(n_tiles,),
            in_specs=[pl.BlockSpec((tm, AUG_W), lambda i, te, na: (jnp.where(i < na[0], i, 0), 0)),
                      pl.BlockSpec((1, 1, d, h2), lambda i, te, na: (layer, te[i], 0, 0), pipeline_mode=once),
                      pl.BlockSpec((1, 1, h2), lambda i, te, na: (te[i], 0, 0)),
                      pl.BlockSpec((1, 1, h2 // 2, d), lambda i, te, na: (layer, te[i], 0, 0), pipeline_mode=once),
                      pl.BlockSpec((1, 1, d), lambda i, te, na: (te[i], 0, 0)),
                      pl.BlockSpec(perm.shape, lambda i, te, na: (0, 0), pipeline_mode=once)],
            out_specs=pl.BlockSpec((tm, d), lambda i, te, na: (i, 0)),
            scratch_shapes=[pltpu.VMEM((d, h2), BF16), pltpu.VMEM((h2 // 2, d), BF16)]),
        out_shape=jax.ShapeDtypeStruct((n_tiles * tm, d), BF16),
        compiler_params=_cparams("arbitrary"),
        name="moe_ffn",
    )(tile_e, n_active, xs, w1, b1p, w2, b2, perm)


def _combine_kernel(cntp_ref, loc_ref, goff_ref, nch_ref, pt_ref, x_ref, g2_ref, lg_ref, lb_ref, ys_ref,
                    o_ref, stage_ref, sems):
    j = pl.program_id(0)
    nt = pl.num_programs(0)
    buf = j % 2

    def in_copy(b):
        return lambda dst, src, rows: pltpu.make_async_copy(
            ys_ref.at[pl.ds(src, rows)], stage_ref.at[b].at[pl.ds(dst, rows)], sems.at[b])

    @pl.when(j == 0)
    def _():
        stage_ref[...] = jnp.zeros(stage_ref.shape, BF16)
        _group_copies(j, cntp_ref, loc_ref, goff_ref, in_copy(buf))

    @pl.when(j + 1 < nt)
    def _():
        _group_copies(j + 1, cntp_ref, loc_ref, goff_ref, in_copy(1 - buf))

    _wait_copies(nch_ref, j, in_copy(buf))
    y = _dot(pt_ref[0], stage_ref[buf])
    o_ref[0] = _ln(DEEPNORM_ALPHA * x_ref[0] + g2_ref[0] * y) * lg_ref[...] + lb_ref[...]


def _combine(pt, ys, x1, g2, lg, lb, cntp, loc, goff, nch):
    npb, l, d = x1.shape
    t = MOE_T
    nt_b = l // t
    tile = lambda w: pl.BlockSpec((1, t, w), lambda j, *_: (j // nt_b, j % nt_b, 0))
    const = lambda a: pl.BlockSpec(a.shape, lambda j, *_: (0,) * a.ndim)
    return pl.pallas_call(
        _combine_kernel,
        grid_spec=pltpu.PrefetchScalarGridSpec(
            num_scalar_prefetch=4,
            grid=(npb * nt_b,),
            in_specs=[pl.BlockSpec((1, t, MOE_SLOTS), lambda j, *_: (j, 0, 0)), tile(d),
                      pl.BlockSpec((1, 1, d), lambda j, *_: (j // nt_b, 0, 0)), const(lg), const(lb),
                      pl.BlockSpec(memory_space=pl.ANY)],
            out_specs=tile(d),
            scratch_shapes=[pltpu.VMEM((2, MOE_SLOTS, d), BF16), pltpu.SemaphoreType.DMA((2,))]),
        out_shape=jax.ShapeDtypeStruct((npb, l, d), F32),
        compiler_params=_cparams("arbitrary"),
        name="moe_combine",
    )(cntp, loc, goff, nch, pt, x1, g2, lg, lb, ys)


def _moe(x1, sh2, sc2, g2, rw, rb, w1, b1p, w2, b2, perm, lg, lb, npb, layer):
    _, l, d = x1.shape
    n_tok = npb * l
    nt = n_tok // MOE_T
    xaug, sel, cnt = _route(x1, sh2, sc2, rw, rb, npb)

    i32 = jnp.int32
    cnt = cnt[:, 0, :N_EXPERTS].astype(i32)
    cntp = (cnt + (MOE_G - 1)) // MOE_G * MOE_G
    loc = jnp.cumsum(cntp, axis=1) - cntp
    tot = cntp.sum(axis=0)
    seg = (tot + (MOE_TM - 1)) // MOE_TM * MOE_TM
    seg_start = jnp.cumsum(seg) - seg
    goff = seg_start[None, :] + jnp.cumsum(cntp, axis=0) - cntp
    s_max = TOP_K * n_tok + nt * N_EXPERTS * (MOE_G - 1) + N_EXPERTS * (MOE_TM - MOE_G)
    n_tiles = -(-s_max // MOE_TM)
    cum_tiles = jnp.cumsum(seg // MOE_TM)
    tile_e = jnp.minimum((jnp.arange(n_tiles, dtype=i32)[:, None] >= cum_tiles[None, :]).astype(i32).sum(axis=1),
                         N_EXPERTS - 1)
    n_active = cum_tiles[-1:].astype(i32)
    gran = cntp // MOE_G
    nch = jnp.stack([(gran // 2).sum(axis=1), (gran % 2).sum(axis=1)], axis=1).reshape(-1).astype(i32)
    locv = jnp.pad(loc.astype(F32), ((0, 0), (0, LANES - N_EXPERTS)))[:, None, :]
    flat = lambda a: a.reshape(-1).astype(i32)

    tail = jnp.concatenate([seg_start + tot, n_active * MOE_TM]).astype(i32)
    tailn = jnp.concatenate([(seg - tot) // MOE_G, n_tiles - n_active]).astype(i32)
    pt, xs = _dispatch(xaug, sel, locv, flat(cntp), flat(loc), flat(goff), nch, tail, tailn, n_tiles * MOE_TM)
    ys = _ffn(xs, tile_e, n_active, w1, b1p, w2, b2, perm, n_tiles, layer)
    return _combine(pt, ys, x1, g2, lg, lb, flat(cntp), flat(loc), flat(goff), nch)


def _deinterleave(n):
    return np.concatenate([np.arange(0, n, 2), np.arange(1, n, 2)])


def _rope_tables(l):
    rows = l // GRID_W
    r = np.repeat(np.arange(rows, dtype=np.float32), GRID_W)
    col = np.tile(np.arange(GRID_W, dtype=np.float32), rows)
    axis_dim = ATT_HEAD_DIM // 2
    inv_freq = (ROPE_THETA ** (-np.arange(0, axis_dim, 2, dtype=np.float32) / axis_dim)).astype(np.float32)
    ang = np.concatenate([r[:, None] * inv_freq, col[:, None] * inv_freq], axis=-1)
    c, s = np.cos(ang).astype(np.float32), np.sin(ang).astype(np.float32)
    z = np.zeros_like(s)
    rc = np.tile(np.concatenate([c, c], -1), (1, 2))
    rs1 = np.tile(np.concatenate([-s, z], -1), (1, 2))
    rs2 = np.tile(np.concatenate([z, s], -1), (1, 2))
    ident = (np.ones_like(rc), np.zeros_like(rc), np.zeros_like(rc))
    return tuple(jnp.asarray(np.stack([a, b])) for a, b in zip((rc, rs1, rs2), ident))


def kernel(x, c, ctx, c_ctx, w_mod, b_mod, w_in, att_q_norm, att_k_norm, gla_wdec_f, gla_bdec_f, gla_wdec_b, gla_bdec_b, gla_out_norm, hy_conv_w, hy_conv_b, hy_fw1, hy_fb1, hy_fw2, hy_fb2, hy_fw3, hy_fb3, hy_skip, w_branch_a, w_branch_b, w_branch_c, w_out, ln1_g, ln1_b, router_w, router_b, exp_w1, exp_b1, exp_w2, exp_b2, ln2_g, ln2_b):
    b, l, d = x.shape
    lc = ctx.shape[1]
    assert d == D_MODEL and (b * lc) % l == 0 and l % lc == 0 and lc % GLA_CHUNK == 0
    pc = (b * lc) // l
    p = b + pc

    perm64 = _deinterleave(ATT_HEAD_DIM)
    q_cols = np.concatenate([np.concatenate([64 * j + perm64, 64 * (j + 4) + perm64]) for j in range(4)])
    k_cols = ATT_Q_W + np.concatenate([perm64, 64 + perm64])
    v_cols = ATT_Q_W + ATT_KV_W + np.arange(ATT_KV_W)
    att_cols = np.concatenate([q_cols, k_cols, v_cols])
    o0 = ATT_Q_W + 2 * ATT_KV_W
    gla_cols = o0 + np.concatenate([np.arange(0, 2 * GLA_K_W + GLA_V_W + 2 * GLA_RANK)])
    og0 = o0 + 2 * GLA_K_W + GLA_V_W + 2 * GLA_RANK
    hy0 = og0 + GLA_V_W
    gt0 = hy0 + 3 * HY_WIDTH
    gain_perm = np.concatenate([perm64, perm64])
    bd = jnp.asarray(np.kron(np.eye(2), np.ones((64, 64))), dtype=BF16)
    rope = _rope_tables(l)
    fwd_l, inv_l = _dft_tables(l)
    fwd_c, inv_c = _dft_tables(lc)
    cmap = np.concatenate([np.arange(b), np.full(pc, b)])
    hh = EXPERT_HIDDEN // 2
    perm_np = np.zeros((2 * hh, 2 * hh), np.float32)
    perm_np[2 * np.arange(hh), np.arange(hh)] = 1.0
    perm_np[2 * np.arange(hh) + 1, hh + np.arange(hh)] = 1.0
    perm = jnp.asarray(perm_np, dtype=BF16)

    x_pair = (x, ctx.reshape(pc, l, d))
    cc = jnp.concatenate([c, c_ctx[None], jnp.zeros((7, d), F32)], axis=0)

    for i in range(DEPTH):
        with_ctx = i < DEPTH - 1
        npb = p if with_ctx else b
        mod = _modulation(cc, w_mod[i], b_mod[i])[cmap]
        sh1, sc1, g1, sh2, sc2, g2 = [m[:, None, :] for m in jnp.split(mod, 6, axis=-1)]

        wi = w_in[i]
        watt = wi[:, att_cols].astype(BF16)
        wgla = jnp.pad(wi[:, gla_cols], ((0, 0), (0, GLA_GROUP_W - gla_cols.size))).astype(BF16)
        wog = wi[:, og0:hy0].astype(BF16)
        why = wi[:, hy0:gt0].astype(BF16)
        wgt = wi[:, gt0:].astype(BF16)
        qg = att_q_norm[i][gain_perm].reshape(1, LANES)
        kg = att_k_norm[i][gain_perm].reshape(1, LANES)
        qn, kn, v, gla, og, hy, gates = _inproj(*x_pair, sh1, sc1, (watt, wgla, wog, why, wgt), qg, kg, rope, bd, b)

        o_a = _attention(qn, kn, v, b, lc, with_ctx)

        wdec = jnp.zeros((LANES, 2 * GLA_K_W), F32)
        wdec = wdec.at[0:GLA_RANK, 0:GLA_K_W].set(gla_wdec_f[i])
        wdec = wdec.at[GLA_RANK:2 * GLA_RANK, GLA_K_W:].set(gla_wdec_b[i])
        bdec = jnp.concatenate([gla_bdec_f[i], gla_bdec_b[i]]).reshape(1, -1)
        o_b, o_b_ctx = _gla(gla, og, wdec, bdec, gla_out_norm[i].reshape(1, -1), b, lc)

        hre, him = _hy_filter(l, hy_fw1[i], hy_fb1[i], hy_fw2[i], hy_fb2[i], hy_fw3[i], hy_fb3[i], fwd_l)
        u_h, x0_h, t_h = _hy_pre(hy, hy_conv_w[i], hy_conv_b[i], hy_skip[i], b, l, b)
        y_h = _hy_conv(u_h, fwd_l, inv_l, hre, him)
        if with_ctx:
            hre_c, him_c = _hy_filter(lc, hy_fw1[i], hy_fb1[i], hy_fw2[i], hy_fb2[i], hy_fw3[i], hy_fb3[i], fwd_c)
            u_c, x0_c, t_c = _hy_pre(hy, hy_conv_w[i], hy_conv_b[i], hy_skip[i], b, lc, b)
            y_c = _hy_conv(u_c, fwd_c, inv_c, hre_c, him_c)
            fold = lambda a_ctx: a_ctx.reshape(pc, l, a_ctx.shape[-1])
            pairs = [(o_b, fold(o_b_ctx)), (y_h, fold(y_c)), (x0_h, fold(x0_c)), (t_h, fold(t_c))]
        else:
            pairs = [(a, a) for a in (o_b, y_h, x0_h, t_h)]

        wa_rows = np.concatenate([np.concatenate([64 * j + np.arange(64), 64 * (j + 4) + np.arange(64)])
                                  for j in range(4)])
        x1 = _merge(x_pair, o_a, *pairs, gates, g1,
                    w_branch_a[i][wa_rows].astype(BF16), w_branch_b[i].astype(BF16),
                    w_branch_c[i].astype(BF16), w_out[i].astype(BF16),
                    ln1_g[i].reshape(1, d), ln1_b[i].reshape(1, d), npb)

        rw = jnp.pad(router_w[i], ((0, 0), (0, LANES - N_EXPERTS)))
        rb = jnp.concatenate([router_b[i], jnp.full((LANES - N_EXPERTS,), NEG_BIG, F32)]).reshape(1, LANES)
        b1p = jnp.concatenate([exp_b1[i][..., 0::2], exp_b1[i][..., 1::2]], axis=-1)[:, None, :]
        x_all = _moe(x1, sh2, sc2, g2, rw, rb, exp_w1, b1p, exp_w2, exp_b2[i][:, None, :], perm,
                     ln2_g[i].reshape(1, d), ln2_b[i].reshape(1, d), npb, i)
        x_pair = (x_all, x_all)
    return x_all
```

```python
import functools
import math

import numpy as np
import jax
import jax.numpy as jnp
from jax import lax
from jax.experimental import pallas as pl
from jax.experimental.pallas import tpu as pltpu

F32 = jnp.float32
BF16 = jnp.bfloat16

D_MODEL = 1024
DEPTH = 2
GRID_W = 64
ATT_HEADS = 8
ATT_KV_HEADS = 2
ATT_HEAD_DIM = 64
ROPE_THETA = 10000.0
GLA_HEADS = 4
GLA_DK = 64
GLA_DV = 128
GLA_RANK = 16
GLA_TAU = 16.0
GLA_CHUNK = 64
HY_WIDTH = 512
HY_POS_FREQS = 16
HY_DECAY_SLOW = math.log(1e-2) / 1.5
HY_DECAY_FAST = math.log(1e-2) / 0.3
N_EXPERTS = 32
TOP_K = 4
EXPERT_HIDDEN = D_MODEL
SWIGLU_LIMIT = 7.0
SWIGLU_ALPHA = 1.702
DEEPNORM_ALPHA = (2 * DEPTH) ** 0.25
LN_EPS = 1e-5
RMS_EPS = 1e-6

ATT_Q_W = ATT_HEADS * ATT_HEAD_DIM
ATT_KV_W = ATT_KV_HEADS * ATT_HEAD_DIM
GLA_K_W = GLA_HEADS * GLA_DK
GLA_V_W = GLA_HEADS * GLA_DV
GLA_GROUP_W = 1152
LANES = 128
NEG_BIG = -3.0e38

VMEM_LIMIT = 56 * 1024 * 1024


def _cparams(*sem):
    return pltpu.CompilerParams(dimension_semantics=sem, vmem_limit_bytes=VMEM_LIMIT)


def _dot(a, b):
    return jnp.dot(a, b, preferred_element_type=F32)


def _dot_nt(a, b):
    return lax.dot_general(a, b, (((1,), (1,)), ((), ())), preferred_element_type=F32)


def _dot_tn(a, b):
    return lax.dot_general(a, b, (((0,), (0,)), ((), ())), preferred_element_type=F32)


def _split2(x):
    hi = x.astype(BF16)
    lo = (x - hi.astype(F32)).astype(BF16)
    return hi, lo


def _split3(x):
    hi = x.astype(BF16)
    r = x - hi.astype(F32)
    mid = r.astype(BF16)
    lo = (r - mid.astype(F32)).astype(BF16)
    return hi, mid, lo


def _dot3(a, b):
    ah, al = _split2(a)
    bh, bl = _split2(b)
    return _dot(ah, bh) + (_dot(ah, bl) + _dot(al, bh))


def _dot_exact_lhs(a_bf16, b):
    b0, b1, b2 = _split3(b)
    return _dot(a_bf16, b0) + (_dot(a_bf16, b1) + _dot(a_bf16, b2))


def _ln(x):
    mu = jnp.mean(x, axis=-1, keepdims=True)
    xc = x - mu
    var = jnp.mean(xc * xc, axis=-1, keepdims=True)
    return xc * lax.rsqrt(var + LN_EPS)


def _sigmoid(x):
    return 1.0 / (1.0 + jnp.exp(-x))


def _mod_kernel(c_ref, w_ref, b_ref, o_ref):
    c = c_ref[...]
    o_ref[...] = _dot3(c * _sigmoid(c), w_ref[...]) + b_ref[...]


def _modulation(cc, w_mod, b_mod):
    rows, d = cc.shape
    n = w_mod.shape[1]
    tn = 1024
    return pl.pallas_call(
        _mod_kernel,
        grid=(n // tn,),
        in_specs=[pl.BlockSpec((rows, d), lambda j: (0, 0)),
                  pl.BlockSpec((d, tn), lambda j: (0, j)),
                  pl.BlockSpec((1, tn), lambda j: (0, j))],
        out_specs=pl.BlockSpec((rows, tn), lambda j: (0, j)),
        out_shape=jax.ShapeDtypeStruct((rows, n), F32),
        compiler_params=_cparams("arbitrary"),
        name="modulation",
    )(cc, w_mod, b_mod.reshape(1, n))


def _head_norm_rope(t, gain, rc, rs1, rs2, bd):
    sh, sl = _split2(t * t)
    ss = _dot(sh, bd) + _dot(sl, bd)
    tn = t * lax.rsqrt(ss * (1.0 / ATT_HEAD_DIM) + RMS_EPS) * gain
    return tn * rc + pltpu.roll(tn, 96, 1) * rs1 + pltpu.roll(tn, 32, 1) * rs2


def _pair_specs(tm, w, n_split, n_i):
    first = pl.BlockSpec((1, tm, w), lambda b, i: (jnp.minimum(b, n_split - 1), jnp.where(b < n_split, i, n_i - 1), 0))
    rest = pl.BlockSpec((1, tm, w), lambda b, i: (jnp.maximum(b - n_split, 0), jnp.where(b >= n_split, i, 0), 0))
    return [first, rest]


def _pick(use_rest, first_ref, rest_ref):
    return jnp.where(use_rest, rest_ref[0], first_ref[0])


def _inproj_kernel(x_ref, xr_ref, sh_ref, sc_ref, watt_ref, wgla_ref, wog_ref, why_ref,
                   qg_ref, kg_ref, rc_ref, rs1_ref, rs2_ref, bd_ref,
                   qn_ref, kn_ref, v_ref, gla_ref, og_ref, hy_ref, *, n_split):
    x = _pick(pl.program_id(0) >= n_split, x_ref, xr_ref)
    u = _ln(x) * (1.0 + sc_ref[0]) + sh_ref[0]
    ub = u.astype(BF16)
    rc, rs1, rs2, bd = rc_ref[0], rs1_ref[0], rs2_ref[0], bd_ref[...]
    att = _dot(ub, watt_ref[...])
    for j in range(ATT_Q_W // LANES):
        t = att[:, LANES * j:LANES * (j + 1)]
        qn = _head_norm_rope(t, qg_ref[...], rc, rs1, rs2, bd) * (ATT_HEAD_DIM ** -0.5)
        qn_ref[0, :, LANES * j:LANES * (j + 1)] = qn.astype(BF16)
    kn = _head_norm_rope(att[:, ATT_Q_W:ATT_Q_W + LANES], kg_ref[...], rc, rs1, rs2, bd)
    kn_ref[0] = kn.astype(BF16)
    v_ref[0, :, 0:LANES] = att[:, ATT_Q_W + LANES:].astype(BF16)
    v_ref[0, :, LANES:2 * LANES] = jnp.ones((att.shape[0], LANES), BF16)
    gla_ref[0] = _dot(ub, wgla_ref[...])
    og_ref[0] = _dot(ub, wog_ref[...]).astype(BF16)
    hy_ref[0] = _dot(ub, why_ref[...]).astype(BF16)


def _inproj(x_first, x_rest, sh, sc, wts, qg, kg, rope, bd, n_lat):
    n_split, l, d = x_first.shape
    p = sh.shape[0]
    tm = min(256, l)
    watt, wgla, wog, why = wts
    rc, rs1, rs2 = rope
    const = lambda a: pl.BlockSpec(a.shape, lambda b, i: (0,) * a.ndim)
    row = lambda w: pl.BlockSpec((1, tm, w), lambda b, i: (b, i, 0))
    tab = pl.BlockSpec((1, tm, LANES), lambda b, i: (jnp.where(b >= n_lat, 1, 0), i, 0))
    vec = pl.BlockSpec((1, 1, d), lambda b, i: (b, 0, 0))
    widths = (ATT_Q_W, LANES, 2 * LANES, GLA_GROUP_W, GLA_V_W, 3 * HY_WIDTH)
    dtypes = (BF16, BF16, BF16, F32, BF16, BF16)
    return pl.pallas_call(
        functools.partial(_inproj_kernel, n_split=n_split),
        grid=(p, l // tm),
        in_specs=_pair_specs(tm, d, n_split, l // tm) + [
            vec, vec, const(watt), const(wgla), const(wog), const(why),
            const(qg), const(kg), tab, tab, tab, const(bd)],
        out_specs=[row(w) for w in widths],
        out_shape=[jax.ShapeDtypeStruct((p, l, w), dt) for w, dt in zip(widths, dtypes)],
        compiler_params=_cparams("parallel", "arbitrary"),
        name="inproj",
    )(x_first, x_rest, sh, sc, watt, wgla, wog, why, qg, kg, rc, rs1, rs2, bd)


def _attend_group(qg, ks, vs, lo):
    zero = jnp.zeros_like(qg)
    outs = []
    for qh in (jnp.where(lo, qg, zero), jnp.where(lo, zero, qg)):
        ss = [_dot_nt(qh, k) for k in ks]
        m = ss[0].max(axis=-1, keepdims=True)
        for s in ss[1:]:
            m = jnp.maximum(m, s.max(axis=-1, keepdims=True))
        acc = 0.0
        for s, v in zip(ss, vs):
            acc = acc + _dot(jnp.exp((s - m).astype(BF16)), v)
        outs.append(acc[:, 0:LANES] / acc[:, LANES:2 * LANES])
    return jnp.where(lo, outs[0], outs[1])


def _attn_kernel(q_ref, k_ref, v_ref, kc_ref, vc_ref, o_ref, *, n_lat_tiles, with_ctx):
    i = pl.program_id(1)
    tq = q_ref.shape[1]
    lo = lax.broadcasted_iota(jnp.int32, (tq, LANES), 1) < ATT_HEAD_DIM

    def run(ks, vs):
        for j in range(ATT_Q_W // LANES):
            o = _attend_group(q_ref[0, :, LANES * j:LANES * (j + 1)], ks, vs, lo)
            o_ref[0, :, LANES * j:LANES * (j + 1)] = o.astype(BF16)

    if with_ctx:
        @pl.when(i < n_lat_tiles)
        def _():
            run([k_ref[0], kc_ref[0]], [v_ref[0], vc_ref[0]])

        @pl.when(i == n_lat_tiles)
        def _():
            run([kc_ref[0]], [vc_ref[0]])
    else:
        run([k_ref[0], kc_ref[0]], [v_ref[0], vc_ref[0]])


def _attention(qn, kn, v, n_lat, lc, with_ctx):
    p, l, _ = qn.shape
    tq = lc
    r = l // lc
    nt = l // tq
    ctx_idx = lambda b: (n_lat + b // r, b % r, 0)
    if with_ctx:
        qmap = lambda b, i: (jnp.where(i < nt, b, n_lat + b // r), jnp.where(i < nt, i, b % r), 0)
    else:
        qmap = lambda b, i: (b, i, 0)
    return pl.pallas_call(
        functools.partial(_attn_kernel, n_lat_tiles=nt, with_ctx=with_ctx),
        grid=(n_lat, nt + (1 if with_ctx else 0)),
        in_specs=[pl.BlockSpec((1, tq, ATT_Q_W), qmap),
                  pl.BlockSpec((1, l, LANES), lambda b, i: (b, 0, 0)),
                  pl.BlockSpec((1, l, 2 * LANES), lambda b, i: (b, 0, 0)),
                  pl.BlockSpec((1, lc, LANES), lambda b, i: ctx_idx(b)),
                  pl.BlockSpec((1, lc, 2 * LANES), lambda b, i: ctx_idx(b))],
        out_specs=pl.BlockSpec((1, tq, ATT_Q_W), qmap),
        out_shape=jax.ShapeDtypeStruct((p if with_ctx else n_lat, l, ATT_Q_W), BF16),
        compiler_params=_cparams("parallel", "arbitrary"),
        name="attention",
    )(qn, kn, v, kn, v)


def _gla_segment(src_ref, og_ref, out_ref, acc_ref, st_ref, qd_ref, ke_ref, vb_ref, dec_ref,
                 wdec_ref, bdec_ref, gn_ref, ls):
    c = GLA_CHUNK
    nc = ls // c
    rb = min(256, ls)

    bi = lax.broadcasted_iota(jnp.int32, (rb, rb), 0)
    bj = lax.broadcasted_iota(jnp.int32, (rb, rb), 1)
    same = (bi // c) == (bj // c)
    att_masks = (same & (bi >= bj), same & (bi <= bj))
    tri_blk = tuple(jnp.where(m, 1.0, 0.0).astype(BF16) for m in att_masks)
    ones_blk = jnp.where(same, 1.0, 0.0).astype(BF16)
    lo_b = lax.broadcasted_iota(jnp.int32, (rb, LANES), 1) < GLA_DK

    def prep(t, carry):
        r0 = pl.multiple_of(t * rb, rb)
        q = src_ref[0, pl.ds(r0, rb), 0:GLA_K_W] * (GLA_DK ** -0.5)
        k = src_ref[0, pl.ds(r0, rb), GLA_K_W:2 * GLA_K_W]
        vb = src_ref[0, pl.ds(r0, rb), 2 * GLA_K_W:2 * GLA_K_W + GLA_V_W].astype(BF16)
        vb_ref[pl.ds(r0, rb), :] = vb
        lr = src_ref[0, pl.ds(r0, rb), 2 * GLA_K_W + GLA_V_W:GLA_GROUP_W]
        la = jax.nn.log_sigmoid(_dot3(lr, wdec_ref[...]) + bdec_ref[...]) * (1.0 / GLA_TAU)
        for d in range(2):
            l0, l1, l2 = _split3(la[:, d * GLA_K_W:(d + 1) * GLA_K_W])
            cum = _dot(tri_blk[d], l0) + (_dot(tri_blk[d], l1) + _dot(tri_blk[d], l2))
            tot = _dot(ones_blk, l0) + (_dot(ones_blk, l1) + _dot(ones_blk, l2))
            qd = (q * jnp.exp(cum)).astype(BF16)
            ki = (k * jnp.exp(-cum)).astype(BF16)
            qd_ref[d, pl.ds(r0, rb), :] = qd
            ke_ref[d, pl.ds(r0, rb), :] = (k * jnp.exp(tot - cum)).astype(BF16)
            dec_ref[d, pl.ds(r0, rb), :] = jnp.exp(tot)
            for h in range(GLA_HEADS):
                g, half = h // 2, h % 2
                sl = slice(LANES * g, LANES * (g + 1))
                zero = jnp.zeros_like(qd[:, sl])
                qh = jnp.where(lo_b, qd[:, sl], zero) if half == 0 else jnp.where(lo_b, zero, qd[:, sl])
                att = jnp.where(att_masks[d], _dot_nt(qh, ki[:, sl]), 0.0)
                o = _dot(att.astype(BF16), vb[:, GLA_DV * h:GLA_DV * (h + 1)])
                if d == 0:
                    acc_ref[pl.ds(r0, rb), GLA_DV * h:GLA_DV * (h + 1)] = o
                else:
                    acc_ref[pl.ds(r0, rb), GLA_DV * h:GLA_DV * (h + 1)] += o
        return carry

    lax.fori_loop(0, ls // rb, prep, 0, unroll=2 if (ls // rb) % 2 == 0 else 1)

    lo = lax.broadcasted_iota(jnp.int32, (c, LANES), 1) < GLA_DK

    def body(n, carry):
        for d, cn in ((0, n), (1, nc - 1 - n)):
            r0 = pl.multiple_of(cn * c, c)
            dec = dec_ref[d, pl.ds(r0, 1), :]
            for h in range(GLA_HEADS):
                g, half = h // 2, h % 2
                sl = slice(LANES * g, LANES * (g + 1))
                qd = qd_ref[d, pl.ds(r0, c), sl]
                zero = jnp.zeros_like(qd)
                qd = jnp.where(lo, qd, zero) if half == 0 else jnp.where(lo, zero, qd)
                vh = vb_ref[pl.ds(r0, c), GLA_DV * h:GLA_DV * (h + 1)]
                st = st_ref[d * GLA_HEADS + h]
                acc_ref[pl.ds(r0, c), GLA_DV * h:GLA_DV * (h + 1)] += _dot_nt(qd, st.astype(BF16))
                st_ref[d * GLA_HEADS + h] = dec[:, sl] * st + _dot_tn(vh, ke_ref[d, pl.ds(r0, c), sl])
        return carry

    lax.fori_loop(0, nc, body, 0, unroll=4 if nc % 4 == 0 else (2 if nc % 2 == 0 else 1))

    tr = min(256, ls)

    def fin(t, carry):
        r0 = pl.multiple_of(t * tr, tr)
        for h in range(GLA_HEADS):
            sl = slice(GLA_DV * h, GLA_DV * (h + 1))
            o = acc_ref[pl.ds(r0, tr), sl]
            y = o * lax.rsqrt(jnp.mean(o * o, axis=-1, keepdims=True) + RMS_EPS)
            y = y * gn_ref[...]
            og = og_ref[0, pl.ds(r0, tr), sl].astype(F32)
            out_ref[0, pl.ds(r0, tr), sl] = (y * (og * _sigmoid(og))).astype(BF16)
        return carry

    lax.fori_loop(0, ls // tr, fin, 0)


def _gla_kernel(gl_ref, glc_ref, og_ref, ogc_ref, wdec_ref, bdec_ref, gn_ref, o_ref, oc_ref,
                acc_ref, st_ref, qd_ref, ke_ref, vb_ref, dec_ref):
    st_ref[...] = jnp.zeros(st_ref.shape, F32)
    scratch = (acc_ref, st_ref, qd_ref, ke_ref, vb_ref, dec_ref)
    _gla_segment(glc_ref, ogc_ref, oc_ref, *scratch, wdec_ref, bdec_ref, gn_ref, glc_ref.shape[1])
    _gla_segment(gl_ref, og_ref, o_ref, *scratch, wdec_ref, bdec_ref, gn_ref, gl_ref.shape[1])


def _gla(gla, og, wdec, bdec, gn, n_lat, lc):
    p, l, _ = gla.shape
    r = l // lc
    ctx_idx = lambda b: (n_lat + b // r, b % r, 0)
    const = lambda a: pl.BlockSpec(a.shape, lambda b: (0,) * a.ndim)
    return pl.pallas_call(
        _gla_kernel,
        grid=(n_lat,),
        in_specs=[pl.BlockSpec((1, l, GLA_GROUP_W), lambda b: (b, 0, 0)),
                  pl.BlockSpec((1, lc, GLA_GROUP_W), ctx_idx),
                  pl.BlockSpec((1, l, GLA_V_W), lambda b: (b, 0, 0)),
                  pl.BlockSpec((1, lc, GLA_V_W), ctx_idx),
                  const(wdec), const(bdec), const(gn)],
        out_specs=[pl.BlockSpec((1, l, GLA_V_W), lambda b: (b, 0, 0)),
                   pl.BlockSpec((1, lc, GLA_V_W), lambda b: (b, 0, 0))],
        out_shape=[jax.ShapeDtypeStruct((n_lat, l, GLA_V_W), BF16),
                   jax.ShapeDtypeStruct((n_lat, lc, GLA_V_W), BF16)],
        scratch_shapes=[pltpu.VMEM((l, GLA_V_W), F32),
                        pltpu.VMEM((2 * GLA_HEADS, GLA_DV, LANES), F32),
                        pltpu.VMEM((2, l, GLA_K_W), BF16),
                        pltpu.VMEM((2, l, GLA_K_W), BF16), pltpu.VMEM((l, GLA_V_W), BF16),
                        pltpu.VMEM((2, l, GLA_K_W), F32)],
        compiler_params=_cparams("arbitrary"),
        name="gla",
    )(gla, gla, og, og, wdec, bdec, gn)


HY_FREQ_TILE = 512


def _dft_tables(l):
    n = 2 * l
    k = np.arange(l, dtype=np.int64)[:, None]
    t = np.arange(l, dtype=np.int64)[None, :]
    ang = 2.0 * np.pi * ((k * t) % n).astype(np.float64) / n
    fre = np.cos(ang)
    fim = -np.sin(ang)
    fim[0, :] = np.where(np.arange(l) % 2 == 0, 1.0, -1.0)
    tk = min(HY_FREQ_TILE, l)
    nk = l // tk
    fwd = np.concatenate([fre.reshape(nk, tk, l), fim.reshape(nk, tk, l)], axis=1)
    return jnp.asarray(fwd, dtype=BF16), jnp.asarray(fwd.transpose(0, 2, 1), dtype=BF16)


def _hy_filter_kernel(z_ref, win_ref, w1_ref, b1_ref, w2_ref, b2_ref, w3_ref, b3_ref, f_ref,
                      hre_ref, him_ref, hf_ref):
    j = pl.program_id(0)
    l = z_ref.shape[0]

    @pl.when(j == 0)
    def _():
        h = jnp.sin(_dot3(z_ref[...], w1_ref[...]) + b1_ref[...])
        h = jnp.sin(_dot3(h, w2_ref[...]) + b2_ref[...])
        h = _dot3(h, w3_ref[...]) + b3_ref[...]
        win = win_ref[...]
        hf = h[:, :HY_WIDTH] * win
        hb = h[:, HY_WIDTH:] * win
        row = lax.broadcasted_iota(jnp.int32, (l, HY_WIDTH), 0)
        hf_ref[:, :HY_WIDTH] = hf.astype(BF16)
        hf_ref[:, HY_WIDTH:] = jnp.where(row == 0, 0.0, hb).astype(BF16)

    tk = f_ref.shape[1] // 2
    ab = _dot(f_ref[0], hf_ref[...])
    a, b = ab[:tk], ab[tk:]
    first = (lax.broadcasted_iota(jnp.int32, (tk, HY_WIDTH), 0) == 0) & (j == 0)
    scl = jnp.where(first, 1.0 / (2 * l), 2.0 / (2 * l))
    hre_ref[...] = (a[:, :HY_WIDTH] + a[:, HY_WIDTH:]) * scl
    him_ref[...] = (b[:, :HY_WIDTH] + jnp.where(first, 1.0, -1.0) * b[:, HY_WIDTH:]) * scl


def _hy_filter(l, fw1, fb1, fw2, fb2, fw3, fb3, fwd):
    t = np.arange(l, dtype=np.float32)[:, None] / np.float32(l)
    freqs = np.arange(1, HY_POS_FREQS + 1, dtype=np.float32)
    z = np.concatenate([t, np.cos(2.0 * math.pi * freqs * t), np.sin(2.0 * math.pi * freqs * t)], axis=-1)
    z = np.pad(z, ((0, 0), (0, LANES - z.shape[1])))
    hid = fw2.shape[0]
    fw1 = jnp.pad(fw1, ((0, LANES - fw1.shape[0]), (0, LANES - hid)))
    fb1 = jnp.pad(fb1, (0, LANES - hid))
    fw2 = jnp.pad(fw2, ((0, LANES - hid), (0, LANES - hid)))
    fb2 = jnp.pad(fb2, (0, LANES - hid))
    fw3 = jnp.pad(fw3, ((0, LANES - hid), (0, 0)))
    deltas = np.abs(np.linspace(HY_DECAY_SLOW, HY_DECAY_FAST, HY_WIDTH, dtype=np.float32))
    win = np.exp(-t * deltas).astype(np.float32)
    nk, tk2, _ = fwd.shape
    tk = tk2 // 2
    const = lambda a: pl.BlockSpec(a.shape, lambda j: (0,) * a.ndim)
    args = (jnp.asarray(z), jnp.asarray(win), fw1, fb1.reshape(1, -1), fw2, fb2.reshape(1, -1),
            fw3, fb3.reshape(1, -1))
    return pl.pallas_call(
        _hy_filter_kernel,
        grid=(nk,),
        in_specs=[const(a) for a in args] + [pl.BlockSpec((1, tk2, l), lambda j: (j, 0, 0))],
        out_specs=[pl.BlockSpec((tk, HY_WIDTH), lambda j: (j, 0))] * 2,
        out_shape=[jax.ShapeDtypeStruct((l, HY_WIDTH), F32)] * 2,
        scratch_shapes=[pltpu.VMEM((l, 2 * HY_WIDTH), BF16)],
        compiler_params=_cparams("arbitrary"),
        name="hyena_filter",
    )(*args, fwd)


def _hy_pre_kernel(p0_ref, p1_ref, p2_ref, w0_ref, w1_ref, w2_ref, b0_ref, b1_ref, b2_ref, skip_ref,
                   u_ref, x0_ref, t_ref):
    ls = p0_ref.shape[1]
    row = lax.broadcasted_iota(jnp.int32, (ls, LANES), 0)

    def conv(p_ref, w_ref, b_ref):
        p = p_ref[0].astype(F32)
        prev = jnp.where(row == 0, 0.0, pltpu.roll(p, 1, 0))
        nxt = jnp.where(row == ls - 1, 0.0, pltpu.roll(p, ls - 1, 0))
        return prev * w_ref[0:1, :] + p * w_ref[1:2, :] + nxt * w_ref[2:3, :] + b_ref[...]

    x0 = conv(p0_ref, w0_ref, b0_ref)
    u = conv(p1_ref, w1_ref, b1_ref) * conv(p2_ref, w2_ref, b2_ref)
    u_ref[0] = u.astype(BF16)
    x0_ref[0] = x0.astype(BF16)
    t_ref[0] = (x0 * u * skip_ref[...]).astype(BF16)


def _hy_pre(hy, conv_w, conv_b, skip, nseq, ls, n_lat):
    p, l, _ = hy.shape
    r = l // ls
    nj = HY_WIDTH // LANES
    base = 0 if ls == l else n_lat
    src = lambda off: pl.BlockSpec((1, ls, LANES), lambda b, j: (base + b // r, b % r, off * nj + j))
    wsp = lambda off: pl.BlockSpec((3, LANES), lambda b, j: (0, off * nj + j))
    bsp = lambda off: pl.BlockSpec((1, LANES), lambda b, j: (0, off * nj + j))
    out = pl.BlockSpec((1, ls, LANES), lambda b, j: (b, 0, j))
    cb = conv_b.reshape(1, -1)
    return pl.pallas_call(
        _hy_pre_kernel,
        grid=(nseq, nj),
        in_specs=[src(0), src(1), src(2), wsp(0), wsp(1), wsp(2), bsp(0), bsp(1), bsp(2), bsp(0)],
        out_specs=[out] * 3,
        out_shape=[jax.ShapeDtypeStruct((nseq, ls, HY_WIDTH), BF16)] * 3,
        compiler_params=_cparams("parallel", "arbitrary"),
        name="hyena_pre",
    )(hy, hy, hy, conv_w, conv_w, conv_w, cb, cb, cb, skip.reshape(1, -1))


def _hy_conv_kernel(u_ref, f_ref, i_ref, hre_ref, him_ref, y_ref):
    j = pl.program_id(1)
    tk = hre_ref.shape[0]
    uu = _dot(f_ref[0], u_ref[0])
    ure, uim = uu[:tk], uu[tk:]
    hre, him = hre_ref[...], him_ref[...]
    first = (lax.broadcasted_iota(jnp.int32, ure.shape, 0) == 0) & (j == 0)
    yre = jnp.where(first, ure * hre, ure * hre - uim * him)
    yim = jnp.where(first, uim * him, ure * him + uim * hre)
    y = _dot(i_ref[0], jnp.concatenate([yre.astype(BF16), yim.astype(BF16)], axis=0))

    @pl.when(j == 0)
    def _():
        y_ref[0] = y

    @pl.when(j > 0)
    def _():
        y_ref[0] += y


def _hy_conv(u, fwd, inv, hre, him):
    nseq, ls, _ = u.shape
    nk, tk2, _ = fwd.shape
    tk = tk2 // 2
    return pl.pallas_call(
        _hy_conv_kernel,
        grid=(nseq, nk),
        in_specs=[pl.BlockSpec((1, ls, HY_WIDTH), lambda b, j: (b, 0, 0)),
                  pl.BlockSpec((1, tk2, ls), lambda b, j: (j, 0, 0)),
                  pl.BlockSpec((1, ls, tk2), lambda b, j: (j, 0, 0)),
                  pl.BlockSpec((tk, HY_WIDTH), lambda b, j: (j, 0)),
                  pl.BlockSpec((tk, HY_WIDTH), lambda b, j: (j, 0))],
        out_specs=pl.BlockSpec((1, ls, HY_WIDTH), lambda b, j: (b, 0, 0)),
        out_shape=jax.ShapeDtypeStruct((nseq, ls, HY_WIDTH), F32),
        compiler_params=_cparams("parallel", "arbitrary"),
        name="hyena_conv",
    )(u, fwd, inv, hre, him)


def _merge_kernel(x_ref, xr_ref, oa_ref, ob_ref, obr_ref, y_ref, yr_ref, x0_ref, x0r_ref, t_ref, tr_ref,
                  sh_ref, sc_ref, g1_ref, wgt_ref, wa_ref, wb_ref, wc_ref, wo_ref, lg_ref, lb_ref, o_ref,
                  *, n_split):
    d = x_ref.shape[2]
    rest = pl.program_id(0) >= n_split
    x = _pick(rest, x_ref, xr_ref)
    u = _ln(x) * (1.0 + sc_ref[0]) + sh_ref[0]
    ob = _pick(rest, ob_ref, obr_ref)
    oc = (_pick(rest, x0_ref, x0r_ref).astype(F32) * _pick(rest, y_ref, yr_ref)
          + _pick(rest, t_ref, tr_ref).astype(F32)).astype(BF16)
    gt = _dot(u.astype(BF16), wgt_ref[...])
    m = (_sigmoid(gt[:, 0:d]) * _dot(oa_ref[0], wa_ref[...])
         + _sigmoid(gt[:, d:2 * d]) * _dot(ob, wb_ref[...])
         + _sigmoid(gt[:, 2 * d:3 * d]) * _dot(oc, wc_ref[...]))
    mix = _dot(m.astype(BF16), wo_ref[...])
    o_ref[0] = _ln(DEEPNORM_ALPHA * x + g1_ref[0] * mix) * lg_ref[...] + lb_ref[...]


def _merge(x, o_a, o_b, y, x0, t, sh, sc, g1, wgt, wa, wb, wc, wo, lg, lb, npb):
    n_split, l, d = x[0].shape
    tm = min(512, l)
    n_i = l // tm
    row = lambda w: pl.BlockSpec((1, tm, w), lambda b, i: (b, i, 0))
    const = lambda a: pl.BlockSpec(a.shape, lambda b, i: (0,) * a.ndim)
    pair = lambda w: _pair_specs(tm, w, n_split, n_i)
    vec = pl.BlockSpec((1, 1, d), lambda b, i: (b, 0, 0))
    wgt_spec = pl.BlockSpec(wgt.shape, lambda b, i: (0, 0), pipeline_mode=pl.Buffered(1))
    return pl.pallas_call(
        functools.partial(_merge_kernel, n_split=n_split),
        grid=(npb, n_i),
        in_specs=pair(d) + [row(ATT_Q_W)] + pair(GLA_V_W) + pair(HY_WIDTH) + pair(HY_WIDTH) + pair(HY_WIDTH) + [
            vec, vec, vec, wgt_spec,
            const(wa), const(wb), const(wc), const(wo), const(lg), const(lb)],
        out_specs=row(d),
        out_shape=jax.ShapeDtypeStruct((npb, l, d), F32),
        compiler_params=_cparams("parallel", "arbitrary"),
        name="merge",
    )(*x, o_a, *o_b, *y, *x0, *t, sh, sc, g1, wgt, wa, wb, wc, wo, lg, lb)


MOE_T = 256
MOE_G = 16
MOE_TM = 1024
MOE_SLOTS = TOP_K * MOE_T + 512
AUG_W = D_MODEL + LANES


def _route_kernel(x_ref, sh_ref, sc_ref, rw_ref, rb_ref, xa_ref, sel_ref, cnt_ref):
    d = x_ref.shape[2]
    for h in range(x_ref.shape[1] // MOE_T):
        rows = slice(h * MOE_T, (h + 1) * MOE_T)
        u = _ln(x_ref[0, rows, :]) * (1.0 + sc_ref[0]) + sh_ref[0]
        xa_ref[0, rows, 0:d] = u.astype(BF16)
        logits = _dot3(u, rw_ref[...]) + rb_ref[...]
        lane = lax.broadcasted_iota(jnp.int32, logits.shape, 1).astype(F32)
        work = logits
        hots, vals = [], []
        for _ in range(TOP_K):
            m = work.max(axis=-1, keepdims=True)
            idx = jnp.where(work == m, lane, float(LANES)).min(axis=-1, keepdims=True)
            hot = lane == idx
            hots.append(hot)
            vals.append(m)
            work = jnp.where(hot, NEG_BIG, work)
        es = [jnp.exp(v - vals[0]) for v in vals]
        den = es[0] + es[1] + es[2] + es[3]
        comb = jnp.zeros_like(logits)
        sel = jnp.zeros_like(logits)
        for hot, e in zip(hots, es):
            comb = comb + jnp.where(hot, e / den, 0.0)
            sel = sel + jnp.where(hot, 1.0, 0.0)
        c0, c1, c2 = _split3(comb)
        aug = c0.astype(F32) + pltpu.roll(c1.astype(F32), 32, 1) + pltpu.roll(c2.astype(F32), 64, 1)
        xa_ref[0, rows, d:d + LANES] = aug.astype(BF16)
        sel_ref[0, rows, :] = sel.astype(BF16)
        cnt_ref[h] = sel.sum(axis=0, keepdims=True)


def _route(x_all, sh, sc, rw, rb, npb):
    _, l, d = x_all.shape
    per = next(n for n in (4, 2, 1) if l % (n * MOE_T) == 0)
    t = per * MOE_T
    n_i = l // t
    row = lambda w: pl.BlockSpec((1, t, w), lambda b, i: (b, i, 0))
    vec = pl.BlockSpec((1, 1, d), lambda b, i: (b, 0, 0))
    const = lambda a: pl.BlockSpec(a.shape, lambda b, i: (0,) * a.ndim)
    return pl.pallas_call(
        _route_kernel,
        grid=(npb, n_i),
        in_specs=[row(d), vec, vec, const(rw), const(rb)],
        out_specs=[row(AUG_W), row(LANES), pl.BlockSpec((per, 1, LANES), lambda b, i: (b * n_i + i, 0, 0))],
        out_shape=[jax.ShapeDtypeStruct((npb, l, AUG_W), BF16), jax.ShapeDtypeStruct((npb, l, LANES), BF16),
                   jax.ShapeDtypeStruct((npb * n_i * per, 1, LANES), F32)],
        compiler_params=_cparams("parallel", "arbitrary"),
        name="route",
    )(x_all, sh, sc, rw, rb)


def _group_copies(j, cntp_ref, loc_ref, goff_ref, make_copy):
    for e in range(N_EXPERTS):
        n = lax.shift_right_logical(cntp_ref[j * N_EXPERTS + e], int(math.log2(MOE_G)))
        loc = loc_ref[j * N_EXPERTS + e]
        off = goff_ref[j * N_EXPERTS + e]
        n2 = lax.shift_right_logical(n, 1)

        def per_pair(c, carry, loc=loc, off=off):
            make_copy(pl.multiple_of(loc + c * (2 * MOE_G), MOE_G),
                      pl.multiple_of(off + c * (2 * MOE_G), MOE_G), 2 * MOE_G).start()
            return carry

        lax.fori_loop(0, n2, per_pair, 0)

        @pl.when((n & 1) == 1)
        def _(loc=loc, off=off, n2=n2):
            make_copy(pl.multiple_of(loc + n2 * (2 * MOE_G), MOE_G),
                      pl.multiple_of(off + n2 * (2 * MOE_G), MOE_G), MOE_G).start()


def _wait_copies(nch_ref, j, make_copy):
    for k, rows in ((0, 2 * MOE_G), (1, MOE_G)):
        def wait_one(c, carry, rows=rows):
            make_copy(0, 0, rows).wait()
            return carry

        lax.fori_loop(0, nch_ref[2 * j + k], wait_one, 0)


def _dispatch_kernel(cntp_ref, loc_ref, goff_ref, nch_ref, tail_ref, tailn_ref,
                     xa_ref, sel_ref, locv_ref, pt_ref, xs_ref, stage_ref, zero_ref, sems):
    j = pl.program_id(0)
    nt = pl.num_programs(0)
    buf = j % 2
    sem = sems.at[buf]
    stage = stage_ref.at[buf]
    t = MOE_T

    def out_copy(src, dst, rows):
        return pltpu.make_async_copy(stage.at[pl.ds(src, rows)], xs_ref.at[pl.ds(dst, rows)], sem)

    @pl.when(j >= 2)
    def _():
        _wait_copies(nch_ref, jnp.maximum(j - 2, 0), out_copy)

    sel = sel_ref[0]
    ri = lax.broadcasted_iota(jnp.int32, (t, t), 0)
    ci = lax.broadcasted_iota(jnp.int32, (t, t), 1)
    rank = _dot(jnp.where(ci < ri, 1.0, 0.0).astype(BF16), sel)
    dest = rank + locv_ref[0]
    lane = lax.broadcasted_iota(jnp.int32, (t, LANES), 1).astype(F32)
    slot = lax.broadcasted_iota(jnp.int32, (t, MOE_SLOTS), 1)
    avail = sel.astype(F32)
    pt = jnp.zeros((t, MOE_SLOTS), F32)
    for _ in range(TOP_K):
        ek = jnp.where(avail > 0.0, lane, float(LANES)).min(axis=-1, keepdims=True)
        hot = lane == ek
        dk = jnp.where(hot, dest, 0.0).sum(axis=-1, keepdims=True).astype(jnp.int32)
        pt = pt + jnp.where(slot == dk, 1.0, 0.0)
        avail = jnp.where(hot, 0.0, avail)
    ptb = pt.astype(BF16)
    pt_ref[0] = ptb
    stage[...] = _dot_tn(ptb, xa_ref[0]).astype(BF16)
    _group_copies(j, cntp_ref, loc_ref, goff_ref, out_copy)

    @pl.when(j == nt - 1)
    def _():
        _wait_copies(nch_ref, j, out_copy)

        @pl.when(j >= 1)
        def _():
            other = sems.at[1 - buf]
            _wait_copies(nch_ref, jnp.maximum(j - 1, 0), lambda s, d, rows: pltpu.make_async_copy(
                stage_ref.at[1 - buf].at[pl.ds(s, rows)], xs_ref.at[pl.ds(d, rows)], other))

        zero_ref[...] = jnp.zeros(zero_ref.shape, BF16)

        def zcopy(dst):
            return pltpu.make_async_copy(zero_ref.at[pl.ds(0, MOE_G)], xs_ref.at[pl.ds(dst, MOE_G)], sem)

        def zcopy_big(dst):
            return pltpu.make_async_copy(zero_ref, xs_ref.at[pl.ds(dst, MOE_TM)], sem)

        def per_expert(e, tot):
            n = tailn_ref[e]
            base = tail_ref[e]

            def per_chunk(c, carry):
                zcopy(pl.multiple_of(base + c * MOE_G, MOE_G)).start()
                return carry

            lax.fori_loop(0, n, per_chunk, 0)
            return tot + n

        ztotal = lax.fori_loop(0, N_EXPERTS, per_expert, 0)

        def zwait(c, carry):
            zcopy(0).wait()
            return carry

        lax.fori_loop(0, ztotal, zwait, 0)

        nbig = tailn_ref[N_EXPERTS]
        big0 = tail_ref[N_EXPERTS]

        def big_start(c, carry):
            zcopy_big(pl.multiple_of(big0 + c * MOE_TM, MOE_TM)).start()
            return carry

        def big_wait(c, carry):
            zcopy_big(0).wait()
            return carry

        lax.fori_loop(0, nbig, big_start, 0)
        lax.fori_loop(0, nbig, big_wait, 0)


def _dispatch(xaug, sel, locv, cntp, loc, goff, nch, tail, tailn, s_rows):
    npb, l, _ = xaug.shape
    t = MOE_T
    nt_b = l // t
    nt = npb * nt_b
    tile = lambda w: pl.BlockSpec((1, t, w), lambda j, *_: (j // nt_b, j % nt_b, 0))
    return pl.pallas_call(
        _dispatch_kernel,
        grid_spec=pltpu.PrefetchScalarGridSpec(
            num_scalar_prefetch=6,
            grid=(nt,),
            in_specs=[tile(AUG_W), tile(LANES), pl.BlockSpec((1, 1, LANES), lambda j, *_: (j, 0, 0))],
            out_specs=[pl.BlockSpec((1, t, MOE_SLOTS), lambda j, *_: (j, 0, 0)),
                       pl.BlockSpec(memory_space=pl.ANY)],
            scratch_shapes=[pltpu.VMEM((2, MOE_SLOTS, AUG_W), BF16), pltpu.VMEM((MOE_TM, AUG_W), BF16),
                            pltpu.SemaphoreType.DMA((2,))]),
        out_shape=[jax.ShapeDtypeStruct((nt, t, MOE_SLOTS), BF16), jax.ShapeDtypeStruct((s_rows, AUG_W), BF16)],
        compiler_params=_cparams("arbitrary"),
        name="moe_dispatch",
    )(cntp, loc, goff, nch, tail, tailn, xaug, sel, locv)


def _ffn_kernel(te_ref, na_ref, xs_ref, w1_ref, b1_ref, w2_ref, b2_ref, perm_ref, ys_ref, w1p_ref, w2b_ref):
    i = pl.program_id(0)
    e = te_ref[i]
    active = i < na_ref[0]
    fresh = jnp.logical_or(i == 0, e != te_ref[jnp.maximum(i - 1, 0)])
    hh = EXPERT_HIDDEN // 2

    @pl.when(jnp.logical_and(active, fresh))
    def _():
        for c in range(2):
            blk = w1_ref[0, 0, :, 2 * hh * c:2 * hh * (c + 1)].astype(BF16)
            r = _dot(blk, perm_ref[...])
            w1p_ref[:, hh * c:hh * (c + 1)] = r[:, :hh].astype(BF16)
            w1p_ref[:, EXPERT_HIDDEN + hh * c:EXPERT_HIDDEN + hh * (c + 1)] = r[:, hh:].astype(BF16)
        w2b_ref[...] = w2_ref[0, 0].astype(BF16)

    @pl.when(active)
    def _():
        x = xs_ref[:, 0:D_MODEL]
        extra = xs_ref[:, D_MODEL:AUG_W].astype(F32)
        lane = lax.broadcasted_iota(jnp.int32, extra.shape, 1)
        mine = jnp.logical_and(lane % N_EXPERTS == e, lane < 3 * N_EXPERTS)
        wslot = jnp.where(mine, extra, 0.0).sum(axis=-1, keepdims=True)
        h = _dot(x, w1p_ref[...]) + b1_ref[0]
        g = jnp.minimum(h[:, :EXPERT_HIDDEN], SWIGLU_LIMIT)
        up = jnp.clip(h[:, EXPERT_HIDDEN:], -SWIGLU_LIMIT, SWIGLU_LIMIT)
        act = g * _sigmoid(SWIGLU_ALPHA * g) * (up + 1.0)
        y = _dot(act.astype(BF16), w2b_ref[...]) + b2_ref[0]
        ys_ref[...] = (wslot * y).astype(BF16)

    @pl.when(jnp.logical_not(active))
    def _():
        ys_ref[...] = jnp.zeros(ys_ref.shape, BF16)


def _ffn(xs, tile_e, n_active, w1, b1p, w2, b2, perm, n_tiles, layer):
    tm = MOE_TM
    d, h2 = w1.shape[2], w1.shape[3]
    once = pl.Buffered(1)
    return pl.pallas_call(
        _ffn_kernel,
        grid_spec=pltpu.PrefetchScalarGridSpec(
            num_scalar_prefetch=2,
            grid=(n_tiles,),
            in_specs=[pl.BlockSpec((tm, AUG_W), lambda i, te, na: (jnp.where(i < na[0], i, 0), 0)),
                      pl.BlockSpec((1, 1, d, h2), lambda i, te, na: (layer, te[i], 0, 0), pipeline_mode=once),
                      pl.BlockSpec((1, 1, h2), lambda i, te, na: (te[i], 0, 0)),
                      pl.BlockSpec((1, 1, h2 // 2, d), lambda i, te, na: (layer, te[i], 0, 0), pipeline_mode=once),
                      pl.BlockSpec((1, 1, d), lambda i, te, na: (te[i], 0, 0)),
                      pl.BlockSpec(perm.shape, lambda i, te, na: (0, 0), pipeline_mode=once)],
            out_specs=pl.BlockSpec((tm, d), lambda i, te, na: (i, 0)),
            scratch_shapes=[pltpu.VMEM((d, h2), BF16), pltpu.VMEM((h2 // 2, d), BF16)]),
        out_shape=jax.ShapeDtypeStruct((n_tiles * tm, d), BF16),
        compiler_params=_cparams("arbitrary"),
        name="moe_ffn",
    )(tile_e, n_active, xs, w1, b1p, w2, b2, perm)


def _combine_kernel(cntp_ref, loc_ref, goff_ref, nch_ref, pt_ref, x_ref, g2_ref, lg_ref, lb_ref, ys_ref,
                    o_ref, stage_ref, sems):
    j = pl.program_id(0)
    nt = pl.num_programs(0)
    buf = j % 2

    def in_copy(b):
        return lambda dst, src, rows: pltpu.make_async_copy(
            ys_ref.at[pl.ds(src, rows)], stage_ref.at[b].at[pl.ds(dst, rows)], sems.at[b])

    @pl.when(j == 0)
    def _():
        stage_ref[...] = jnp.zeros(stage_ref.shape, BF16)
        _group_copies(j, cntp_ref, loc_ref, goff_ref, in_copy(buf))

    @pl.when(j + 1 < nt)
    def _():
        _group_copies(j + 1, cntp_ref, loc_ref, goff_ref, in_copy(1 - buf))

    _wait_copies(nch_ref, j, in_copy(buf))
    y = _dot(pt_ref[0], stage_ref[buf])
    o_ref[0] = _ln(DEEPNORM_ALPHA * x_ref[0] + g2_ref[0] * y) * lg_ref[...] + lb_ref[...]


def _combine(pt, ys, x1, g2, lg, lb, cntp, loc, goff, nch):
    npb, l, d = x1.shape
    t = MOE_T
    nt_b = l // t
    tile = lambda w: pl.BlockSpec((1, t, w), lambda j, *_: (j // nt_b, j % nt_b, 0))
    const = lambda a: pl.BlockSpec(a.shape, lambda j, *_: (0,) * a.ndim)
    return pl.pallas_call(
        _combine_kernel,
        grid_spec=pltpu.PrefetchScalarGridSpec(
            num_scalar_prefetch=4,
            grid=(npb * nt_b,),
            in_specs=[pl.BlockSpec((1, t, MOE_SLOTS), lambda j, *_: (j, 0, 0)), tile(d),
                      pl.BlockSpec((1, 1, d), lambda j, *_: (j // nt_b, 0, 0)), const(lg), const(lb),
                      pl.BlockSpec(memory_space=pl.ANY)],
            out_specs=tile(d),
            scratch_shapes=[pltpu.VMEM((2, MOE_SLOTS, d), BF16), pltpu.SemaphoreType.DMA((2,))]),
        out_shape=jax.ShapeDtypeStruct((npb, l, d), F32),
        compiler_params=_cparams("arbitrary"),
        name="moe_combine",
    )(cntp, loc, goff, nch, pt, x1, g2, lg, lb, ys)


def _moe(x1, sh2, sc2, g2, rw, rb, w1, b1p, w2, b2, perm, lg, lb, npb, layer):
    _, l, d = x1.shape
    n_tok = npb * l
    nt = n_tok // MOE_T
    xaug, sel, cnt = _route(x1, sh2, sc2, rw, rb, npb)

    i32 = jnp.int32
    cnt = cnt[:, 0, :N_EXPERTS].astype(i32)
    cntp = (cnt + (MOE_G - 1)) // MOE_G * MOE_G
    loc = jnp.cumsum(cntp, axis=1) - cntp
    tot = cntp.sum(axis=0)
    seg = (tot + (MOE_TM - 1)) // MOE_TM * MOE_TM
    seg_start = jnp.cumsum(seg) - seg
    goff = seg_start[None, :] + jnp.cumsum(cntp, axis=0) - cntp
    s_max = TOP_K * n_tok + nt * N_EXPERTS * (MOE_G - 1) + N_EXPERTS * (MOE_TM - MOE_G)
    n_tiles = -(-s_max // MOE_TM)
    cum_tiles = jnp.cumsum(seg // MOE_TM)
    tile_e = jnp.minimum((jnp.arange(n_tiles, dtype=i32)[:, None] >= cum_tiles[None, :]).astype(i32).sum(axis=1),
                         N_EXPERTS - 1)
    n_active = cum_tiles[-1:].astype(i32)
    gran = cntp // MOE_G
    nch = jnp.stack([(gran // 2).sum(axis=1), (gran % 2).sum(axis=1)], axis=1).reshape(-1).astype(i32)
    locv = jnp.pad(loc.astype(F32), ((0, 0), (0, LANES - N_EXPERTS)))[:, None, :]
    flat = lambda a: a.reshape(-1).astype(i32)

    tail = jnp.concatenate([seg_start + tot, n_active * MOE_TM]).astype(i32)
    tailn = jnp.concatenate([(seg - tot) // MOE_G, n_tiles - n_active]).astype(i32)
    pt, xs = _dispatch(xaug, sel, locv, flat(cntp), flat(loc), flat(goff), nch, tail, tailn, n_tiles * MOE_TM)
    ys = _ffn(xs, tile_e, n_active, w1, b1p, w2, b2, perm, n_tiles, layer)
    return _combine(pt, ys, x1, g2, lg, lb, flat(cntp), flat(loc), flat(goff), nch)


def _deinterleave(n):
    return np.concatenate([np.arange(0, n, 2), np.arange(1, n, 2)])


def _rope_tables(l):
    rows = l // GRID_W
    r = np.repeat(np.arange(rows, dtype=np.float32), GRID_W)
    col = np.tile(np.arange(GRID_W, dtype=np.float32), rows)
    axis_dim = ATT_HEAD_DIM // 2
    inv_freq = (ROPE_THETA ** (-np.arange(0, axis_dim, 2, dtype=np.float32) / axis_dim)).astype(np.float32)
    ang = np.concatenate([r[:, None] * inv_freq, col[:, None] * inv_freq], axis=-1)
    c, s = np.cos(ang).astype(np.float32), np.sin(ang).astype(np.float32)
    z = np.zeros_like(s)
    rc = np.tile(np.concatenate([c, c], -1), (1, 2))
    rs1 = np.tile(np.concatenate([-s, z], -1), (1, 2))
    rs2 = np.tile(np.concatenate([z, s], -1), (1, 2))
    ident = (np.ones_like(rc), np.zeros_like(rc), np.zeros_like(rc))
    return tuple(jnp.asarray(np.stack([a, b])) for a, b in zip((rc, rs1, rs2), ident))


def kernel(x, c, ctx, c_ctx, w_mod, b_mod, w_in, att_q_norm, att_k_norm, gla_wdec_f, gla_bdec_f, gla_wdec_b, gla_bdec_b, gla_out_norm, hy_conv_w, hy_conv_b, hy_fw1, hy_fb1, hy_fw2, hy_fb2, hy_fw3, hy_fb3, hy_skip, w_branch_a, w_branch_b, w_branch_c, w_out, ln1_g, ln1_b, router_w, router_b, exp_w1, exp_b1, exp_w2, exp_b2, ln2_g, ln2_b):
    b, l, d = x.shape
    lc = ctx.shape[1]
    assert d == D_MODEL and (b * lc) % l == 0 and l % lc == 0 and lc % GLA_CHUNK == 0
    pc = (b * lc) // l
    p = b + pc

    perm64 = _deinterleave(ATT_HEAD_DIM)
    q_cols = np.concatenate([np.concatenate([64 * j + perm64, 64 * (j + 4) + perm64]) for j in range(4)])
    k_cols = ATT_Q_W + np.concatenate([perm64, 64 + perm64])
    v_cols = ATT_Q_W + ATT_KV_W + np.arange(ATT_KV_W)
    att_cols = np.concatenate([q_cols, k_cols, v_cols])
    o0 = ATT_Q_W + 2 * ATT_KV_W
    gla_cols = o0 + np.concatenate([np.arange(0, 2 * GLA_K_W + GLA_V_W + 2 * GLA_RANK)])
    og0 = o0 + 2 * GLA_K_W + GLA_V_W + 2 * GLA_RANK
    hy0 = og0 + GLA_V_W
    gt0 = hy0 + 3 * HY_WIDTH
    gain_perm = np.concatenate([perm64, perm64])
    bd = jnp.asarray(np.kron(np.eye(2), np.ones((64, 64))), dtype=BF16)
    rope = _rope_tables(l)
    fwd_l, inv_l = _dft_tables(l)
    fwd_c, inv_c = _dft_tables(lc)
    cmap = np.concatenate([np.arange(b), np.full(pc, b)])
    hh = EXPERT_HIDDEN // 2
    perm_np = np.zeros((2 * hh, 2 * hh), np.float32)
    perm_np[2 * np.arange(hh), np.arange(hh)] = 1.0
    perm_np[2 * np.arange(hh) + 1, hh + np.arange(hh)] = 1.0
    perm = jnp.asarray(perm_np, dtype=BF16)

    x_pair = (x, ctx.reshape(pc, l, d))
    cc = jnp.concatenate([c, c_ctx[None], jnp.zeros((7, d), F32)], axis=0)

    for i in range(DEPTH):
        with_ctx = i < DEPTH - 1
        npb = p if with_ctx else b
        mod = _modulation(cc, w_mod[i], b_mod[i])[cmap]
        sh1, sc1, g1, sh2, sc2, g2 = [m[:, None, :] for m in jnp.split(mod, 6, axis=-1)]

        wi = w_in[i]
        watt = wi[:, att_cols].astype(BF16)
        wgla = jnp.pad(wi[:, gla_cols], ((0, 0), (0, GLA_GROUP_W - gla_cols.size))).astype(BF16)
        wog = wi[:, og0:hy0].astype(BF16)
        why = wi[:, hy0:gt0].astype(BF16)
        wgt = wi[:, gt0:].astype(BF16)
        qg = att_q_norm[i][gain_perm].reshape(1, LANES)
        kg = att_k_norm[i][gain_perm].reshape(1, LANES)
        qn, kn, v, gla, og, hy = _inproj(*x_pair, sh1, sc1, (watt, wgla, wog, why), qg, kg, rope, bd, b)

        o_a = _attention(qn, kn, v, b, lc, with_ctx)

        wdec = jnp.zeros((LANES, 2 * GLA_K_W), F32)
        wdec = wdec.at[0:GLA_RANK, 0:GLA_K_W].set(gla_wdec_f[i])
        wdec = wdec.at[GLA_RANK:2 * GLA_RANK, GLA_K_W:].set(gla_wdec_b[i])
        bdec = jnp.concatenate([gla_bdec_f[i], gla_bdec_b[i]]).reshape(1, -1)
        o_b, o_b_ctx = _gla(gla, og, wdec, bdec, gla_out_norm[i].reshape(1, -1), b, lc)

        hre, him = _hy_filter(l, hy_fw1[i], hy_fb1[i], hy_fw2[i], hy_fb2[i], hy_fw3[i], hy_fb3[i], fwd_l)
        u_h, x0_h, t_h = _hy_pre(hy, hy_conv_w[i], hy_conv_b[i], hy_skip[i], b, l, b)
        y_h = _hy_conv(u_h, fwd_l, inv_l, hre, him)
        if with_ctx:
            hre_c, him_c = _hy_filter(lc, hy_fw1[i], hy_fb1[i], hy_fw2[i], hy_fb2[i], hy_fw3[i], hy_fb3[i], fwd_c)
            u_c, x0_c, t_c = _hy_pre(hy, hy_conv_w[i], hy_conv_b[i], hy_skip[i], b, lc, b)
            y_c = _hy_conv(u_c, fwd_c, inv_c, hre_c, him_c)
            fold = lambda a_ctx: a_ctx.reshape(pc, l, a_ctx.shape[-1])
            pairs = [(o_b, fold(o_b_ctx)), (y_h, fold(y_c)), (x0_h, fold(x0_c)), (t_h, fold(t_c))]
        else:
            pairs = [(a, a) for a in (o_b, y_h, x0_h, t_h)]

        wa_rows = np.concatenate([np.concatenate([64 * j + np.arange(64), 64 * (j + 4) + np.arange(64)])
                                  for j in range(4)])
        x1 = _merge(x_pair, o_a, *pairs, sh1, sc1, g1, wgt,
                    w_branch_a[i][wa_rows].astype(BF16), w_branch_b[i].astype(BF16),
                    w_branch_c[i].astype(BF16), w_out[i].astype(BF16),
                    ln1_g[i].reshape(1, d), ln1_b[i].reshape(1, d), npb)

        rw = jnp.pad(router_w[i], ((0, 0), (0, LANES - N_EXPERTS)))
        rb = jnp.concatenate([router_b[i], jnp.full((LANES - N_EXPERTS,), NEG_BIG, F32)]).reshape(1, LANES)
        b1p = jnp.concatenate([exp_b1[i][..., 0::2], exp_b1[i][..., 1::2]], axis=-1)[:, None, :]
        x_all = _moe(x1, sh2, sc2, g2, rw, rb, exp_w1, b1p, exp_w2, exp_b2[i][:, None, :], perm,
                     ln2_g[i].reshape(1, d), ln2_b[i].reshape(1, d), npb, i)
        x_pair = (x_all, x_all)
    return x_all
```

```python
import functools
import math

import numpy as np
import jax
import jax.numpy as jnp
from jax import lax
from jax.experimental import pallas as pl
from jax.experimental.pallas import tpu as pltpu

F32 = jnp.float32
BF16 = jnp.bfloat16

D_MODEL = 1024
DEPTH = 2
GRID_W = 64
ATT_HEADS = 8
ATT_KV_HEADS = 2
ATT_HEAD_DIM = 64
ROPE_THETA = 10000.0
GLA_HEADS = 4
GLA_DK = 64
GLA_DV = 128
GLA_RANK = 16
GLA_TAU = 16.0
GLA_CHUNK = 64
HY_WIDTH = 512
HY_POS_FREQS = 16
HY_DECAY_SLOW = math.log(1e-2) / 1.5
HY_DECAY_FAST = math.log(1e-2) / 0.3
N_EXPERTS = 32
TOP_K = 4
EXPERT_HIDDEN = D_MODEL
SWIGLU_LIMIT = 7.0
SWIGLU_ALPHA = 1.702
DEEPNORM_ALPHA = (2 * DEPTH) ** 0.25
LN_EPS = 1e-5
RMS_EPS = 1e-6

ATT_Q_W = ATT_HEADS * ATT_HEAD_DIM
ATT_KV_W = ATT_KV_HEADS * ATT_HEAD_DIM
GLA_K_W = GLA_HEADS * GLA_DK
GLA_V_W = GLA_HEADS * GLA_DV
GLA_GROUP_W = 1152
LANES = 128
NEG_BIG = -3.0e38

VMEM_LIMIT = 56 * 1024 * 1024


def _cparams(*sem):
    return pltpu.CompilerParams(dimension_semantics=sem, vmem_limit_bytes=VMEM_LIMIT)


def _dot(a, b):
    return jnp.dot(a, b, preferred_element_type=F32)


def _dot_nt(a, b):
    return lax.dot_general(a, b, (((1,), (1,)), ((), ())), preferred_element_type=F32)


def _dot_tn(a, b):
    return lax.dot_general(a, b, (((0,), (0,)), ((), ())), preferred_element_type=F32)


def _split2(x):
    hi = x.astype(BF16)
    lo = (x - hi.astype(F32)).astype(BF16)
    return hi, lo


def _split3(x):
    hi = x.astype(BF16)
    r = x - hi.astype(F32)
    mid = r.astype(BF16)
    lo = (r - mid.astype(F32)).astype(BF16)
    return hi, mid, lo


def _dot3(a, b):
    ah, al = _split2(a)
    bh, bl = _split2(b)
    return _dot(ah, bh) + (_dot(ah, bl) + _dot(al, bh))


def _dot_exact_lhs(a_bf16, b):
    b0, b1, b2 = _split3(b)
    return _dot(a_bf16, b0) + (_dot(a_bf16, b1) + _dot(a_bf16, b2))


def _ln(x):
    mu = jnp.mean(x, axis=-1, keepdims=True)
    xc = x - mu
    var = jnp.mean(xc * xc, axis=-1, keepdims=True)
    return xc * lax.rsqrt(var + LN_EPS)


def _sigmoid(x):
    return 1.0 / (1.0 + jnp.exp(-x))


def _mod_kernel(c_ref, w_ref, b_ref, o_ref):
    c = c_ref[...]
    o_ref[...] = _dot3(c * _sigmoid(c), w_ref[...]) + b_ref[...]


def _modulation(cc, w_mod, b_mod):
    rows, d = cc.shape
    n = w_mod.shape[1]
    tn = 1024
    return pl.pallas_call(
        _mod_kernel,
        grid=(n // tn,),
        in_specs=[pl.BlockSpec((rows, d), lambda j: (0, 0)),
                  pl.BlockSpec((d, tn), lambda j: (0, j)),
                  pl.BlockSpec((1, tn), lambda j: (0, j))],
        out_specs=pl.BlockSpec((rows, tn), lambda j: (0, j)),
        out_shape=jax.ShapeDtypeStruct((rows, n), F32),
        compiler_params=_cparams("arbitrary"),
        name="modulation",
    )(cc, w_mod, b_mod.reshape(1, n))


def _head_norm_rope(t, gain, rc, rs1, rs2, bd):
    sh, sl = _split2(t * t)
    ss = _dot(sh, bd) + _dot(sl, bd)
    tn = t * lax.rsqrt(ss * (1.0 / ATT_HEAD_DIM) + RMS_EPS) * gain
    return tn * rc + pltpu.roll(tn, 96, 1) * rs1 + pltpu.roll(tn, 32, 1) * rs2


def _pair_specs(tm, w, n_split, n_i):
    first = pl.BlockSpec((1, tm, w), lambda b, i: (jnp.minimum(b, n_split - 1), jnp.where(b < n_split, i, n_i - 1), 0))
    rest = pl.BlockSpec((1, tm, w), lambda b, i: (jnp.maximum(b - n_split, 0), jnp.where(b >= n_split, i, 0), 0))
    return [first, rest]


def _pick(use_rest, first_ref, rest_ref):
    return jnp.where(use_rest, rest_ref[0], first_ref[0])


def _inproj_kernel(x_ref, xr_ref, sh_ref, sc_ref, watt_ref, wgla_ref, wog_ref, why_ref,
                   qg_ref, kg_ref, rc_ref, rs1_ref, rs2_ref, bd_ref,
                   qn_ref, kn_ref, v_ref, gla_ref, og_ref, hy_ref, *, n_split):
    x = _pick(pl.program_id(0) >= n_split, x_ref, xr_ref)
    u = _ln(x) * (1.0 + sc_ref[0]) + sh_ref[0]
    ub = u.astype(BF16)
    rc, rs1, rs2, bd = rc_ref[0], rs1_ref[0], rs2_ref[0], bd_ref[...]
    att = _dot(ub, watt_ref[...])
    for j in range(ATT_Q_W // LANES):
        t = att[:, LANES * j:LANES * (j + 1)]
        qn = _head_norm_rope(t, qg_ref[...], rc, rs1, rs2, bd) * (ATT_HEAD_DIM ** -0.5)
        qn_ref[0, :, LANES * j:LANES * (j + 1)] = qn.astype(BF16)
    kn = _head_norm_rope(att[:, ATT_Q_W:ATT_Q_W + LANES], kg_ref[...], rc, rs1, rs2, bd)
    kn_ref[0] = kn.astype(BF16)
    v_ref[0, :, 0:LANES] = att[:, ATT_Q_W + LANES:].astype(BF16)
    v_ref[0, :, LANES:2 * LANES] = jnp.ones((att.shape[0], LANES), BF16)
    gla_ref[0] = _dot(ub, wgla_ref[...])
    og_ref[0] = _dot(ub, wog_ref[...]).astype(BF16)
    hy_ref[0] = _dot(ub, why_ref[...]).astype(BF16)


def _inproj(x_first, x_rest, sh, sc, wts, qg, kg, rope, bd, n_lat):
    n_split, l, d = x_first.shape
    p = sh.shape[0]
    tm = min(256, l)
    watt, wgla, wog, why = wts
    rc, rs1, rs2 = rope
    const = lambda a: pl.BlockSpec(a.shape, lambda b, i: (0,) * a.ndim)
    row = lambda w: pl.BlockSpec((1, tm, w), lambda b, i: (b, i, 0))
    tab = pl.BlockSpec((1, tm, LANES), lambda b, i: (jnp.where(b >= n_lat, 1, 0), i, 0))
    vec = pl.BlockSpec((1, 1, d), lambda b, i: (b, 0, 0))
    widths = (ATT_Q_W, LANES, 2 * LANES, GLA_GROUP_W, GLA_V_W, 3 * HY_WIDTH)
    dtypes = (BF16, BF16, BF16, F32, BF16, BF16)
    return pl.pallas_call(
        functools.partial(_inproj_kernel, n_split=n_split),
        grid=(p, l // tm),
        in_specs=_pair_specs(tm, d, n_split, l // tm) + [
            vec, vec, const(watt), const(wgla), const(wog), const(why),
            const(qg), const(kg), tab, tab, tab, const(bd)],
        out_specs=[row(w) for w in widths],
        out_shape=[jax.ShapeDtypeStruct((p, l, w), dt) for w, dt in zip(widths, dtypes)],
        compiler_params=_cparams("parallel", "arbitrary"),
        name="inproj",
    )(x_first, x_rest, sh, sc, watt, wgla, wog, why, qg, kg, rc, rs1, rs2, bd)


def _attend_group(qg, ks, vs, lo):
    zero = jnp.zeros_like(qg)
    outs = []
    for qh in (jnp.where(lo, qg, zero), jnp.where(lo, zero, qg)):
        ss = [_dot_nt(qh, k) for k in ks]
        m = ss[0].max(axis=-1, keepdims=True)
        for s in ss[1:]:
            m = jnp.maximum(m, s.max(axis=-1, keepdims=True))
        acc = 0.0
        for s, v in zip(ss, vs):
            acc = acc + _dot(jnp.exp((s - m).astype(BF16)), v)
        outs.append(acc[:, 0:LANES] / acc[:, LANES:2 * LANES])
    return jnp.where(lo, outs[0], outs[1])


def _attn_kernel(q_ref, k_ref, v_ref, kc_ref, vc_ref, o_ref, *, n_lat_tiles, with_ctx):
    i = pl.program_id(1)
    tq = q_ref.shape[1]
    lo = lax.broadcasted_iota(jnp.int32, (tq, LANES), 1) < ATT_HEAD_DIM

    def run(ks, vs):
        for j in range(ATT_Q_W // LANES):
            o = _attend_group(q_ref[0, :, LANES * j:LANES * (j + 1)], ks, vs, lo)
            o_ref[0, :, LANES * j:LANES * (j + 1)] = o.astype(BF16)

    if with_ctx:
        @pl.when(i < n_lat_tiles)
        def _():
            run([k_ref[0], kc_ref[0]], [v_ref[0], vc_ref[0]])

        @pl.when(i == n_lat_tiles)
        def _():
            run([kc_ref[0]], [vc_ref[0]])
    else:
        run([k_ref[0], kc_ref[0]], [v_ref[0], vc_ref[0]])


def _attention(qn, kn, v, n_lat, lc, with_ctx):
    p, l, _ = qn.shape
    tq = lc
    r = l // lc
    nt = l // tq
    ctx_idx = lambda b: (n_lat + b // r, b % r, 0)
    if with_ctx:
        qmap = lambda b, i: (jnp.where(i < nt, b, n_lat + b // r), jnp.where(i < nt, i, b % r), 0)
    else:
        qmap = lambda b, i: (b, i, 0)
    return pl.pallas_call(
        functools.partial(_attn_kernel, n_lat_tiles=nt, with_ctx=with_ctx),
        grid=(n_lat, nt + (1 if with_ctx else 0)),
        in_specs=[pl.BlockSpec((1, tq, ATT_Q_W), qmap),
                  pl.BlockSpec((1, l, LANES), lambda b, i: (b, 0, 0)),
                  pl.BlockSpec((1, l, 2 * LANES), lambda b, i: (b, 0, 0)),
                  pl.BlockSpec((1, lc, LANES), lambda b, i: ctx_idx(b)),
                  pl.BlockSpec((1, lc, 2 * LANES), lambda b, i: ctx_idx(b))],
        out_specs=pl.BlockSpec((1, tq, ATT_Q_W), qmap),
        out_shape=jax.ShapeDtypeStruct((p if with_ctx else n_lat, l, ATT_Q_W), BF16),
        compiler_params=_cparams("parallel", "arbitrary"),
        name="attention",
    )(qn, kn, v, kn, v)


def _gla_segment(src_ref, og_ref, out_ref, acc_ref, st_ref, qd_ref, ke_ref, vb_ref, dec_ref,
                 wdec_ref, bdec_ref, gn_ref, ls):
    c = GLA_CHUNK
    nc = ls // c
    rb = min(256, ls)

    bi = lax.broadcasted_iota(jnp.int32, (rb, rb), 0)
    bj = lax.broadcasted_iota(jnp.int32, (rb, rb), 1)
    same = (bi // c) == (bj // c)
    att_masks = (same & (bi >= bj), same & (bi <= bj))
    tri_blk = tuple(jnp.where(m, 1.0, 0.0).astype(BF16) for m in att_masks)
    ones_blk = jnp.where(same, 1.0, 0.0).astype(BF16)
    lo_b = lax.broadcasted_iota(jnp.int32, (rb, LANES), 1) < GLA_DK

    def prep(t, carry):
        r0 = pl.multiple_of(t * rb, rb)
        q = src_ref[0, pl.ds(r0, rb), 0:GLA_K_W] * (GLA_DK ** -0.5)
        k = src_ref[0, pl.ds(r0, rb), GLA_K_W:2 * GLA_K_W]
        vb = src_ref[0, pl.ds(r0, rb), 2 * GLA_K_W:2 * GLA_K_W + GLA_V_W].astype(BF16)
        vb_ref[pl.ds(r0, rb), :] = vb
        lr = src_ref[0, pl.ds(r0, rb), 2 * GLA_K_W + GLA_V_W:GLA_GROUP_W]
        la = jax.nn.log_sigmoid(_dot3(lr, wdec_ref[...]) + bdec_ref[...]) * (1.0 / GLA_TAU)
        for d in range(2):
            l0, l1, l2 = _split3(la[:, d * GLA_K_W:(d + 1) * GLA_K_W])
            cum = _dot(tri_blk[d], l0) + (_dot(tri_blk[d], l1) + _dot(tri_blk[d], l2))
            tot = _dot(ones_blk, l0) + (_dot(ones_blk, l1) + _dot(ones_blk, l2))
            qd = (q * jnp.exp(cum)).astype(BF16)
            ki = (k * jnp.exp(-cum)).astype(BF16)
            qd_ref[d, pl.ds(r0, rb), :] = qd
            ke_ref[d, pl.ds(r0, rb), :] = (k * jnp.exp(tot - cum)).astype(BF16)
            dec_ref[d, pl.ds(r0, rb), :] = jnp.exp(tot)
            for h in range(GLA_HEADS):
                g, half = h // 2, h % 2
                sl = slice(LANES * g, LANES * (g + 1))
                zero = jnp.zeros_like(qd[:, sl])
                qh = jnp.where(lo_b, qd[:, sl], zero) if half == 0 else jnp.where(lo_b, zero, qd[:, sl])
                att = jnp.where(att_masks[d], _dot_nt(qh, ki[:, sl]), 0.0)
                o = _dot(att.astype(BF16), vb[:, GLA_DV * h:GLA_DV * (h + 1)])
                if d == 0:
                    acc_ref[pl.ds(r0, rb), GLA_DV * h:GLA_DV * (h + 1)] = o
                else:
                    acc_ref[pl.ds(r0, rb), GLA_DV * h:GLA_DV * (h + 1)] += o
        return carry

    lax.fori_loop(0, ls // rb, prep, 0, unroll=2 if (ls // rb) % 2 == 0 else 1)

    lo = lax.broadcasted_iota(jnp.int32, (c, LANES), 1) < GLA_DK

    def body(n, carry):
        for d, cn in ((0, n), (1, nc - 1 - n)):
            r0 = pl.multiple_of(cn * c, c)
            dec = dec_ref[d, pl.ds(r0, 1), :]
            for h in range(GLA_HEADS):
                g, half = h // 2, h % 2
                sl = slice(LANES * g, LANES * (g + 1))
                qd = qd_ref[d, pl.ds(r0, c), sl]
                zero = jnp.zeros_like(qd)
                qd = jnp.where(lo, qd, zero) if half == 0 else jnp.where(lo, zero, qd)
                vh = vb_ref[pl.ds(r0, c), GLA_DV * h:GLA_DV * (h + 1)]
                st = st_ref[d * GLA_HEADS + h]
                acc_ref[pl.ds(r0, c), GLA_DV * h:GLA_DV * (h + 1)] += _dot_nt(qd, st.astype(BF16))
                st_ref[d * GLA_HEADS + h] = dec[:, sl] * st + _dot_tn(vh, ke_ref[d, pl.ds(r0, c), sl])
        return carry

    lax.fori_loop(0, nc, body, 0, unroll=4 if nc % 4 == 0 else (2 if nc % 2 == 0 else 1))

    tr = min(256, ls)

    def fin(t, carry):
        r0 = pl.multiple_of(t * tr, tr)
        for h in range(GLA_HEADS):
            sl = slice(GLA_DV * h, GLA_DV * (h + 1))
            o = acc_ref[pl.ds(r0, tr), sl]
            y = o * lax.rsqrt(jnp.mean(o * o, axis=-1, keepdims=True) + RMS_EPS)
            y = y * gn_ref[...]
            og = og_ref[0, pl.ds(r0, tr), sl].astype(F32)
            out_ref[0, pl.ds(r0, tr), sl] = (y * (og * _sigmoid(og))).astype(BF16)
        return carry

    lax.fori_loop(0, ls // tr, fin, 0)


def _gla_kernel(gl_ref, glc_ref, og_ref, ogc_ref, wdec_ref, bdec_ref, gn_ref, o_ref, oc_ref,
                acc_ref, st_ref, qd_ref, ke_ref, vb_ref, dec_ref):
    st_ref[...] = jnp.zeros(st_ref.shape, F32)
    scratch = (acc_ref, st_ref, qd_ref, ke_ref, vb_ref, dec_ref)
    _gla_segment(glc_ref, ogc_ref, oc_ref, *scratch, wdec_ref, bdec_ref, gn_ref, glc_ref.shape[1])
    _gla_segment(gl_ref, og_ref, o_ref, *scratch, wdec_ref, bdec_ref, gn_ref, gl_ref.shape[1])


def _gla(gla, og, wdec, bdec, gn, n_lat, lc):
    p, l, _ = gla.shape
    r = l // lc
    ctx_idx = lambda b: (n_lat + b // r, b % r, 0)
    const = lambda a: pl.BlockSpec(a.shape, lambda b: (0,) * a.ndim)
    return pl.pallas_call(
        _gla_kernel,
        grid=(n_lat,),
        in_specs=[pl.BlockSpec((1, l, GLA_GROUP_W), lambda b: (b, 0, 0)),
                  pl.BlockSpec((1, lc, GLA_GROUP_W), ctx_idx),
                  pl.BlockSpec((1, l, GLA_V_W), lambda b: (b, 0, 0)),
                  pl.BlockSpec((1, lc, GLA_V_W), ctx_idx),
                  const(wdec), const(bdec), const(gn)],
        out_specs=[pl.BlockSpec((1, l, GLA_V_W), lambda b: (b, 0, 0)),
                   pl.BlockSpec((1, lc, GLA_V_W), lambda b: (b, 0, 0))],
        out_shape=[jax.ShapeDtypeStruct((n_lat, l, GLA_V_W), BF16),
                   jax.ShapeDtypeStruct((n_lat, lc, GLA_V_W), BF16)],
        scratch_shapes=[pltpu.VMEM((l, GLA_V_W), F32),
                        pltpu.VMEM((2 * GLA_HEADS, GLA_DV, LANES), F32),
                        pltpu.VMEM((2, l, GLA_K_W), BF16),
                        pltpu.VMEM((2, l, GLA_K_W), BF16), pltpu.VMEM((l, GLA_V_W), BF16),
                        pltpu.VMEM((2, l, GLA_K_W), F32)],
        compiler_params=_cparams("arbitrary"),
        name="gla",
    )(gla, gla, og, og, wdec, bdec, gn)


HY_FREQ_TILE = 512


def _dft_tables(l):
    n = 2 * l
    k = np.arange(l, dtype=np.int64)[:, None]
    t = np.arange(l, dtype=np.int64)[None, :]
    ang = 2.0 * np.pi * ((k * t) % n).astype(np.float64) / n
    fre = np.cos(ang)
    fim = -np.sin(ang)
    fim[0, :] = np.where(np.arange(l) % 2 == 0, 1.0, -1.0)
    tk = min(HY_FREQ_TILE, l)
    nk = l // tk
    fwd = np.concatenate([fre.reshape(nk, tk, l), fim.reshape(nk, tk, l)], axis=1)
    return jnp.asarray(fwd, dtype=BF16), jnp.asarray(fwd.transpose(0, 2, 1), dtype=BF16)


def _hy_filter_kernel(z_ref, win_ref, w1_ref, b1_ref, w2_ref, b2_ref, w3_ref, b3_ref, f_ref,
                      hre_ref, him_ref, hf_ref):
    j = pl.program_id(0)
    l = z_ref.shape[0]

    @pl.when(j == 0)
    def _():
        h = jnp.sin(_dot3(z_ref[...], w1_ref[...]) + b1_ref[...])
        h = jnp.sin(_dot3(h, w2_ref[...]) + b2_ref[...])
        h = _dot3(h, w3_ref[...]) + b3_ref[...]
        win = win_ref[...]
        hf = h[:, :HY_WIDTH] * win
        hb = h[:, HY_WIDTH:] * win
        row = lax.broadcasted_iota(jnp.int32, (l, HY_WIDTH), 0)
        hf_ref[:, :HY_WIDTH] = hf.astype(BF16)
        hf_ref[:, HY_WIDTH:] = jnp.where(row == 0, 0.0, hb).astype(BF16)

    tk = f_ref.shape[1] // 2
    ab = _dot(f_ref[0], hf_ref[...])
    a, b = ab[:tk], ab[tk:]
    first = (lax.broadcasted_iota(jnp.int32, (tk, HY_WIDTH), 0) == 0) & (j == 0)
    scl = jnp.where(first, 1.0 / (2 * l), 2.0 / (2 * l))
    hre_ref[...] = (a[:, :HY_WIDTH] + a[:, HY_WIDTH:]) * scl
    him_ref[...] = (b[:, :HY_WIDTH] + jnp.where(first, 1.0, -1.0) * b[:, HY_WIDTH:]) * scl


def _hy_filter(l, fw1, fb1, fw2, fb2, fw3, fb3, fwd):
    t = np.arange(l, dtype=np.float32)[:, None] / np.float32(l)
    freqs = np.arange(1, HY_POS_FREQS + 1, dtype=np.float32)
    z = np.concatenate([t, np.cos(2.0 * math.pi * freqs * t), np.sin(2.0 * math.pi * freqs * t)], axis=-1)
    z = np.pad(z, ((0, 0), (0, LANES - z.shape[1])))
    hid = fw2.shape[0]
    fw1 = jnp.pad(fw1, ((0, LANES - fw1.shape[0]), (0, LANES - hid)))
    fb1 = jnp.pad(fb1, (0, LANES - hid))
    fw2 = jnp.pad(fw2, ((0, LANES - hid), (0, LANES - hid)))
    fb2 = jnp.pad(fb2, (0, LANES - hid))
    fw3 = jnp.pad(fw3, ((0, LANES - hid), (0, 0)))
    deltas = np.abs(np.linspace(HY_DECAY_SLOW, HY_DECAY_FAST, HY_WIDTH, dtype=np.float32))
    win = np.exp(-t * deltas).astype(np.float32)
    nk, tk2, _ = fwd.shape
    tk = tk2 // 2
    const = lambda a: pl.BlockSpec(a.shape, lambda j: (0,) * a.ndim)
    args = (jnp.asarray(z), jnp.asarray(win), fw1, fb1.reshape(1, -1), fw2, fb2.reshape(1, -1),
            fw3, fb3.reshape(1, -1))
    return pl.pallas_call(
        _hy_filter_kernel,
        grid=(nk,),
        in_specs=[const(a) for a in args] + [pl.BlockSpec((1, tk2, l), lambda j: (j, 0, 0))],
        out_specs=[pl.BlockSpec((tk, HY_WIDTH), lambda j: (j, 0))] * 2,
        out_shape=[jax.ShapeDtypeStruct((l, HY_WIDTH), F32)] * 2,
        scratch_shapes=[pltpu.VMEM((l, 2 * HY_WIDTH), BF16)],
        compiler_params=_cparams("arbitrary"),
        name="hyena_filter",
    )(*args, fwd)


def _hy_pre_kernel(p0_ref, p1_ref, p2_ref, w0_ref, w1_ref, w2_ref, b0_ref, b1_ref, b2_ref, skip_ref,
                   u_ref, x0_ref, t_ref):
    ls, cw = p0_ref.shape[1], p0_ref.shape[2]
    row = lax.broadcasted_iota(jnp.int32, (ls, cw), 0)

    def conv(p_ref, w_ref, b_ref):
        p = p_ref[0].astype(F32)
        prev = jnp.where(row == 0, 0.0, pltpu.roll(p, 1, 0))
        nxt = jnp.where(row == ls - 1, 0.0, pltpu.roll(p, ls - 1, 0))
        return prev * w_ref[0:1, :] + p * w_ref[1:2, :] + nxt * w_ref[2:3, :] + b_ref[...]

    x0 = conv(p0_ref, w0_ref, b0_ref)
    u = conv(p1_ref, w1_ref, b1_ref) * conv(p2_ref, w2_ref, b2_ref)
    u_ref[0] = u.astype(BF16)
    x0_ref[0] = x0.astype(BF16)
    t_ref[0] = (x0 * u * skip_ref[...]).astype(BF16)


def _hy_pre(hy, conv_w, conv_b, skip, nseq, ls, n_lat):
    p, l, _ = hy.shape
    r = l // ls
    cw = 2 * LANES
    nj = HY_WIDTH // cw
    base = 0 if ls == l else n_lat
    src = lambda off: pl.BlockSpec((1, ls, cw), lambda b, j: (base + b // r, b % r, off * nj + j))
    wsp = lambda off: pl.BlockSpec((3, cw), lambda b, j: (0, off * nj + j))
    bsp = lambda off: pl.BlockSpec((1, cw), lambda b, j: (0, off * nj + j))
    out = pl.BlockSpec((1, ls, cw), lambda b, j: (b, 0, j))
    cb = conv_b.reshape(1, -1)
    return pl.pallas_call(
        _hy_pre_kernel,
        grid=(nseq, nj),
        in_specs=[src(0), src(1), src(2), wsp(0), wsp(1), wsp(2), bsp(0), bsp(1), bsp(2), bsp(0)],
        out_specs=[out] * 3,
        out_shape=[jax.ShapeDtypeStruct((nseq, ls, HY_WIDTH), BF16)] * 3,
        compiler_params=_cparams("parallel", "arbitrary"),
        name="hyena_pre",
    )(hy, hy, hy, conv_w, conv_w, conv_w, cb, cb, cb, skip.reshape(1, -1))


def _hy_conv_kernel(u_ref, f_ref, i_ref, hre_ref, him_ref, y_ref):
    j = pl.program_id(1)
    tk = hre_ref.shape[0]
    uu = _dot(f_ref[0], u_ref[0])
    ure, uim = uu[:tk], uu[tk:]
    hre, him = hre_ref[...], him_ref[...]
    first = (lax.broadcasted_iota(jnp.int32, ure.shape, 0) == 0) & (j == 0)
    yre = jnp.where(first, ure * hre, ure * hre - uim * him)
    yim = jnp.where(first, uim * him, ure * him + uim * hre)
    y = _dot(i_ref[0], jnp.concatenate([yre.astype(BF16), yim.astype(BF16)], axis=0))

    @pl.when(j == 0)
    def _():
        y_ref[0] = y

    @pl.when(j > 0)
    def _():
        y_ref[0] += y


def _hy_conv(u, fwd, inv, hre, him):
    nseq, ls, _ = u.shape
    nk, tk2, _ = fwd.shape
    tk = tk2 // 2
    return pl.pallas_call(
        _hy_conv_kernel,
        grid=(nseq, nk),
        in_specs=[pl.BlockSpec((1, ls, HY_WIDTH), lambda b, j: (b, 0, 0)),
                  pl.BlockSpec((1, tk2, ls), lambda b, j: (j, 0, 0)),
                  pl.BlockSpec((1, ls, tk2), lambda b, j: (j, 0, 0)),
                  pl.BlockSpec((tk, HY_WIDTH), lambda b, j: (j, 0)),
                  pl.BlockSpec((tk, HY_WIDTH), lambda b, j: (j, 0))],
        out_specs=pl.BlockSpec((1, ls, HY_WIDTH), lambda b, j: (b, 0, 0)),
        out_shape=jax.ShapeDtypeStruct((nseq, ls, HY_WIDTH), F32),
        compiler_params=_cparams("parallel", "arbitrary"),
        name="hyena_conv",
    )(u, fwd, inv, hre, him)


def _merge_kernel(x_ref, xr_ref, oa_ref, ob_ref, obr_ref, y_ref, yr_ref, x0_ref, x0r_ref, t_ref, tr_ref,
                  sh_ref, sc_ref, g1_ref, wgt_ref, wa_ref, wb_ref, wc_ref, wo_ref, lg_ref, lb_ref, o_ref,
                  *, n_split):
    d = x_ref.shape[2]
    rest = pl.program_id(0) >= n_split
    x = _pick(rest, x_ref, xr_ref)
    u = _ln(x) * (1.0 + sc_ref[0]) + sh_ref[0]
    ob = _pick(rest, ob_ref, obr_ref)
    oc = (_pick(rest, x0_ref, x0r_ref).astype(F32) * _pick(rest, y_ref, yr_ref)
          + _pick(rest, t_ref, tr_ref).astype(F32)).astype(BF16)
    gt = _dot(u.astype(BF16), wgt_ref[...])
    m = (_sigmoid(gt[:, 0:d]) * _dot(oa_ref[0], wa_ref[...])
         + _sigmoid(gt[:, d:2 * d]) * _dot(ob, wb_ref[...])
         + _sigmoid(gt[:, 2 * d:3 * d]) * _dot(oc, wc_ref[...]))
    mix = _dot(m.astype(BF16), wo_ref[...])
    o_ref[0] = _ln(DEEPNORM_ALPHA * x + g1_ref[0] * mix) * lg_ref[...] + lb_ref[...]


def _merge(x, o_a, o_b, y, x0, t, sh, sc, g1, wgt, wa, wb, wc, wo, lg, lb, npb):
    n_split, l, d = x[0].shape
    tm = min(512, l)
    n_i = l // tm
    row = lambda w: pl.BlockSpec((1, tm, w), lambda b, i: (b, i, 0))
    const = lambda a: pl.BlockSpec(a.shape, lambda b, i: (0,) * a.ndim)
    pair = lambda w: _pair_specs(tm, w, n_split, n_i)
    vec = pl.BlockSpec((1, 1, d), lambda b, i: (b, 0, 0))
    wgt_spec = pl.BlockSpec(wgt.shape, lambda b, i: (0, 0), pipeline_mode=pl.Buffered(1))
    return pl.pallas_call(
        functools.partial(_merge_kernel, n_split=n_split),
        grid=(npb, n_i),
        in_specs=pair(d) + [row(ATT_Q_W)] + pair(GLA_V_W) + pair(HY_WIDTH) + pair(HY_WIDTH) + pair(HY_WIDTH) + [
            vec, vec, vec, wgt_spec,
            const(wa), const(wb), const(wc), const(wo), const(lg), const(lb)],
        out_specs=row(d),
        out_shape=jax.ShapeDtypeStruct((npb, l, d), F32),
        compiler_params=_cparams("parallel", "arbitrary"),
        name="merge",
    )(*x, o_a, *o_b, *y, *x0, *t, sh, sc, g1, wgt, wa, wb, wc, wo, lg, lb)


MOE_T = 256
MOE_G = 16
MOE_TM = 1024
MOE_SLOTS = TOP_K * MOE_T + 512
AUG_W = D_MODEL + LANES


def _route_kernel(x_ref, sh_ref, sc_ref, rw_ref, rb_ref, xa_ref, sel_ref, cnt_ref):
    d = x_ref.shape[2]
    for h in range(x_ref.shape[1] // MOE_T):
        rows = slice(h * MOE_T, (h + 1) * MOE_T)
        u = _ln(x_ref[0, rows, :]) * (1.0 + sc_ref[0]) + sh_ref[0]
        xa_ref[0, rows, 0:d] = u.astype(BF16)
        logits = _dot3(u, rw_ref[...]) + rb_ref[...]
        lane = lax.broadcasted_iota(jnp.int32, logits.shape, 1).astype(F32)
        work = logits
        hots, vals = [], []
        for _ in range(TOP_K):
            m = work.max(axis=-1, keepdims=True)
            idx = jnp.where(work == m, lane, float(LANES)).min(axis=-1, keepdims=True)
            hot = lane == idx
            hots.append(hot)
            vals.append(m)
            work = jnp.where(hot, NEG_BIG, work)
        es = [jnp.exp(v - vals[0]) for v in vals]
        den = es[0] + es[1] + es[2] + es[3]
        comb = jnp.zeros_like(logits)
        sel = jnp.zeros_like(logits)
        for hot, e in zip(hots, es):
            comb = comb + jnp.where(hot, e / den, 0.0)
            sel = sel + jnp.where(hot, 1.0, 0.0)
        c0, c1, c2 = _split3(comb)
        aug = c0.astype(F32) + pltpu.roll(c1.astype(F32), 32, 1) + pltpu.roll(c2.astype(F32), 64, 1)
        xa_ref[0, rows, d:d + LANES] = aug.astype(BF16)
        sel_ref[0, rows, :] = sel.astype(BF16)
        cnt_ref[h] = sel.sum(axis=0, keepdims=True)


def _route(x_all, sh, sc, rw, rb, npb):
    _, l, d = x_all.shape
    per = next(n for n in (4, 2, 1) if l % (n * MOE_T) == 0)
    t = per * MOE_T
    n_i = l // t
    row = lambda w: pl.BlockSpec((1, t, w), lambda b, i: (b, i, 0))
    vec = pl.BlockSpec((1, 1, d), lambda b, i: (b, 0, 0))
    const = lambda a: pl.BlockSpec(a.shape, lambda b, i: (0,) * a.ndim)
    return pl.pallas_call(
        _route_kernel,
        grid=(npb, n_i),
        in_specs=[row(d), vec, vec, const(rw), const(rb)],
        out_specs=[row(AUG_W), row(LANES), pl.BlockSpec((per, 1, LANES), lambda b, i: (b * n_i + i, 0, 0))],
        out_shape=[jax.ShapeDtypeStruct((npb, l, AUG_W), BF16), jax.ShapeDtypeStruct((npb, l, LANES), BF16),
                   jax.ShapeDtypeStruct((npb * n_i * per, 1, LANES), F32)],
        compiler_params=_cparams("parallel", "arbitrary"),
        name="route",
    )(x_all, sh, sc, rw, rb)


def _group_copies(j, cntp_ref, loc_ref, goff_ref, make_copy):
    for e in range(N_EXPERTS):
        n = lax.shift_right_logical(cntp_ref[j * N_EXPERTS + e], int(math.log2(MOE_G)))
        loc = loc_ref[j * N_EXPERTS + e]
        off = goff_ref[j * N_EXPERTS + e]
        n2 = lax.shift_right_logical(n, 1)

        def per_pair(c, carry, loc=loc, off=off):
            make_copy(pl.multiple_of(loc + c * (2 * MOE_G), MOE_G),
                      pl.multiple_of(off + c * (2 * MOE_G), MOE_G), 2 * MOE_G).start()
            return carry

        lax.fori_loop(0, n2, per_pair, 0)

        @pl.when((n & 1) == 1)
        def _(loc=loc, off=off, n2=n2):
            make_copy(pl.multiple_of(loc + n2 * (2 * MOE_G), MOE_G),
                      pl.multiple_of(off + n2 * (2 * MOE_G), MOE_G), MOE_G).start()


def _wait_copies(nch_ref, j, make_copy):
    for k, rows in ((0, 2 * MOE_G), (1, MOE_G)):
        def wait_one(c, carry, rows=rows):
            make_copy(0, 0, rows).wait()
            return carry

        lax.fori_loop(0, nch_ref[2 * j + k], wait_one, 0)


def _dispatch_kernel(cntp_ref, loc_ref, goff_ref, nch_ref, tail_ref, tailn_ref,
                     xa_ref, sel_ref, locv_ref, pt_ref, xs_ref, stage_ref, zero_ref, sems):
    j = pl.program_id(0)
    nt = pl.num_programs(0)
    buf = j % 2
    sem = sems.at[buf]
    stage = stage_ref.at[buf]
    t = MOE_T

    def out_copy(src, dst, rows):
        return pltpu.make_async_copy(stage.at[pl.ds(src, rows)], xs_ref.at[pl.ds(dst, rows)], sem)

    @pl.when(j >= 2)
    def _():
        _wait_copies(nch_ref, jnp.maximum(j - 2, 0), out_copy)

    sel = sel_ref[0]
    ri = lax.broadcasted_iota(jnp.int32, (t, t), 0)
    ci = lax.broadcasted_iota(jnp.int32, (t, t), 1)
    rank = _dot(jnp.where(ci < ri, 1.0, 0.0).astype(BF16), sel)
    dest = rank + locv_ref[0]
    lane = lax.broadcasted_iota(jnp.int32, (t, LANES), 1).astype(F32)
    slot = lax.broadcasted_iota(jnp.int32, (t, MOE_SLOTS), 1)
    avail = sel.astype(F32)
    pt = jnp.zeros((t, MOE_SLOTS), F32)
    for _ in range(TOP_K):
        ek = jnp.where(avail > 0.0, lane, float(LANES)).min(axis=-1, keepdims=True)
        hot = lane == ek
        dk = jnp.where(hot, dest, 0.0).sum(axis=-1, keepdims=True).astype(jnp.int32)
        pt = pt + jnp.where(slot == dk, 1.0, 0.0)
        avail = jnp.where(hot, 0.0, avail)
    ptb = pt.astype(BF16)
    pt_ref[0] = ptb
    stage[...] = _dot_tn(ptb, xa_ref[0]).astype(BF16)
    _group_copies(j, cntp_ref, loc_ref, goff_ref, out_copy)

    @pl.when(j == nt - 1)
    def _():
        _wait_copies(nch_ref, j, out_copy)

        @pl.when(j >= 1)
        def _():
            other = sems.at[1 - buf]
            _wait_copies(nch_ref, jnp.maximum(j - 1, 0), lambda s, d, rows: pltpu.make_async_copy(
                stage_ref.at[1 - buf].at[pl.ds(s, rows)], xs_ref.at[pl.ds(d, rows)], other))

        zero_ref[...] = jnp.zeros(zero_ref.shape, BF16)

        def zcopy(dst):
            return pltpu.make_async_copy(zero_ref.at[pl.ds(0, MOE_G)], xs_ref.at[pl.ds(dst, MOE_G)], sem)

        def zcopy_big(dst):
            return pltpu.make_async_copy(zero_ref, xs_ref.at[pl.ds(dst, MOE_TM)], sem)

        def per_expert(e, tot):
            n = tailn_ref[e]
            base = tail_ref[e]

            def per_chunk(c, carry):
                zcopy(pl.multiple_of(base + c * MOE_G, MOE_G)).start()
                return carry

            lax.fori_loop(0, n, per_chunk, 0)
            return tot + n

        ztotal = lax.fori_loop(0, N_EXPERTS, per_expert, 0)

        def zwait(c, carry):
            zcopy(0).wait()
            return carry

        lax.fori_loop(0, ztotal, zwait, 0)

        nbig = tailn_ref[N_EXPERTS]
        big0 = tail_ref[N_EXPERTS]

        def big_start(c, carry):
            zcopy_big(pl.multiple_of(big0 + c * MOE_TM, MOE_TM)).start()
            return carry

        def big_wait(c, carry):
            zcopy_big(0).wait()
            return carry

        lax.fori_loop(0, nbig, big_start, 0)
        lax.fori_loop(0, nbig, big_wait, 0)


def _dispatch(xaug, sel, locv, cntp, loc, goff, nch, tail, tailn, s_rows):
    npb, l, _ = xaug.shape
    t = MOE_T
    nt_b = l // t
    nt = npb * nt_b
    tile = lambda w: pl.BlockSpec((1, t, w), lambda j, *_: (j // nt_b, j % nt_b, 0))
    return pl.pallas_call(
        _dispatch_kernel,
        grid_spec=pltpu.PrefetchScalarGridSpec(
            num_scalar_prefetch=6,
            grid=(nt,),
            in_specs=[tile(AUG_W), tile(LANES), pl.BlockSpec((1, 1, LANES), lambda j, *_: (j, 0, 0))],
            out_specs=[pl.BlockSpec((1, t, MOE_SLOTS), lambda j, *_: (j, 0, 0)),
                       pl.BlockSpec(memory_space=pl.ANY)],
            scratch_shapes=[pltpu.VMEM((2, MOE_SLOTS, AUG_W), BF16), pltpu.VMEM((MOE_TM, AUG_W), BF16),
                            pltpu.SemaphoreType.DMA((2,))]),
        out_shape=[jax.ShapeDtypeStruct((nt, t, MOE_SLOTS), BF16), jax.ShapeDtypeStruct((s_rows, AUG_W), BF16)],
        compiler_params=_cparams("arbitrary"),
        name="moe_dispatch",
    )(cntp, loc, goff, nch, tail, tailn, xaug, sel, locv)


def _ffn_kernel(te_ref, na_ref, xs_ref, w1_ref, b1_ref, w2_ref, b2_ref, perm_ref, ys_ref, w1p_ref, w2b_ref):
    i = pl.program_id(0)
    e = te_ref[i]
    active = i < na_ref[0]
    fresh = jnp.logical_or(i == 0, e != te_ref[jnp.maximum(i - 1, 0)])
    hh = EXPERT_HIDDEN // 2

    @pl.when(jnp.logical_and(active, fresh))
    def _():
        for c in range(2):
            blk = w1_ref[0, 0, :, 2 * hh * c:2 * hh * (c + 1)].astype(BF16)
            r = _dot(blk, perm_ref[...])
            w1p_ref[:, hh * c:hh * (c + 1)] = r[:, :hh].astype(BF16)
            w1p_ref[:, EXPERT_HIDDEN + hh * c:EXPERT_HIDDEN + hh * (c + 1)] = r[:, hh:].astype(BF16)
        w2b_ref[...] = w2_ref[0, 0].astype(BF16)

    @pl.when(active)
    def _():
        x = xs_ref[:, 0:D_MODEL]
        extra = xs_ref[:, D_MODEL:AUG_W].astype(F32)
        lane = lax.broadcasted_iota(jnp.int32, extra.shape, 1)
        mine = jnp.logical_and(lane % N_EXPERTS == e, lane < 3 * N_EXPERTS)
        wslot = jnp.where(mine, extra, 0.0).sum(axis=-1, keepdims=True)
        y = b2_ref[0]
        for c in range(2):
            gs = slice(hh * c, hh * (c + 1))
            us = slice(EXPERT_HIDDEN + hh * c, EXPERT_HIDDEN + hh * (c + 1))
            g = jnp.minimum(_dot(x, w1p_ref[:, gs]) + b1_ref[0, :, gs], SWIGLU_LIMIT)
            up = jnp.clip(_dot(x, w1p_ref[:, us]) + b1_ref[0, :, us], -SWIGLU_LIMIT, SWIGLU_LIMIT)
            act = g * _sigmoid(SWIGLU_ALPHA * g) * (up + 1.0)
            y = y + _dot(act.astype(BF16), w2b_ref[gs, :])
        ys_ref[...] = (wslot * y).astype(BF16)

    @pl.when(jnp.logical_not(active))
    def _():
        ys_ref[...] = jnp.zeros(ys_ref.shape, BF16)


def _ffn(xs, tile_e, n_active, w1, b1p, w2, b2, perm, n_tiles, layer):
    tm = MOE_TM
    d, h2 = w1.shape[2], w1.shape[3]
    once = pl.Buffered(1)
    return pl.pallas_call(
        _ffn_kernel,
        grid_spec=pltpu.PrefetchScalarGridSpec(
            num_scalar_prefetch=2,
            grid=(n_tiles,),
            in_specs=[pl.BlockSpec((tm, AUG_W), lambda i, te, na: (jnp.where(i < na[0], i, 0), 0)),
                      pl.BlockSpec((1, 1, d, h2), lambda i, te, na: (layer, te[i], 0, 0), pipeline_mode=once),
                      pl.BlockSpec((1, 1, h2), lambda i, te, na: (te[i], 0, 0)),
                      pl.BlockSpec((1, 1, h2 // 2, d), lambda i, te, na: (layer, te[i], 0, 0), pipeline_mode=once),
                      pl.BlockSpec((1, 1, d), lambda i, te, na: (te[i], 0, 0)),
                      pl.BlockSpec(perm.shape, lambda i, te, na: (0, 0), pipeline_mode=once)],
            out_specs=pl.BlockSpec((tm, d), lambda i, te, na: (i, 0)),
            scratch_shapes=[pltpu.VMEM((d, h2), BF16), pltpu.VMEM((h2 // 2, d), BF16)]),
        out_shape=jax.ShapeDtypeStruct((n_tiles * tm, d), BF16),
        compiler_params=_cparams("arbitrary"),
        name="moe_ffn",
    )(tile_e, n_active, xs, w1, b1p, w2, b2, perm)


def _combine_kernel(cntp_ref, loc_ref, goff_ref, nch_ref, pt_ref, x_ref, g2_ref, lg_ref, lb_ref, ys_ref,
                    o_ref, stage_ref, sems):
    j = pl.program_id(0)
    nt = pl.num_programs(0)
    buf = j % 2

    def in_copy(b):
        return lambda dst, src, rows: pltpu.make_async_copy(
            ys_ref.at[pl.ds(src, rows)], stage_ref.at[b].at[pl.ds(dst, rows)], sems.at[b])

    @pl.when(j == 0)
    def _():
        stage_ref[...] = jnp.zeros(stage_ref.shape, BF16)
        _group_copies(j, cntp_ref, loc_ref, goff_ref, in_copy(buf))

    @pl.when(j + 1 < nt)
    def _():
        _group_copies(j + 1, cntp_ref, loc_ref, goff_ref, in_copy(1 - buf))

    _wait_copies(nch_ref, j, in_copy(buf))
    y = _dot(pt_ref[0], stage_ref[buf])
    o_ref[0] = _ln(DEEPNORM_ALPHA * x_ref[0] + g2_ref[0] * y) * lg_ref[...] + lb_ref[...]


def _combine(pt, ys, x1, g2, lg, lb, cntp, loc, goff, nch):
    npb, l, d = x1.shape
    t = MOE_T
    nt_b = l // t
    tile = lambda w: pl.BlockSpec((1, t, w), lambda j, *_: (j // nt_b, j % nt_b, 0))
    const = lambda a: pl.BlockSpec(a.shape, lambda j, *_: (0,) * a.ndim)
    return pl.pallas_call(
        _combine_kernel,
        grid_spec=pltpu.PrefetchScalarGridSpec(
            num_scalar_prefetch=4,
            grid=(npb * nt_b,),
            in_specs=[pl.BlockSpec((1, t, MOE_SLOTS), lambda j, *_: (j, 0, 0)), tile(d),
                      pl.BlockSpec((1, 1, d), lambda j, *_: (j // nt_b, 0, 0)), const(lg), const(lb),
                      pl.BlockSpec(memory_space=pl.ANY)],
            out_specs=tile(d),
            scratch_shapes=[pltpu.VMEM((2, MOE_SLOTS, d), BF16), pltpu.SemaphoreType.DMA((2,))]),
        out_shape=jax.ShapeDtypeStruct((npb, l, d), F32),
        compiler_params=_cparams("arbitrary"),
        name="moe_combine",
    )(cntp, loc, goff, nch, pt, x1, g2, lg, lb, ys)


def _moe(x1, sh2, sc2, g2, rw, rb, w1, b1p, w2, b2, perm, lg, lb, npb, layer):
    _, l, d = x1.shape
    n_tok = npb * l
    nt = n_tok // MOE_T
    xaug, sel, cnt = _route(x1, sh2, sc2, rw, rb, npb)

    i32 = jnp.int32
    cnt = cnt[:, 0, :N_EXPERTS].astype(i32)
    cntp = (cnt + (MOE_G - 1)) // MOE_G * MOE_G
    loc = jnp.cumsum(cntp, axis=1) - cntp
    tot = cntp.sum(axis=0)
    seg = (tot + (MOE_TM - 1)) // MOE_TM * MOE_TM
    seg_start = jnp.cumsum(seg) - seg
    goff = seg_start[None, :] + jnp.cumsum(cntp, axis=0) - cntp
    s_max = TOP_K * n_tok + nt * N_EXPERTS * (MOE_G - 1) + N_EXPERTS * (MOE_TM - MOE_G)
    n_tiles = -(-s_max // MOE_TM)
    cum_tiles = jnp.cumsum(seg // MOE_TM)
    tile_e = jnp.minimum((jnp.arange(n_tiles, dtype=i32)[:, None] >= cum_tiles[None, :]).astype(i32).sum(axis=1),
                         N_EXPERTS - 1)
    n_active = cum_tiles[-1:].astype(i32)
    gran = cntp // MOE_G
    nch = jnp.stack([(gran // 2).sum(axis=1), (gran % 2).sum(axis=1)], axis=1).reshape(-1).astype(i32)
    locv = jnp.pad(loc.astype(F32), ((0, 0), (0, LANES - N_EXPERTS)))[:, None, :]
    flat = lambda a: a.reshape(-1).astype(i32)

    tail = jnp.concatenate([seg_start + tot, n_active * MOE_TM]).astype(i32)
    tailn = jnp.concatenate([(seg - tot) // MOE_G, n_tiles - n_active]).astype(i32)
    pt, xs = _dispatch(xaug, sel, locv, flat(cntp), flat(loc), flat(goff), nch, tail, tailn, n_tiles * MOE_TM)
    ys = _ffn(xs, tile_e, n_active, w1, b1p, w2, b2, perm, n_tiles, layer)
    return _combine(pt, ys, x1, g2, lg, lb, flat(cntp), flat(loc), flat(goff), nch)


def _deinterleave(n):
    return np.concatenate([np.arange(0, n, 2), np.arange(1, n, 2)])


def _rope_tables(l):
    rows = l // GRID_W
    r = np.repeat(np.arange(rows, dtype=np.float32), GRID_W)
    col = np.tile(np.arange(GRID_W, dtype=np.float32), rows)
    axis_dim = ATT_HEAD_DIM // 2
    inv_freq = (ROPE_THETA ** (-np.arange(0, axis_dim, 2, dtype=np.float32) / axis_dim)).astype(np.float32)
    ang = np.concatenate([r[:, None] * inv_freq, col[:, None] * inv_freq], axis=-1)
    c, s = np.cos(ang).astype(np.float32), np.sin(ang).astype(np.float32)
    z = np.zeros_like(s)
    rc = np.tile(np.concatenate([c, c], -1), (1, 2))
    rs1 = np.tile(np.concatenate([-s, z], -1), (1, 2))
    rs2 = np.tile(np.concatenate([z, s], -1), (1, 2))
    ident = (np.ones_like(rc), np.zeros_like(rc), np.zeros_like(rc))
    return tuple(jnp.asarray(np.stack([a, b])) for a, b in zip((rc, rs1, rs2), ident))


def kernel(x, c, ctx, c_ctx, w_mod, b_mod, w_in, att_q_norm, att_k_norm, gla_wdec_f, gla_bdec_f, gla_wdec_b, gla_bdec_b, gla_out_norm, hy_conv_w, hy_conv_b, hy_fw1, hy_fb1, hy_fw2, hy_fb2, hy_fw3, hy_fb3, hy_skip, w_branch_a, w_branch_b, w_branch_c, w_out, ln1_g, ln1_b, router_w, router_b, exp_w1, exp_b1, exp_w2, exp_b2, ln2_g, ln2_b):
    b, l, d = x.shape
    lc = ctx.shape[1]
    assert d == D_MODEL and (b * lc) % l == 0 and l % lc == 0 and lc % GLA_CHUNK == 0
    pc = (b * lc) // l
    p = b + pc

    perm64 = _deinterleave(ATT_HEAD_DIM)
    q_cols = np.concatenate([np.concatenate([64 * j + perm64, 64 * (j + 4) + perm64]) for j in range(4)])
    k_cols = ATT_Q_W + np.concatenate([perm64, 64 + perm64])
    v_cols = ATT_Q_W + ATT_KV_W + np.arange(ATT_KV_W)
    att_cols = np.concatenate([q_cols, k_cols, v_cols])
    o0 = ATT_Q_W + 2 * ATT_KV_W
    gla_cols = o0 + np.concatenate([np.arange(0, 2 * GLA_K_W + GLA_V_W + 2 * GLA_RANK)])
    og0 = o0 + 2 * GLA_K_W + GLA_V_W + 2 * GLA_RANK
    hy0 = og0 + GLA_V_W
    gt0 = hy0 + 3 * HY_WIDTH
    gain_perm = np.concatenate([perm64, perm64])
    bd = jnp.asarray(np.kron(np.eye(2), np.ones((64, 64))), dtype=BF16)
    rope = _rope_tables(l)
    fwd_l, inv_l = _dft_tables(l)
    fwd_c, inv_c = _dft_tables(lc)
    cmap = np.concatenate([np.arange(b), np.full(pc, b)])
    hh = EXPERT_HIDDEN // 2
    perm_np = np.zeros((2 * hh, 2 * hh), np.float32)
    perm_np[2 * np.arange(hh), np.arange(hh)] = 1.0
    perm_np[2 * np.arange(hh) + 1, hh + np.arange(hh)] = 1.0
    perm = jnp.asarray(perm_np, dtype=BF16)

    x_pair = (x, ctx.reshape(pc, l, d))
    cc = jnp.concatenate([c, c_ctx[None], jnp.zeros((7, d), F32)], axis=0)

    for i in range(DEPTH):
        with_ctx = i < DEPTH - 1
        npb = p if with_ctx else b
        mod = _modulation(cc, w_mod[i], b_mod[i])[cmap]
        sh1, sc1, g1, sh2, sc2, g2 = [m[:, None, :] for m in jnp.split(mod, 6, axis=-1)]

        wi = w_in[i]
        watt = wi[:, att_cols].astype(BF16)
        wgla = jnp.pad(wi[:, gla_cols], ((0, 0), (0, GLA_GROUP_W - gla_cols.size))).astype(BF16)
        wog = wi[:, og0:hy0].astype(BF16)
        why = wi[:, hy0:gt0].astype(BF16)
        wgt = wi[:, gt0:].astype(BF16)
        qg = att_q_norm[i][gain_perm].reshape(1, LANES)
        kg = att_k_norm[i][gain_perm].reshape(1, LANES)
        qn, kn, v, gla, og, hy = _inproj(*x_pair, sh1, sc1, (watt, wgla, wog, why), qg, kg, rope, bd, b)

        o_a = _attention(qn, kn, v, b, lc, with_ctx)

        wdec = jnp.zeros((LANES, 2 * GLA_K_W), F32)
        wdec = wdec.at[0:GLA_RANK, 0:GLA_K_W].set(gla_wdec_f[i])
        wdec = wdec.at[GLA_RANK:2 * GLA_RANK, GLA_K_W:].set(gla_wdec_b[i])
        bdec = jnp.concatenate([gla_bdec_f[i], gla_bdec_b[i]]).reshape(1, -1)
        o_b, o_b_ctx = _gla(gla, og, wdec, bdec, gla_out_norm[i].reshape(1, -1), b, lc)

        hre, him = _hy_filter(l, hy_fw1[i], hy_fb1[i], hy_fw2[i], hy_fb2[i], hy_fw3[i], hy_fb3[i], fwd_l)
        u_h, x0_h, t_h = _hy_pre(hy, hy_conv_w[i], hy_conv_b[i], hy_skip[i], b, l, b)
        y_h = _hy_conv(u_h, fwd_l, inv_l, hre, him)
        if with_ctx:
            hre_c, him_c = _hy_filter(lc, hy_fw1[i], hy_fb1[i], hy_fw2[i], hy_fb2[i], hy_fw3[i], hy_fb3[i], fwd_c)
            u_c, x0_c, t_c = _hy_pre(hy, hy_conv_w[i], hy_conv_b[i], hy_skip[i], b, lc, b)
            y_c = _hy_conv(u_c, fwd_c, inv_c, hre_c, him_c)
            fold = lambda a_ctx: a_ctx.reshape(pc, l, a_ctx.shape[-1])
            pairs = [(o_b, fold(o_b_ctx)), (y_h, fold(y_c)), (x0_h, fold(x0_c)), (t_h, fold(t_c))]
        else:
            pairs = [(a, a) for a in (o_b, y_h, x0_h, t_h)]

        wa_rows = np.concatenate([np.concatenate([64 * j + np.arange(64), 64 * (j + 4) + np.arange(64)])
                                  for j in range(4)])
        x1 = _merge(x_pair, o_a, *pairs, sh1, sc1, g1, wgt,
                    w_branch_a[i][wa_rows].astype(BF16), w_branch_b[i].astype(BF16),
                    w_branch_c[i].astype(BF16), w_out[i].astype(BF16),
                    ln1_g[i].reshape(1, d), ln1_b[i].reshape(1, d), npb)

        rw = jnp.pad(router_w[i], ((0, 0), (0, LANES - N_EXPERTS)))
        rb = jnp.concatenate([router_b[i], jnp.full((LANES - N_EXPERTS,), NEG_BIG, F32)]).reshape(1, LANES)
        b1p = jnp.concatenate([exp_b1[i][..., 0::2], exp_b1[i][..., 1::2]], axis=-1)[:, None, :]
        x_all = _moe(x1, sh2, sc2, g2, rw, rb, exp_w1, b1p, exp_w2, exp_b2[i][:, None, :], perm,
                     ln2_g[i].reshape(1, d), ln2_b[i].reshape(1, d), npb, i)
        x_pair = (x_all, x_all)
    return x_all
```

```python
import functools
import math

import numpy as np
import jax
import jax.numpy as jnp
from jax import lax
from jax.experimental import pallas as pl
from jax.experimental.pallas import tpu as pltpu

F32 = jnp.float32
BF16 = jnp.bfloat16

D_MODEL = 1024
DEPTH = 2
GRID_W = 64
ATT_HEADS = 8
ATT_KV_HEADS = 2
ATT_HEAD_DIM = 64
ROPE_THETA = 10000.0
GLA_HEADS = 4
GLA_DK = 64
GLA_DV = 128
GLA_RANK = 16
GLA_TAU = 16.0
GLA_CHUNK = 64
HY_WIDTH = 512
HY_POS_FREQS = 16
HY_DECAY_SLOW = math.log(1e-2) / 1.5
HY_DECAY_FAST = math.log(1e-2) / 0.3
N_EXPERTS = 32
TOP_K = 4
EXPERT_HIDDEN = D_MODEL
SWIGLU_LIMIT = 7.0
SWIGLU_ALPHA = 1.702
DEEPNORM_ALPHA = (2 * DEPTH) ** 0.25
LN_EPS = 1e-5
RMS_EPS = 1e-6

ATT_Q_W = ATT_HEADS * ATT_HEAD_DIM
ATT_KV_W = ATT_KV_HEADS * ATT_HEAD_DIM
GLA_K_W = GLA_HEADS * GLA_DK
GLA_V_W = GLA_HEADS * GLA_DV
GLA_GROUP_W = 1152
LANES = 128
NEG_BIG = -3.0e38

VMEM_LIMIT = 56 * 1024 * 1024


def _cparams(*sem):
    return pltpu.CompilerParams(dimension_semantics=sem, vmem_limit_bytes=VMEM_LIMIT)


def _dot(a, b):
    return jnp.dot(a, b, preferred_element_type=F32)


def _dot_nt(a, b):
    return lax.dot_general(a, b, (((1,), (1,)), ((), ())), preferred_element_type=F32)


def _dot_tn(a, b):
    return lax.dot_general(a, b, (((0,), (0,)), ((), ())), preferred_element_type=F32)


def _split2(x):
    hi = x.astype(BF16)
    lo = (x - hi.astype(F32)).astype(BF16)
    return hi, lo


def _split3(x):
    hi = x.astype(BF16)
    r = x - hi.astype(F32)
    mid = r.astype(BF16)
    lo = (r - mid.astype(F32)).astype(BF16)
    return hi, mid, lo


def _dot3(a, b):
    ah, al = _split2(a)
    bh, bl = _split2(b)
    return _dot(ah, bh) + (_dot(ah, bl) + _dot(al, bh))


def _dot_exact_lhs(a_bf16, b):
    b0, b1, b2 = _split3(b)
    return _dot(a_bf16, b0) + (_dot(a_bf16, b1) + _dot(a_bf16, b2))


def _ln(x):
    mu = jnp.mean(x, axis=-1, keepdims=True)
    xc = x - mu
    var = jnp.mean(xc * xc, axis=-1, keepdims=True)
    return xc * lax.rsqrt(var + LN_EPS)


def _sigmoid(x):
    return 1.0 / (1.0 + jnp.exp(-x))


def _mod_kernel(c_ref, w_ref, b_ref, o_ref):
    c = c_ref[...]
    o_ref[...] = _dot3(c * _sigmoid(c), w_ref[...]) + b_ref[...]


def _modulation(cc, w_mod, b_mod):
    rows, d = cc.shape
    n = w_mod.shape[1]
    tn = 1024
    return pl.pallas_call(
        _mod_kernel,
        grid=(n // tn,),
        in_specs=[pl.BlockSpec((rows, d), lambda j: (0, 0)),
                  pl.BlockSpec((d, tn), lambda j: (0, j)),
                  pl.BlockSpec((1, tn), lambda j: (0, j))],
        out_specs=pl.BlockSpec((rows, tn), lambda j: (0, j)),
        out_shape=jax.ShapeDtypeStruct((rows, n), F32),
        compiler_params=_cparams("arbitrary"),
        name="modulation",
    )(cc, w_mod, b_mod.reshape(1, n))


def _head_norm_rope(t, gain, rc, rs1, rs2, bd):
    sh, sl = _split2(t * t)
    ss = _dot(sh, bd) + _dot(sl, bd)
    tn = t * lax.rsqrt(ss * (1.0 / ATT_HEAD_DIM) + RMS_EPS) * gain
    return tn * rc + pltpu.roll(tn, 96, 1) * rs1 + pltpu.roll(tn, 32, 1) * rs2


def _pair_specs(tm, w, n_split, n_i):
    first = pl.BlockSpec((1, tm, w), lambda b, i: (jnp.minimum(b, n_split - 1), jnp.where(b < n_split, i, n_i - 1), 0))
    rest = pl.BlockSpec((1, tm, w), lambda b, i: (jnp.maximum(b - n_split, 0), jnp.where(b >= n_split, i, 0), 0))
    return [first, rest]


def _pick(use_rest, first_ref, rest_ref):
    return jnp.where(use_rest, rest_ref[0], first_ref[0])


def _inproj_kernel(x_ref, xr_ref, sh_ref, sc_ref, watt_ref, wgla_ref, wog_ref, why_ref,
                   qg_ref, kg_ref, rc_ref, rs1_ref, rs2_ref, bd_ref,
                   qn_ref, kn_ref, v_ref, gla_ref, og_ref, hy_ref, *, n_split):
    x = _pick(pl.program_id(0) >= n_split, x_ref, xr_ref)
    u = _ln(x) * (1.0 + sc_ref[0]) + sh_ref[0]
    ub = u.astype(BF16)
    rc, rs1, rs2, bd = rc_ref[0], rs1_ref[0], rs2_ref[0], bd_ref[...]
    att = _dot(ub, watt_ref[...])
    for j in range(ATT_Q_W // LANES):
        t = att[:, LANES * j:LANES * (j + 1)]
        qn = _head_norm_rope(t, qg_ref[...], rc, rs1, rs2, bd) * (ATT_HEAD_DIM ** -0.5)
        qn_ref[0, :, LANES * j:LANES * (j + 1)] = qn.astype(BF16)
    kn = _head_norm_rope(att[:, ATT_Q_W:ATT_Q_W + LANES], kg_ref[...], rc, rs1, rs2, bd)
    kn_ref[0] = kn.astype(BF16)
    v_ref[0, :, 0:LANES] = att[:, ATT_Q_W + LANES:].astype(BF16)
    v_ref[0, :, LANES:2 * LANES] = jnp.ones((att.shape[0], LANES), BF16)
    gla_ref[0] = _dot(ub, wgla_ref[...])
    og_ref[0] = _dot(ub, wog_ref[...]).astype(BF16)
    hy_ref[0] = _dot(ub, why_ref[...]).astype(BF16)


def _inproj(x_first, x_rest, sh, sc, wts, qg, kg, rope, bd, n_lat):
    n_split, l, d = x_first.shape
    p = sh.shape[0]
    tm = min(256, l)
    watt, wgla, wog, why = wts
    rc, rs1, rs2 = rope
    const = lambda a: pl.BlockSpec(a.shape, lambda b, i: (0,) * a.ndim)
    row = lambda w: pl.BlockSpec((1, tm, w), lambda b, i: (b, i, 0))
    tab = pl.BlockSpec((1, tm, LANES), lambda b, i: (jnp.where(b >= n_lat, 1, 0), i, 0))
    vec = pl.BlockSpec((1, 1, d), lambda b, i: (b, 0, 0))
    widths = (ATT_Q_W, LANES, 2 * LANES, GLA_GROUP_W, GLA_V_W, 3 * HY_WIDTH)
    dtypes = (BF16, BF16, BF16, F32, BF16, BF16)
    return pl.pallas_call(
        functools.partial(_inproj_kernel, n_split=n_split),
        grid=(p, l // tm),
        in_specs=_pair_specs(tm, d, n_split, l // tm) + [
            vec, vec, const(watt), const(wgla), const(wog), const(why),
            const(qg), const(kg), tab, tab, tab, const(bd)],
        out_specs=[row(w) for w in widths],
        out_shape=[jax.ShapeDtypeStruct((p, l, w), dt) for w, dt in zip(widths, dtypes)],
        compiler_params=_cparams("parallel", "arbitrary"),
        name="inproj",
    )(x_first, x_rest, sh, sc, watt, wgla, wog, why, qg, kg, rc, rs1, rs2, bd)


def _attend_group(qg, ks, vs, lo):
    zero = jnp.zeros_like(qg)
    outs = []
    for qh in (jnp.where(lo, qg, zero), jnp.where(lo, zero, qg)):
        ss = [_dot_nt(qh, k) for k in ks]
        m = ss[0].max(axis=-1, keepdims=True)
        for s in ss[1:]:
            m = jnp.maximum(m, s.max(axis=-1, keepdims=True))
        acc = 0.0
        for s, v in zip(ss, vs):
            acc = acc + _dot(jnp.exp((s - m).astype(BF16)), v)
        outs.append(acc[:, 0:LANES] / acc[:, LANES:2 * LANES])
    return jnp.where(lo, outs[0], outs[1])


def _attn_kernel(q_ref, k_ref, v_ref, kc_ref, vc_ref, o_ref, *, n_lat_tiles, with_ctx):
    i = pl.program_id(1)
    tq = q_ref.shape[1]
    lo = lax.broadcasted_iota(jnp.int32, (tq, LANES), 1) < ATT_HEAD_DIM

    def run(ks, vs):
        for j in range(ATT_Q_W // LANES):
            o = _attend_group(q_ref[0, :, LANES * j:LANES * (j + 1)], ks, vs, lo)
            o_ref[0, :, LANES * j:LANES * (j + 1)] = o.astype(BF16)

    if with_ctx:
        @pl.when(i < n_lat_tiles)
        def _():
            run([k_ref[0], kc_ref[0]], [v_ref[0], vc_ref[0]])

        @pl.when(i == n_lat_tiles)
        def _():
            run([kc_ref[0]], [vc_ref[0]])
    else:
        run([k_ref[0], kc_ref[0]], [v_ref[0], vc_ref[0]])


def _attention(qn, kn, v, n_lat, lc, with_ctx):
    p, l, _ = qn.shape
    tq = lc
    r = l // lc
    nt = l // tq
    ctx_idx = lambda b: (n_lat + b // r, b % r, 0)
    if with_ctx:
        qmap = lambda b, i: (jnp.where(i < nt, b, n_lat + b // r), jnp.where(i < nt, i, b % r), 0)
    else:
        qmap = lambda b, i: (b, i, 0)
    return pl.pallas_call(
        functools.partial(_attn_kernel, n_lat_tiles=nt, with_ctx=with_ctx),
        grid=(n_lat, nt + (1 if with_ctx else 0)),
        in_specs=[pl.BlockSpec((1, tq, ATT_Q_W), qmap),
                  pl.BlockSpec((1, l, LANES), lambda b, i: (b, 0, 0)),
                  pl.BlockSpec((1, l, 2 * LANES), lambda b, i: (b, 0, 0)),
                  pl.BlockSpec((1, lc, LANES), lambda b, i: ctx_idx(b)),
                  pl.BlockSpec((1, lc, 2 * LANES), lambda b, i: ctx_idx(b))],
        out_specs=pl.BlockSpec((1, tq, ATT_Q_W), qmap),
        out_shape=jax.ShapeDtypeStruct((p if with_ctx else n_lat, l, ATT_Q_W), BF16),
        compiler_params=_cparams("parallel", "arbitrary"),
        name="attention",
    )(qn, kn, v, kn, v)


def _gla_segment(src_ref, og_ref, out_ref, acc_ref, st_ref, qd_ref, ke_ref, vb_ref, dec_ref,
                 wdec_ref, bdec_ref, gn_ref, ls):
    c = GLA_CHUNK
    nc = ls // c
    rb = min(256, ls)

    bi = lax.broadcasted_iota(jnp.int32, (rb, rb), 0)
    bj = lax.broadcasted_iota(jnp.int32, (rb, rb), 1)
    same = (bi // c) == (bj // c)
    att_masks = (same & (bi >= bj), same & (bi <= bj))
    tri_blk = tuple(jnp.where(m, 1.0, 0.0).astype(BF16) for m in att_masks)
    ones_blk = jnp.where(same, 1.0, 0.0).astype(BF16)
    lo_b = lax.broadcasted_iota(jnp.int32, (rb, LANES), 1) < GLA_DK

    def prep(t, carry):
        r0 = pl.multiple_of(t * rb, rb)
        q = src_ref[0, pl.ds(r0, rb), 0:GLA_K_W] * (GLA_DK ** -0.5)
        k = src_ref[0, pl.ds(r0, rb), GLA_K_W:2 * GLA_K_W]
        vb = src_ref[0, pl.ds(r0, rb), 2 * GLA_K_W:2 * GLA_K_W + GLA_V_W].astype(BF16)
        vb_ref[pl.ds(r0, rb), :] = vb
        lr = src_ref[0, pl.ds(r0, rb), 2 * GLA_K_W + GLA_V_W:GLA_GROUP_W]
        la = jax.nn.log_sigmoid(_dot3(lr, wdec_ref[...]) + bdec_ref[...]) * (1.0 / GLA_TAU)
        for d in range(2):
            l0, l1, l2 = _split3(la[:, d * GLA_K_W:(d + 1) * GLA_K_W])
            cum = _dot(tri_blk[d], l0) + (_dot(tri_blk[d], l1) + _dot(tri_blk[d], l2))
            tot = _dot(ones_blk, l0) + (_dot(ones_blk, l1) + _dot(ones_blk, l2))
            qd = (q * jnp.exp(cum)).astype(BF16)
            ki = (k * jnp.exp(-cum)).astype(BF16)
            qd_ref[d, pl.ds(r0, rb), :] = qd
            ke_ref[d, pl.ds(r0, rb), :] = (k * jnp.exp(tot - cum)).astype(BF16)
            dec_ref[d, pl.ds(r0, rb), :] = jnp.exp(tot)
            for h in range(GLA_HEADS):
                g, half = h // 2, h % 2
                sl = slice(LANES * g, LANES * (g + 1))
                zero = jnp.zeros_like(qd[:, sl])
                qh = jnp.where(lo_b, qd[:, sl], zero) if half == 0 else jnp.where(lo_b, zero, qd[:, sl])
                att = jnp.where(att_masks[d], _dot_nt(qh, ki[:, sl]), 0.0)
                o = _dot(att.astype(BF16), vb[:, GLA_DV * h:GLA_DV * (h + 1)])
                if d == 0:
                    acc_ref[pl.ds(r0, rb), GLA_DV * h:GLA_DV * (h + 1)] = o
                else:
                    acc_ref[pl.ds(r0, rb), GLA_DV * h:GLA_DV * (h + 1)] += o
        return carry

    lax.fori_loop(0, ls // rb, prep, 0, unroll=2 if (ls // rb) % 2 == 0 else 1)

    lo = lax.broadcasted_iota(jnp.int32, (c, LANES), 1) < GLA_DK

    def body(n, carry):
        for d, cn in ((0, n), (1, nc - 1 - n)):
            r0 = pl.multiple_of(cn * c, c)
            dec = dec_ref[d, pl.ds(r0, 1), :]
            for h in range(GLA_HEADS):
                g, half = h // 2, h % 2
                sl = slice(LANES * g, LANES * (g + 1))
                qd = qd_ref[d, pl.ds(r0, c), sl]
                zero = jnp.zeros_like(qd)
                qd = jnp.where(lo, qd, zero) if half == 0 else jnp.where(lo, zero, qd)
                vh = vb_ref[pl.ds(r0, c), GLA_DV * h:GLA_DV * (h + 1)]
                st = st_ref[d * GLA_HEADS + h]
                acc_ref[pl.ds(r0, c), GLA_DV * h:GLA_DV * (h + 1)] += _dot_nt(qd, st.astype(BF16))
                st_ref[d * GLA_HEADS + h] = dec[:, sl] * st + _dot_tn(vh, ke_ref[d, pl.ds(r0, c), sl])
        return carry

    lax.fori_loop(0, nc, body, 0, unroll=4 if nc % 4 == 0 else (2 if nc % 2 == 0 else 1))

    tr = min(256, ls)

    def fin(t, carry):
        r0 = pl.multiple_of(t * tr, tr)
        for h in range(GLA_HEADS):
            sl = slice(GLA_DV * h, GLA_DV * (h + 1))
            o = acc_ref[pl.ds(r0, tr), sl]
            y = o * lax.rsqrt(jnp.mean(o * o, axis=-1, keepdims=True) + RMS_EPS)
            y = y * gn_ref[...]
            og = og_ref[0, pl.ds(r0, tr), sl].astype(F32)
            out_ref[0, pl.ds(r0, tr), sl] = (y * (og * _sigmoid(og))).astype(BF16)
        return carry

    lax.fori_loop(0, ls // tr, fin, 0)


def _gla_kernel(gl_ref, glc_ref, og_ref, ogc_ref, wdec_ref, bdec_ref, gn_ref, o_ref, oc_ref,
                acc_ref, st_ref, qd_ref, ke_ref, vb_ref, dec_ref):
    st_ref[...] = jnp.zeros(st_ref.shape, F32)
    scratch = (acc_ref, st_ref, qd_ref, ke_ref, vb_ref, dec_ref)
    _gla_segment(glc_ref, ogc_ref, oc_ref, *scratch, wdec_ref, bdec_ref, gn_ref, glc_ref.shape[1])
    _gla_segment(gl_ref, og_ref, o_ref, *scratch, wdec_ref, bdec_ref, gn_ref, gl_ref.shape[1])


def _gla(gla, og, wdec, bdec, gn, n_lat, lc):
    p, l, _ = gla.shape
    r = l // lc
    ctx_idx = lambda b: (n_lat + b // r, b % r, 0)
    const = lambda a: pl.BlockSpec(a.shape, lambda b: (0,) * a.ndim)
    return pl.pallas_call(
        _gla_kernel,
        grid=(n_lat,),
        in_specs=[pl.BlockSpec((1, l, GLA_GROUP_W), lambda b: (b, 0, 0)),
                  pl.BlockSpec((1, lc, GLA_GROUP_W), ctx_idx),
                  pl.BlockSpec((1, l, GLA_V_W), lambda b: (b, 0, 0)),
                  pl.BlockSpec((1, lc, GLA_V_W), ctx_idx),
                  const(wdec), const(bdec), const(gn)],
        out_specs=[pl.BlockSpec((1, l, GLA_V_W), lambda b: (b, 0, 0)),
                   pl.BlockSpec((1, lc, GLA_V_W), lambda b: (b, 0, 0))],
        out_shape=[jax.ShapeDtypeStruct((n_lat, l, GLA_V_W), BF16),
                   jax.ShapeDtypeStruct((n_lat, lc, GLA_V_W), BF16)],
        scratch_shapes=[pltpu.VMEM((l, GLA_V_W), F32),
                        pltpu.VMEM((2 * GLA_HEADS, GLA_DV, LANES), F32),
                        pltpu.VMEM((2, l, GLA_K_W), BF16),
                        pltpu.VMEM((2, l, GLA_K_W), BF16), pltpu.VMEM((l, GLA_V_W), BF16),
                        pltpu.VMEM((2, l, GLA_K_W), F32)],
        compiler_params=_cparams("arbitrary"),
        name="gla",
    )(gla, gla, og, og, wdec, bdec, gn)


HY_FREQ_TILE = 512


def _dft_tables(l):
    n = 2 * l
    k = np.arange(l, dtype=np.int64)[:, None]
    t = np.arange(l, dtype=np.int64)[None, :]
    ang = 2.0 * np.pi * ((k * t) % n).astype(np.float64) / n
    fre = np.cos(ang)
    fim = -np.sin(ang)
    fim[0, :] = np.where(np.arange(l) % 2 == 0, 1.0, -1.0)
    tk = min(HY_FREQ_TILE, l)
    nk = l // tk
    fwd = np.concatenate([fre.reshape(nk, tk, l), fim.reshape(nk, tk, l)], axis=1)
    return jnp.asarray(fwd, dtype=BF16), jnp.asarray(fwd.transpose(0, 2, 1), dtype=BF16)


def _hy_filter_kernel(z_ref, win_ref, w1_ref, b1_ref, w2_ref, b2_ref, w3_ref, b3_ref, f_ref,
                      hre_ref, him_ref, hf_ref):
    j = pl.program_id(0)
    l = z_ref.shape[0]

    @pl.when(j == 0)
    def _():
        h = jnp.sin(_dot3(z_ref[...], w1_ref[...]) + b1_ref[...])
        h = jnp.sin(_dot3(h, w2_ref[...]) + b2_ref[...])
        h = _dot3(h, w3_ref[...]) + b3_ref[...]
        win = win_ref[...]
        hf = h[:, :HY_WIDTH] * win
        hb = h[:, HY_WIDTH:] * win
        row = lax.broadcasted_iota(jnp.int32, (l, HY_WIDTH), 0)
        hf_ref[:, :HY_WIDTH] = hf.astype(BF16)
        hf_ref[:, HY_WIDTH:] = jnp.where(row == 0, 0.0, hb).astype(BF16)

    tk = f_ref.shape[1] // 2
    ab = _dot(f_ref[0], hf_ref[...])
    a, b = ab[:tk], ab[tk:]
    first = (lax.broadcasted_iota(jnp.int32, (tk, HY_WIDTH), 0) == 0) & (j == 0)
    scl = jnp.where(first, 1.0 / (2 * l), 2.0 / (2 * l))
    hre_ref[...] = (a[:, :HY_WIDTH] + a[:, HY_WIDTH:]) * scl
    him_ref[...] = (b[:, :HY_WIDTH] + jnp.where(first, 1.0, -1.0) * b[:, HY_WIDTH:]) * scl


def _hy_filter(l, fw1, fb1, fw2, fb2, fw3, fb3, fwd):
    t = np.arange(l, dtype=np.float32)[:, None] / np.float32(l)
    freqs = np.arange(1, HY_POS_FREQS + 1, dtype=np.float32)
    z = np.concatenate([t, np.cos(2.0 * math.pi * freqs * t), np.sin(2.0 * math.pi * freqs * t)], axis=-1)
    z = np.pad(z, ((0, 0), (0, LANES - z.shape[1])))
    hid = fw2.shape[0]
    fw1 = jnp.pad(fw1, ((0, LANES - fw1.shape[0]), (0, LANES - hid)))
    fb1 = jnp.pad(fb1, (0, LANES - hid))
    fw2 = jnp.pad(fw2, ((0, LANES - hid), (0, LANES - hid)))
    fb2 = jnp.pad(fb2, (0, LANES - hid))
    fw3 = jnp.pad(fw3, ((0, LANES - hid), (0, 0)))
    deltas = np.abs(np.linspace(HY_DECAY_SLOW, HY_DECAY_FAST, HY_WIDTH, dtype=np.float32))
    win = np.exp(-t * deltas).astype(np.float32)
    nk, tk2, _ = fwd.shape
    tk = tk2 // 2
    const = lambda a: pl.BlockSpec(a.shape, lambda j: (0,) * a.ndim)
    args = (jnp.asarray(z), jnp.asarray(win), fw1, fb1.reshape(1, -1), fw2, fb2.reshape(1, -1),
            fw3, fb3.reshape(1, -1))
    return pl.pallas_call(
        _hy_filter_kernel,
        grid=(nk,),
        in_specs=[const(a) for a in args] + [pl.BlockSpec((1, tk2, l), lambda j: (j, 0, 0))],
        out_specs=[pl.BlockSpec((tk, HY_WIDTH), lambda j: (j, 0))] * 2,
        out_shape=[jax.ShapeDtypeStruct((l, HY_WIDTH), F32)] * 2,
        scratch_shapes=[pltpu.VMEM((l, 2 * HY_WIDTH), BF16)],
        compiler_params=_cparams("arbitrary"),
        name="hyena_filter",
    )(*args, fwd)


def _hy_pre_kernel(p0_ref, p1_ref, p2_ref, w0_ref, w1_ref, w2_ref, b0_ref, b1_ref, b2_ref, skip_ref,
                   u_ref, x0_ref, t_ref):
    ls, cw = p0_ref.shape[1], p0_ref.shape[2]
    row = lax.broadcasted_iota(jnp.int32, (ls, cw), 0)

    def conv(p_ref, w_ref, b_ref):
        p = p_ref[0].astype(F32)
        prev = jnp.where(row == 0, 0.0, pltpu.roll(p, 1, 0))
        nxt = jnp.where(row == ls - 1, 0.0, pltpu.roll(p, ls - 1, 0))
        return prev * w_ref[0:1, :] + p * w_ref[1:2, :] + nxt * w_ref[2:3, :] + b_ref[...]

    x0 = conv(p0_ref, w0_ref, b0_ref)
    u = conv(p1_ref, w1_ref, b1_ref) * conv(p2_ref, w2_ref, b2_ref)
    u_ref[0] = u.astype(BF16)
    x0_ref[0] = x0.astype(BF16)
    t_ref[0] = (x0 * u * skip_ref[...]).astype(BF16)


def _hy_pre(hy, conv_w, conv_b, skip, nseq, ls, n_lat):
    p, l, _ = hy.shape
    r = l // ls
    cw = 2 * LANES
    nj = HY_WIDTH // cw
    base = 0 if ls == l else n_lat
    src = lambda off: pl.BlockSpec((1, ls, cw), lambda b, j: (base + b // r, b % r, off * nj + j))
    wsp = lambda off: pl.BlockSpec((3, cw), lambda b, j: (0, off * nj + j))
    bsp = lambda off: pl.BlockSpec((1, cw), lambda b, j: (0, off * nj + j))
    out = pl.BlockSpec((1, ls, cw), lambda b, j: (b, 0, j))
    cb = conv_b.reshape(1, -1)
    return pl.pallas_call(
        _hy_pre_kernel,
        grid=(nseq, nj),
        in_specs=[src(0), src(1), src(2), wsp(0), wsp(1), wsp(2), bsp(0), bsp(1), bsp(2), bsp(0)],
        out_specs=[out] * 3,
        out_shape=[jax.ShapeDtypeStruct((nseq, ls, HY_WIDTH), BF16)] * 3,
        compiler_params=_cparams("parallel", "arbitrary"),
        name="hyena_pre",
    )(hy, hy, hy, conv_w, conv_w, conv_w, cb, cb, cb, skip.reshape(1, -1))


def _hy_conv_kernel(u_ref, f_ref, i_ref, hre_ref, him_ref, y_ref):
    j = pl.program_id(1)
    tk = hre_ref.shape[0]
    uu = _dot(f_ref[0], u_ref[0])
    ure, uim = uu[:tk], uu[tk:]
    hre, him = hre_ref[...], him_ref[...]
    first = (lax.broadcasted_iota(jnp.int32, ure.shape, 0) == 0) & (j == 0)
    yre = jnp.where(first, ure * hre, ure * hre - uim * him)
    yim = jnp.where(first, uim * him, ure * him + uim * hre)
    y = _dot(i_ref[0], jnp.concatenate([yre.astype(BF16), yim.astype(BF16)], axis=0))

    @pl.when(j == 0)
    def _():
        y_ref[0] = y

    @pl.when(j > 0)
    def _():
        y_ref[0] += y


def _hy_conv(u, fwd, inv, hre, him):
    nseq, ls, _ = u.shape
    nk, tk2, _ = fwd.shape
    tk = tk2 // 2
    return pl.pallas_call(
        _hy_conv_kernel,
        grid=(nseq, nk),
        in_specs=[pl.BlockSpec((1, ls, HY_WIDTH), lambda b, j: (b, 0, 0)),
                  pl.BlockSpec((1, tk2, ls), lambda b, j: (j, 0, 0)),
                  pl.BlockSpec((1, ls, tk2), lambda b, j: (j, 0, 0)),
                  pl.BlockSpec((tk, HY_WIDTH), lambda b, j: (j, 0)),
                  pl.BlockSpec((tk, HY_WIDTH), lambda b, j: (j, 0))],
        out_specs=pl.BlockSpec((1, ls, HY_WIDTH), lambda b, j: (b, 0, 0)),
        out_shape=jax.ShapeDtypeStruct((nseq, ls, HY_WIDTH), F32),
        compiler_params=_cparams("parallel", "arbitrary"),
        name="hyena_conv",
    )(u, fwd, inv, hre, him)


def _merge_kernel(x_ref, xr_ref, oa_ref, ob_ref, obr_ref, y_ref, yr_ref, x0_ref, x0r_ref, t_ref, tr_ref,
                  sh_ref, sc_ref, g1_ref, wgt_ref, wa_ref, wb_ref, wc_ref, wo_ref, lg_ref, lb_ref, o_ref,
                  *, n_split):
    d = x_ref.shape[2]
    rest = pl.program_id(0) >= n_split
    x = _pick(rest, x_ref, xr_ref)
    u = _ln(x) * (1.0 + sc_ref[0]) + sh_ref[0]
    ob = _pick(rest, ob_ref, obr_ref)
    oc = (_pick(rest, x0_ref, x0r_ref).astype(F32) * _pick(rest, y_ref, yr_ref)
          + _pick(rest, t_ref, tr_ref).astype(F32)).astype(BF16)
    gt = _dot(u.astype(BF16), wgt_ref[...])
    m = (_sigmoid(gt[:, 0:d]) * _dot(oa_ref[0], wa_ref[...])
         + _sigmoid(gt[:, d:2 * d]) * _dot(ob, wb_ref[...])
         + _sigmoid(gt[:, 2 * d:3 * d]) * _dot(oc, wc_ref[...]))
    mix = _dot(m.astype(BF16), wo_ref[...])
    o_ref[0] = _ln(DEEPNORM_ALPHA * x + g1_ref[0] * mix) * lg_ref[...] + lb_ref[...]


def _merge(x, o_a, o_b, y, x0, t, sh, sc, g1, wgt, wa, wb, wc, wo, lg, lb, npb):
    n_split, l, d = x[0].shape
    tm = min(512, l)
    n_i = l // tm
    row = lambda w: pl.BlockSpec((1, tm, w), lambda b, i: (b, i, 0))
    const = lambda a: pl.BlockSpec(a.shape, lambda b, i: (0,) * a.ndim)
    pair = lambda w: _pair_specs(tm, w, n_split, n_i)
    vec = pl.BlockSpec((1, 1, d), lambda b, i: (b, 0, 0))
    wgt_spec = pl.BlockSpec(wgt.shape, lambda b, i: (0, 0), pipeline_mode=pl.Buffered(1))
    return pl.pallas_call(
        functools.partial(_merge_kernel, n_split=n_split),
        grid=(npb, n_i),
        in_specs=pair(d) + [row(ATT_Q_W)] + pair(GLA_V_W) + pair(HY_WIDTH) + pair(HY_WIDTH) + pair(HY_WIDTH) + [
            vec, vec, vec, wgt_spec,
            const(wa), const(wb), const(wc), const(wo), const(lg), const(lb)],
        out_specs=row(d),
        out_shape=jax.ShapeDtypeStruct((npb, l, d), F32),
        compiler_params=_cparams("parallel", "arbitrary"),
        name="merge",
    )(*x, o_a, *o_b, *y, *x0, *t, sh, sc, g1, wgt, wa, wb, wc, wo, lg, lb)


MOE_T = 256
MOE_G = 16
MOE_TM = 1024
MOE_SLOTS = TOP_K * MOE_T + 512
AUG_W = D_MODEL + LANES


def _route_kernel(x_ref, sh_ref, sc_ref, rw_ref, rb_ref, xa_ref, sel_ref, cnt_ref):
    d = x_ref.shape[2]
    for h in range(x_ref.shape[1] // MOE_T):
        rows = slice(h * MOE_T, (h + 1) * MOE_T)
        u = _ln(x_ref[0, rows, :]) * (1.0 + sc_ref[0]) + sh_ref[0]
        xa_ref[0, rows, 0:d] = u.astype(BF16)
        logits = _dot3(u, rw_ref[...]) + rb_ref[...]
        lane = lax.broadcasted_iota(jnp.int32, logits.shape, 1).astype(F32)
        work = logits
        hots, vals = [], []
        for _ in range(TOP_K):
            m = work.max(axis=-1, keepdims=True)
            idx = jnp.where(work == m, lane, float(LANES)).min(axis=-1, keepdims=True)
            hot = lane == idx
            hots.append(hot)
            vals.append(m)
            work = jnp.where(hot, NEG_BIG, work)
        es = [jnp.exp(v - vals[0]) for v in vals]
        den = es[0] + es[1] + es[2] + es[3]
        comb = jnp.zeros_like(logits)
        sel = jnp.zeros_like(logits)
        for hot, e in zip(hots, es):
            comb = comb + jnp.where(hot, e / den, 0.0)
            sel = sel + jnp.where(hot, 1.0, 0.0)
        c0, c1, c2 = _split3(comb)
        aug = c0.astype(F32) + pltpu.roll(c1.astype(F32), 32, 1) + pltpu.roll(c2.astype(F32), 64, 1)
        xa_ref[0, rows, d:d + LANES] = aug.astype(BF16)
        sel_ref[0, rows, :] = sel.astype(BF16)
        cnt_ref[h] = sel.sum(axis=0, keepdims=True)


def _route(x_all, sh, sc, rw, rb, npb):
    _, l, d = x_all.shape
    per = next(n for n in (4, 2, 1) if l % (n * MOE_T) == 0)
    t = per * MOE_T
    n_i = l // t
    row = lambda w: pl.BlockSpec((1, t, w), lambda b, i: (b, i, 0))
    vec = pl.BlockSpec((1, 1, d), lambda b, i: (b, 0, 0))
    const = lambda a: pl.BlockSpec(a.shape, lambda b, i: (0,) * a.ndim)
    return pl.pallas_call(
        _route_kernel,
        grid=(npb, n_i),
        in_specs=[row(d), vec, vec, const(rw), const(rb)],
        out_specs=[row(AUG_W), row(LANES), pl.BlockSpec((per, 1, LANES), lambda b, i: (b * n_i + i, 0, 0))],
        out_shape=[jax.ShapeDtypeStruct((npb, l, AUG_W), BF16), jax.ShapeDtypeStruct((npb, l, LANES), BF16),
                   jax.ShapeDtypeStruct((npb * n_i * per, 1, LANES), F32)],
        compiler_params=_cparams("parallel", "arbitrary"),
        name="route",
    )(x_all, sh, sc, rw, rb)


def _group_copies(j, cntp_ref, loc_ref, goff_ref, make_copy):
    for e in range(N_EXPERTS):
        n = lax.shift_right_logical(cntp_ref[j * N_EXPERTS + e], int(math.log2(MOE_G)))
        loc = loc_ref[j * N_EXPERTS + e]
        off = goff_ref[j * N_EXPERTS + e]
        n2 = lax.shift_right_logical(n, 1)

        def per_pair(c, carry, loc=loc, off=off):
            make_copy(pl.multiple_of(loc + c * (2 * MOE_G), MOE_G),
                      pl.multiple_of(off + c * (2 * MOE_G), MOE_G), 2 * MOE_G).start()
            return carry

        lax.fori_loop(0, n2, per_pair, 0)

        @pl.when((n & 1) == 1)
        def _(loc=loc, off=off, n2=n2):
            make_copy(pl.multiple_of(loc + n2 * (2 * MOE_G), MOE_G),
                      pl.multiple_of(off + n2 * (2 * MOE_G), MOE_G), MOE_G).start()


def _wait_copies(nch_ref, j, make_copy):
    for k, rows in ((0, 2 * MOE_G), (1, MOE_G)):
        def wait_one(c, carry, rows=rows):
            make_copy(0, 0, rows).wait()
            return carry

        lax.fori_loop(0, nch_ref[2 * j + k], wait_one, 0)


def _dispatch_kernel(cntp_ref, loc_ref, goff_ref, nch_ref, tail_ref, tailn_ref,
                     xa_ref, sel_ref, locv_ref, pt_ref, xs_ref, stage_ref, zero_ref, sems):
    j = pl.program_id(0)
    nt = pl.num_programs(0)
    buf = j % 2
    sem = sems.at[buf]
    stage = stage_ref.at[buf]
    t = MOE_T

    def out_copy(src, dst, rows):
        return pltpu.make_async_copy(stage.at[pl.ds(src, rows)], xs_ref.at[pl.ds(dst, rows)], sem)

    @pl.when(j >= 2)
    def _():
        _wait_copies(nch_ref, jnp.maximum(j - 2, 0), out_copy)

    sel = sel_ref[0]
    ri = lax.broadcasted_iota(jnp.int32, (t, t), 0)
    ci = lax.broadcasted_iota(jnp.int32, (t, t), 1)
    rank = _dot(jnp.where(ci < ri, 1.0, 0.0).astype(BF16), sel)
    dest = rank + locv_ref[0]
    lane = lax.broadcasted_iota(jnp.int32, (t, LANES), 1).astype(F32)
    slot = lax.broadcasted_iota(jnp.int32, (t, MOE_SLOTS), 1)
    avail = sel.astype(F32)
    pt = jnp.zeros((t, MOE_SLOTS), F32)
    for _ in range(TOP_K):
        ek = jnp.where(avail > 0.0, lane, float(LANES)).min(axis=-1, keepdims=True)
        hot = lane == ek
        dk = jnp.where(hot, dest, 0.0).sum(axis=-1, keepdims=True).astype(jnp.int32)
        pt = pt + jnp.where(slot == dk, 1.0, 0.0)
        avail = jnp.where(hot, 0.0, avail)
    ptb = pt.astype(BF16)
    pt_ref[0] = ptb
    stage[...] = _dot_tn(ptb, xa_ref[0]).astype(BF16)
    _group_copies(j, cntp_ref, loc_ref, goff_ref, out_copy)

    @pl.when(j == nt - 1)
    def _():
        _wait_copies(nch_ref, j, out_copy)

        @pl.when(j >= 1)
        def _():
            other = sems.at[1 - buf]
            _wait_copies(nch_ref, jnp.maximum(j - 1, 0), lambda s, d, rows: pltpu.make_async_copy(
                stage_ref.at[1 - buf].at[pl.ds(s, rows)], xs_ref.at[pl.ds(d, rows)], other))

        zero_ref[...] = jnp.zeros(zero_ref.shape, BF16)

        def zcopy(dst):
            return pltpu.make_async_copy(zero_ref.at[pl.ds(0, MOE_G)], xs_ref.at[pl.ds(dst, MOE_G)], sem)

        def zcopy_big(dst):
            return pltpu.make_async_copy(zero_ref, xs_ref.at[pl.ds(dst, MOE_TM)], sem)

        def per_expert(e, tot):
            n = tailn_ref[e]
            base = tail_ref[e]

            def per_chunk(c, carry):
                zcopy(pl.multiple_of(base + c * MOE_G, MOE_G)).start()
                return carry

            lax.fori_loop(0, n, per_chunk, 0)
            return tot + n

        ztotal = lax.fori_loop(0, N_EXPERTS, per_expert, 0)

        def zwait(c, carry):
            zcopy(0).wait()
            return carry

        lax.fori_loop(0, ztotal, zwait, 0)

        nbig = tailn_ref[N_EXPERTS]
        big0 = tail_ref[N_EXPERTS]

        def big_start(c, carry):
            zcopy_big(pl.multiple_of(big0 + c * MOE_TM, MOE_TM)).start()
            return carry

        def big_wait(c, carry):
            zcopy_big(0).wait()
            return carry

        lax.fori_loop(0, nbig, big_start, 0)
        lax.fori_loop(0, nbig, big_wait, 0)


def _dispatch(xaug, sel, locv, cntp, loc, goff, nch, tail, tailn, s_rows):
    npb, l, _ = xaug.shape
    t = MOE_T
    nt_b = l // t
    nt = npb * nt_b
    tile = lambda w: pl.BlockSpec((1, t, w), lambda j, *_: (j // nt_b, j % nt_b, 0))
    return pl.pallas_call(
        _dispatch_kernel,
        grid_spec=pltpu.PrefetchScalarGridSpec(
            num_scalar_prefetch=6,
            grid=(nt,),
            in_specs=[tile(AUG_W), tile(LANES), pl.BlockSpec((1, 1, LANES), lambda j, *_: (j, 0, 0))],
            out_specs=[pl.BlockSpec((1, t, MOE_SLOTS), lambda j, *_: (j, 0, 0)),
                       pl.BlockSpec(memory_space=pl.ANY)],
            scratch_shapes=[pltpu.VMEM((2, MOE_SLOTS, AUG_W), BF16), pltpu.VMEM((MOE_TM, AUG_W), BF16),
                            pltpu.SemaphoreType.DMA((2,))]),
        out_shape=[jax.ShapeDtypeStruct((nt, t, MOE_SLOTS), BF16), jax.ShapeDtypeStruct((s_rows, AUG_W), BF16)],
        compiler_params=_cparams("arbitrary"),
        name="moe_dispatch",
    )(cntp, loc, goff, nch, tail, tailn, xaug, sel, locv)


def _ffn_kernel(te_ref, na_ref, xs_ref, w1_ref, b1_ref, w2_ref, b2_ref, perm_ref, ys_ref, w1p_ref, w2b_ref):
    i = pl.program_id(0)
    e = te_ref[i]
    active = i < na_ref[0]
    fresh = jnp.logical_or(i == 0, e != te_ref[jnp.maximum(i - 1, 0)])
    hh = EXPERT_HIDDEN // 2

    @pl.when(jnp.logical_and(active, fresh))
    def _():
        for c in range(2):
            blk = w1_ref[0, 0, :, 2 * hh * c:2 * hh * (c + 1)].astype(BF16)
            r = _dot(blk, perm_ref[...])
            w1p_ref[:, hh * c:hh * (c + 1)] = r[:, :hh].astype(BF16)
            w1p_ref[:, EXPERT_HIDDEN + hh * c:EXPERT_HIDDEN + hh * (c + 1)] = r[:, hh:].astype(BF16)
        w2b_ref[...] = w2_ref[0, 0].astype(BF16)

    @pl.when(active)
    def _():
        x = xs_ref[:, 0:D_MODEL]
        extra = xs_ref[:, D_MODEL:AUG_W].astype(F32)
        lane = lax.broadcasted_iota(jnp.int32, extra.shape, 1)
        mine = jnp.logical_and(lane % N_EXPERTS == e, lane < 3 * N_EXPERTS)
        wslot = jnp.where(mine, extra, 0.0).sum(axis=-1, keepdims=True)
        h = _dot(x, w1p_ref[...]) + b1_ref[0]
        g = jnp.minimum(h[:, :EXPERT_HIDDEN], SWIGLU_LIMIT)
        up = jnp.clip(h[:, EXPERT_HIDDEN:], -SWIGLU_LIMIT, SWIGLU_LIMIT)
        act = g * _sigmoid(SWIGLU_ALPHA * g) * (up + 1.0)
        y = _dot(act.astype(BF16), w2b_ref[...]) + b2_ref[0]
        ys_ref[...] = (wslot * y).astype(BF16)

    @pl.when(jnp.logical_not(active))
    def _():
        ys_ref[...] = jnp.zeros(ys_ref.shape, BF16)


def _ffn(xs, tile_e, n_active, w1, b1p, w2, b2, perm, n_tiles, layer):
    tm = MOE_TM
    d, h2 = w1.shape[2], w1.shape[3]
    once = pl.Buffered(1)
    return pl.pallas_call(
        _ffn_kernel,
        grid_spec=pltpu.PrefetchScalarGridSpec(
            num_scalar_prefetch=2,
            grid=(n_tiles,),
            in_specs=[pl.BlockSpec((tm, AUG_W), lambda i, te, na: (jnp.where(i < na[0], i, 0), 0)),
                      pl.BlockSpec((1, 1, d, h2), lambda i, te, na: (layer, te[i], 0, 0), pipeline_mode=once),
                      pl.BlockSpec((1, 1, h2), lambda i, te, na: (te[i], 0, 0)),
                      pl.BlockSpec((1, 1, h2 // 2, d), lambda i, te, na: (layer, te[i], 0, 0), pipeline_mode=once),
                      pl.BlockSpec((1, 1, d), lambda i, te, na: (te[i], 0, 0)),
                      pl.BlockSpec(perm.shape, lambda i, te, na: (0, 0), pipeline_mode=once)],
            out_specs=pl.BlockSpec((tm, d), lambda i, te, na: (i, 0)),
            scratch_shapes=[pltpu.VMEM((d, h2), BF16), pltpu.VMEM((h2 // 2, d), BF16)]),
        out_shape=jax.ShapeDtypeStruct((n_tiles * tm, d), BF16),
        compiler_params=_cparams("arbitrary"),
        name="moe_ffn",
    )(tile_e, n_active, xs, w1, b1p, w2, b2, perm)


def _combine_kernel(cntp_ref, loc_ref, goff_ref, nch_ref, pt_ref, x_ref, g2_ref, lg_ref, lb_ref, ys_ref,
                    o_ref, stage_ref, sems):
    j = pl.program_id(0)
    nt = pl.num_programs(0)
    buf = j % 2

    def in_copy(b):
        return lambda dst, src, rows: pltpu.make_async_copy(
            ys_ref.at[pl.ds(src, rows)], stage_ref.at[b].at[pl.ds(dst, rows)], sems.at[b])

    @pl.when(j == 0)
    def _():
        stage_ref[...] = jnp.zeros(stage_ref.shape, BF16)
        _group_copies(j, cntp_ref, loc_ref, goff_ref, in_copy(buf))

    @pl.when(j + 1 < nt)
    def _():
        _group_copies(j + 1, cntp_ref, loc_ref, goff_ref, in_copy(1 - buf))

    _wait_copies(nch_ref, j, in_copy(buf))
    y = _dot(pt_ref[0], stage_ref[buf])
    o_ref[0] = _ln(DEEPNORM_ALPHA * x_ref[0] + g2_ref[0] * y) * lg_ref[...] + lb_ref[...]


def _combine(pt, ys, x1, g2, lg, lb, cntp, loc, goff, nch):
    npb, l, d = x1.shape
    t = MOE_T
    nt_b = l // t
    tile = lambda w: pl.BlockSpec((1, t, w), lambda j, *_: (j // nt_b, j % nt_b, 0))
    const = lambda a: pl.BlockSpec(a.shape, lambda j, *_: (0,) * a.ndim)
    return pl.pallas_call(
        _combine_kernel,
        grid_spec=pltpu.PrefetchScalarGridSpec(
            num_scalar_prefetch=4,
            grid=(npb * nt_b,),
            in_specs=[pl.BlockSpec((1, t, MOE_SLOTS), lambda j, *_: (j, 0, 0)), tile(d),
                      pl.BlockSpec((1, 1, d), lambda j, *_: (j // nt_b, 0, 0)), const(lg), const(lb),
                      pl.BlockSpec(memory_space=pl.ANY)],
            out_specs=tile(d),
            scratch_shapes=[pltpu.VMEM((2, MOE_SLOTS, d), BF16), pltpu.SemaphoreType.DMA((2,))]),
        out_shape=jax.ShapeDtypeStruct((npb, l, d), F32),
        compiler_params=_cparams("arbitrary"),
        name="moe_combine",
    )(cntp, loc, goff, nch, pt, x1, g2, lg, lb, ys)


def _moe(x1, sh2, sc2, g2, rw, rb, w1, b1p, w2, b2, perm, lg, lb, npb, layer):
    _, l, d = x1.shape
    n_tok = npb * l
    nt = n_tok // MOE_T
    xaug, sel, cnt = _route(x1, sh2, sc2, rw, rb, npb)

    i32 = jnp.int32
    cnt = cnt[:, 0, :N_EXPERTS].astype(i32)
    cntp = (cnt + (MOE_G - 1)) // MOE_G * MOE_G
    loc = jnp.cumsum(cntp, axis=1) - cntp
    tot = cntp.sum(axis=0)
    seg = (tot + (MOE_TM - 1)) // MOE_TM * MOE_TM
    seg_start = jnp.cumsum(seg) - seg
    goff = seg_start[None, :] + jnp.cumsum(cntp, axis=0) - cntp
    s_max = TOP_K * n_tok + nt * N_EXPERTS * (MOE_G - 1) + N_EXPERTS * (MOE_TM - MOE_G)
    n_tiles = -(-s_max // MOE_TM)
    cum_tiles = jnp.cumsum(seg // MOE_TM)
    tile_e = jnp.minimum((jnp.arange(n_tiles, dtype=i32)[:, None] >= cum_tiles[None, :]).astype(i32).sum(axis=1),
                         N_EXPERTS - 1)
    n_active = cum_tiles[-1:].astype(i32)
    gran = cntp // MOE_G
    nch = jnp.stack([(gran // 2).sum(axis=1), (gran % 2).sum(axis=1)], axis=1).reshape(-1).astype(i32)
    locv = jnp.pad(loc.astype(F32), ((0, 0), (0, LANES - N_EXPERTS)))[:, None, :]
    flat = lambda a: a.reshape(-1).astype(i32)

    tail = jnp.concatenate([seg_start + tot, n_active * MOE_TM]).astype(i32)
    tailn = jnp.concatenate([(seg - tot) // MOE_G, n_tiles - n_active]).astype(i32)
    pt, xs = _dispatch(xaug, sel, locv, flat(cntp), flat(loc), flat(goff), nch, tail, tailn, n_tiles * MOE_TM)
    ys = _ffn(xs, tile_e, n_active, w1, b1p, w2, b2, perm, n_tiles, layer)
    return _combine(pt, ys, x1, g2, lg, lb, flat(cntp), flat(loc), flat(goff), nch)


def _deinterleave(n):
    return np.concatenate([np.arange(0, n, 2), np.arange(1, n, 2)])


def _rope_tables(l):
    rows = l // GRID_W
    r = np.repeat(np.arange(rows, dtype=np.float32), GRID_W)
    col = np.tile(np.arange(GRID_W, dtype=np.float32), rows)
    axis_dim = ATT_HEAD_DIM // 2
    inv_freq = (ROPE_THETA ** (-np.arange(0, axis_dim, 2, dtype=np.float32) / axis_dim)).astype(np.float32)
    ang = np.concatenate([r[:, None] * inv_freq, col[:, None] * inv_freq], axis=-1)
    c, s = np.cos(ang).astype(np.float32), np.sin(ang).astype(np.float32)
    z = np.zeros_like(s)
    rc = np.tile(np.concatenate([c, c], -1), (1, 2))
    rs1 = np.tile(np.concatenate([-s, z], -1), (1, 2))
    rs2 = np.tile(np.concatenate([z, s], -1), (1, 2))
    ident = (np.ones_like(rc), np.zeros_like(rc), np.zeros_like(rc))
    return tuple(jnp.asarray(np.stack([a, b])) for a, b in zip((rc, rs1, rs2), ident))


def kernel(x, c, ctx, c_ctx, w_mod, b_mod, w_in, att_q_norm, att_k_norm, gla_wdec_f, gla_bdec_f, gla_wdec_b, gla_bdec_b, gla_out_norm, hy_conv_w, hy_conv_b, hy_fw1, hy_fb1, hy_fw2, hy_fb2, hy_fw3, hy_fb3, hy_skip, w_branch_a, w_branch_b, w_branch_c, w_out, ln1_g, ln1_b, router_w, router_b, exp_w1, exp_b1, exp_w2, exp_b2, ln2_g, ln2_b):
    b, l, d = x.shape
    lc = ctx.shape[1]
    assert d == D_MODEL and (b * lc) % l == 0 and l % lc == 0 and lc % GLA_CHUNK == 0
    pc = (b * lc) // l
    p = b + pc

    perm64 = _deinterleave(ATT_HEAD_DIM)
    q_cols = np.concatenate([np.concatenate([64 * j + perm64, 64 * (j + 4) + perm64]) for j in range(4)])
    k_cols = ATT_Q_W + np.concatenate([perm64, 64 + perm64])
    v_cols = ATT_Q_W + ATT_KV_W + np.arange(ATT_KV_W)
    att_cols = np.concatenate([q_cols, k_cols, v_cols])
    o0 = ATT_Q_W + 2 * ATT_KV_W
    gla_cols = o0 + np.concatenate([np.arange(0, 2 * GLA_K_W + GLA_V_W + 2 * GLA_RANK)])
    og0 = o0 + 2 * GLA_K_W + GLA_V_W + 2 * GLA_RANK
    hy0 = og0 + GLA_V_W
    gt0 = hy0 + 3 * HY_WIDTH
    gain_perm = np.concatenate([perm64, perm64])
    bd = jnp.asarray(np.kron(np.eye(2), np.ones((64, 64))), dtype=BF16)
    rope = _rope_tables(l)
    fwd_l, inv_l = _dft_tables(l)
    fwd_c, inv_c = _dft_tables(lc)
    cmap = np.concatenate([np.arange(b), np.full(pc, b)])
    hh = EXPERT_HIDDEN // 2
    perm_np = np.zeros((2 * hh, 2 * hh), np.float32)
    perm_np[2 * np.arange(hh), np.arange(hh)] = 1.0
    perm_np[2 * np.arange(hh) + 1, hh + np.arange(hh)] = 1.0
    perm = jnp.asarray(perm_np, dtype=BF16)

    x_pair = (x, ctx.reshape(pc, l, d))
    cc = jnp.concatenate([c, c_ctx[None], jnp.zeros((7, d), F32)], axis=0)

    for i in range(DEPTH):
        with_ctx = i < DEPTH - 1
        npb = p if with_ctx else b
        mod = _modulation(cc, w_mod[i], b_mod[i])[cmap]
        sh1, sc1, g1, sh2, sc2, g2 = [m[:, None, :] for m in jnp.split(mod, 6, axis=-1)]

        wi = w_in[i]
        watt = wi[:, att_cols].astype(BF16)
        wgla = jnp.pad(wi[:, gla_cols], ((0, 0), (0, GLA_GROUP_W - gla_cols.size))).astype(BF16)
        wog = wi[:, og0:hy0].astype(BF16)
        why = wi[:, hy0:gt0].astype(BF16)
        wgt = wi[:, gt0:].astype(BF16)
        qg = att_q_norm[i][gain_perm].reshape(1, LANES)
        kg = att_k_norm[i][gain_perm].reshape(1, LANES)
        qn, kn, v, gla, og, hy = _inproj(*x_pair, sh1, sc1, (watt, wgla, wog, why), qg, kg, rope, bd, b)

        o_a = _attention(qn, kn, v, b, lc, with_ctx)

        wdec = jnp.zeros((LANES, 2 * GLA_K_W), F32)
        wdec = wdec.at[0:GLA_RANK, 0:GLA_K_W].set(gla_wdec_f[i])
        wdec = wdec.at[GLA_RANK:2 * GLA_RANK, GLA_K_W:].set(gla_wdec_b[i])
        bdec = jnp.concatenate([gla_bdec_f[i], gla_bdec_b[i]]).reshape(1, -1)
        o_b, o_b_ctx = _gla(gla, og, wdec, bdec, gla_out_norm[i].reshape(1, -1), b, lc)

        hre, him = _hy_filter(l, hy_fw1[i], hy_fb1[i], hy_fw2[i], hy_fb2[i], hy_fw3[i], hy_fb3[i], fwd_l)
        u_h, x0_h, t_h = _hy_pre(hy, hy_conv_w[i], hy_conv_b[i], hy_skip[i], b, l, b)
        y_h = _hy_conv(u_h, fwd_l, inv_l, hre, him)
        if with_ctx:
            hre_c, him_c = _hy_filter(lc, hy_fw1[i], hy_fb1[i], hy_fw2[i], hy_fb2[i], hy_fw3[i], hy_fb3[i], fwd_c)
            u_c, x0_c, t_c = _hy_pre(hy, hy_conv_w[i], hy_conv_b[i], hy_skip[i], b, lc, b)
            y_c = _hy_conv(u_c, fwd_c, inv_c, hre_c, him_c)
            fold = lambda a_ctx: a_ctx.reshape(pc, l, a_ctx.shape[-1])
            pairs = [(o_b, fold(o_b_ctx)), (y_h, fold(y_c)), (x0_h, fold(x0_c)), (t_h, fold(t_c))]
        else:
            pairs = [(a, a) for a in (o_b, y_h, x0_h, t_h)]

        wa_rows = np.concatenate([np.concatenate([64 * j + np.arange(64), 64 * (j + 4) + np.arange(64)])
                                  for j in range(4)])
        x1 = _merge(x_pair, o_a, *pairs, sh1, sc1, g1, wgt,
                    w_branch_a[i][wa_rows].astype(BF16), w_branch_b[i].astype(BF16),
                    w_branch_c[i].astype(BF16), w_out[i].astype(BF16),
                    ln1_g[i].reshape(1, d), ln1_b[i].reshape(1, d), npb)

        rw = jnp.pad(router_w[i], ((0, 0), (0, LANES - N_EXPERTS)))
        rb = jnp.concatenate([router_b[i], jnp.full((LANES - N_EXPERTS,), NEG_BIG, F32)]).reshape(1, LANES)
        b1p = jnp.concatenate([exp_b1[i][..., 0::2], exp_b1[i][..., 1::2]], axis=-1)[:, None, :]
        x_all = _moe(x1, sh2, sc2, g2, rw, rb, exp_w1, b1p, exp_w2, exp_b2[i][:, None, :], perm,
                     ln2_g[i].reshape(1, d), ln2_b[i].reshape(1, d), npb, i)
        x_pair = (x_all, x_all)
    return x_all
```

```python
import functools
import math

import numpy as np
import jax
import jax.numpy as jnp
from jax import lax
from jax.experimental import pallas as pl
from jax.experimental.pallas import tpu as pltpu

F32 = jnp.float32
BF16 = jnp.bfloat16

D_MODEL = 1024
DEPTH = 2
GRID_W = 64
ATT_HEADS = 8
ATT_KV_HEADS = 2
ATT_HEAD_DIM = 64
ROPE_THETA = 10000.0
GLA_HEADS = 4
GLA_DK = 64
GLA_DV = 128
GLA_RANK = 16
GLA_TAU = 16.0
GLA_CHUNK = 64
HY_WIDTH = 512
HY_POS_FREQS = 16
HY_DECAY_SLOW = math.log(1e-2) / 1.5
HY_DECAY_FAST = math.log(1e-2) / 0.3
N_EXPERTS = 32
TOP_K = 4
EXPERT_HIDDEN = D_MODEL
SWIGLU_LIMIT = 7.0
SWIGLU_ALPHA = 1.702
DEEPNORM_ALPHA = (2 * DEPTH) ** 0.25
LN_EPS = 1e-5
RMS_EPS = 1e-6

ATT_Q_W = ATT_HEADS * ATT_HEAD_DIM
ATT_KV_W = ATT_KV_HEADS * ATT_HEAD_DIM
GLA_K_W = GLA_HEADS * GLA_DK
GLA_V_W = GLA_HEADS * GLA_DV
GLA_GROUP_W = 1152
LANES = 128
NEG_BIG = -3.0e38

VMEM_LIMIT = 56 * 1024 * 1024


def _cparams(*sem):
    return pltpu.CompilerParams(dimension_semantics=sem, vmem_limit_bytes=VMEM_LIMIT)


def _dot(a, b):
    return jnp.dot(a, b, preferred_element_type=F32)


def _dot_nt(a, b):
    return lax.dot_general(a, b, (((1,), (1,)), ((), ())), preferred_element_type=F32)


def _dot_tn(a, b):
    return lax.dot_general(a, b, (((0,), (0,)), ((), ())), preferred_element_type=F32)


def _split2(x):
    hi = x.astype(BF16)
    lo = (x - hi.astype(F32)).astype(BF16)
    return hi, lo


def _split3(x):
    hi = x.astype(BF16)
    r = x - hi.astype(F32)
    mid = r.astype(BF16)
    lo = (r - mid.astype(F32)).astype(BF16)
    return hi, mid, lo


def _dot3(a, b):
    ah, al = _split2(a)
    bh, bl = _split2(b)
    return _dot(ah, bh) + (_dot(ah, bl) + _dot(al, bh))


def _ln(x):
    mu = jnp.mean(x, axis=-1, keepdims=True)
    xc = x - mu
    var = jnp.mean(xc * xc, axis=-1, keepdims=True)
    return xc * lax.rsqrt(var + LN_EPS)


def _sigmoid(x):
    return 1.0 / (1.0 + jnp.exp(-x))


def _mod_kernel(c_ref, w_ref, b_ref, o_ref):
    c = c_ref[...]
    o_ref[...] = _dot3(c * _sigmoid(c), w_ref[...]) + b_ref[...]


def _modulation(cc, w_mod, b_mod):
    rows, d = cc.shape
    n = w_mod.shape[1]
    tn = 1024
    return pl.pallas_call(
        _mod_kernel,
        grid=(n // tn,),
        in_specs=[pl.BlockSpec((rows, d), lambda j: (0, 0)),
                  pl.BlockSpec((d, tn), lambda j: (0, j)),
                  pl.BlockSpec((1, tn), lambda j: (0, j))],
        out_specs=pl.BlockSpec((rows, tn), lambda j: (0, j)),
        out_shape=jax.ShapeDtypeStruct((rows, n), F32),
        compiler_params=_cparams("arbitrary"),
        name="modulation",
    )(cc, w_mod, b_mod.reshape(1, n))


def _head_norm_rope(t, gain, rc, rs1, rs2, bd):
    ss = _dot((t * t).astype(BF16), bd)
    tn = t * lax.rsqrt(ss * (1.0 / ATT_HEAD_DIM) + RMS_EPS) * gain
    return tn * rc + pltpu.roll(tn, 96, 1) * rs1 + pltpu.roll(tn, 32, 1) * rs2


def _pair_specs(tm, w, n_split, n_i):
    first = pl.BlockSpec((1, tm, w), lambda b, i: (jnp.minimum(b, n_split - 1), jnp.where(b < n_split, i, n_i - 1), 0))
    rest = pl.BlockSpec((1, tm, w), lambda b, i: (jnp.maximum(b - n_split, 0), jnp.where(b >= n_split, i, 0), 0))
    return [first, rest]


def _pick(use_rest, first_ref, rest_ref):
    return jnp.where(use_rest, rest_ref[0], first_ref[0])


def _inproj_kernel(x_ref, xr_ref, sh_ref, sc_ref, watt_ref, wgla_ref, wog_ref, why_ref,
                   qg_ref, kg_ref, rc_ref, rs1_ref, rs2_ref, bd_ref,
                   qn_ref, kn_ref, v_ref, gla_ref, og_ref, hy_ref, *, n_split):
    x = _pick(pl.program_id(0) >= n_split, x_ref, xr_ref)
    u = _ln(x) * (1.0 + sc_ref[0]) + sh_ref[0]
    ub = u.astype(BF16)
    rc, rs1, rs2, bd = rc_ref[0], rs1_ref[0], rs2_ref[0], bd_ref[...]
    att = _dot(ub, watt_ref[...])
    for j in range(ATT_Q_W // LANES):
        t = att[:, LANES * j:LANES * (j + 1)]
        qn = _head_norm_rope(t, qg_ref[...], rc, rs1, rs2, bd) * (ATT_HEAD_DIM ** -0.5)
        qn_ref[0, :, LANES * j:LANES * (j + 1)] = qn.astype(BF16)
    kn = _head_norm_rope(att[:, ATT_Q_W:ATT_Q_W + LANES], kg_ref[...], rc, rs1, rs2, bd)
    kn_ref[0] = kn.astype(BF16)
    v_ref[0, :, 0:LANES] = att[:, ATT_Q_W + LANES:].astype(BF16)
    v_ref[0, :, LANES:2 * LANES] = jnp.ones((att.shape[0], LANES), BF16)
    gla_ref[0] = _dot(ub, wgla_ref[...])
    og_ref[0] = _dot(ub, wog_ref[...]).astype(BF16)
    hy_ref[0] = _dot(ub, why_ref[...]).astype(BF16)


def _inproj(x_first, x_rest, sh, sc, wts, qg, kg, rope, bd, n_lat):
    n_split, l, d = x_first.shape
    p = sh.shape[0]
    tm = min(256, l)
    watt, wgla, wog, why = wts
    rc, rs1, rs2 = rope
    const = lambda a: pl.BlockSpec(a.shape, lambda b, i: (0,) * a.ndim)
    row = lambda w: pl.BlockSpec((1, tm, w), lambda b, i: (b, i, 0))
    tab = pl.BlockSpec((1, tm, LANES), lambda b, i: (jnp.where(b >= n_lat, 1, 0), i, 0))
    vec = pl.BlockSpec((1, 1, d), lambda b, i: (b, 0, 0))
    widths = (ATT_Q_W, LANES, 2 * LANES, GLA_GROUP_W, GLA_V_W, 3 * HY_WIDTH)
    dtypes = (BF16, BF16, BF16, F32, BF16, BF16)
    return pl.pallas_call(
        functools.partial(_inproj_kernel, n_split=n_split),
        grid=(p, l // tm),
        in_specs=_pair_specs(tm, d, n_split, l // tm) + [
            vec, vec, const(watt), const(wgla), const(wog), const(why),
            const(qg), const(kg), tab, tab, tab, const(bd)],
        out_specs=[row(w) for w in widths],
        out_shape=[jax.ShapeDtypeStruct((p, l, w), dt) for w, dt in zip(widths, dtypes)],
        compiler_params=_cparams("parallel", "arbitrary"),
        name="inproj",
    )(x_first, x_rest, sh, sc, watt, wgla, wog, why, qg, kg, rc, rs1, rs2, bd)


def _attend_group(qg, ks, vs, lo):
    zero = jnp.zeros_like(qg)
    outs = []
    for qh in (jnp.where(lo, qg, zero), jnp.where(lo, zero, qg)):
        ss = [_dot_nt(qh, k) for k in ks]
        m = ss[0].max(axis=-1, keepdims=True)
        for s in ss[1:]:
            m = jnp.maximum(m, s.max(axis=-1, keepdims=True))
        acc = 0.0
        for s, v in zip(ss, vs):
            acc = acc + _dot(jnp.exp((s - m).astype(BF16)), v)
        outs.append(acc[:, 0:LANES] / acc[:, LANES:2 * LANES])
    return jnp.where(lo, outs[0], outs[1])


def _attn_kernel(q_ref, k_ref, v_ref, kc_ref, vc_ref, o_ref, *, n_lat_tiles, with_ctx):
    i = pl.program_id(1)
    tq = q_ref.shape[1]
    lo = lax.broadcasted_iota(jnp.int32, (tq, LANES), 1) < ATT_HEAD_DIM

    def run(ks, vs):
        for j in range(ATT_Q_W // LANES):
            o = _attend_group(q_ref[0, :, LANES * j:LANES * (j + 1)], ks, vs, lo)
            o_ref[0, :, LANES * j:LANES * (j + 1)] = o.astype(BF16)

    if with_ctx:
        @pl.when(i < n_lat_tiles)
        def _():
            run([k_ref[0], kc_ref[0]], [v_ref[0], vc_ref[0]])

        @pl.when(i == n_lat_tiles)
        def _():
            run([kc_ref[0]], [vc_ref[0]])
    else:
        run([k_ref[0], kc_ref[0]], [v_ref[0], vc_ref[0]])


def _attention(qn, kn, v, n_lat, lc, with_ctx):
    p, l, _ = qn.shape
    tq = lc
    r = l // lc
    nt = l // tq
    ctx_idx = lambda b: (n_lat + b // r, b % r, 0)
    if with_ctx:
        qmap = lambda b, i: (jnp.where(i < nt, b, n_lat + b // r), jnp.where(i < nt, i, b % r), 0)
    else:
        qmap = lambda b, i: (b, i, 0)
    return pl.pallas_call(
        functools.partial(_attn_kernel, n_lat_tiles=nt, with_ctx=with_ctx),
        grid=(n_lat, nt + (1 if with_ctx else 0)),
        in_specs=[pl.BlockSpec((1, tq, ATT_Q_W), qmap),
                  pl.BlockSpec((1, l, LANES), lambda b, i: (b, 0, 0)),
                  pl.BlockSpec((1, l, 2 * LANES), lambda b, i: (b, 0, 0)),
                  pl.BlockSpec((1, lc, LANES), lambda b, i: ctx_idx(b)),
                  pl.BlockSpec((1, lc, 2 * LANES), lambda b, i: ctx_idx(b))],
        out_specs=pl.BlockSpec((1, tq, ATT_Q_W), qmap),
        out_shape=jax.ShapeDtypeStruct((p if with_ctx else n_lat, l, ATT_Q_W), BF16),
        compiler_params=_cparams("parallel", "arbitrary"),
        name="attention",
    )(qn, kn, v, kn, v)


def _gla_segment(src_ref, og_ref, out_ref, acc_ref, st_ref, qd_ref, ke_ref, vb_ref, dec_ref,
                 wdec_ref, bdec_ref, gn_ref, ls):
    c = GLA_CHUNK
    nc = ls // c
    rb = min(256, ls)

    bi = lax.broadcasted_iota(jnp.int32, (rb, rb), 0)
    bj = lax.broadcasted_iota(jnp.int32, (rb, rb), 1)
    same = (bi // c) == (bj // c)
    att_masks = (same & (bi >= bj), same & (bi <= bj))
    tri_blk = tuple(jnp.where(m, 1.0, 0.0).astype(BF16) for m in att_masks)
    ones_blk = jnp.where(same, 1.0, 0.0).astype(BF16)
    lo_b = lax.broadcasted_iota(jnp.int32, (rb, LANES), 1) < GLA_DK

    def prep(t, carry):
        r0 = pl.multiple_of(t * rb, rb)
        q = src_ref[0, pl.ds(r0, rb), 0:GLA_K_W] * (GLA_DK ** -0.5)
        k = src_ref[0, pl.ds(r0, rb), GLA_K_W:2 * GLA_K_W]
        vb = src_ref[0, pl.ds(r0, rb), 2 * GLA_K_W:2 * GLA_K_W + GLA_V_W].astype(BF16)
        vb_ref[pl.ds(r0, rb), :] = vb
        lr = src_ref[0, pl.ds(r0, rb), 2 * GLA_K_W + GLA_V_W:GLA_GROUP_W]
        la = jax.nn.log_sigmoid(_dot3(lr, wdec_ref[...]) + bdec_ref[...]) * (1.0 / GLA_TAU)
        for d in range(2):
            l0, l1 = _split2(la[:, d * GLA_K_W:(d + 1) * GLA_K_W])
            cum = _dot(tri_blk[d], l0) + _dot(tri_blk[d], l1)
            tot = _dot(ones_blk, l0) + _dot(ones_blk, l1)
            qd = (q * jnp.exp(cum)).astype(BF16)
            ki = (k * jnp.exp(-cum)).astype(BF16)
            qd_ref[d, pl.ds(r0, rb), :] = qd
            ke_ref[d, pl.ds(r0, rb), :] = (k * jnp.exp(tot - cum)).astype(BF16)
            dec_ref[d, pl.ds(r0, rb), :] = jnp.exp(tot)
            for h in range(GLA_HEADS):
                g, half = h // 2, h % 2
                sl = slice(LANES * g, LANES * (g + 1))
                zero = jnp.zeros_like(qd[:, sl])
                qh = jnp.where(lo_b, qd[:, sl], zero) if half == 0 else jnp.where(lo_b, zero, qd[:, sl])
                att = jnp.where(att_masks[d], _dot_nt(qh, ki[:, sl]), 0.0)
                o = _dot(att.astype(BF16), vb[:, GLA_DV * h:GLA_DV * (h + 1)])
                if d == 0:
                    acc_ref[pl.ds(r0, rb), GLA_DV * h:GLA_DV * (h + 1)] = o
                else:
                    acc_ref[pl.ds(r0, rb), GLA_DV * h:GLA_DV * (h + 1)] += o
        return carry

    lax.fori_loop(0, ls // rb, prep, 0, unroll=2 if (ls // rb) % 2 == 0 else 1)

    lo = lax.broadcasted_iota(jnp.int32, (c, LANES), 1) < GLA_DK

    def body(n, carry):
        for d, cn in ((0, n), (1, nc - 1 - n)):
            r0 = pl.multiple_of(cn * c, c)
            dec = dec_ref[d, pl.ds(r0, 1), :]
            for h in range(GLA_HEADS):
                g, half = h // 2, h % 2
                sl = slice(LANES * g, LANES * (g + 1))
                qd = qd_ref[d, pl.ds(r0, c), sl]
                zero = jnp.zeros_like(qd)
                qd = jnp.where(lo, qd, zero) if half == 0 else jnp.where(lo, zero, qd)
                vh = vb_ref[pl.ds(r0, c), GLA_DV * h:GLA_DV * (h + 1)]
                st = st_ref[d * GLA_HEADS + h]
                acc_ref[pl.ds(r0, c), GLA_DV * h:GLA_DV * (h + 1)] += _dot_nt(qd, st.astype(BF16))
                st_ref[d * GLA_HEADS + h] = dec[:, sl] * st + _dot_tn(vh, ke_ref[d, pl.ds(r0, c), sl])
        return carry

    lax.fori_loop(0, nc, body, 0, unroll=4 if nc % 4 == 0 else (2 if nc % 2 == 0 else 1))

    tr = min(256, ls)

    def fin(t, carry):
        r0 = pl.multiple_of(t * tr, tr)
        for h in range(GLA_HEADS):
            sl = slice(GLA_DV * h, GLA_DV * (h + 1))
            o = acc_ref[pl.ds(r0, tr), sl]
            y = o * lax.rsqrt(jnp.mean(o * o, axis=-1, keepdims=True) + RMS_EPS)
            y = y * gn_ref[...]
            og = og_ref[0, pl.ds(r0, tr), sl].astype(F32)
            out_ref[0, pl.ds(r0, tr), sl] = (y * (og * _sigmoid(og))).astype(BF16)
        return carry

    lax.fori_loop(0, ls // tr, fin, 0)


def _gla_kernel(gl_ref, glc_ref, og_ref, ogc_ref, wdec_ref, bdec_ref, gn_ref, o_ref, oc_ref,
                acc_ref, st_ref, qd_ref, ke_ref, vb_ref, dec_ref):
    st_ref[...] = jnp.zeros(st_ref.shape, F32)
    scratch = (acc_ref, st_ref, qd_ref, ke_ref, vb_ref, dec_ref)
    _gla_segment(glc_ref, ogc_ref, oc_ref, *scratch, wdec_ref, bdec_ref, gn_ref, glc_ref.shape[1])
    _gla_segment(gl_ref, og_ref, o_ref, *scratch, wdec_ref, bdec_ref, gn_ref, gl_ref.shape[1])


def _gla(gla, og, wdec, bdec, gn, n_lat, lc):
    p, l, _ = gla.shape
    r = l // lc
    ctx_idx = lambda b: (n_lat + b // r, b % r, 0)
    const = lambda a: pl.BlockSpec(a.shape, lambda b: (0,) * a.ndim)
    return pl.pallas_call(
        _gla_kernel,
        grid=(n_lat,),
        in_specs=[pl.BlockSpec((1, l, GLA_GROUP_W), lambda b: (b, 0, 0)),
                  pl.BlockSpec((1, lc, GLA_GROUP_W), ctx_idx),
                  pl.BlockSpec((1, l, GLA_V_W), lambda b: (b, 0, 0)),
                  pl.BlockSpec((1, lc, GLA_V_W), ctx_idx),
                  const(wdec), const(bdec), const(gn)],
        out_specs=[pl.BlockSpec((1, l, GLA_V_W), lambda b: (b, 0, 0)),
                   pl.BlockSpec((1, lc, GLA_V_W), lambda b: (b, 0, 0))],
        out_shape=[jax.ShapeDtypeStruct((n_lat, l, GLA_V_W), BF16),
                   jax.ShapeDtypeStruct((n_lat, lc, GLA_V_W), BF16)],
        scratch_shapes=[pltpu.VMEM((l, GLA_V_W), F32),
                        pltpu.VMEM((2 * GLA_HEADS, GLA_DV, LANES), F32),
                        pltpu.VMEM((2, l, GLA_K_W), BF16),
                        pltpu.VMEM((2, l, GLA_K_W), BF16), pltpu.VMEM((l, GLA_V_W), BF16),
                        pltpu.VMEM((2, l, GLA_K_W), F32)],
        compiler_params=_cparams("arbitrary"),
        name="gla",
    )(gla, gla, og, og, wdec, bdec, gn)


HY_FREQ_TILE = 512


def _dft_tables(l):
    n = 2 * l
    k = np.arange(l, dtype=np.int64)[:, None]
    t = np.arange(l, dtype=np.int64)[None, :]
    ang = 2.0 * np.pi * ((k * t) % n).astype(np.float64) / n
    fre = np.cos(ang)
    fim = -np.sin(ang)
    fim[0, :] = np.where(np.arange(l) % 2 == 0, 1.0, -1.0)
    tk = min(HY_FREQ_TILE, l)
    nk = l // tk
    fwd = np.concatenate([fre.reshape(nk, tk, l), fim.reshape(nk, tk, l)], axis=1)
    return jnp.asarray(fwd, dtype=BF16), jnp.asarray(fwd.transpose(0, 2, 1), dtype=BF16)


def _hy_filter_kernel(z_ref, win_ref, w1_ref, b1_ref, w2_ref, b2_ref, w3_ref, b3_ref, f_ref,
                      hre_ref, him_ref, hf_ref):
    j = pl.program_id(0)
    l = z_ref.shape[0]

    @pl.when(j == 0)
    def _():
        h = jnp.sin(_dot3(z_ref[...], w1_ref[...]) + b1_ref[...])
        h = jnp.sin(_dot3(h, w2_ref[...]) + b2_ref[...])
        h = _dot3(h, w3_ref[...]) + b3_ref[...]
        win = win_ref[...]
        hf = h[:, :HY_WIDTH] * win
        hb = h[:, HY_WIDTH:] * win
        row = lax.broadcasted_iota(jnp.int32, (l, HY_WIDTH), 0)
        hf_ref[:, :HY_WIDTH] = hf.astype(BF16)
        hf_ref[:, HY_WIDTH:] = jnp.where(row == 0, 0.0, hb).astype(BF16)

    tk = f_ref.shape[1] // 2
    ab = _dot(f_ref[0], hf_ref[...])
    a, b = ab[:tk], ab[tk:]
    first = (lax.broadcasted_iota(jnp.int32, (tk, HY_WIDTH), 0) == 0) & (j == 0)
    scl = jnp.where(first, 1.0 / (2 * l), 2.0 / (2 * l))
    hre_ref[...] = (a[:, :HY_WIDTH] + a[:, HY_WIDTH:]) * scl
    him_ref[...] = (b[:, :HY_WIDTH] + jnp.where(first, 1.0, -1.0) * b[:, HY_WIDTH:]) * scl


def _hy_filter(l, fw1, fb1, fw2, fb2, fw3, fb3, fwd):
    t = np.arange(l, dtype=np.float32)[:, None] / np.float32(l)
    freqs = np.arange(1, HY_POS_FREQS + 1, dtype=np.float32)
    z = np.concatenate([t, np.cos(2.0 * math.pi * freqs * t), np.sin(2.0 * math.pi * freqs * t)], axis=-1)
    z = np.pad(z, ((0, 0), (0, LANES - z.shape[1])))
    hid = fw2.shape[0]
    fw1 = jnp.pad(fw1, ((0, LANES - fw1.shape[0]), (0, LANES - hid)))
    fb1 = jnp.pad(fb1, (0, LANES - hid))
    fw2 = jnp.pad(fw2, ((0, LANES - hid), (0, LANES - hid)))
    fb2 = jnp.pad(fb2, (0, LANES - hid))
    fw3 = jnp.pad(fw3, ((0, LANES - hid), (0, 0)))
    deltas = np.abs(np.linspace(HY_DECAY_SLOW, HY_DECAY_FAST, HY_WIDTH, dtype=np.float32))
    win = np.exp(-t * deltas).astype(np.float32)
    nk, tk2, _ = fwd.shape
    tk = tk2 // 2
    const = lambda a: pl.BlockSpec(a.shape, lambda j: (0,) * a.ndim)
    args = (jnp.asarray(z), jnp.asarray(win), fw1, fb1.reshape(1, -1), fw2, fb2.reshape(1, -1),
            fw3, fb3.reshape(1, -1))
    return pl.pallas_call(
        _hy_filter_kernel,
        grid=(nk,),
        in_specs=[const(a) for a in args] + [pl.BlockSpec((1, tk2, l), lambda j: (j, 0, 0))],
        out_specs=[pl.BlockSpec((tk, HY_WIDTH), lambda j: (j, 0))] * 2,
        out_shape=[jax.ShapeDtypeStruct((l, HY_WIDTH), F32)] * 2,
        scratch_shapes=[pltpu.VMEM((l, 2 * HY_WIDTH), BF16)],
        compiler_params=_cparams("arbitrary"),
        name="hyena_filter",
    )(*args, fwd)


def _hy_pre_kernel(p0_ref, p1_ref, p2_ref, w0_ref, w1_ref, w2_ref, b0_ref, b1_ref, b2_ref, skip_ref,
                   u_ref, x0_ref, t_ref):
    ls, cw = p0_ref.shape[1], p0_ref.shape[2]
    row = lax.broadcasted_iota(jnp.int32, (ls, cw), 0)

    def conv(p_ref, w_ref, b_ref):
        p = p_ref[0].astype(F32)
        prev = jnp.where(row == 0, 0.0, pltpu.roll(p, 1, 0))
        nxt = jnp.where(row == ls - 1, 0.0, pltpu.roll(p, ls - 1, 0))
        return prev * w_ref[0:1, :] + p * w_ref[1:2, :] + nxt * w_ref[2:3, :] + b_ref[...]

    x0 = conv(p0_ref, w0_ref, b0_ref)
    u = conv(p1_ref, w1_ref, b1_ref) * conv(p2_ref, w2_ref, b2_ref)
    u_ref[0] = u.astype(BF16)
    x0_ref[0] = x0.astype(BF16)
    t_ref[0] = (x0 * u * skip_ref[...]).astype(BF16)


def _hy_pre(hy, conv_w, conv_b, skip, nseq, ls, n_lat):
    p, l, _ = hy.shape
    r = l // ls
    cw = 2 * LANES
    nj = HY_WIDTH // cw
    base = 0 if ls == l else n_lat
    src = lambda off: pl.BlockSpec((1, ls, cw), lambda b, j: (base + b // r, b % r, off * nj + j))
    wsp = lambda off: pl.BlockSpec((3, cw), lambda b, j: (0, off * nj + j))
    bsp = lambda off: pl.BlockSpec((1, cw), lambda b, j: (0, off * nj + j))
    out = pl.BlockSpec((1, ls, cw), lambda b, j: (b, 0, j))
    cb = conv_b.reshape(1, -1)
    return pl.pallas_call(
        _hy_pre_kernel,
        grid=(nseq, nj),
        in_specs=[src(0), src(1), src(2), wsp(0), wsp(1), wsp(2), bsp(0), bsp(1), bsp(2), bsp(0)],
        out_specs=[out] * 3,
        out_shape=[jax.ShapeDtypeStruct((nseq, ls, HY_WIDTH), BF16)] * 3,
        compiler_params=_cparams("parallel", "arbitrary"),
        name="hyena_pre",
    )(hy, hy, hy, conv_w, conv_w, conv_w, cb, cb, cb, skip.reshape(1, -1))


def _hy_conv_kernel(u_ref, f_ref, i_ref, hre_ref, him_ref, y_ref):
    j = pl.program_id(1)
    tk = hre_ref.shape[0]
    uu = _dot(f_ref[0], u_ref[0])
    ure, uim = uu[:tk], uu[tk:]
    hre, him = hre_ref[...], him_ref[...]
    first = (lax.broadcasted_iota(jnp.int32, ure.shape, 0) == 0) & (j == 0)
    yre = jnp.where(first, ure * hre, ure * hre - uim * him)
    yim = jnp.where(first, uim * him, ure * him + uim * hre)
    y = _dot(i_ref[0], jnp.concatenate([yre.astype(BF16), yim.astype(BF16)], axis=0))

    @pl.when(j == 0)
    def _():
        y_ref[0] = y

    @pl.when(j > 0)
    def _():
        y_ref[0] += y


def _hy_conv(u, fwd, inv, hre, him):
    nseq, ls, _ = u.shape
    nk, tk2, _ = fwd.shape
    tk = tk2 // 2
    return pl.pallas_call(
        _hy_conv_kernel,
        grid=(nseq, nk),
        in_specs=[pl.BlockSpec((1, ls, HY_WIDTH), lambda b, j: (b, 0, 0)),
                  pl.BlockSpec((1, tk2, ls), lambda b, j: (j, 0, 0)),
                  pl.BlockSpec((1, ls, tk2), lambda b, j: (j, 0, 0)),
                  pl.BlockSpec((tk, HY_WIDTH), lambda b, j: (j, 0)),
                  pl.BlockSpec((tk, HY_WIDTH), lambda b, j: (j, 0))],
        out_specs=pl.BlockSpec((1, ls, HY_WIDTH), lambda b, j: (b, 0, 0)),
        out_shape=jax.ShapeDtypeStruct((nseq, ls, HY_WIDTH), F32),
        compiler_params=_cparams("parallel", "arbitrary"),
        name="hyena_conv",
    )(u, fwd, inv, hre, him)


def _merge_kernel(x_ref, xr_ref, oa_ref, ob_ref, obr_ref, y_ref, yr_ref, x0_ref, x0r_ref, t_ref, tr_ref,
                  sh_ref, sc_ref, g1_ref, wgt_ref, wa_ref, wb_ref, wc_ref, wo_ref, lg_ref, lb_ref, o_ref,
                  *, n_split):
    d = x_ref.shape[2]
    rest = pl.program_id(0) >= n_split
    x = _pick(rest, x_ref, xr_ref)
    u = _ln(x) * (1.0 + sc_ref[0]) + sh_ref[0]
    ob = _pick(rest, ob_ref, obr_ref)
    oc = (_pick(rest, x0_ref, x0r_ref).astype(F32) * _pick(rest, y_ref, yr_ref)
          + _pick(rest, t_ref, tr_ref).astype(F32)).astype(BF16)
    gt = _dot(u.astype(BF16), wgt_ref[...])
    m = (_sigmoid(gt[:, 0:d]) * _dot(oa_ref[0], wa_ref[...])
         + _sigmoid(gt[:, d:2 * d]) * _dot(ob, wb_ref[...])
         + _sigmoid(gt[:, 2 * d:3 * d]) * _dot(oc, wc_ref[...]))
    mix = _dot(m.astype(BF16), wo_ref[...])
    o_ref[0] = _ln(DEEPNORM_ALPHA * x + g1_ref[0] * mix) * lg_ref[...] + lb_ref[...]


def _merge(x, o_a, o_b, y, x0, t, sh, sc, g1, wgt, wa, wb, wc, wo, lg, lb, npb):
    n_split, l, d = x[0].shape
    tm = min(512, l)
    n_i = l // tm
    row = lambda w: pl.BlockSpec((1, tm, w), lambda b, i: (b, i, 0))
    const = lambda a: pl.BlockSpec(a.shape, lambda b, i: (0,) * a.ndim)
    pair = lambda w: _pair_specs(tm, w, n_split, n_i)
    vec = pl.BlockSpec((1, 1, d), lambda b, i: (b, 0, 0))
    wgt_spec = pl.BlockSpec(wgt.shape, lambda b, i: (0, 0), pipeline_mode=pl.Buffered(1))
    return pl.pallas_call(
        functools.partial(_merge_kernel, n_split=n_split),
        grid=(npb, n_i),
        in_specs=pair(d) + [row(ATT_Q_W)] + pair(GLA_V_W) + pair(HY_WIDTH) + pair(HY_WIDTH) + pair(HY_WIDTH) + [
            vec, vec, vec, wgt_spec,
            const(wa), const(wb), const(wc), const(wo), const(lg), const(lb)],
        out_specs=row(d),
        out_shape=jax.ShapeDtypeStruct((npb, l, d), F32),
        compiler_params=_cparams("parallel", "arbitrary"),
        name="merge",
    )(*x, o_a, *o_b, *y, *x0, *t, sh, sc, g1, wgt, wa, wb, wc, wo, lg, lb)


MOE_T = 256
MOE_G = 16
MOE_TM = 1024
MOE_SLOTS = TOP_K * MOE_T + 512
AUG_W = D_MODEL + LANES


def _route_kernel(x_ref, sh_ref, sc_ref, rw_ref, rb_ref, xa_ref, sel_ref, cnt_ref):
    d = x_ref.shape[2]
    for h in range(x_ref.shape[1] // MOE_T):
        rows = slice(h * MOE_T, (h + 1) * MOE_T)
        u = _ln(x_ref[0, rows, :]) * (1.0 + sc_ref[0]) + sh_ref[0]
        xa_ref[0, rows, 0:d] = u.astype(BF16)
        logits = _dot3(u, rw_ref[...]) + rb_ref[...]
        lane = lax.broadcasted_iota(jnp.int32, logits.shape, 1).astype(F32)
        work = logits
        hots, vals = [], []
        for _ in range(TOP_K):
            m = work.max(axis=-1, keepdims=True)
            idx = jnp.where(work == m, lane, float(LANES)).min(axis=-1, keepdims=True)
            hot = lane == idx
            hots.append(hot)
            vals.append(m)
            work = jnp.where(hot, NEG_BIG, work)
        es = [jnp.exp(v - vals[0]) for v in vals]
        den = es[0] + es[1] + es[2] + es[3]
        comb = jnp.zeros_like(logits)
        sel = jnp.zeros_like(logits)
        for hot, e in zip(hots, es):
            comb = comb + jnp.where(hot, e / den, 0.0)
            sel = sel + jnp.where(hot, 1.0, 0.0)
        c0, c1, c2 = _split3(comb)
        aug = c0.astype(F32) + pltpu.roll(c1.astype(F32), 32, 1) + pltpu.roll(c2.astype(F32), 64, 1)
        xa_ref[0, rows, d:d + LANES] = aug.astype(BF16)
        sel_ref[0, rows, :] = sel.astype(BF16)
        cnt_ref[h] = sel.sum(axis=0, keepdims=True)


def _route(x_all, sh, sc, rw, rb, npb):
    _, l, d = x_all.shape
    per = next(n for n in (4, 2, 1) if l % (n * MOE_T) == 0)
    t = per * MOE_T
    n_i = l // t
    row = lambda w: pl.BlockSpec((1, t, w), lambda b, i: (b, i, 0))
    vec = pl.BlockSpec((1, 1, d), lambda b, i: (b, 0, 0))
    const = lambda a: pl.BlockSpec(a.shape, lambda b, i: (0,) * a.ndim)
    return pl.pallas_call(
        _route_kernel,
        grid=(npb, n_i),
        in_specs=[row(d), vec, vec, const(rw), const(rb)],
        out_specs=[row(AUG_W), row(LANES), pl.BlockSpec((per, 1, LANES), lambda b, i: (b * n_i + i, 0, 0))],
        out_shape=[jax.ShapeDtypeStruct((npb, l, AUG_W), BF16), jax.ShapeDtypeStruct((npb, l, LANES), BF16),
                   jax.ShapeDtypeStruct((npb * n_i * per, 1, LANES), F32)],
        compiler_params=_cparams("parallel", "arbitrary"),
        name="route",
    )(x_all, sh, sc, rw, rb)


def _group_copies(j, cntp_ref, loc_ref, goff_ref, make_copy):
    for e in range(N_EXPERTS):
        n = lax.shift_right_logical(cntp_ref[j * N_EXPERTS + e], int(math.log2(MOE_G)))
        loc = loc_ref[j * N_EXPERTS + e]
        off = goff_ref[j * N_EXPERTS + e]
        n2 = lax.shift_right_logical(n, 1)

        def per_pair(c, carry, loc=loc, off=off):
            make_copy(pl.multiple_of(loc + c * (2 * MOE_G), MOE_G),
                      pl.multiple_of(off + c * (2 * MOE_G), MOE_G), 2 * MOE_G).start()
            return carry

        lax.fori_loop(0, n2, per_pair, 0)

        @pl.when((n & 1) == 1)
        def _(loc=loc, off=off, n2=n2):
            make_copy(pl.multiple_of(loc + n2 * (2 * MOE_G), MOE_G),
                      pl.multiple_of(off + n2 * (2 * MOE_G), MOE_G), MOE_G).start()


def _wait_copies(nch_ref, j, make_copy):
    for k, rows in ((0, 2 * MOE_G), (1, MOE_G)):
        def wait_one(c, carry, rows=rows):
            make_copy(0, 0, rows).wait()
            return carry

        lax.fori_loop(0, nch_ref[2 * j + k], wait_one, 0)


def _dispatch_kernel(cntp_ref, loc_ref, goff_ref, nch_ref, tail_ref, tailn_ref,
                     xa_ref, sel_ref, locv_ref, pt_ref, xs_ref, stage_ref, zero_ref, sems):
    j = pl.program_id(0)
    nt = pl.num_programs(0)
    buf = j % 2
    sem = sems.at[buf]
    stage = stage_ref.at[buf]
    t = MOE_T

    def out_copy(src, dst, rows):
        return pltpu.make_async_copy(stage.at[pl.ds(src, rows)], xs_ref.at[pl.ds(dst, rows)], sem)

    @pl.when(j >= 2)
    def _():
        _wait_copies(nch_ref, jnp.maximum(j - 2, 0), out_copy)

    sel = sel_ref[0]
    ri = lax.broadcasted_iota(jnp.int32, (t, t), 0)
    ci = lax.broadcasted_iota(jnp.int32, (t, t), 1)
    rank = _dot(jnp.where(ci < ri, 1.0, 0.0).astype(BF16), sel)
    dest = rank + locv_ref[0]
    lane = lax.broadcasted_iota(jnp.int32, (t, LANES), 1).astype(F32)
    slot = lax.broadcasted_iota(jnp.int32, (t, MOE_SLOTS), 1)
    avail = sel.astype(F32)
    pt = jnp.zeros((t, MOE_SLOTS), F32)
    for _ in range(TOP_K):
        ek = jnp.where(avail > 0.0, lane, float(LANES)).min(axis=-1, keepdims=True)
        hot = lane == ek
        dk = jnp.where(hot, dest, 0.0).sum(axis=-1, keepdims=True).astype(jnp.int32)
        pt = pt + jnp.where(slot == dk, 1.0, 0.0)
        avail = jnp.where(hot, 0.0, avail)
    ptb = pt.astype(BF16)
    pt_ref[0] = ptb
    stage[...] = _dot_tn(ptb, xa_ref[0]).astype(BF16)
    _group_copies(j, cntp_ref, loc_ref, goff_ref, out_copy)

    @pl.when(j == nt - 1)
    def _():
        _wait_copies(nch_ref, j, out_copy)

        @pl.when(j >= 1)
        def _():
            other = sems.at[1 - buf]
            _wait_copies(nch_ref, jnp.maximum(j - 1, 0), lambda s, d, rows: pltpu.make_async_copy(
                stage_ref.at[1 - buf].at[pl.ds(s, rows)], xs_ref.at[pl.ds(d, rows)], other))

        zero_ref[...] = jnp.zeros(zero_ref.shape, BF16)

        def zcopy(dst):
            return pltpu.make_async_copy(zero_ref.at[pl.ds(0, MOE_G)], xs_ref.at[pl.ds(dst, MOE_G)], sem)

        def zcopy_big(dst):
            return pltpu.make_async_copy(zero_ref, xs_ref.at[pl.ds(dst, MOE_TM)], sem)

        def per_expert(e, tot):
            n = tailn_ref[e]
            base = tail_ref[e]

            def per_chunk(c, carry):
                zcopy(pl.multiple_of(base + c * MOE_G, MOE_G)).start()
                return carry

            lax.fori_loop(0, n, per_chunk, 0)
            return tot + n

        ztotal = lax.fori_loop(0, N_EXPERTS, per_expert, 0)

        def zwait(c, carry):
            zcopy(0).wait()
            return carry

        lax.fori_loop(0, ztotal, zwait, 0)

        nbig = tailn_ref[N_EXPERTS]
        big0 = tail_ref[N_EXPERTS]

        def big_start(c, carry):
            zcopy_big(pl.multiple_of(big0 + c * MOE_TM, MOE_TM)).start()
            return carry

        def big_wait(c, carry):
            zcopy_big(0).wait()
            return carry

        lax.fori_loop(0, nbig, big_start, 0)
        lax.fori_loop(0, nbig, big_wait, 0)


def _dispatch(xaug, sel, locv, cntp, loc, goff, nch, tail, tailn, s_rows):
    npb, l, _ = xaug.shape
    t = MOE_T
    nt_b = l // t
    nt = npb * nt_b
    tile = lambda w: pl.BlockSpec((1, t, w), lambda j, *_: (j // nt_b, j % nt_b, 0))
    return pl.pallas_call(
        _dispatch_kernel,
        grid_spec=pltpu.PrefetchScalarGridSpec(
            num_scalar_prefetch=6,
            grid=(nt,),
            in_specs=[tile(AUG_W), tile(LANES), pl.BlockSpec((1, 1, LANES), lambda j, *_: (j, 0, 0))],
            out_specs=[pl.BlockSpec((1, t, MOE_SLOTS), lambda j, *_: (j, 0, 0)),
                       pl.BlockSpec(memory_space=pl.ANY)],
            scratch_shapes=[pltpu.VMEM((2, MOE_SLOTS, AUG_W), BF16), pltpu.VMEM((MOE_TM, AUG_W), BF16),
                            pltpu.SemaphoreType.DMA((2,))]),
        out_shape=[jax.ShapeDtypeStruct((nt, t, MOE_SLOTS), BF16), jax.ShapeDtypeStruct((s_rows, AUG_W), BF16)],
        compiler_params=_cparams("arbitrary"),
        name="moe_dispatch",
    )(cntp, loc, goff, nch, tail, tailn, xaug, sel, locv)


def _ffn_kernel(te_ref, na_ref, xs_ref, w1_ref, b1_ref, w2_ref, b2_ref, perm_ref, ys_ref, w1p_ref, w2b_ref):
    i = pl.program_id(0)
    e = te_ref[i]
    active = i < na_ref[0]
    fresh = jnp.logical_or(i == 0, e != te_ref[jnp.maximum(i - 1, 0)])
    hh = EXPERT_HIDDEN // 2

    @pl.when(jnp.logical_and(active, fresh))
    def _():
        for c in range(2):
            blk = w1_ref[0, 0, :, 2 * hh * c:2 * hh * (c + 1)].astype(BF16)
            r = _dot(blk, perm_ref[...])
            w1p_ref[:, hh * c:hh * (c + 1)] = r[:, :hh].astype(BF16)
            w1p_ref[:, EXPERT_HIDDEN + hh * c:EXPERT_HIDDEN + hh * (c + 1)] = r[:, hh:].astype(BF16)
        w2b_ref[...] = w2_ref[0, 0].astype(BF16)

    @pl.when(active)
    def _():
        x = xs_ref[:, 0:D_MODEL]
        extra = xs_ref[:, D_MODEL:AUG_W].astype(F32)
        lane = lax.broadcasted_iota(jnp.int32, extra.shape, 1)
        mine = jnp.logical_and(lane % N_EXPERTS == e, lane < 3 * N_EXPERTS)
        wslot = jnp.where(mine, extra, 0.0).sum(axis=-1, keepdims=True)
        h = _dot(x, w1p_ref[...]) + b1_ref[0]
        g = jnp.minimum(h[:, :EXPERT_HIDDEN], SWIGLU_LIMIT)
        up = jnp.clip(h[:, EXPERT_HIDDEN:], -SWIGLU_LIMIT, SWIGLU_LIMIT)
        act = g * _sigmoid(SWIGLU_ALPHA * g) * (up + 1.0)
        y = _dot(act.astype(BF16), w2b_ref[...]) + b2_ref[0]
        ys_ref[...] = (wslot * y).astype(BF16)

    @pl.when(jnp.logical_not(active))
    def _():
        ys_ref[...] = jnp.zeros(ys_ref.shape, BF16)


def _ffn(xs, tile_e, n_active, w1, b1p, w2, b2, perm, n_tiles, layer):
    tm = MOE_TM
    d, h2 = w1.shape[2], w1.shape[3]
    once = pl.Buffered(1)
    return pl.pallas_call(
        _ffn_kernel,
        grid_spec=pltpu.PrefetchScalarGridSpec(
            num_scalar_prefetch=2,
            grid=(n_tiles,),
            in_specs=[pl.BlockSpec((tm, AUG_W), lambda i, te, na: (jnp.where(i < na[0], i, 0), 0)),
                      pl.BlockSpec((1, 1, d, h2), lambda i, te, na: (layer, te[i], 0, 0), pipeline_mode=once),
                      pl.BlockSpec((1, 1, h2), lambda i, te, na: (te[i], 0, 0)),
                      pl.BlockSpec((1, 1, h2 // 2, d), lambda i, te, na: (layer, te[i], 0, 0), pipeline_mode=once),
                      pl.BlockSpec((1, 1, d), lambda i, te, na: (te[i], 0, 0)),
                      pl.BlockSpec(perm.shape, lambda i, te, na: (0, 0), pipeline_mode=once)],
            out_specs=pl.BlockSpec((tm, d), lambda i, te, na: (i, 0)),
            scratch_shapes=[pltpu.VMEM((d, h2), BF16), pltpu.VMEM((h2 // 2, d), BF16)]),
        out_shape=jax.ShapeDtypeStruct((n_tiles * tm, d), BF16),
        compiler_params=_cparams("arbitrary"),
        name="moe_ffn",
    )(tile_e, n_active, xs, w1, b1p, w2, b2, perm)


def _combine_kernel(cntp_ref, loc_ref, goff_ref, nch_ref, pt_ref, x_ref, g2_ref, lg_ref, lb_ref, ys_ref,
                    o_ref, stage_ref, sems):
    j = pl.program_id(0)
    nt = pl.num_programs(0)
    buf = j % 2

    def in_copy(b):
        return lambda dst, src, rows: pltpu.make_async_copy(
            ys_ref.at[pl.ds(src, rows)], stage_ref.at[b].at[pl.ds(dst, rows)], sems.at[b])

    @pl.when(j == 0)
    def _():
        stage_ref[...] = jnp.zeros(stage_ref.shape, BF16)
        _group_copies(j, cntp_ref, loc_ref, goff_ref, in_copy(buf))

    @pl.when(j + 1 < nt)
    def _():
        _group_copies(j + 1, cntp_ref, loc_ref, goff_ref, in_copy(1 - buf))

    _wait_copies(nch_ref, j, in_copy(buf))
    y = _dot(pt_ref[0], stage_ref[buf])
    o_ref[0] = _ln(DEEPNORM_ALPHA * x_ref[0] + g2_ref[0] * y) * lg_ref[...] + lb_ref[...]


def _combine(pt, ys, x1, g2, lg, lb, cntp, loc, goff, nch):
    npb, l, d = x1.shape
    t = MOE_T
    nt_b = l // t
    tile = lambda w: pl.BlockSpec((1, t, w), lambda j, *_: (j // nt_b, j % nt_b, 0))
    const = lambda a: pl.BlockSpec(a.shape, lambda j, *_: (0,) * a.ndim)
    return pl.pallas_call(
        _combine_kernel,
        grid_spec=pltpu.PrefetchScalarGridSpec(
            num_scalar_prefetch=4,
            grid=(npb * nt_b,),
            in_specs=[pl.BlockSpec((1, t, MOE_SLOTS), lambda j, *_: (j, 0, 0)), tile(d),
                      pl.BlockSpec((1, 1, d), lambda j, *_: (j // nt_b, 0, 0)), const(lg), const(lb),
                      pl.BlockSpec(memory_space=pl.ANY)],
            out_specs=tile(d),
            scratch_shapes=[pltpu.VMEM((2, MOE_SLOTS, d), BF16), pltpu.SemaphoreType.DMA((2,))]),
        out_shape=jax.ShapeDtypeStruct((npb, l, d), F32),
        compiler_params=_cparams("arbitrary"),
        name="moe_combine",
    )(cntp, loc, goff, nch, pt, x1, g2, lg, lb, ys)


def _moe(x1, sh2, sc2, g2, rw, rb, w1, b1p, w2, b2, perm, lg, lb, npb, layer):
    _, l, d = x1.shape
    n_tok = npb * l
    nt = n_tok // MOE_T
    xaug, sel, cnt = _route(x1, sh2, sc2, rw, rb, npb)

    i32 = jnp.int32
    cnt = cnt[:, 0, :N_EXPERTS].astype(i32)
    cntp = (cnt + (MOE_G - 1)) // MOE_G * MOE_G
    loc = jnp.cumsum(cntp, axis=1) - cntp
    tot = cntp.sum(axis=0)
    seg = (tot + (MOE_TM - 1)) // MOE_TM * MOE_TM
    seg_start = jnp.cumsum(seg) - seg
    goff = seg_start[None, :] + jnp.cumsum(cntp, axis=0) - cntp
    s_max = TOP_K * n_tok + nt * N_EXPERTS * (MOE_G - 1) + N_EXPERTS * (MOE_TM - MOE_G)
    n_tiles = -(-s_max // MOE_TM)
    cum_tiles = jnp.cumsum(seg // MOE_TM)
    tile_e = jnp.minimum((jnp.arange(n_tiles, dtype=i32)[:, None] >= cum_tiles[None, :]).astype(i32).sum(axis=1),
                         N_EXPERTS - 1)
    n_active = cum_tiles[-1:].astype(i32)
    gran = cntp // MOE_G
    nch = jnp.stack([(gran // 2).sum(axis=1), (gran % 2).sum(axis=1)], axis=1).reshape(-1).astype(i32)
    locv = jnp.pad(loc.astype(F32), ((0, 0), (0, LANES - N_EXPERTS)))[:, None, :]
    flat = lambda a: a.reshape(-1).astype(i32)

    tail = jnp.concatenate([seg_start + tot, n_active * MOE_TM]).astype(i32)
    tailn = jnp.concatenate([(seg - tot) // MOE_G, n_tiles - n_active]).astype(i32)
    pt, xs = _dispatch(xaug, sel, locv, flat(cntp), flat(loc), flat(goff), nch, tail, tailn, n_tiles * MOE_TM)
    ys = _ffn(xs, tile_e, n_active, w1, b1p, w2, b2, perm, n_tiles, layer)
    return _combine(pt, ys, x1, g2, lg, lb, flat(cntp), flat(loc), flat(goff), nch)


def _deinterleave(n):
    return np.concatenate([np.arange(0, n, 2), np.arange(1, n, 2)])


def _rope_tables(l):
    rows = l // GRID_W
    r = np.repeat(np.arange(rows, dtype=np.float32), GRID_W)
    col = np.tile(np.arange(GRID_W, dtype=np.float32), rows)
    axis_dim = ATT_HEAD_DIM // 2
    inv_freq = (ROPE_THETA ** (-np.arange(0, axis_dim, 2, dtype=np.float32) / axis_dim)).astype(np.float32)
    ang = np.concatenate([r[:, None] * inv_freq, col[:, None] * inv_freq], axis=-1)
    c, s = np.cos(ang).astype(np.float32), np.sin(ang).astype(np.float32)
    z = np.zeros_like(s)
    rc = np.tile(np.concatenate([c, c], -1), (1, 2))
    rs1 = np.tile(np.concatenate([-s, z], -1), (1, 2))
    rs2 = np.tile(np.concatenate([z, s], -1), (1, 2))
    ident = (np.ones_like(rc), np.zeros_like(rc), np.zeros_like(rc))
    return tuple(jnp.asarray(np.stack([a, b])) for a, b in zip((rc, rs1, rs2), ident))


def kernel(x, c, ctx, c_ctx, w_mod, b_mod, w_in, att_q_norm, att_k_norm, gla_wdec_f, gla_bdec_f, gla_wdec_b, gla_bdec_b, gla_out_norm, hy_conv_w, hy_conv_b, hy_fw1, hy_fb1, hy_fw2, hy_fb2, hy_fw3, hy_fb3, hy_skip, w_branch_a, w_branch_b, w_branch_c, w_out, ln1_g, ln1_b, router_w, router_b, exp_w1, exp_b1, exp_w2, exp_b2, ln2_g, ln2_b):
    b, l, d = x.shape
    lc = ctx.shape[1]
    assert d == D_MODEL and (b * lc) % l == 0 and l % lc == 0 and lc % GLA_CHUNK == 0
    pc = (b * lc) // l
    p = b + pc

    perm64 = _deinterleave(ATT_HEAD_DIM)
    q_cols = np.concatenate([np.concatenate([64 * j + perm64, 64 * (j + 4) + perm64]) for j in range(4)])
    k_cols = ATT_Q_W + np.concatenate([perm64, 64 + perm64])
    v_cols = ATT_Q_W + ATT_KV_W + np.arange(ATT_KV_W)
    att_cols = np.concatenate([q_cols, k_cols, v_cols])
    o0 = ATT_Q_W + 2 * ATT_KV_W
    gla_cols = o0 + np.concatenate([np.arange(0, 2 * GLA_K_W + GLA_V_W + 2 * GLA_RANK)])
    og0 = o0 + 2 * GLA_K_W + GLA_V_W + 2 * GLA_RANK
    hy0 = og0 + GLA_V_W
    gt0 = hy0 + 3 * HY_WIDTH
    gain_perm = np.concatenate([perm64, perm64])
    bd = jnp.asarray(np.kron(np.eye(2), np.ones((64, 64))), dtype=BF16)
    rope = _rope_tables(l)
    fwd_l, inv_l = _dft_tables(l)
    fwd_c, inv_c = _dft_tables(lc)
    cmap = np.concatenate([np.arange(b), np.full(pc, b)])
    hh = EXPERT_HIDDEN // 2
    perm_np = np.zeros((2 * hh, 2 * hh), np.float32)
    perm_np[2 * np.arange(hh), np.arange(hh)] = 1.0
    perm_np[2 * np.arange(hh) + 1, hh + np.arange(hh)] = 1.0
    perm = jnp.asarray(perm_np, dtype=BF16)

    x_pair = (x, ctx.reshape(pc, l, d))
    cc = jnp.concatenate([c, c_ctx[None], jnp.zeros((7, d), F32)], axis=0)

    for i in range(DEPTH):
        with_ctx = i < DEPTH - 1
        npb = p if with_ctx else b
        mod = _modulation(cc, w_mod[i], b_mod[i])[cmap]
        sh1, sc1, g1, sh2, sc2, g2 = [m[:, None, :] for m in jnp.split(mod, 6, axis=-1)]

        wi = w_in[i]
        watt = wi[:, att_cols].astype(BF16)
        wgla = jnp.pad(wi[:, gla_cols], ((0, 0), (0, GLA_GROUP_W - gla_cols.size))).astype(BF16)
        wog = wi[:, og0:hy0].astype(BF16)
        why = wi[:, hy0:gt0].astype(BF16)
        wgt = wi[:, gt0:].astype(BF16)
        qg = att_q_norm[i][gain_perm].reshape(1, LANES)
        kg = att_k_norm[i][gain_perm].reshape(1, LANES)
        qn, kn, v, gla, og, hy = _inproj(*x_pair, sh1, sc1, (watt, wgla, wog, why), qg, kg, rope, bd, b)

        o_a = _attention(qn, kn, v, b, lc, with_ctx)

        wdec = jnp.zeros((LANES, 2 * GLA_K_W), F32)
        wdec = wdec.at[0:GLA_RANK, 0:GLA_K_W].set(gla_wdec_f[i])
        wdec = wdec.at[GLA_RANK:2 * GLA_RANK, GLA_K_W:].set(gla_wdec_b[i])
        bdec = jnp.concatenate([gla_bdec_f[i], gla_bdec_b[i]]).reshape(1, -1)
        o_b, o_b_ctx = _gla(gla, og, wdec, bdec, gla_out_norm[i].reshape(1, -1), b, lc)

        hre, him = _hy_filter(l, hy_fw1[i], hy_fb1[i], hy_fw2[i], hy_fb2[i], hy_fw3[i], hy_fb3[i], fwd_l)
        u_h, x0_h, t_h = _hy_pre(hy, hy_conv_w[i], hy_conv_b[i], hy_skip[i], b, l, b)
        y_h = _hy_conv(u_h, fwd_l, inv_l, hre, him)
        if with_ctx:
            hre_c, him_c = _hy_filter(lc, hy_fw1[i], hy_fb1[i], hy_fw2[i], hy_fb2[i], hy_fw3[i], hy_fb3[i], fwd_c)
            u_c, x0_c, t_c = _hy_pre(hy, hy_conv_w[i], hy_conv_b[i], hy_skip[i], b, lc, b)
            y_c = _hy_conv(u_c, fwd_c, inv_c, hre_c, him_c)
            fold = lambda a_ctx: a_ctx.reshape(pc, l, a_ctx.shape[-1])
            pairs = [(o_b, fold(o_b_ctx)), (y_h, fold(y_c)), (x0_h, fold(x0_c)), (t_h, fold(t_c))]
        else:
            pairs = [(a, a) for a in (o_b, y_h, x0_h, t_h)]

        wa_rows = np.concatenate([np.concatenate([64 * j + np.arange(64), 64 * (j + 4) + np.arange(64)])
                                  for j in range(4)])
        x1 = _merge(x_pair, o_a, *pairs, sh1, sc1, g1, wgt,
                    w_branch_a[i][wa_rows].astype(BF16), w_branch_b[i].astype(BF16),
                    w_branch_c[i].astype(BF16), w_out[i].astype(BF16),
                    ln1_g[i].reshape(1, d), ln1_b[i].reshape(1, d), npb)

        rw = jnp.pad(router_w[i], ((0, 0), (0, LANES - N_EXPERTS)))
        rb = jnp.concatenate([router_b[i], jnp.full((LANES - N_EXPERTS,), NEG_BIG, F32)]).reshape(1, LANES)
        b1p = jnp.concatenate([exp_b1[i][..., 0::2], exp_b1[i][..., 1::2]], axis=-1)[:, None, :]
        x_all = _moe(x1, sh2, sc2, g2, rw, rb, exp_w1, b1p, exp_w2, exp_b2[i][:, None, :], perm,
                     ln2_g[i].reshape(1, d), ln2_b[i].reshape(1, d), npb, i)
        x_pair = (x_all, x_all)
    return x_all
```

```python
import functools
import math

import numpy as np
import jax
import jax.numpy as jnp
from jax import lax
from jax.experimental import pallas as pl
from jax.experimental.pallas import tpu as pltpu

F32 = jnp.float32
BF16 = jnp.bfloat16

D_MODEL = 1024
DEPTH = 2
GRID_W = 64
ATT_HEADS = 8
ATT_KV_HEADS = 2
ATT_HEAD_DIM = 64
ROPE_THETA = 10000.0
GLA_HEADS = 4
GLA_DK = 64
GLA_DV = 128
GLA_RANK = 16
GLA_TAU = 16.0
GLA_CHUNK = 64
HY_WIDTH = 512
HY_POS_FREQS = 16
HY_DECAY_SLOW = math.log(1e-2) / 1.5
HY_DECAY_FAST = math.log(1e-2) / 0.3
N_EXPERTS = 32
TOP_K = 4
EXPERT_HIDDEN = D_MODEL
SWIGLU_LIMIT = 7.0
SWIGLU_ALPHA = 1.702
DEEPNORM_ALPHA = (2 * DEPTH) ** 0.25
LN_EPS = 1e-5
RMS_EPS = 1e-6

ATT_Q_W = ATT_HEADS * ATT_HEAD_DIM
ATT_KV_W = ATT_KV_HEADS * ATT_HEAD_DIM
GLA_K_W = GLA_HEADS * GLA_DK
GLA_V_W = GLA_HEADS * GLA_DV
GLA_GROUP_W = 1152
LANES = 128
NEG_BIG = -3.0e38

VMEM_LIMIT = 56 * 1024 * 1024


def _cparams(*sem):
    return pltpu.CompilerParams(dimension_semantics=sem, vmem_limit_bytes=VMEM_LIMIT)


def _dot(a, b):
    return jnp.dot(a, b, preferred_element_type=F32)


def _dot_nt(a, b):
    return lax.dot_general(a, b, (((1,), (1,)), ((), ())), preferred_element_type=F32)


def _dot_tn(a, b):
    return lax.dot_general(a, b, (((0,), (0,)), ((), ())), preferred_element_type=F32)


def _split2(x):
    hi = x.astype(BF16)
    lo = (x - hi.astype(F32)).astype(BF16)
    return hi, lo


def _split3(x):
    hi = x.astype(BF16)
    r = x - hi.astype(F32)
    mid = r.astype(BF16)
    lo = (r - mid.astype(F32)).astype(BF16)
    return hi, mid, lo


def _dot3(a, b):
    ah, al = _split2(a)
    bh, bl = _split2(b)
    return _dot(ah, bh) + (_dot(ah, bl) + _dot(al, bh))


def _ln(x):
    mu = jnp.mean(x, axis=-1, keepdims=True)
    xc = x - mu
    var = jnp.mean(xc * xc, axis=-1, keepdims=True)
    return xc * lax.rsqrt(var + LN_EPS)


def _sigmoid(x):
    return 1.0 / (1.0 + jnp.exp(-x))


def _mod_kernel(c_ref, w_ref, b_ref, o_ref):
    c = c_ref[...]
    o_ref[...] = _dot3(c * _sigmoid(c), w_ref[...]) + b_ref[...]


def _modulation(cc, w_mod, b_mod):
    rows, d = cc.shape
    n = w_mod.shape[1]
    tn = 1024
    return pl.pallas_call(
        _mod_kernel,
        grid=(n // tn,),
        in_specs=[pl.BlockSpec((rows, d), lambda j: (0, 0)),
                  pl.BlockSpec((d, tn), lambda j: (0, j)),
                  pl.BlockSpec((1, tn), lambda j: (0, j))],
        out_specs=pl.BlockSpec((rows, tn), lambda j: (0, j)),
        out_shape=jax.ShapeDtypeStruct((rows, n), F32),
        compiler_params=_cparams("arbitrary"),
        name="modulation",
    )(cc, w_mod, b_mod.reshape(1, n))


def _head_norm_rope(t, gain, rc, rs1, rs2, bd):
    ss = _dot((t * t).astype(BF16), bd)
    tn = t * lax.rsqrt(ss * (1.0 / ATT_HEAD_DIM) + RMS_EPS) * gain
    return tn * rc + pltpu.roll(tn, 96, 1) * rs1 + pltpu.roll(tn, 32, 1) * rs2


def _pair_specs(tm, w, n_split, n_i):
    first = pl.BlockSpec((1, tm, w), lambda b, i: (jnp.minimum(b, n_split - 1), jnp.where(b < n_split, i, n_i - 1), 0))
    rest = pl.BlockSpec((1, tm, w), lambda b, i: (jnp.maximum(b - n_split, 0), jnp.where(b >= n_split, i, 0), 0))
    return [first, rest]


def _pick(use_rest, first_ref, rest_ref):
    return jnp.where(use_rest, rest_ref[0], first_ref[0])


def _inproj_kernel(x_ref, xr_ref, sh_ref, sc_ref, watt_ref, wgla_ref, wog_ref, why_ref,
                   qg_ref, kg_ref, rc_ref, rs1_ref, rs2_ref, bd_ref,
                   qn_ref, kn_ref, v_ref, gla_ref, og_ref, hy_ref, *, n_split):
    x = _pick(pl.program_id(0) >= n_split, x_ref, xr_ref)
    u = _ln(x) * (1.0 + sc_ref[0]) + sh_ref[0]
    ub = u.astype(BF16)
    rc, rs1, rs2, bd = rc_ref[0], rs1_ref[0], rs2_ref[0], bd_ref[...]
    att = _dot(ub, watt_ref[...])
    for j in range(ATT_Q_W // LANES):
        t = att[:, LANES * j:LANES * (j + 1)]
        qn = _head_norm_rope(t, qg_ref[...], rc, rs1, rs2, bd) * (ATT_HEAD_DIM ** -0.5)
        qn_ref[0, :, LANES * j:LANES * (j + 1)] = qn.astype(BF16)
    kn = _head_norm_rope(att[:, ATT_Q_W:ATT_Q_W + LANES], kg_ref[...], rc, rs1, rs2, bd)
    kn_ref[0] = kn.astype(BF16)
    v_ref[0, :, 0:LANES] = att[:, ATT_Q_W + LANES:].astype(BF16)
    v_ref[0, :, LANES:2 * LANES] = jnp.ones((att.shape[0], LANES), BF16)
    gla_ref[0] = _dot(ub, wgla_ref[...])
    og_ref[0] = _dot(ub, wog_ref[...]).astype(BF16)
    hy_ref[0] = _dot(ub, why_ref[...]).astype(BF16)


def _inproj(x_first, x_rest, sh, sc, wts, qg, kg, rope, bd, n_lat):
    n_split, l, d = x_first.shape
    p = sh.shape[0]
    tm = min(256, l)
    watt, wgla, wog, why = wts
    rc, rs1, rs2 = rope
    const = lambda a: pl.BlockSpec(a.shape, lambda b, i: (0,) * a.ndim)
    row = lambda w: pl.BlockSpec((1, tm, w), lambda b, i: (b, i, 0))
    tab = pl.BlockSpec((1, tm, LANES), lambda b, i: (jnp.where(b >= n_lat, 1, 0), i, 0))
    vec = pl.BlockSpec((1, 1, d), lambda b, i: (b, 0, 0))
    widths = (ATT_Q_W, LANES, 2 * LANES, GLA_GROUP_W, GLA_V_W, 3 * HY_WIDTH)
    dtypes = (BF16, BF16, BF16, F32, BF16, BF16)
    return pl.pallas_call(
        functools.partial(_inproj_kernel, n_split=n_split),
        grid=(p, l // tm),
        in_specs=_pair_specs(tm, d, n_split, l // tm) + [
            vec, vec, const(watt), const(wgla), const(wog), const(why),
            const(qg), const(kg), tab, tab, tab, const(bd)],
        out_specs=[row(w) for w in widths],
        out_shape=[jax.ShapeDtypeStruct((p, l, w), dt) for w, dt in zip(widths, dtypes)],
        compiler_params=_cparams("parallel", "arbitrary"),
        name="inproj",
    )(x_first, x_rest, sh, sc, watt, wgla, wog, why, qg, kg, rc, rs1, rs2, bd)


def _attend_group(qg, ks, vs, lo):
    zero = jnp.zeros_like(qg)
    outs = []
    for qh in (jnp.where(lo, qg, zero), jnp.where(lo, zero, qg)):
        ss = [_dot_nt(qh, k) for k in ks]
        m = ss[0].max(axis=-1, keepdims=True)
        for s in ss[1:]:
            m = jnp.maximum(m, s.max(axis=-1, keepdims=True))
        acc = 0.0
        for s, v in zip(ss, vs):
            acc = acc + _dot(jnp.exp((s - m).astype(BF16)), v)
        outs.append(acc[:, 0:LANES] / acc[:, LANES:2 * LANES])
    return jnp.where(lo, outs[0], outs[1])


def _attn_kernel(q_ref, k_ref, v_ref, kc_ref, vc_ref, o_ref, *, n_lat_tiles, with_ctx):
    i = pl.program_id(1)
    tq = q_ref.shape[1]
    lo = lax.broadcasted_iota(jnp.int32, (tq, LANES), 1) < ATT_HEAD_DIM

    def run(ks, vs):
        for j in range(ATT_Q_W // LANES):
            o = _attend_group(q_ref[0, :, LANES * j:LANES * (j + 1)], ks, vs, lo)
            o_ref[0, :, LANES * j:LANES * (j + 1)] = o.astype(BF16)

    if with_ctx:
        @pl.when(i < n_lat_tiles)
        def _():
            run([k_ref[0], kc_ref[0]], [v_ref[0], vc_ref[0]])

        @pl.when(i == n_lat_tiles)
        def _():
            run([kc_ref[0]], [vc_ref[0]])
    else:
        run([k_ref[0], kc_ref[0]], [v_ref[0], vc_ref[0]])


def _attention(qn, kn, v, n_lat, lc, with_ctx):
    p, l, _ = qn.shape
    tq = lc
    r = l // lc
    nt = l // tq
    ctx_idx = lambda b: (n_lat + b // r, b % r, 0)
    if with_ctx:
        qmap = lambda b, i: (jnp.where(i < nt, b, n_lat + b // r), jnp.where(i < nt, i, b % r), 0)
    else:
        qmap = lambda b, i: (b, i, 0)
    return pl.pallas_call(
        functools.partial(_attn_kernel, n_lat_tiles=nt, with_ctx=with_ctx),
        grid=(n_lat, nt + (1 if with_ctx else 0)),
        in_specs=[pl.BlockSpec((1, tq, ATT_Q_W), qmap),
                  pl.BlockSpec((1, l, LANES), lambda b, i: (b, 0, 0)),
                  pl.BlockSpec((1, l, 2 * LANES), lambda b, i: (b, 0, 0)),
                  pl.BlockSpec((1, lc, LANES), lambda b, i: ctx_idx(b)),
                  pl.BlockSpec((1, lc, 2 * LANES), lambda b, i: ctx_idx(b))],
        out_specs=pl.BlockSpec((1, tq, ATT_Q_W), qmap),
        out_shape=jax.ShapeDtypeStruct((p if with_ctx else n_lat, l, ATT_Q_W), BF16),
        compiler_params=_cparams("parallel", "arbitrary"),
        name="attention",
    )(qn, kn, v, kn, v)


def _gla_segment(src_ref, og_ref, out_ref, acc_ref, st_ref, qd_ref, ke_ref, vb_ref, dec_ref,
                 wdec_ref, bdec_ref, gn_ref, ls):
    c = GLA_CHUNK
    nc = ls // c
    rb = min(256, ls)

    bi = lax.broadcasted_iota(jnp.int32, (rb, rb), 0)
    bj = lax.broadcasted_iota(jnp.int32, (rb, rb), 1)
    same = (bi // c) == (bj // c)
    att_masks = (same & (bi >= bj), same & (bi <= bj))
    tri_blk = tuple(jnp.where(m, 1.0, 0.0).astype(BF16) for m in att_masks)
    ones_blk = jnp.where(same, 1.0, 0.0).astype(BF16)
    lo_b = lax.broadcasted_iota(jnp.int32, (rb, LANES), 1) < GLA_DK

    def prep(t, carry):
        r0 = pl.multiple_of(t * rb, rb)
        q = src_ref[0, pl.ds(r0, rb), 0:GLA_K_W] * (GLA_DK ** -0.5)
        k = src_ref[0, pl.ds(r0, rb), GLA_K_W:2 * GLA_K_W]
        vb = src_ref[0, pl.ds(r0, rb), 2 * GLA_K_W:2 * GLA_K_W + GLA_V_W].astype(BF16)
        vb_ref[pl.ds(r0, rb), :] = vb
        lr = src_ref[0, pl.ds(r0, rb), 2 * GLA_K_W + GLA_V_W:GLA_GROUP_W]
        la = jax.nn.log_sigmoid(_dot3(lr, wdec_ref[...]) + bdec_ref[...]) * (1.0 / GLA_TAU)
        for d in range(2):
            l0, l1 = _split2(la[:, d * GLA_K_W:(d + 1) * GLA_K_W])
            cum = _dot(tri_blk[d], l0) + _dot(tri_blk[d], l1)
            tot = _dot(ones_blk, l0) + _dot(ones_blk, l1)
            qd = (q * jnp.exp(cum)).astype(BF16)
            ki = (k * jnp.exp(-cum)).astype(BF16)
            qd_ref[d, pl.ds(r0, rb), :] = qd
            ke_ref[d, pl.ds(r0, rb), :] = (k * jnp.exp(tot - cum)).astype(BF16)
            dec_ref[d, pl.ds(r0, rb), :] = jnp.exp(tot)
            for h in range(GLA_HEADS):
                g, half = h // 2, h % 2
                sl = slice(LANES * g, LANES * (g + 1))
                zero = jnp.zeros_like(qd[:, sl])
                qh = jnp.where(lo_b, qd[:, sl], zero) if half == 0 else jnp.where(lo_b, zero, qd[:, sl])
                att = jnp.where(att_masks[d], _dot_nt(qh, ki[:, sl]), 0.0)
                o = _dot(att.astype(BF16), vb[:, GLA_DV * h:GLA_DV * (h + 1)])
                if d == 0:
                    acc_ref[pl.ds(r0, rb), GLA_DV * h:GLA_DV * (h + 1)] = o
                else:
                    acc_ref[pl.ds(r0, rb), GLA_DV * h:GLA_DV * (h + 1)] += o
        return carry

    lax.fori_loop(0, ls // rb, prep, 0, unroll=2 if (ls // rb) % 2 == 0 else 1)

    lo = lax.broadcasted_iota(jnp.int32, (c, LANES), 1) < GLA_DK

    def body(n, carry):
        for d, cn in ((0, n), (1, nc - 1 - n)):
            r0 = pl.multiple_of(cn * c, c)
            dec = dec_ref[d, pl.ds(r0, 1), :]
            for h in range(GLA_HEADS):
                g, half = h // 2, h % 2
                sl = slice(LANES * g, LANES * (g + 1))
                qd = qd_ref[d, pl.ds(r0, c), sl]
                zero = jnp.zeros_like(qd)
                qd = jnp.where(lo, qd, zero) if half == 0 else jnp.where(lo, zero, qd)
                vh = vb_ref[pl.ds(r0, c), GLA_DV * h:GLA_DV * (h + 1)]
                st = st_ref[d * GLA_HEADS + h]
                acc_ref[pl.ds(r0, c), GLA_DV * h:GLA_DV * (h + 1)] += _dot_nt(qd, st.astype(BF16))
                st_ref[d * GLA_HEADS + h] = dec[:, sl] * st + _dot_tn(vh, ke_ref[d, pl.ds(r0, c), sl])
        return carry

    lax.fori_loop(0, nc, body, 0, unroll=4 if nc % 4 == 0 else (2 if nc % 2 == 0 else 1))

    tr = min(256, ls)

    def fin(t, carry):
        r0 = pl.multiple_of(t * tr, tr)
        for h in range(GLA_HEADS):
            sl = slice(GLA_DV * h, GLA_DV * (h + 1))
            o = acc_ref[pl.ds(r0, tr), sl]
            y = o * lax.rsqrt(jnp.mean(o * o, axis=-1, keepdims=True) + RMS_EPS)
            y = y * gn_ref[...]
            og = og_ref[0, pl.ds(r0, tr), sl].astype(F32)
            out_ref[0, pl.ds(r0, tr), sl] = (y * (og * _sigmoid(og))).astype(BF16)
        return carry

    lax.fori_loop(0, ls // tr, fin, 0)


def _gla_kernel(gl_ref, glc_ref, og_ref, ogc_ref, wdec_ref, bdec_ref, gn_ref, o_ref, oc_ref,
                acc_ref, st_ref, qd_ref, ke_ref, vb_ref, dec_ref):
    st_ref[...] = jnp.zeros(st_ref.shape, F32)
    scratch = (acc_ref, st_ref, qd_ref, ke_ref, vb_ref, dec_ref)
    _gla_segment(glc_ref, ogc_ref, oc_ref, *scratch, wdec_ref, bdec_ref, gn_ref, glc_ref.shape[1])
    _gla_segment(gl_ref, og_ref, o_ref, *scratch, wdec_ref, bdec_ref, gn_ref, gl_ref.shape[1])


def _gla(gla, og, wdec, bdec, gn, n_lat, lc):
    p, l, _ = gla.shape
    r = l // lc
    ctx_idx = lambda b: (n_lat + b // r, b % r, 0)
    const = lambda a: pl.BlockSpec(a.shape, lambda b: (0,) * a.ndim)
    return pl.pallas_call(
        _gla_kernel,
        grid=(n_lat,),
        in_specs=[pl.BlockSpec((1, l, GLA_GROUP_W), lambda b: (b, 0, 0)),
                  pl.BlockSpec((1, lc, GLA_GROUP_W), ctx_idx),
                  pl.BlockSpec((1, l, GLA_V_W), lambda b: (b, 0, 0)),
                  pl.BlockSpec((1, lc, GLA_V_W), ctx_idx),
                  const(wdec), const(bdec), const(gn)],
        out_specs=[pl.BlockSpec((1, l, GLA_V_W), lambda b: (b, 0, 0)),
                   pl.BlockSpec((1, lc, GLA_V_W), lambda b: (b, 0, 0))],
        out_shape=[jax.ShapeDtypeStruct((n_lat, l, GLA_V_W), BF16),
                   jax.ShapeDtypeStruct((n_lat, lc, GLA_V_W), BF16)],
        scratch_shapes=[pltpu.VMEM((l, GLA_V_W), F32),
                        pltpu.VMEM((2 * GLA_HEADS, GLA_DV, LANES), F32),
                        pltpu.VMEM((2, l, GLA_K_W), BF16),
                        pltpu.VMEM((2, l, GLA_K_W), BF16), pltpu.VMEM((l, GLA_V_W), BF16),
                        pltpu.VMEM((2, l, GLA_K_W), F32)],
        compiler_params=_cparams("arbitrary"),
        name="gla",
    )(gla, gla, og, og, wdec, bdec, gn)


HY_FREQ_TILE = 512


def _dft_tables(l):
    n = 2 * l
    k = np.arange(l, dtype=np.int64)[:, None]
    t = np.arange(l, dtype=np.int64)[None, :]
    ang = 2.0 * np.pi * ((k * t) % n).astype(np.float64) / n
    fre = np.cos(ang)
    fim = -np.sin(ang)
    fim[0, :] = np.where(np.arange(l) % 2 == 0, 1.0, -1.0)
    tk = min(HY_FREQ_TILE, l)
    nk = l // tk
    fwd = np.concatenate([fre.reshape(nk, tk, l), fim.reshape(nk, tk, l)], axis=1)
    return jnp.asarray(fwd, dtype=BF16), jnp.asarray(fwd.transpose(0, 2, 1), dtype=BF16)


def _hy_filter_kernel(z_ref, win_ref, w1_ref, b1_ref, w2_ref, b2_ref, w3_ref, b3_ref, f_ref,
                      hre_ref, him_ref, hf_ref):
    j = pl.program_id(0)
    l = z_ref.shape[0]

    @pl.when(j == 0)
    def _():
        h = jnp.sin(_dot3(z_ref[...], w1_ref[...]) + b1_ref[...])
        h = jnp.sin(_dot3(h, w2_ref[...]) + b2_ref[...])
        h = _dot3(h, w3_ref[...]) + b3_ref[...]
        win = win_ref[...]
        hf = h[:, :HY_WIDTH] * win
        hb = h[:, HY_WIDTH:] * win
        row = lax.broadcasted_iota(jnp.int32, (l, HY_WIDTH), 0)
        hf_ref[:, :HY_WIDTH] = hf.astype(BF16)
        hf_ref[:, HY_WIDTH:] = jnp.where(row == 0, 0.0, hb).astype(BF16)

    tk = f_ref.shape[1] // 2
    ab = _dot(f_ref[0], hf_ref[...])
    a, b = ab[:tk], ab[tk:]
    first = (lax.broadcasted_iota(jnp.int32, (tk, HY_WIDTH), 0) == 0) & (j == 0)
    scl = jnp.where(first, 1.0 / (2 * l), 2.0 / (2 * l))
    hre_ref[...] = (a[:, :HY_WIDTH] + a[:, HY_WIDTH:]) * scl
    him_ref[...] = (b[:, :HY_WIDTH] + jnp.where(first, 1.0, -1.0) * b[:, HY_WIDTH:]) * scl


def _hy_filter(l, fw1, fb1, fw2, fb2, fw3, fb3, fwd):
    t = np.arange(l, dtype=np.float32)[:, None] / np.float32(l)
    freqs = np.arange(1, HY_POS_FREQS + 1, dtype=np.float32)
    z = np.concatenate([t, np.cos(2.0 * math.pi * freqs * t), np.sin(2.0 * math.pi * freqs * t)], axis=-1)
    z = np.pad(z, ((0, 0), (0, LANES - z.shape[1])))
    hid = fw2.shape[0]
    fw1 = jnp.pad(fw1, ((0, LANES - fw1.shape[0]), (0, LANES - hid)))
    fb1 = jnp.pad(fb1, (0, LANES - hid))
    fw2 = jnp.pad(fw2, ((0, LANES - hid), (0, LANES - hid)))
    fb2 = jnp.pad(fb2, (0, LANES - hid))
    fw3 = jnp.pad(fw3, ((0, LANES - hid), (0, 0)))
    deltas = np.abs(np.linspace(HY_DECAY_SLOW, HY_DECAY_FAST, HY_WIDTH, dtype=np.float32))
    win = np.exp(-t * deltas).astype(np.float32)
    nk, tk2, _ = fwd.shape
    tk = tk2 // 2
    const = lambda a: pl.BlockSpec(a.shape, lambda j: (0,) * a.ndim)
    args = (jnp.asarray(z), jnp.asarray(win), fw1, fb1.reshape(1, -1), fw2, fb2.reshape(1, -1),
            fw3, fb3.reshape(1, -1))
    return pl.pallas_call(
        _hy_filter_kernel,
        grid=(nk,),
        in_specs=[const(a) for a in args] + [pl.BlockSpec((1, tk2, l), lambda j: (j, 0, 0))],
        out_specs=[pl.BlockSpec((tk, HY_WIDTH), lambda j: (j, 0))] * 2,
        out_shape=[jax.ShapeDtypeStruct((l, HY_WIDTH), F32)] * 2,
        scratch_shapes=[pltpu.VMEM((l, 2 * HY_WIDTH), BF16)],
        compiler_params=_cparams("arbitrary"),
        name="hyena_filter",
    )(*args, fwd)


def _hy_pre_kernel(p0_ref, p1_ref, p2_ref, w0_ref, w1_ref, w2_ref, b0_ref, b1_ref, b2_ref, skip_ref,
                   u_ref, x0_ref, t_ref):
    ls, cw = p0_ref.shape[1], p0_ref.shape[2]
    row = lax.broadcasted_iota(jnp.int32, (ls, cw), 0)

    def conv(p_ref, w_ref, b_ref):
        p = p_ref[0].astype(F32)
        prev = jnp.where(row == 0, 0.0, pltpu.roll(p, 1, 0))
        nxt = jnp.where(row == ls - 1, 0.0, pltpu.roll(p, ls - 1, 0))
        return prev * w_ref[0:1, :] + p * w_ref[1:2, :] + nxt * w_ref[2:3, :] + b_ref[...]

    x0 = conv(p0_ref, w0_ref, b0_ref)
    u = conv(p1_ref, w1_ref, b1_ref) * conv(p2_ref, w2_ref, b2_ref)
    u_ref[0] = u.astype(BF16)
    x0_ref[0] = x0.astype(BF16)
    t_ref[0] = (x0 * u * skip_ref[...]).astype(BF16)


def _hy_pre(hy, conv_w, conv_b, skip, nseq, ls, n_lat):
    p, l, _ = hy.shape
    r = l // ls
    cw = 2 * LANES
    nj = HY_WIDTH // cw
    base = 0 if ls == l else n_lat
    src = lambda off: pl.BlockSpec((1, ls, cw), lambda b, j: (base + b // r, b % r, off * nj + j))
    wsp = lambda off: pl.BlockSpec((3, cw), lambda b, j: (0, off * nj + j))
    bsp = lambda off: pl.BlockSpec((1, cw), lambda b, j: (0, off * nj + j))
    out = pl.BlockSpec((1, ls, cw), lambda b, j: (b, 0, j))
    cb = conv_b.reshape(1, -1)
    return pl.pallas_call(
        _hy_pre_kernel,
        grid=(nseq, nj),
        in_specs=[src(0), src(1), src(2), wsp(0), wsp(1), wsp(2), bsp(0), bsp(1), bsp(2), bsp(0)],
        out_specs=[out] * 3,
        out_shape=[jax.ShapeDtypeStruct((nseq, ls, HY_WIDTH), BF16)] * 3,
        compiler_params=_cparams("parallel", "arbitrary"),
        name="hyena_pre",
    )(hy, hy, hy, conv_w, conv_w, conv_w, cb, cb, cb, skip.reshape(1, -1))


def _hy_conv_kernel(u_ref, f_ref, i_ref, hre_ref, him_ref, y_ref):
    j = pl.program_id(1)
    tk = hre_ref.shape[0]
    uu = _dot(f_ref[0], u_ref[0])
    ure, uim = uu[:tk], uu[tk:]
    hre, him = hre_ref[...], him_ref[...]
    first = (lax.broadcasted_iota(jnp.int32, ure.shape, 0) == 0) & (j == 0)
    yre = jnp.where(first, ure * hre, ure * hre - uim * him)
    yim = jnp.where(first, uim * him, ure * him + uim * hre)
    y = _dot(i_ref[0], jnp.concatenate([yre.astype(BF16), yim.astype(BF16)], axis=0))

    @pl.when(j == 0)
    def _():
        y_ref[0] = y

    @pl.when(j > 0)
    def _():
        y_ref[0] += y


def _hy_conv(u, fwd, inv, hre, him):
    nseq, ls, _ = u.shape
    nk, tk2, _ = fwd.shape
    tk = tk2 // 2
    return pl.pallas_call(
        _hy_conv_kernel,
        grid=(nseq, nk),
        in_specs=[pl.BlockSpec((1, ls, HY_WIDTH), lambda b, j: (b, 0, 0)),
                  pl.BlockSpec((1, tk2, ls), lambda b, j: (j, 0, 0)),
                  pl.BlockSpec((1, ls, tk2), lambda b, j: (j, 0, 0)),
                  pl.BlockSpec((tk, HY_WIDTH), lambda b, j: (j, 0)),
                  pl.BlockSpec((tk, HY_WIDTH), lambda b, j: (j, 0))],
        out_specs=pl.BlockSpec((1, ls, HY_WIDTH), lambda b, j: (b, 0, 0)),
        out_shape=jax.ShapeDtypeStruct((nseq, ls, HY_WIDTH), F32),
        compiler_params=_cparams("parallel", "arbitrary"),
        name="hyena_conv",
    )(u, fwd, inv, hre, him)


def _merge_kernel(x_ref, xr_ref, oa_ref, ob_ref, obr_ref, y_ref, yr_ref, x0_ref, x0r_ref, t_ref, tr_ref,
                  sh_ref, sc_ref, g1_ref, wgt_ref, wa_ref, wb_ref, wc_ref, wo_ref, lg_ref, lb_ref, o_ref,
                  *, n_split):
    d = x_ref.shape[2]
    rest = pl.program_id(0) >= n_split
    x = _pick(rest, x_ref, xr_ref)
    u = _ln(x) * (1.0 + sc_ref[0]) + sh_ref[0]
    ob = _pick(rest, ob_ref, obr_ref)
    oc = (_pick(rest, x0_ref, x0r_ref).astype(F32) * _pick(rest, y_ref, yr_ref)
          + _pick(rest, t_ref, tr_ref).astype(F32)).astype(BF16)
    gt = _dot(u.astype(BF16), wgt_ref[...])
    m = (_sigmoid(gt[:, 0:d]) * _dot(oa_ref[0], wa_ref[...])
         + _sigmoid(gt[:, d:2 * d]) * _dot(ob, wb_ref[...])
         + _sigmoid(gt[:, 2 * d:3 * d]) * _dot(oc, wc_ref[...]))
    mix = _dot(m.astype(BF16), wo_ref[...])
    o_ref[0] = _ln(DEEPNORM_ALPHA * x + g1_ref[0] * mix) * lg_ref[...] + lb_ref[...]


def _merge(x, o_a, o_b, y, x0, t, sh, sc, g1, wgt, wa, wb, wc, wo, lg, lb, npb):
    n_split, l, d = x[0].shape
    tm = min(512, l)
    n_i = l // tm
    row = lambda w: pl.BlockSpec((1, tm, w), lambda b, i: (b, i, 0))
    const = lambda a: pl.BlockSpec(a.shape, lambda b, i: (0,) * a.ndim)
    pair = lambda w: _pair_specs(tm, w, n_split, n_i)
    vec = pl.BlockSpec((1, 1, d), lambda b, i: (b, 0, 0))
    wgt_spec = pl.BlockSpec(wgt.shape, lambda b, i: (0, 0), pipeline_mode=pl.Buffered(1))
    return pl.pallas_call(
        functools.partial(_merge_kernel, n_split=n_split),
        grid=(npb, n_i),
        in_specs=pair(d) + [row(ATT_Q_W)] + pair(GLA_V_W) + pair(HY_WIDTH) + pair(HY_WIDTH) + pair(HY_WIDTH) + [
            vec, vec, vec, wgt_spec,
            const(wa), const(wb), const(wc), const(wo), const(lg), const(lb)],
        out_specs=row(d),
        out_shape=jax.ShapeDtypeStruct((npb, l, d), F32),
        compiler_params=_cparams("parallel", "arbitrary"),
        name="merge",
    )(*x, o_a, *o_b, *y, *x0, *t, sh, sc, g1, wgt, wa, wb, wc, wo, lg, lb)


MOE_T = 256
MOE_G = 16
MOE_TM = 1024
MOE_SLOTS = TOP_K * MOE_T + 512
AUG_W = D_MODEL + LANES


def _route_kernel(x_ref, sh_ref, sc_ref, rw_ref, rb_ref, xa_ref, sel_ref, cnt_ref):
    d = x_ref.shape[2]
    for h in range(x_ref.shape[1] // MOE_T):
        rows = slice(h * MOE_T, (h + 1) * MOE_T)
        u = _ln(x_ref[0, rows, :]) * (1.0 + sc_ref[0]) + sh_ref[0]
        xa_ref[0, rows, 0:d] = u.astype(BF16)
        logits = _dot3(u, rw_ref[...]) + rb_ref[...]
        lane = lax.broadcasted_iota(jnp.int32, logits.shape, 1).astype(F32)
        work = logits
        hots, vals = [], []
        for _ in range(TOP_K):
            m = work.max(axis=-1, keepdims=True)
            idx = jnp.where(work == m, lane, float(LANES)).min(axis=-1, keepdims=True)
            hot = lane == idx
            hots.append(hot)
            vals.append(m)
            work = jnp.where(hot, NEG_BIG, work)
        es = [jnp.exp(v - vals[0]) for v in vals]
        den = es[0] + es[1] + es[2] + es[3]
        comb = jnp.zeros_like(logits)
        sel = jnp.zeros_like(logits)
        for hot, e in zip(hots, es):
            comb = comb + jnp.where(hot, e / den, 0.0)
            sel = sel + jnp.where(hot, 1.0, 0.0)
        c0, c1, c2 = _split3(comb)
        aug = c0.astype(F32) + pltpu.roll(c1.astype(F32), 32, 1) + pltpu.roll(c2.astype(F32), 64, 1)
        xa_ref[0, rows, d:d + LANES] = aug.astype(BF16)
        sel_ref[0, rows, :] = sel.astype(BF16)
        cnt_ref[h] = sel.sum(axis=0, keepdims=True)


def _route(x_all, sh, sc, rw, rb, npb):
    _, l, d = x_all.shape
    per = next(n for n in (4, 2, 1) if l % (n * MOE_T) == 0)
    t = per * MOE_T
    n_i = l // t
    row = lambda w: pl.BlockSpec((1, t, w), lambda b, i: (b, i, 0))
    vec = pl.BlockSpec((1, 1, d), lambda b, i: (b, 0, 0))
    const = lambda a: pl.BlockSpec(a.shape, lambda b, i: (0,) * a.ndim)
    return pl.pallas_call(
        _route_kernel,
        grid=(npb, n_i),
        in_specs=[row(d), vec, vec, const(rw), const(rb)],
        out_specs=[row(AUG_W), row(LANES), pl.BlockSpec((per, 1, LANES), lambda b, i: (b * n_i + i, 0, 0))],
        out_shape=[jax.ShapeDtypeStruct((npb, l, AUG_W), BF16), jax.ShapeDtypeStruct((npb, l, LANES), BF16),
                   jax.ShapeDtypeStruct((npb * n_i * per, 1, LANES), F32)],
        compiler_params=_cparams("parallel", "arbitrary"),
        name="route",
    )(x_all, sh, sc, rw, rb)


def _group_copies(j, cntp_ref, loc_ref, goff_ref, make_copy):
    for e in range(N_EXPERTS):
        n = lax.shift_right_logical(cntp_ref[j * N_EXPERTS + e], int(math.log2(MOE_G)))
        loc = loc_ref[j * N_EXPERTS + e]
        off = goff_ref[j * N_EXPERTS + e]
        n2 = lax.shift_right_logical(n, 1)

        prio = e % 2

        def per_pair(c, carry, loc=loc, off=off, prio=prio):
            make_copy(pl.multiple_of(loc + c * (2 * MOE_G), MOE_G),
                      pl.multiple_of(off + c * (2 * MOE_G), MOE_G), 2 * MOE_G).start(priority=prio)
            return carry

        lax.fori_loop(0, n2, per_pair, 0)

        @pl.when((n & 1) == 1)
        def _(loc=loc, off=off, n2=n2, prio=prio):
            make_copy(pl.multiple_of(loc + n2 * (2 * MOE_G), MOE_G),
                      pl.multiple_of(off + n2 * (2 * MOE_G), MOE_G), MOE_G).start(priority=prio)


def _wait_copies(nch_ref, j, make_copy):
    for k, rows in ((0, 2 * MOE_G), (1, MOE_G)):
        def wait_one(c, carry, rows=rows):
            make_copy(0, 0, rows).wait()
            return carry

        lax.fori_loop(0, nch_ref[2 * j + k], wait_one, 0)


def _dispatch_kernel(cntp_ref, loc_ref, goff_ref, nch_ref, tail_ref, tailn_ref,
                     xa_ref, sel_ref, locv_ref, pt_ref, xs_ref, stage_ref, zero_ref, sems):
    j = pl.program_id(0)
    nt = pl.num_programs(0)
    buf = j % 2
    sem = sems.at[buf]
    stage = stage_ref.at[buf]
    t = MOE_T

    def out_copy(src, dst, rows):
        return pltpu.make_async_copy(stage.at[pl.ds(src, rows)], xs_ref.at[pl.ds(dst, rows)], sem)

    @pl.when(j >= 2)
    def _():
        _wait_copies(nch_ref, jnp.maximum(j - 2, 0), out_copy)

    sel = sel_ref[0]
    ri = lax.broadcasted_iota(jnp.int32, (t, t), 0)
    ci = lax.broadcasted_iota(jnp.int32, (t, t), 1)
    rank = _dot(jnp.where(ci < ri, 1.0, 0.0).astype(BF16), sel)
    dest = rank + locv_ref[0]
    lane = lax.broadcasted_iota(jnp.int32, (t, LANES), 1).astype(F32)
    slot = lax.broadcasted_iota(jnp.int32, (t, MOE_SLOTS), 1)
    avail = sel.astype(F32)
    pt = jnp.zeros((t, MOE_SLOTS), F32)
    for _ in range(TOP_K):
        ek = jnp.where(avail > 0.0, lane, float(LANES)).min(axis=-1, keepdims=True)
        hot = lane == ek
        dk = jnp.where(hot, dest, 0.0).sum(axis=-1, keepdims=True).astype(jnp.int32)
        pt = pt + jnp.where(slot == dk, 1.0, 0.0)
        avail = jnp.where(hot, 0.0, avail)
    ptb = pt.astype(BF16)
    pt_ref[0] = ptb
    stage[...] = _dot_tn(ptb, xa_ref[0]).astype(BF16)
    _group_copies(j, cntp_ref, loc_ref, goff_ref, out_copy)

    @pl.when(j == nt - 1)
    def _():
        _wait_copies(nch_ref, j, out_copy)

        @pl.when(j >= 1)
        def _():
            other = sems.at[1 - buf]
            _wait_copies(nch_ref, jnp.maximum(j - 1, 0), lambda s, d, rows: pltpu.make_async_copy(
                stage_ref.at[1 - buf].at[pl.ds(s, rows)], xs_ref.at[pl.ds(d, rows)], other))

        zero_ref[...] = jnp.zeros(zero_ref.shape, BF16)

        def zcopy(dst):
            return pltpu.make_async_copy(zero_ref.at[pl.ds(0, MOE_G)], xs_ref.at[pl.ds(dst, MOE_G)], sem)

        def zcopy_big(dst):
            return pltpu.make_async_copy(zero_ref, xs_ref.at[pl.ds(dst, MOE_TM)], sem)

        def per_expert(e, tot):
            n = tailn_ref[e]
            base = tail_ref[e]

            def per_chunk(c, carry):
                zcopy(pl.multiple_of(base + c * MOE_G, MOE_G)).start()
                return carry

            lax.fori_loop(0, n, per_chunk, 0)
            return tot + n

        ztotal = lax.fori_loop(0, N_EXPERTS, per_expert, 0)

        def zwait(c, carry):
            zcopy(0).wait()
            return carry

        lax.fori_loop(0, ztotal, zwait, 0)

        nbig = tailn_ref[N_EXPERTS]
        big0 = tail_ref[N_EXPERTS]

        def big_start(c, carry):
            zcopy_big(pl.multiple_of(big0 + c * MOE_TM, MOE_TM)).start()
            return carry

        def big_wait(c, carry):
            zcopy_big(0).wait()
            return carry

        lax.fori_loop(0, nbig, big_start, 0)
        lax.fori_loop(0, nbig, big_wait, 0)


def _dispatch(xaug, sel, locv, cntp, loc, goff, nch, tail, tailn, s_rows):
    npb, l, _ = xaug.shape
    t = MOE_T
    nt_b = l // t
    nt = npb * nt_b
    tile = lambda w: pl.BlockSpec((1, t, w), lambda j, *_: (j // nt_b, j % nt_b, 0))
    return pl.pallas_call(
        _dispatch_kernel,
        grid_spec=pltpu.PrefetchScalarGridSpec(
            num_scalar_prefetch=6,
            grid=(nt,),
            in_specs=[tile(AUG_W), tile(LANES), pl.BlockSpec((1, 1, LANES), lambda j, *_: (j, 0, 0))],
            out_specs=[pl.BlockSpec((1, t, MOE_SLOTS), lambda j, *_: (j, 0, 0)),
                       pl.BlockSpec(memory_space=pl.ANY)],
            scratch_shapes=[pltpu.VMEM((2, MOE_SLOTS, AUG_W), BF16), pltpu.VMEM((MOE_TM, AUG_W), BF16),
                            pltpu.SemaphoreType.DMA((2,))]),
        out_shape=[jax.ShapeDtypeStruct((nt, t, MOE_SLOTS), BF16), jax.ShapeDtypeStruct((s_rows, AUG_W), BF16)],
        compiler_params=_cparams("arbitrary"),
        name="moe_dispatch",
    )(cntp, loc, goff, nch, tail, tailn, xaug, sel, locv)


def _ffn_kernel(te_ref, na_ref, xs_ref, w1_ref, b1_ref, w2_ref, b2_ref, perm_ref, ys_ref, w1p_ref, w2b_ref):
    i = pl.program_id(0)
    e = te_ref[i]
    active = i < na_ref[0]
    fresh = jnp.logical_or(i == 0, e != te_ref[jnp.maximum(i - 1, 0)])
    hh = EXPERT_HIDDEN // 2

    @pl.when(jnp.logical_and(active, fresh))
    def _():
        for c in range(2):
            blk = w1_ref[0, 0, :, 2 * hh * c:2 * hh * (c + 1)].astype(BF16)
            r = _dot(blk, perm_ref[...])
            w1p_ref[:, hh * c:hh * (c + 1)] = r[:, :hh].astype(BF16)
            w1p_ref[:, EXPERT_HIDDEN + hh * c:EXPERT_HIDDEN + hh * (c + 1)] = r[:, hh:].astype(BF16)
        w2b_ref[...] = w2_ref[0, 0].astype(BF16)

    @pl.when(active)
    def _():
        x = xs_ref[:, 0:D_MODEL]
        extra = xs_ref[:, D_MODEL:AUG_W].astype(F32)
        lane = lax.broadcasted_iota(jnp.int32, extra.shape, 1)
        mine = jnp.logical_and(lane % N_EXPERTS == e, lane < 3 * N_EXPERTS)
        wslot = jnp.where(mine, extra, 0.0).sum(axis=-1, keepdims=True)
        h = _dot(x, w1p_ref[...]) + b1_ref[0]
        g = jnp.minimum(h[:, :EXPERT_HIDDEN], SWIGLU_LIMIT)
        up = jnp.clip(h[:, EXPERT_HIDDEN:], -SWIGLU_LIMIT, SWIGLU_LIMIT)
        act = g * _sigmoid(SWIGLU_ALPHA * g) * (up + 1.0)
        y = _dot(act.astype(BF16), w2b_ref[...]) + b2_ref[0]
        ys_ref[...] = (wslot * y).astype(BF16)

    @pl.when(jnp.logical_not(active))
    def _():
        ys_ref[...] = jnp.zeros(ys_ref.shape, BF16)


def _ffn(xs, tile_e, n_active, w1, b1p, w2, b2, perm, n_tiles, layer):
    tm = MOE_TM
    d, h2 = w1.shape[2], w1.shape[3]
    once = pl.Buffered(1)
    return pl.pallas_call(
        _ffn_kernel,
        grid_spec=pltpu.PrefetchScalarGridSpec(
            num_scalar_prefetch=2,
            grid=(n_tiles,),
            in_specs=[pl.BlockSpec((tm, AUG_W), lambda i, te, na: (jnp.where(i < na[0], i, 0), 0)),
                      pl.BlockSpec((1, 1, d, h2), lambda i, te, na: (layer, te[i], 0, 0), pipeline_mode=once),
                      pl.BlockSpec((1, 1, h2), lambda i, te, na: (te[i], 0, 0)),
                      pl.BlockSpec((1, 1, h2 // 2, d), lambda i, te, na: (layer, te[i], 0, 0), pipeline_mode=once),
                      pl.BlockSpec((1, 1, d), lambda i, te, na: (te[i], 0, 0)),
                      pl.BlockSpec(perm.shape, lambda i, te, na: (0, 0), pipeline_mode=once)],
            out_specs=pl.BlockSpec((tm, d), lambda i, te, na: (i, 0)),
            scratch_shapes=[pltpu.VMEM((d, h2), BF16), pltpu.VMEM((h2 // 2, d), BF16)]),
        out_shape=jax.ShapeDtypeStruct((n_tiles * tm, d), BF16),
        compiler_params=_cparams("arbitrary"),
        name="moe_ffn",
    )(tile_e, n_active, xs, w1, b1p, w2, b2, perm)


def _combine_kernel(cntp_ref, loc_ref, goff_ref, nch_ref, pt_ref, x_ref, g2_ref, lg_ref, lb_ref, ys_ref,
                    o_ref, stage_ref, sems):
    j = pl.program_id(0)
    nt = pl.num_programs(0)
    buf = j % 2

    def in_copy(b):
        return lambda dst, src, rows: pltpu.make_async_copy(
            ys_ref.at[pl.ds(src, rows)], stage_ref.at[b].at[pl.ds(dst, rows)], sems.at[b])

    @pl.when(j == 0)
    def _():
        stage_ref[...] = jnp.zeros(stage_ref.shape, BF16)
        _group_copies(j, cntp_ref, loc_ref, goff_ref, in_copy(buf))

    @pl.when(j + 1 < nt)
    def _():
        _group_copies(j + 1, cntp_ref, loc_ref, goff_ref, in_copy(1 - buf))

    _wait_copies(nch_ref, j, in_copy(buf))
    y = _dot(pt_ref[0], stage_ref[buf])
    o_ref[0] = _ln(DEEPNORM_ALPHA * x_ref[0] + g2_ref[0] * y) * lg_ref[...] + lb_ref[...]


def _combine(pt, ys, x1, g2, lg, lb, cntp, loc, goff, nch):
    npb, l, d = x1.shape
    t = MOE_T
    nt_b = l // t
    tile = lambda w: pl.BlockSpec((1, t, w), lambda j, *_: (j // nt_b, j % nt_b, 0))
    const = lambda a: pl.BlockSpec(a.shape, lambda j, *_: (0,) * a.ndim)
    return pl.pallas_call(
        _combine_kernel,
        grid_spec=pltpu.PrefetchScalarGridSpec(
            num_scalar_prefetch=4,
            grid=(npb * nt_b,),
            in_specs=[pl.BlockSpec((1, t, MOE_SLOTS), lambda j, *_: (j, 0, 0)), tile(d),
                      pl.BlockSpec((1, 1, d), lambda j, *_: (j // nt_b, 0, 0)), const(lg), const(lb),
                      pl.BlockSpec(memory_space=pl.ANY)],
            out_specs=tile(d),
            scratch_shapes=[pltpu.VMEM((2, MOE_SLOTS, d), BF16), pltpu.SemaphoreType.DMA((2,))]),
        out_shape=jax.ShapeDtypeStruct((npb, l, d), F32),
        compiler_params=_cparams("arbitrary"),
        name="moe_combine",
    )(cntp, loc, goff, nch, pt, x1, g2, lg, lb, ys)


def _moe(x1, sh2, sc2, g2, rw, rb, w1, b1p, w2, b2, perm, lg, lb, npb, layer):
    _, l, d = x1.shape
    n_tok = npb * l
    nt = n_tok // MOE_T
    xaug, sel, cnt = _route(x1, sh2, sc2, rw, rb, npb)

    i32 = jnp.int32
    cnt = cnt[:, 0, :N_EXPERTS].astype(i32)
    cntp = (cnt + (MOE_G - 1)) // MOE_G * MOE_G
    loc = jnp.cumsum(cntp, axis=1) - cntp
    tot = cntp.sum(axis=0)
    seg = (tot + (MOE_TM - 1)) // MOE_TM * MOE_TM
    seg_start = jnp.cumsum(seg) - seg
    goff = seg_start[None, :] + jnp.cumsum(cntp, axis=0) - cntp
    s_max = TOP_K * n_tok + nt * N_EXPERTS * (MOE_G - 1) + N_EXPERTS * (MOE_TM - MOE_G)
    n_tiles = -(-s_max // MOE_TM)
    cum_tiles = jnp.cumsum(seg // MOE_TM)
    tile_e = jnp.minimum((jnp.arange(n_tiles, dtype=i32)[:, None] >= cum_tiles[None, :]).astype(i32).sum(axis=1),
                         N_EXPERTS - 1)
    n_active = cum_tiles[-1:].astype(i32)
    gran = cntp // MOE_G
    nch = jnp.stack([(gran // 2).sum(axis=1), (gran % 2).sum(axis=1)], axis=1).reshape(-1).astype(i32)
    locv = jnp.pad(loc.astype(F32), ((0, 0), (0, LANES - N_EXPERTS)))[:, None, :]
    flat = lambda a: a.reshape(-1).astype(i32)

    tail = jnp.concatenate([seg_start + tot, n_active * MOE_TM]).astype(i32)
    tailn = jnp.concatenate([(seg - tot) // MOE_G, n_tiles - n_active]).astype(i32)
    pt, xs = _dispatch(xaug, sel, locv, flat(cntp), flat(loc), flat(goff), nch, tail, tailn, n_tiles * MOE_TM)
    ys = _ffn(xs, tile_e, n_active, w1, b1p, w2, b2, perm, n_tiles, layer)
    return _combine(pt, ys, x1, g2, lg, lb, flat(cntp), flat(loc), flat(goff), nch)


def _deinterleave(n):
    return np.concatenate([np.arange(0, n, 2), np.arange(1, n, 2)])


def _rope_tables(l):
    rows = l // GRID_W
    r = np.repeat(np.arange(rows, dtype=np.float32), GRID_W)
    col = np.tile(np.arange(GRID_W, dtype=np.float32), rows)
    axis_dim = ATT_HEAD_DIM // 2
    inv_freq = (ROPE_THETA ** (-np.arange(0, axis_dim, 2, dtype=np.float32) / axis_dim)).astype(np.float32)
    ang = np.concatenate([r[:, None] * inv_freq, col[:, None] * inv_freq], axis=-1)
    c, s = np.cos(ang).astype(np.float32), np.sin(ang).astype(np.float32)
    z = np.zeros_like(s)
    rc = np.tile(np.concatenate([c, c], -1), (1, 2))
    rs1 = np.tile(np.concatenate([-s, z], -1), (1, 2))
    rs2 = np.tile(np.concatenate([z, s], -1), (1, 2))
    ident = (np.ones_like(rc), np.zeros_like(rc), np.zeros_like(rc))
    return tuple(jnp.asarray(np.stack([a, b])) for a, b in zip((rc, rs1, rs2), ident))


def kernel(x, c, ctx, c_ctx, w_mod, b_mod, w_in, att_q_norm, att_k_norm, gla_wdec_f, gla_bdec_f, gla_wdec_b, gla_bdec_b, gla_out_norm, hy_conv_w, hy_conv_b, hy_fw1, hy_fb1, hy_fw2, hy_fb2, hy_fw3, hy_fb3, hy_skip, w_branch_a, w_branch_b, w_branch_c, w_out, ln1_g, ln1_b, router_w, router_b, exp_w1, exp_b1, exp_w2, exp_b2, ln2_g, ln2_b):
    b, l, d = x.shape
    lc = ctx.shape[1]
    assert d == D_MODEL and (b * lc) % l == 0 and l % lc == 0 and lc % GLA_CHUNK == 0
    pc = (b * lc) // l
    p = b + pc

    perm64 = _deinterleave(ATT_HEAD_DIM)
    q_cols = np.concatenate([np.concatenate([64 * j + perm64, 64 * (j + 4) + perm64]) for j in range(4)])
    k_cols = ATT_Q_W + np.concatenate([perm64, 64 + perm64])
    v_cols = ATT_Q_W + ATT_KV_W + np.arange(ATT_KV_W)
    att_cols = np.concatenate([q_cols, k_cols, v_cols])
    o0 = ATT_Q_W + 2 * ATT_KV_W
    gla_cols = o0 + np.concatenate([np.arange(0, 2 * GLA_K_W + GLA_V_W + 2 * GLA_RANK)])
    og0 = o0 + 2 * GLA_K_W + GLA_V_W + 2 * GLA_RANK
    hy0 = og0 + GLA_V_W
    gt0 = hy0 + 3 * HY_WIDTH
    gain_perm = np.concatenate([perm64, perm64])
    bd = jnp.asarray(np.kron(np.eye(2), np.ones((64, 64))), dtype=BF16)
    rope = _rope_tables(l)
    fwd_l, inv_l = _dft_tables(l)
    fwd_c, inv_c = _dft_tables(lc)
    cmap = np.concatenate([np.arange(b), np.full(pc, b)])
    hh = EXPERT_HIDDEN // 2
    perm_np = np.zeros((2 * hh, 2 * hh), np.float32)
    perm_np[2 * np.arange(hh), np.arange(hh)] = 1.0
    perm_np[2 * np.arange(hh) + 1, hh + np.arange(hh)] = 1.0
    perm = jnp.asarray(perm_np, dtype=BF16)

    x_pair = (x, ctx.reshape(pc, l, d))
    cc = jnp.concatenate([c, c_ctx[None], jnp.zeros((7, d), F32)], axis=0)

    for i in range(DEPTH):
        with_ctx = i < DEPTH - 1
        npb = p if with_ctx else b
        mod = _modulation(cc, w_mod[i], b_mod[i])[cmap]
        sh1, sc1, g1, sh2, sc2, g2 = [m[:, None, :] for m in jnp.split(mod, 6, axis=-1)]

        wi = w_in[i]
        watt = wi[:, att_cols].astype(BF16)
        wgla = jnp.pad(wi[:, gla_cols], ((0, 0), (0, GLA_GROUP_W - gla_cols.size))).astype(BF16)
        wog = wi[:, og0:hy0].astype(BF16)
        why = wi[:, hy0:gt0].astype(BF16)
        wgt = wi[:, gt0:].astype(BF16)
        qg = att_q_norm[i][gain_perm].reshape(1, LANES)
        kg = att_k_norm[i][gain_perm].reshape(1, LANES)
        qn, kn, v, gla, og, hy = _inproj(*x_pair, sh1, sc1, (watt, wgla, wog, why), qg, kg, rope, bd, b)

        o_a = _attention(qn, kn, v, b, lc, with_ctx)

        wdec = jnp.zeros((LANES, 2 * GLA_K_W), F32)
        wdec = wdec.at[0:GLA_RANK, 0:GLA_K_W].set(gla_wdec_f[i])
        wdec = wdec.at[GLA_RANK:2 * GLA_RANK, GLA_K_W:].set(gla_wdec_b[i])
        bdec = jnp.concatenate([gla_bdec_f[i], gla_bdec_b[i]]).reshape(1, -1)
        o_b, o_b_ctx = _gla(gla, og, wdec, bdec, gla_out_norm[i].reshape(1, -1), b, lc)

        hre, him = _hy_filter(l, hy_fw1[i], hy_fb1[i], hy_fw2[i], hy_fb2[i], hy_fw3[i], hy_fb3[i], fwd_l)
        u_h, x0_h, t_h = _hy_pre(hy, hy_conv_w[i], hy_conv_b[i], hy_skip[i], b, l, b)
        y_h = _hy_conv(u_h, fwd_l, inv_l, hre, him)
        if with_ctx:
            hre_c, him_c = _hy_filter(lc, hy_fw1[i], hy_fb1[i], hy_fw2[i], hy_fb2[i], hy_fw3[i], hy_fb3[i], fwd_c)
            u_c, x0_c, t_c = _hy_pre(hy, hy_conv_w[i], hy_conv_b[i], hy_skip[i], b, lc, b)
            y_c = _hy_conv(u_c, fwd_c, inv_c, hre_c, him_c)
            fold = lambda a_ctx: a_ctx.reshape(pc, l, a_ctx.shape[-1])
            pairs = [(o_b, fold(o_b_ctx)), (y_h, fold(y_c)), (x0_h, fold(x0_c)), (t_h, fold(t_c))]
        else:
            pairs = [(a, a) for a in (o_b, y_h, x0_h, t_h)]

        wa_rows = np.concatenate([np.concatenate([64 * j + np.arange(64), 64 * (j + 4) + np.arange(64)])
                                  for j in range(4)])
        x1 = _merge(x_pair, o_a, *pairs, sh1, sc1, g1, wgt,
                    w_branch_a[i][wa_rows].astype(BF16), w_branch_b[i].astype(BF16),
                    w_branch_c[i].astype(BF16), w_out[i].astype(BF16),
                    ln1_g[i].reshape(1, d), ln1_b[i].reshape(1, d), npb)

        rw = jnp.pad(router_w[i], ((0, 0), (0, LANES - N_EXPERTS)))
        rb = jnp.concatenate([router_b[i], jnp.full((LANES - N_EXPERTS,), NEG_BIG, F32)]).reshape(1, LANES)
        b1p = jnp.concatenate([exp_b1[i][..., 0::2], exp_b1[i][..., 1::2]], axis=-1)[:, None, :]
        x_all = _moe(x1, sh2, sc2, g2, rw, rb, exp_w1, b1p, exp_w2, exp_b2[i][:, None, :], perm,
                     ln2_g[i].reshape(1, d), ln2_b[i].reshape(1, d), npb, i)
        x_pair = (x_all, x_all)
    return x_all
```
